```python
import math, functools
import jax, jax.numpy as jnp
from jax import lax
import numpy as np

D_MODEL = 1024
BATCH = 8
SEQ = 8192
DEPTH = 2

GRID_W = 64
CTX_LEN = 256
Q_BLOCK = 128
ROPE_THETA = 10000.0
EPS = 1e-6

DIFF_WIDTH = D_MODEL // 4
DIFF_V_DIM = 64
DIFF_HEADS = DIFF_WIDTH // DIFF_V_DIM
DIFF_QK_DIM = DIFF_V_DIM // 2
CONV_CH = D_MODEL // 4
CONV_K = 31
GQA_WIDTH = D_MODEL // 2
GQA_HEAD_DIM = 64
GQA_HEADS = GQA_WIDTH // GQA_HEAD_DIM
GQA_KV_HEADS = GQA_HEADS // 4
GQA_GROUP = GQA_HEADS // GQA_KV_HEADS
MIX_WIDTH = DIFF_WIDTH + CONV_CH + GQA_WIDTH

IN_SPLITS = (2 * DIFF_HEADS * DIFF_QK_DIM, 2 * DIFF_HEADS * DIFF_QK_DIM, DIFF_HEADS * DIFF_V_DIM,
             2 * CONV_CH,
             GQA_HEADS * GQA_HEAD_DIM, GQA_KV_HEADS * GQA_HEAD_DIM, GQA_KV_HEADS * GQA_HEAD_DIM)
IN_WIDTH = sum(IN_SPLITS)

FFN_DENSE = ((8 * D_MODEL // 3 + 127) // 128) * 128
N_EXPERTS = 8
TOP_K = 2
MOE_FF = 7 * D_MODEL // 2
MOE_BLOCK = 512
N_DENSE = (DEPTH + 1) // 2
N_MOE = DEPTH // 2

kernel_name = "hybrid_diffattn_conformer_gqa_moe_dit"


def rmsnorm(x, g):
    xf = x.astype(jnp.float32)
    y = xf * lax.rsqrt(jnp.mean(xf * xf, axis=-1, keepdims=True) + EPS)
    return (y * g.astype(jnp.float32)).astype(x.dtype)


def layernorm(x, g, b):
    xf = x.astype(jnp.float32)
    mu = jnp.mean(xf, axis=-1, keepdims=True)
    var = jnp.mean(jnp.square(xf - mu), axis=-1, keepdims=True)
    y = (xf - mu) * lax.rsqrt(var + EPS) * g.astype(jnp.float32) + b.astype(jnp.float32)
    return y.astype(x.dtype)


def modulate(h, shift, scale):
    return h * (1 + scale) + shift


def axial_rope_tables(rows, cols, dim, dtype):
    half = dim // 2
    inv = 1.0 / (ROPE_THETA ** (jnp.arange(0, half, 2, dtype=jnp.float32) / half))
    ang_r = rows.astype(jnp.float32)[:, None] * inv
    ang_c = cols.astype(jnp.float32)[:, None] * inv
    ang = jnp.concatenate([ang_r, ang_r, ang_c, ang_c], axis=-1)
    return jnp.cos(ang).astype(dtype), jnp.sin(ang).astype(dtype)


def apply_rope(x, cos, sin):
    d = x.shape[-1]
    xr = x.reshape(x.shape[:-1] + (2, 2, d // 4))
    rot = jnp.concatenate([-xr[..., 1:2, :], xr[..., 0:1, :]], axis=-2).reshape(x.shape)
    return x * cos[None, :, None, :] + rot * sin[None, :, None, :]


def diff_attention(q, k, v, lam):
    s = jnp.einsum('bqhcd,bthcd->bhcqt', q, k, preferred_element_type=jnp.float32) * (DIFF_QK_DIM ** -0.5)
    p = jax.nn.softmax(s, axis=-1)
    a = p[:, :, 0] - lam * p[:, :, 1]
    return jnp.einsum('bhqt,bthe->bqhe', a.astype(v.dtype), v)


def gqa_attention(q, k, v):
    s = jnp.einsum('bqkgd,btkd->bkgqt', q, k, preferred_element_type=jnp.float32) * (GQA_HEAD_DIM ** -0.5)
    p = jax.nn.softmax(s, axis=-1)
    return jnp.einsum('bkgqt,btkd->bqkgd', p.astype(v.dtype), v)


def sweep_query_blocks(attn_fn, q):
    b, t = q.shape[:2]
    nb = t // Q_BLOCK
    qb = jnp.moveaxis(q.reshape((b, nb, Q_BLOCK) + q.shape[2:]), 1, 0)
    ob = lax.map(attn_fn, qb)
    return jnp.moveaxis(ob, 0, 1).reshape((b, t) + ob.shape[3:])


def depthwise_conv(u, w, b):
    y = lax.conv_general_dilated(u, w[:, None, :].astype(u.dtype), window_strides=(1,),
                                 padding=[(CONV_K // 2, CONV_K // 2)],
                                 dimension_numbers=('NWC', 'WIO', 'NWC'),
                                 feature_group_count=u.shape[-1])
    return y + b


def conformer_conv(gb, w, b, ln_g, ln_b):
    a, g = jnp.split(gb, 2, axis=-1)
    u = a * jax.nn.sigmoid(g)
    u = depthwise_conv(u, w, b)
    return jax.nn.silu(layernorm(u, ln_g, ln_b))


def mixer_inputs(p, qn_g, kn_g):
    b, t, _ = p.shape
    offs = np.cumsum(IN_SPLITS)[:-1].tolist()
    qa, ka, va, gb, qc, kc, vc = jnp.split(p, offs, axis=-1)
    qa = qa.reshape(b, t, 2 * DIFF_HEADS, DIFF_QK_DIM)
    ka = ka.reshape(b, t, 2 * DIFF_HEADS, DIFF_QK_DIM)
    va = va.reshape(b, t, DIFF_HEADS, DIFF_V_DIM)
    qc = rmsnorm(qc.reshape(b, t, GQA_HEADS, GQA_HEAD_DIM), qn_g)
    kc = rmsnorm(kc.reshape(b, t, GQA_KV_HEADS, GQA_HEAD_DIM), kn_g)
    vc = vc.reshape(b, t, GQA_KV_HEADS, GQA_HEAD_DIM)
    return qa, ka, va, gb, qc, kc, vc


def to_pairs(t):
    return t.reshape(t.shape[:2] + (DIFF_HEADS, 2, DIFF_QK_DIM))


def to_groups(q):
    return q.reshape(q.shape[:2] + (GQA_KV_HEADS, GQA_GROUP, GQA_HEAD_DIM))


def merge_groups(oa, ob, oc, subln_g, lam_init, w_out):
    b, t = oa.shape[:2]
    oa = (rmsnorm(oa, subln_g) * (1.0 - lam_init)).reshape(b, t, DIFF_WIDTH)
    oc = oc.reshape(b, t, GQA_WIDTH)
    return jnp.concatenate([oa, ob, oc], axis=-1) @ w_out


def swiglu(h, w_gate, w_up, w_down):
    return (jax.nn.silu(h @ w_gate) * (h @ w_up)) @ w_down


def moe_swiglu(h, router_w, w_gate, w_up, w_down):
    lead = h.shape[:-1]
    d = h.shape[-1]
    hf = h.reshape(-1, d)
    n = hf.shape[0]
    m = n * TOP_K
    logits = jnp.matmul(hf, router_w, preferred_element_type=jnp.float32)
    top_logit, top_idx = lax.top_k(logits, TOP_K)
    gates = jax.nn.softmax(top_logit, axis=-1).astype(h.dtype)
    e_flat = top_idx.reshape(m)
    tok = jnp.arange(m, dtype=jnp.int32) // TOP_K
    order = jnp.argsort(e_flat)
    e_sorted = e_flat[order]
    counts = jnp.bincount(e_flat, length=N_EXPERTS)
    padded = (counts + MOE_BLOCK - 1) // MOE_BLOCK * MOE_BLOCK
    pad_end = jnp.cumsum(padded)
    pad_start = pad_end - padded
    start = jnp.cumsum(counts) - counts
    dest_sorted = (pad_start[e_sorted] + jnp.arange(m) - start[e_sorted]).astype(jnp.int32)
    dest = jnp.zeros((m,), jnp.int32).at[order].set(dest_sorted)
    n_blocks = -(-m // MOE_BLOCK) + N_EXPERTS
    rows_total = n_blocks * MOE_BLOCK
    buf_tok = jnp.full((rows_total,), n, jnp.int32).at[dest].set(tok)
    h_pad = jnp.concatenate([hf, jnp.zeros((1, d), hf.dtype)], axis=0)
    xb = h_pad[buf_tok].reshape(n_blocks, MOE_BLOCK, d)
    blk_start = jnp.arange(n_blocks, dtype=pad_end.dtype) * MOE_BLOCK
    blk_expert = jnp.minimum(jnp.searchsorted(pad_end, blk_start, side='right'), N_EXPERTS - 1)

    def expert_block(args):
        xblk, e = args
        return swiglu(xblk, w_gate[e], w_up[e], w_down[e])

    yb = lax.map(expert_block, (xb, blk_expert)).reshape(rows_total, d)
    y = jnp.sum(yb[dest].reshape(n, TOP_K, d) * gates[..., None], axis=1)
    return y.reshape(lead + (d,))


def channel_mix(h, i, ffn_gate, ffn_up, ffn_down, router_w, moe_gate, moe_up, moe_down):
    j = i // 2
    if i % 2 == 0:
        return swiglu(h, ffn_gate[j], ffn_up[j], ffn_down[j])
    return moe_swiglu(h, router_w[j], moe_gate[j], moe_up[j], moe_down[j])


def setup_inputs(seed: int = 0) -> dict:
    key = jax.random.key(seed)
    ks = jax.random.split(key, 29)

    def nrm(k, shape, scale):
        return scale * jax.random.normal(k, shape, jnp.float32)

    L = DEPTH
    return {
        "x": nrm(ks[0], (BATCH, SEQ, D_MODEL), 1.0),
        "c": nrm(ks[1], (BATCH, D_MODEL), 1.0),
        "ctx": nrm(ks[2], (BATCH, CTX_LEN, D_MODEL), 1.0),
        "c_ctx": nrm(ks[3], (D_MODEL,), 1.0),
        "ada_w": nrm(ks[4], (L, D_MODEL, 6 * D_MODEL), D_MODEL ** -0.5),
        "ada_b": nrm(ks[5], (L, 6 * D_MODEL), 0.01),
        "norm1_g": 1.0 + nrm(ks[6], (L, D_MODEL), 0.02),
        "norm2_g": 1.0 + nrm(ks[7], (L, D_MODEL), 0.02),
        "w_in": nrm(ks[8], (L, D_MODEL, IN_WIDTH), D_MODEL ** -0.5),
        "w_out": nrm(ks[9], (L, MIX_WIDTH, D_MODEL), MIX_WIDTH ** -0.5),
        "lam_q1": nrm(ks[10], (L, DIFF_QK_DIM), 0.1),
        "lam_k1": nrm(ks[11], (L, DIFF_QK_DIM), 0.1),
        "lam_q2": nrm(ks[12], (L, DIFF_QK_DIM), 0.1),
        "lam_k2": nrm(ks[13], (L, DIFF_QK_DIM), 0.1),
        "diff_subln_g": 1.0 + nrm(ks[14], (L, DIFF_V_DIM), 0.02),
        "conv_w": nrm(ks[15], (L, CONV_K, CONV_CH), CONV_K ** -0.5),
        "conv_b": nrm(ks[16], (L, CONV_CH), 0.01),
        "conv_ln_g": 1.0 + nrm(ks[17], (L, CONV_CH), 0.02),
        "conv_ln_b": nrm(ks[18], (L, CONV_CH), 0.01),
        "q_norm_g": 1.0 + nrm(ks[19], (L, GQA_HEAD_DIM), 0.02),
        "k_norm_g": 1.0 + nrm(ks[20], (L, GQA_HEAD_DIM), 0.02),
        "ffn_gate": nrm(ks[21], (N_DENSE, D_MODEL, FFN_DENSE), D_MODEL ** -0.5),
        "ffn_up": nrm(ks[22], (N_DENSE, D_MODEL, FFN_DENSE), D_MODEL ** -0.5),
        "ffn_down": nrm(ks[23], (N_DENSE, FFN_DENSE, D_MODEL), FFN_DENSE ** -0.5),
        "router_w": nrm(ks[24], (N_MOE, D_MODEL, N_EXPERTS), D_MODEL ** -0.5),
        "moe_gate": nrm(ks[25], (N_MOE, N_EXPERTS, D_MODEL, MOE_FF), D_MODEL ** -0.5),
        "moe_up": nrm(ks[26], (N_MOE, N_EXPERTS, D_MODEL, MOE_FF), D_MODEL ** -0.5),
        "moe_down": nrm(ks[27], (N_MOE, N_EXPERTS, MOE_FF, D_MODEL), MOE_FF ** -0.5),
        "final_g": 1.0 + nrm(ks[28], (D_MODEL,), 0.02),
    }


def reference(x, c, ctx, c_ctx, ada_w, ada_b, norm1_g, norm2_g, w_in, w_out,
              lam_q1, lam_k1, lam_q2, lam_k2, diff_subln_g, conv_w, conv_b, conv_ln_g, conv_ln_b,
              q_norm_g, k_norm_g, ffn_gate, ffn_up, ffn_down, router_w, moe_gate, moe_up, moe_down,
              final_g):
    b, s, d = x.shape
    ROWS = s // GRID_W
    rows = jnp.repeat(jnp.arange(ROWS), GRID_W)
    cols = jnp.tile(jnp.arange(GRID_W), ROWS)
    cos_a, sin_a = axial_rope_tables(rows, cols, DIFF_QK_DIM, x.dtype)
    cos_c, sin_c = axial_rope_tables(rows, cols, GQA_HEAD_DIM, x.dtype)
    silu_c = jax.nn.silu(c)
    silu_cc = jax.nn.silu(c_ctx)

    for i in range(DEPTH):
        last = i == DEPTH - 1
        mod = (silu_c @ ada_w[i] + ada_b[i]).reshape(b, 6, 1, d)
        modc = (silu_cc @ ada_w[i] + ada_b[i]).reshape(6, d)
        lam_init = 0.8 - 0.6 * math.exp(-0.3 * i)
        lam = (jnp.exp(jnp.sum(lam_q1[i].astype(jnp.float32) * lam_k1[i].astype(jnp.float32)))
               - jnp.exp(jnp.sum(lam_q2[i].astype(jnp.float32) * lam_k2[i].astype(jnp.float32)))
               + lam_init)

        h = modulate(rmsnorm(x, norm1_g[i]), mod[:, 0], mod[:, 1])
        hc = modulate(rmsnorm(ctx, norm1_g[i]), modc[0], modc[1])
        qa, ka, va, gb, qc, kc, vc = mixer_inputs(h @ w_in[i], q_norm_g[i], k_norm_g[i])
        qa_x, ka_x, va_x, gb_x, qc_x, kc_x, vc_x = mixer_inputs(hc @ w_in[i], q_norm_g[i], k_norm_g[i])

        qa_l = to_pairs(apply_rope(qa, cos_a, sin_a))
        ka_all = jnp.concatenate([to_pairs(apply_rope(ka, cos_a, sin_a)), to_pairs(ka_x)], axis=1)
        va_all = jnp.concatenate([va, va_x], axis=1)
        oa = sweep_query_blocks(functools.partial(diff_attention, k=ka_all, v=va_all, lam=lam), qa_l)

        ob = conformer_conv(gb, conv_w[i], conv_b[i], conv_ln_g[i], conv_ln_b[i])

        qc_l = to_groups(apply_rope(qc, cos_c, sin_c))
        kc_all = jnp.concatenate([apply_rope(kc, cos_c, sin_c), kc_x], axis=1)
        vc_all = jnp.concatenate([vc, vc_x], axis=1)
        oc = sweep_query_blocks(functools.partial(gqa_attention, k=kc_all, v=vc_all), qc_l)

        x = x + mod[:, 2] * merge_groups(oa, ob, oc, diff_subln_g[i], lam_init, w_out[i])
        h2 = modulate(rmsnorm(x, norm2_g[i]), mod[:, 3], mod[:, 4])
        x = x + mod[:, 5] * channel_mix(h2, i, ffn_gate, ffn_up, ffn_down, router_w, moe_gate, moe_up, moe_down)

        if not last:
            oa_x = diff_attention(to_pairs(qa_x), to_pairs(ka_x), va_x, lam)
            ob_x = conformer_conv(gb_x, conv_w[i], conv_b[i], conv_ln_g[i], conv_ln_b[i])
            oc_x = gqa_attention(to_groups(qc_x), kc_x, vc_x)
            ctx = ctx + modc[2] * merge_groups(oa_x, ob_x, oc_x, diff_subln_g[i], lam_init, w_out[i])
            hc2 = modulate(rmsnorm(ctx, norm2_g[i]), modc[3], modc[4])
            ctx = ctx + modc[5] * channel_mix(hc2, i, ffn_gate, ffn_up, ffn_down, router_w, moe_gate, moe_up, moe_down)

    return rmsnorm(x, final_g)
```

```python
import functools
import math

import numpy as np
import jax
import jax.numpy as jnp
from jax import lax
from jax.experimental import pallas as pl
from jax.experimental.pallas import tpu as pltpu

F32 = jnp.float32
BF16 = jnp.bfloat16
I32 = jnp.int32

EPS = 1e-6
ROPE_THETA = 10000.0
GRID_W = 64

DIFF_HEADS = 4
DIFF_QK = 32
HEAD_V = 64
GQA_HEADS = 8
GQA_KV = 2
GQA_GROUP = GQA_HEADS // GQA_KV
CONV_K = 31
N_EXPERTS = 8
LOG2E = math.log2(math.e)

LANES = 128
MXU_DIM = 256
VMEM_LIMIT = 52 * 1024 * 1024
NEG_BIG = -1e30

C_QA, C_KA, C_VA, C_GB, C_QC, C_KC, C_VC, C_END = 0, 256, 512, 1024, 1536, 2048, 2176, 2432


def _cparams(semantics):
    return pltpu.CompilerParams(dimension_semantics=semantics, vmem_limit_bytes=VMEM_LIMIT)


def _dot(a, b):
    return jnp.dot(a, b, preferred_element_type=F32)


def _sigmoid(z):
    return 1.0 / (1.0 + jnp.exp(-z))


ROW_TILE = 8


def _store_row_tiles(ref, val, rows):
    for a in range(ROW_TILE):
        ref[pl.ds(a, rows, stride=ROW_TILE), :] = val[:, a * LANES:(a + 1) * LANES]


def _load_row_tiles(ref, rows):
    return jnp.concatenate([ref[pl.ds(a, rows, stride=ROW_TILE), :] for a in range(ROW_TILE)], axis=1)


def _mod_kernel(c_ref, w_ref, b_ref, o_ref):
    c = c_ref[...]
    s = c * _sigmoid(c)
    o_ref[...] = jnp.dot(s, w_ref[...], preferred_element_type=F32, precision=lax.Precision.HIGHEST) + b_ref[...]


def _ada_mod(cc, w, b):
    rows, d = cc.shape
    n = w.shape[1]
    tn = d
    return pl.pallas_call(
        _mod_kernel,
        out_shape=jax.ShapeDtypeStruct((rows, n), F32),
        grid=(n // tn,),
        in_specs=[pl.BlockSpec((rows, d), lambda j: (0, 0)),
                  pl.BlockSpec((d, tn), lambda j: (0, j)),
                  pl.BlockSpec((1, tn), lambda j: (0, j))],
        out_specs=pl.BlockSpec((rows, tn), lambda j: (0, j)),
        compiler_params=_cparams(("arbitrary",)),
        name="ada_mod",
    )(cc, w, b.reshape(1, n))


def _inproj_kernel(x_ref, shift_ref, scale_ref, g_ref, w_ref, cosa_ref, sina_ref, cosc_ref, sinc_ref,
                   pa_ref, pc_ref, hm_ref, qg_ref, kg_ref,
                   qa_o, kat_o, va_o, gb_o, qc_o, kct_o, vc_o, *, qa_scale, qc_scale):
    x = x_ref[0]
    ms = jnp.mean(x * x, axis=-1, keepdims=True)
    h = x * lax.rsqrt(ms + EPS) * g_ref[...]
    h = h * (1.0 + scale_ref[0]) + shift_ref[0]
    hb = h.astype(BF16)

    def proj(lo, hi):
        return _dot(hb, w_ref[:, lo:hi])

    def blockmat(y, m_ref):
        yb = y.astype(BF16)
        w = y.shape[1]
        if w == LANES:
            return _dot(yb, m_ref[:LANES, :LANES])
        return jnp.concatenate([_dot(yb[:, c:c + MXU_DIM], m_ref[...]) for c in range(0, w, MXU_DIM)], axis=1)

    def rope(y, cos, sin, p_ref):
        reps = y.shape[1] // LANES
        cos = jnp.tile(cos, (1, reps))
        sin = jnp.tile(sin, (1, reps))
        return y * cos + blockmat(y, p_ref) * sin

    def ones_col(width):
        lane = lax.broadcasted_iota(I32, (1, width), 1)
        return jnp.where(lane % LANES == HEAD_V, 1.0, 0.0).astype(F32)

    cosa, sina, cosc, sinc = cosa_ref[...], sina_ref[...], cosc_ref[...], sinc_ref[...]

    qa_o[0] = (rope(proj(C_QA, C_KA), cosa, sina, pa_ref) * qa_scale).astype(BF16)
    kat_o[0, 0] = rope(proj(C_KA, C_VA), cosa, sina, pa_ref).T.astype(BF16)
    va_o[0] = (proj(C_VA, C_GB) + ones_col(C_GB - C_VA)).astype(BF16)
    gb_o[0] = proj(C_GB, C_QC).astype(BF16)

    y = proj(C_QC, C_KC)
    yn = y * lax.rsqrt(blockmat(y * y, hm_ref) + EPS) * qg_ref[...]
    qc_o[0] = (rope(yn, cosc, sinc, pc_ref) * qc_scale).astype(BF16)

    y = proj(C_KC, C_VC)
    yn = y * lax.rsqrt(blockmat(y * y, hm_ref) + EPS) * kg_ref[...]
    kct_o[0, 0] = rope(yn, cosc, sinc, pc_ref).T.astype(BF16)

    vc_o[0] = (proj(C_VC, C_END) + ones_col(C_END - C_VC)).astype(BF16)


def _in_projection(x, shift, scale, g1, w_aug, tabs, mats, qg, kg, tm):
    b, s, d = x.shape
    nt = s // tm
    cosa, sina, cosc, sinc = tabs
    pa, pc, hm = mats
    row = lambda bb, i: (bb, i, 0)
    const2 = lambda bb, i: (0, 0)
    per_b = lambda bb, i: (bb, 0, 0)
    tab = lambda bb, i: (i, 0)
    kern = functools.partial(_inproj_kernel, qa_scale=DIFF_QK ** -0.5 * LOG2E, qc_scale=HEAD_V ** -0.5 * LOG2E)
    out_shape = (
        jax.ShapeDtypeStruct((b, s, 256), BF16),
        jax.ShapeDtypeStruct((b, nt, 256, tm), BF16),
        jax.ShapeDtypeStruct((b, s, 512), BF16),
        jax.ShapeDtypeStruct((b, s, 512), BF16),
        jax.ShapeDtypeStruct((b, s, 512), BF16),
        jax.ShapeDtypeStruct((b, nt, 128, tm), BF16),
        jax.ShapeDtypeStruct((b, s, 256), BF16),
    )
    out_specs = (
        pl.BlockSpec((1, tm, 256), row),
        pl.BlockSpec((1, 1, 256, tm), lambda bb, i: (bb, i, 0, 0)),
        pl.BlockSpec((1, tm, 512), row),
        pl.BlockSpec((1, tm, 512), row),
        pl.BlockSpec((1, tm, 512), row),
        pl.BlockSpec((1, 1, 128, tm), lambda bb, i: (bb, i, 0, 0)),
        pl.BlockSpec((1, tm, 256), row),
    )
    in_specs = [
        pl.BlockSpec((1, tm, d), row),
        pl.BlockSpec((1, 1, d), per_b),
        pl.BlockSpec((1, 1, d), per_b),
        pl.BlockSpec((1, d), const2),
        pl.BlockSpec((d, C_END), const2),
        pl.BlockSpec((tm, LANES), tab), pl.BlockSpec((tm, LANES), tab),
        pl.BlockSpec((tm, LANES), tab), pl.BlockSpec((tm, LANES), tab),
        pl.BlockSpec((MXU_DIM, MXU_DIM), const2), pl.BlockSpec((MXU_DIM, MXU_DIM), const2),
        pl.BlockSpec((MXU_DIM, MXU_DIM), const2),
        pl.BlockSpec((1, 512), const2), pl.BlockSpec((1, 128), const2),
    ]
    return pl.pallas_call(
        kern, out_shape=out_shape, grid=(b, nt), in_specs=in_specs, out_specs=out_specs,
        compiler_params=_cparams(("arbitrary", "arbitrary")), name="in_projection",
    )(x, shift, scale, g1, w_aug, cosa, sina, cosc, sinc, pa, pc, hm, qg, kg)


def _softmax_step(s, m_ref, rows):
    m_old = m_ref[rows, :]
    m_new = jnp.maximum(m_old, jnp.max(s, axis=-1, keepdims=True))
    m_ref[rows, :] = m_new
    return jnp.exp2(s - m_new), jnp.exp2(m_old - m_new)


def _gqa_kernel(*refs, n_parts, tq):
    q_ref = refs[0]
    k_refs = refs[1:1 + n_parts]
    v_refs = refs[1 + n_parts:1 + 2 * n_parts]
    o_ref, m_ref, acc_ref = refs[1 + 2 * n_parts:]
    q = q_ref[0]
    qs = jnp.concatenate([q[:, HEAD_V * j:HEAD_V * (j + 1)] for j in range(GQA_GROUP)], axis=0)
    m_ref[...] = jnp.full(m_ref.shape, NEG_BIG, F32)
    acc_ref[...] = jnp.zeros(acc_ref.shape, F32)
    rows = slice(0, GQA_GROUP * tq)

    for k_ref, v_ref in zip(k_refs, v_refs):
        n_chunks, tk = k_ref.shape[1], k_ref.shape[3]

        def body(c, carry, k_ref=k_ref, v_ref=v_ref, tk=tk):
            kc = k_ref[0, c]
            vc = v_ref[0, pl.ds(pl.multiple_of(c * tk, tk), tk), :]
            p, alpha = _softmax_step(_dot(qs, kc), m_ref, rows)
            acc_ref[...] = acc_ref[...] * alpha + _dot(p.astype(BF16), vc)
            return carry

        lax.fori_loop(0, n_chunks, body, 0)

    acc = acc_ref[...]
    o = acc[:, :HEAD_V] / acc[:, HEAD_V:HEAD_V + 1]
    for j in range(GQA_GROUP):
        o_ref[0, :, HEAD_V * j:HEAD_V * (j + 1)] = o[j * tq:(j + 1) * tq].astype(BF16)


def _gqa_attention(q, k_parts, v_parts, tq):
    b, sq, _ = q.shape
    n_parts = len(k_parts)
    in_specs = [pl.BlockSpec((1, tq, 256), lambda bb, g, i: (bb, i, g))]
    for kp in k_parts:
        in_specs.append(pl.BlockSpec((1, kp.shape[1], HEAD_V, kp.shape[3]), lambda bb, g, i: (bb, 0, g, 0)))
    for vp in v_parts:
        in_specs.append(pl.BlockSpec((1, vp.shape[1], LANES), lambda bb, g, i: (bb, 0, g)))
    return pl.pallas_call(
        functools.partial(_gqa_kernel, n_parts=n_parts, tq=tq),
        out_shape=jax.ShapeDtypeStruct((b, sq, 512), BF16),
        grid=(b, GQA_KV, sq // tq),
        in_specs=in_specs,
        out_specs=pl.BlockSpec((1, tq, 256), lambda bb, g, i: (bb, i, g)),
        scratch_shapes=[pltpu.VMEM((GQA_GROUP * tq, 1), F32), pltpu.VMEM((GQA_GROUP * tq, LANES), F32)],
        compiler_params=_cparams(("arbitrary", "arbitrary", "arbitrary")),
        name="gqa_attention",
    )(q, *k_parts, *v_parts)


def _diff_kernel(*refs, n_parts, tq, lam_init):
    q_ref = refs[0]
    k_refs = refs[1:1 + n_parts]
    v_refs = refs[1 + n_parts:1 + 2 * n_parts]
    lam_ref, sg_ref, o_ref, m_ref, acc_ref = refs[1 + 2 * n_parts:]
    q = q_ref[0]
    qmaps = [q[:, DIFF_QK * j:DIFF_QK * (j + 1)] for j in range(4)]
    m_ref[...] = jnp.full(m_ref.shape, NEG_BIG, F32)
    acc_ref[...] = jnp.zeros(acc_ref.shape, F32)

    for k_ref, v_ref in zip(k_refs, v_refs):
        n_chunks, tk = k_ref.shape[1], k_ref.shape[3]

        def body(c, carry, k_ref=k_ref, v_ref=v_ref, tk=tk):
            kc = k_ref[0, c]
            vc = v_ref[0, pl.ds(pl.multiple_of(c * tk, tk), tk), :]
            for hh in range(2):
                ps, alphas = [], []
                for mp in range(2):
                    j = 2 * hh + mp
                    s = _dot(qmaps[j], kc[DIFF_QK * j:DIFF_QK * (j + 1), :])
                    p, alpha = _softmax_step(s, m_ref, slice(j * tq, (j + 1) * tq))
                    ps.append(p.astype(BF16))
                    alphas.append(alpha)
                rows = slice(2 * hh * tq, (2 * hh + 2) * tq)
                pv = _dot(jnp.concatenate(ps, axis=0), vc[:, LANES * hh:LANES * (hh + 1)])
                acc_ref[rows, :] = acc_ref[rows, :] * jnp.concatenate(alphas, axis=0) + pv
            return carry

        lax.fori_loop(0, n_chunks, body, 0)

    lv = lam_ref[...]
    lam = (jnp.exp(jnp.sum(lv[0:1] * lv[1:2], axis=-1, keepdims=True))
           - jnp.exp(jnp.sum(lv[2:3] * lv[3:4], axis=-1, keepdims=True)) + lam_init)
    acc = acc_ref[...]
    for hh in range(2):
        a0 = acc[(2 * hh) * tq:(2 * hh + 1) * tq]
        a1 = acc[(2 * hh + 1) * tq:(2 * hh + 2) * tq]
        o = a0[:, :HEAD_V] / a0[:, HEAD_V:HEAD_V + 1] - lam * (a1[:, :HEAD_V] / a1[:, HEAD_V:HEAD_V + 1])
        ms = jnp.mean(o * o, axis=-1, keepdims=True)
        on = o * lax.rsqrt(ms + EPS) * sg_ref[...] * (1.0 - lam_init)
        o_ref[0, :, HEAD_V * hh:HEAD_V * (hh + 1)] = on.astype(BF16)


def _diff_attention(q, k_parts, v_parts, lam_vecs, subln_g, lam_init, tq):
    b, sq, _ = q.shape
    n_parts = len(k_parts)
    in_specs = [pl.BlockSpec((1, tq, LANES), lambda bb, p, i: (bb, i, p))]
    for kp in k_parts:
        in_specs.append(pl.BlockSpec((1, kp.shape[1], LANES, kp.shape[3]), lambda bb, p, i: (bb, 0, p, 0)))
    for vp in v_parts:
        in_specs.append(pl.BlockSpec((1, vp.shape[1], 2 * LANES), lambda bb, p, i: (bb, 0, p)))
    in_specs.append(pl.BlockSpec((4, DIFF_QK), lambda bb, p, i: (0, 0)))
    in_specs.append(pl.BlockSpec((1, HEAD_V), lambda bb, p, i: (0, 0)))
    return pl.pallas_call(
        functools.partial(_diff_kernel, n_parts=n_parts, tq=tq, lam_init=lam_init),
        out_shape=jax.ShapeDtypeStruct((b, sq, 256), BF16),
        grid=(b, DIFF_HEADS // 2, sq // tq),
        in_specs=in_specs,
        out_specs=pl.BlockSpec((1, tq, LANES), lambda bb, p, i: (bb, i, p)),
        scratch_shapes=[pltpu.VMEM((4 * tq, 1), F32), pltpu.VMEM((4 * tq, LANES), F32)],
        compiler_params=_cparams(("arbitrary", "arbitrary", "arbitrary")),
        name="diff_attention",
    )(q, *k_parts, *v_parts, lam_vecs, subln_g)


CONV_HALO = 16
CONV_ROWS = 64


def _conv_kernel(gb_ref, prev_ref, next_ref, w_ref, b_ref, lg_ref, lb_ref, o_ref, u_ref, *, tm):
    i = pl.program_id(1)
    last = pl.num_programs(1) - 1
    ch = w_ref.shape[1]

    def glu(z):
        z = z.astype(F32)
        return z[:, :ch] * _sigmoid(z[:, ch:])

    u_ref[CONV_HALO:CONV_HALO + tm, :] = glu(gb_ref[0])
    u_ref[0:CONV_HALO, :] = jnp.where(i > 0, glu(prev_ref[0]), 0.0)
    u_ref[CONV_HALO + tm:2 * CONV_HALO + tm, :] = jnp.where(i < last, glu(next_ref[0]), 0.0)

    off = CONV_HALO - CONV_K // 2
    for r0 in range(0, tm, CONV_ROWS):
        acc = jnp.zeros((CONV_ROWS, ch), F32)
        for j in range(CONV_K):
            acc = acc + u_ref[r0 + off + j:r0 + off + j + CONV_ROWS, :] * w_ref[j:j + 1, :]
        y = acc + b_ref[...]
        mu = jnp.mean(y, axis=-1, keepdims=True)
        yc = y - mu
        var = jnp.mean(yc * yc, axis=-1, keepdims=True)
        z = yc * lax.rsqrt(var + EPS) * lg_ref[...] + lb_ref[...]
        o_ref[0, r0:r0 + CONV_ROWS, :] = (z * _sigmoid(z)).astype(BF16)


def _conformer_conv(gb, w, bias, ln_g, ln_b, tm):
    b, s, two_ch = gb.shape
    ch = two_ch // 2
    hb = tm // CONV_HALO
    n_halo = s // CONV_HALO
    const2 = lambda bb, i: (0, 0)
    return pl.pallas_call(
        functools.partial(_conv_kernel, tm=tm),
        out_shape=jax.ShapeDtypeStruct((b, s, ch), BF16),
        grid=(b, s // tm),
        in_specs=[
            pl.BlockSpec((1, tm, two_ch), lambda bb, i: (bb, i, 0)),
            pl.BlockSpec((1, CONV_HALO, two_ch), lambda bb, i: (bb, jnp.maximum(i * hb - 1, 0), 0)),
            pl.BlockSpec((1, CONV_HALO, two_ch), lambda bb, i: (bb, jnp.minimum((i + 1) * hb, n_halo - 1), 0)),
            pl.BlockSpec((CONV_K, ch), const2), pl.BlockSpec((1, ch), const2),
            pl.BlockSpec((1, ch), const2), pl.BlockSpec((1, ch), const2),
        ],
        out_specs=pl.BlockSpec((1, tm, ch), lambda bb, i: (bb, i, 0)),
        scratch_shapes=[pltpu.VMEM((tm + 2 * CONV_HALO, ch), F32)],
        compiler_params=_cparams(("arbitrary", "arbitrary")),
        name="conformer_conv",
    )(gb, gb, gb, w, bias, ln_g, ln_b)


def _merge_kernel(*refs, with_router):
    oa_ref, ob_ref, oc_ref, x_ref, gate_ref, shift_ref, scale_ref, g2_ref, w_ref = refs[:9]
    if with_router:
        rw_ref, xo_ref, h2_ref, lg_ref = refs[9:]
    else:
        xo_ref, h2_ref = refs[9:]
    tm = x_ref.shape[1]
    wa = oa_ref.shape[2]
    wb = wa + ob_ref.shape[2]
    y = _dot(oa_ref[0], w_ref[0:wa, :]) + _dot(ob_ref[0], w_ref[wa:wb, :]) + _dot(oc_ref[0], w_ref[wb:, :])
    xn = x_ref[0] + gate_ref[0] * y
    xo_ref[0] = xn
    ms = jnp.mean(xn * xn, axis=-1, keepdims=True)
    h2 = xn * lax.rsqrt(ms + EPS) * g2_ref[...] * (1.0 + scale_ref[0]) + shift_ref[0]
    if with_router:
        _store_row_tiles(h2_ref, h2, tm)
        lg_ref[...] = lax.dot_general(rw_ref[...], h2, (((1,), (1,)), ((), ())),
                                      preferred_element_type=F32, precision=lax.Precision.HIGHEST)
    else:
        h2_ref[0] = h2.astype(BF16)


def _merge(oa, ob, oc, x, gate, shift, scale, g2, w_out, router_wt, tm):
    b, s, d = x.shape
    row = lambda bb, i: (bb, i, 0)
    per_b = lambda bb, i: (bb, 0, 0)
    const2 = lambda bb, i: (0, 0)
    nt = s // tm
    in_specs = [
        pl.BlockSpec((1, tm, oa.shape[2]), row), pl.BlockSpec((1, tm, ob.shape[2]), row),
        pl.BlockSpec((1, tm, oc.shape[2]), row), pl.BlockSpec((1, tm, d), row),
        pl.BlockSpec((1, 1, d), per_b), pl.BlockSpec((1, 1, d), per_b), pl.BlockSpec((1, 1, d), per_b),
        pl.BlockSpec((1, d), const2), pl.BlockSpec((d, d), const2),
    ]
    out_shape = [jax.ShapeDtypeStruct((b, s, d), F32)]
    out_specs = [pl.BlockSpec((1, tm, d), row)]
    args = [oa, ob, oc, x, gate, shift, scale, g2, w_out]
    with_router = router_wt is not None
    if not with_router:
        out_shape.append(jax.ShapeDtypeStruct((b, s, d), BF16))
        out_specs.append(pl.BlockSpec((1, tm, d), row))
    else:
        out_shape.append(jax.ShapeDtypeStruct((b * s * ROW_TILE, LANES), F32))
        out_specs.append(pl.BlockSpec((tm * ROW_TILE, LANES), lambda bb, i: (bb * nt + i, 0)))
        in_specs.append(pl.BlockSpec((N_EXPERTS, d), const2))
        out_shape.append(jax.ShapeDtypeStruct((N_EXPERTS, b * s), F32))
        out_specs.append(pl.BlockSpec((N_EXPERTS, tm), lambda bb, i: (0, bb * nt + i)))
        args.append(router_wt)
    return pl.pallas_call(
        functools.partial(_merge_kernel, with_router=with_router),
        out_shape=tuple(out_shape), grid=(b, nt), in_specs=in_specs, out_specs=tuple(out_specs),
        compiler_params=_cparams(("arbitrary", "arbitrary")), name="merge_heads",
    )(*args)


def _ffn_kernel(h_ref, x_ref, gate_ref, wg_ref, wu_ref, wd_ref, o_ref, *, tf):
    h = h_ref[...]
    ff = wg_ref.shape[1]
    acc = jnp.zeros(x_ref.shape, F32)
    for f in range(0, ff, tf):
        g = _dot(h, wg_ref[:, f:f + tf])
        u = _dot(h, wu_ref[:, f:f + tf])
        a = (g * _sigmoid(g) * u).astype(BF16)
        acc = acc + _dot(a, wd_ref[f:f + tf, :])
    o_ref[...] = x_ref[...] + gate_ref[0] * acc


def _dense_ffn(h2, x, gate, wg, wu, wd, tm):
    n, d = x.shape
    ff = wg.shape[1]
    s = n // gate.shape[0]
    row = lambda i: (i, 0)
    const2 = lambda i: (0, 0)
    resident = pl.Buffered(1)
    return pl.pallas_call(
        functools.partial(_ffn_kernel, tf=MXU_DIM),
        out_shape=jax.ShapeDtypeStruct((n, d), F32),
        grid=(n // tm,),
        in_specs=[
            pl.BlockSpec((tm, d), row), pl.BlockSpec((tm, d), row),
            pl.BlockSpec((1, 1, d), lambda i: ((i * tm) // s, 0, 0)),
            pl.BlockSpec((d, ff), const2, pipeline_mode=resident),
            pl.BlockSpec((d, ff), const2, pipeline_mode=resident),
            pl.BlockSpec((ff, d), const2, pipeline_mode=resident),
        ],
        out_specs=pl.BlockSpec((tm, d), row),
        compiler_params=_cparams(("arbitrary",)), name="dense_ffn",
    )(h2, x, gate, wg, wu, wd)


def _top2(lg):
    sub = lax.broadcasted_iota(I32, lg.shape, 0)
    l1 = jnp.max(lg, axis=0, keepdims=True)
    i1 = jnp.min(jnp.where(lg == l1, sub, N_EXPERTS), axis=0, keepdims=True)
    m1 = sub == i1
    lg2 = jnp.where(m1, -jnp.inf, lg)
    l2 = jnp.max(lg2, axis=0, keepdims=True)
    i2 = jnp.min(jnp.where(lg2 == l2, sub, N_EXPERTS), axis=0, keepdims=True)
    m2 = sub == i2
    return l1, l2, m1, m2


def _sublane_cumsum(x):
    sub = lax.broadcasted_iota(I32, x.shape, 0)
    for sh in (1, 2, 4):
        x = x + jnp.where(sub >= sh, pltpu.roll(x, sh, 0), 0.0)
    return x


def _route_kernel(lg_ref, tri_ref, dest_ref, gates_ref, be_ref, base_ref, start_ref, *, block_rows):
    phase = pl.program_id(0)
    j = pl.program_id(1)
    l1, l2, m1, m2 = _top2(lg_ref[...])
    e = jnp.where(m1 | m2, 1.0, 0.0).astype(F32)
    cnt = jnp.sum(e, axis=1, keepdims=True)

    @pl.when((phase == 0) & (j == 0))
    def _():
        base_ref[...] = jnp.zeros(base_ref.shape, F32)

    @pl.when((phase == 1) & (j == 0))
    def _():
        counts = base_ref[...]
        nblk = jnp.floor((counts + (block_rows - 1)) * (1.0 / block_rows))
        end_blk = _sublane_cumsum(nblk)
        start_ref[...] = (end_blk - nblk) * block_rows
        blk = lax.broadcasted_iota(I32, be_ref.shape, 1).astype(F32)
        owner = jnp.sum(jnp.where(end_blk[:, :1] <= blk, 1.0, 0.0), axis=0, keepdims=True)
        be_ref[...] = jnp.broadcast_to(jnp.minimum(owner, N_EXPERTS - 1.0), be_ref.shape).astype(I32)
        base_ref[...] = jnp.zeros(base_ref.shape, F32)

    @pl.when(phase == 1)
    def _():
        prefix = _dot(e.astype(BF16), tri_ref[...]) + base_ref[:, :1] + start_ref[:, :1]
        d1 = jnp.sum(jnp.where(m1, prefix, 0.0), axis=0, keepdims=True)
        d2 = jnp.sum(jnp.where(m2, prefix, 0.0), axis=0, keepdims=True)
        sub = lax.broadcasted_iota(I32, dest_ref.shape, 0)
        dest_ref[...] = jnp.where(sub == 0, d1, jnp.where(sub == 1, d2, 0.0)).astype(I32)
        ex = jnp.exp(l2 - l1)
        g1 = 1.0 / (1.0 + ex)
        g2 = ex / (1.0 + ex)
        half = lax.broadcasted_iota(I32, (LANES, lg_ref.shape[1]), 0) < LANES // 2
        gates_ref[...] = jnp.where(half, g1, g2).T

    base_ref[...] = base_ref[...] + cnt


def _route(logits_t, block_rows, n_blocks_pad, tr):
    n = logits_t.shape[1]
    tri = jnp.asarray(np.triu(np.ones((tr, tr), np.float32), k=1), BF16)
    return pl.pallas_call(
        functools.partial(_route_kernel, block_rows=block_rows),
        out_shape=(jax.ShapeDtypeStruct((N_EXPERTS, n), I32),
                   jax.ShapeDtypeStruct((n, LANES), F32),
                   jax.ShapeDtypeStruct((N_EXPERTS, n_blocks_pad), I32)),
        grid=(2, n // tr),
        in_specs=[pl.BlockSpec((N_EXPERTS, tr), lambda p, j: (0, j)),
                  pl.BlockSpec((tr, tr), lambda p, j: (0, 0))],
        out_specs=(pl.BlockSpec((N_EXPERTS, tr), lambda p, j: (0, j * p)),
                   pl.BlockSpec((tr, LANES), lambda p, j: (j * p, 0)),
                   pl.BlockSpec((N_EXPERTS, n_blocks_pad), lambda p, j: (0, 0))),
        scratch_shapes=[pltpu.VMEM((N_EXPERTS, LANES), F32), pltpu.VMEM((N_EXPERTS, LANES), F32)],
        compiler_params=_cparams(("arbitrary", "arbitrary")), name="moe_route",
    )(logits_t, tri)


def _row_copy(src_hbm, src_row, dst_hbm, dst_row, sem):
    src = pl.ds(pl.multiple_of(src_row * ROW_TILE, ROW_TILE), ROW_TILE)
    dst = pl.ds(pl.multiple_of(dst_row * ROW_TILE, ROW_TILE), ROW_TILE)
    return pltpu.make_async_copy(src_hbm.at[src], dst_hbm.at[dst], sem)


def _scatter_kernel(d1_ref, d2_ref, src_hbm, init_hbm, out_hbm, sem, *, rows):
    del init_hbm
    base = pl.program_id(0) * rows

    def start(r, c):
        _row_copy(src_hbm, base + r, out_hbm, d1_ref[0, 0, r], sem).start()
        _row_copy(src_hbm, base + r, out_hbm, d2_ref[0, 0, r], sem).start()
        return c

    def wait(r, c):
        _row_copy(src_hbm, 0, out_hbm, 0, sem).wait()
        _row_copy(src_hbm, 0, out_hbm, 0, sem).wait()
        return c

    lax.fori_loop(0, rows, start, 0)
    lax.fori_loop(0, rows, wait, 0)


def _scatter_rows(src, d1, d2, total_rows, rows):
    n = d1.shape[0]
    idx_spec = pl.BlockSpec((1, 1, rows), lambda i: (i, 0, 0), memory_space=pltpu.SMEM)
    any_spec = pl.BlockSpec(memory_space=pl.ANY)
    return pl.pallas_call(
        functools.partial(_scatter_kernel, rows=rows),
        out_shape=jax.ShapeDtypeStruct((total_rows * ROW_TILE, LANES), src.dtype),
        grid=(n // rows,),
        in_specs=[idx_spec, idx_spec, any_spec, any_spec],
        out_specs=any_spec,
        scratch_shapes=[pltpu.SemaphoreType.DMA(())],
        input_output_aliases={3: 0},
        compiler_params=pltpu.CompilerParams(dimension_semantics=("arbitrary",), has_side_effects=True),
        name="moe_scatter_rows",
    )(d1.reshape(n // rows, 1, rows), d2.reshape(n // rows, 1, rows), src,
      jnp.zeros((total_rows * ROW_TILE, LANES), src.dtype))


def _gather_kernel(d1_ref, d2_ref, src_hbm, o1_hbm, o2_hbm, sem, *, rows):
    base = pl.program_id(0) * rows

    def start(r, c):
        _row_copy(src_hbm, d1_ref[0, 0, r], o1_hbm, base + r, sem).start()
        _row_copy(src_hbm, d2_ref[0, 0, r], o2_hbm, base + r, sem).start()
        return c

    def wait(r, c):
        _row_copy(src_hbm, 0, o1_hbm, 0, sem).wait()
        _row_copy(src_hbm, 0, o2_hbm, 0, sem).wait()
        return c

    lax.fori_loop(0, rows, start, 0)
    lax.fori_loop(0, rows, wait, 0)


def _gather_rows(src, d1, d2, rows):
    n = d1.shape[0]
    idx_spec = pl.BlockSpec((1, 1, rows), lambda i: (i, 0, 0), memory_space=pltpu.SMEM)
    any_spec = pl.BlockSpec(memory_space=pl.ANY)
    out = jax.ShapeDtypeStruct((n * ROW_TILE, LANES), src.dtype)
    return pl.pallas_call(
        functools.partial(_gather_kernel, rows=rows),
        out_shape=(out, out),
        grid=(n // rows,),
        in_specs=[idx_spec, idx_spec, any_spec],
        out_specs=(any_spec, any_spec),
        scratch_shapes=[pltpu.SemaphoreType.DMA(())],
        compiler_params=pltpu.CompilerParams(dimension_semantics=("arbitrary",), has_side_effects=True),
        name="moe_gather_rows",
    )(d1.reshape(n // rows, 1, rows), d2.reshape(n // rows, 1, rows), src)


def _expert_kernel(be_ref, x_ref, wg_ref, wu_ref, wd_ref, o_ref, xs_ref, acc_ref, *, block_rows):
    del be_ref
    f = pl.program_id(1)

    @pl.when(f == 0)
    def _():
        xs_ref[...] = _load_row_tiles(x_ref, block_rows).astype(BF16)
        acc_ref[...] = jnp.zeros(acc_ref.shape, F32)

    x = xs_ref[...]
    g = _dot(x, wg_ref[0])
    u = _dot(x, wu_ref[0])
    a = (g * _sigmoid(g) * u).astype(BF16)
    acc_ref[...] += _dot(a, wd_ref[0])

    @pl.when(f == pl.num_programs(1) - 1)
    def _():
        _store_row_tiles(o_ref, acc_ref[...], block_rows)


def _expert_ffn(xb, blk_expert, wg, wu, wd, block_rows, tf):
    d, ff = wg.shape[1], wg.shape[2]
    rows = xb.shape[0] // ROW_TILE
    grid_spec = pltpu.PrefetchScalarGridSpec(
        num_scalar_prefetch=1,
        grid=(rows // block_rows, ff // tf),
        in_specs=[
            pl.BlockSpec((block_rows * ROW_TILE, LANES), lambda i, f, be: (i, 0)),
            pl.BlockSpec((1, d, tf), lambda i, f, be: (be[i], 0, f)),
            pl.BlockSpec((1, d, tf), lambda i, f, be: (be[i], 0, f)),
            pl.BlockSpec((1, tf, d), lambda i, f, be: (be[i], f, 0)),
        ],
        out_specs=pl.BlockSpec((block_rows * ROW_TILE, LANES), lambda i, f, be: (i, 0)),
        scratch_shapes=[pltpu.VMEM((block_rows, d), BF16), pltpu.VMEM((block_rows, d), F32)],
    )
    return pl.pallas_call(
        functools.partial(_expert_kernel, block_rows=block_rows),
        out_shape=jax.ShapeDtypeStruct(xb.shape, F32), grid_spec=grid_spec,
        compiler_params=_cparams(("arbitrary", "arbitrary")), name="moe_expert_ffn",
    )(blk_expert, xb, wg, wu, wd)


def _combine_kernel(x_ref, y1_ref, y2_ref, gates_ref, gate_ref, fg_ref, o_ref):
    gts = gates_ref[...]
    tm = x_ref.shape[0]
    y = (gts[:, 0:1] * _load_row_tiles(y1_ref, tm)
         + gts[:, LANES // 2:LANES // 2 + 1] * _load_row_tiles(y2_ref, tm))
    xn = x_ref[...] + gate_ref[0] * y
    ms = jnp.mean(xn * xn, axis=-1, keepdims=True)
    o_ref[...] = xn * lax.rsqrt(ms + EPS) * fg_ref[...]


def _combine_final(x, y1, y2, gates, gate, final_g, tm):
    n, d = x.shape
    s = n // gate.shape[0]
    row = lambda i: (i, 0)
    return pl.pallas_call(
        _combine_kernel,
        out_shape=jax.ShapeDtypeStruct((n, d), F32),
        grid=(n // tm,),
        in_specs=[pl.BlockSpec((tm, d), row),
                  pl.BlockSpec((tm * ROW_TILE, LANES), row), pl.BlockSpec((tm * ROW_TILE, LANES), row),
                  pl.BlockSpec((tm, LANES), row),
                  pl.BlockSpec((1, 1, d), lambda i: ((i * tm) // s, 0, 0)),
                  pl.BlockSpec((1, d), lambda i: (0, 0))],
        out_specs=pl.BlockSpec((tm, d), row),
        compiler_params=_cparams(("arbitrary",)), name="moe_combine_final",
    )(x, y1, y2, gates, gate, final_g)


def _rope_tables(s, dim):
    half = dim // 2
    t = jnp.arange(s)
    inv = 1.0 / (ROPE_THETA ** (jnp.arange(0, half, 2, dtype=F32) / half))
    ang_r = (t // GRID_W).astype(F32)[:, None] * inv
    ang_c = (t % GRID_W).astype(F32)[:, None] * inv
    ang = jnp.concatenate([ang_r, ang_r, ang_c, ang_c], axis=-1)
    reps = LANES // dim
    return jnp.tile(jnp.cos(ang), (1, reps)), jnp.tile(jnp.sin(ang), (1, reps))


def _rotate_matrix(dim):
    q = dim // 4
    p = np.zeros((MXU_DIM, MXU_DIM), np.float32)
    for j in range(MXU_DIM):
        if (j % (2 * q)) < q:
            p[j + q, j] = -1.0
        else:
            p[j - q, j] = 1.0
    return jnp.asarray(p, BF16)


def _head_mean_matrix():
    m = np.kron(np.eye(MXU_DIM // HEAD_V, dtype=np.float32), np.full((HEAD_V, HEAD_V), 1.0 / HEAD_V, np.float32))
    return jnp.asarray(m, BF16)


def _widen_values(w, heads):
    d = w.shape[0]
    w = w.reshape(d, heads, HEAD_V)
    return jnp.concatenate([w, jnp.zeros_like(w)], axis=-1).reshape(d, heads * LANES)


def _widen_in_proj(w):
    qa, ka, va, gb, qc, kc, vc = jnp.split(w, [256, 512, 768, 1280, 1792, 1920], axis=1)
    return jnp.concatenate([qa, ka, _widen_values(va, DIFF_HEADS), gb, qc, kc, _widen_values(vc, GQA_KV)],
                           axis=1).astype(BF16)


def kernel(x, c, ctx, c_ctx, ada_w, ada_b, norm1_g, norm2_g, w_in, w_out, lam_q1, lam_k1, lam_q2, lam_k2,
           diff_subln_g, conv_w, conv_b, conv_ln_g, conv_ln_b, q_norm_g, k_norm_g, ffn_gate, ffn_up, ffn_down,
           router_w, moe_gate, moe_up, moe_down, final_g):
    b, s, d = x.shape
    sc = ctx.shape[1]
    depth = ada_w.shape[0]
    n = b * s
    assert depth % 2 == 0, "the final RMSNorm is fused into the MoE combine of the last (odd) layer"

    tm = min(512, s)
    tmc = min(512, sc)
    tq = min(256, s)
    tqc = min(256, sc)

    tabs_x = _rope_tables(s, DIFF_QK) + _rope_tables(s, HEAD_V)
    ones_c, zeros_c = jnp.ones((sc, LANES), F32), jnp.zeros((sc, LANES), F32)
    tabs_c = (ones_c, zeros_c, ones_c, zeros_c)
    mats = (_rotate_matrix(DIFF_QK), _rotate_matrix(HEAD_V), _head_mean_matrix())

    cc = jnp.zeros((16, d), F32).at[:b].set(c).at[b].set(c_ctx)

    for i in range(depth):
        last = i == depth - 1
        lam_init = 0.8 - 0.6 * math.exp(-0.3 * i)
        mod_all = _ada_mod(cc, ada_w[i], ada_b[i])
        mod = mod_all[:b].reshape(b, 6, 1, d)
        modc = jnp.broadcast_to(mod_all[b].reshape(1, 6, 1, d), (b, 6, 1, d))

        w_aug = _widen_in_proj(w_in[i])
        g1 = norm1_g[i].reshape(1, d)
        qg = jnp.tile(q_norm_g[i], GQA_HEADS).reshape(1, -1)
        kg = jnp.tile(k_norm_g[i], GQA_KV).reshape(1, -1)
        lam_vecs = jnp.stack([lam_q1[i], lam_k1[i], lam_q2[i], lam_k2[i]]).astype(F32)
        subln = diff_subln_g[i].reshape(1, HEAD_V)
        conv_args = (conv_w[i], conv_b[i].reshape(1, -1), conv_ln_g[i].reshape(1, -1), conv_ln_b[i].reshape(1, -1))
        w_o = w_out[i].astype(BF16)
        g2 = norm2_g[i].reshape(1, d)

        qa, kat, va, gb, qc, kct, vc = _in_projection(x, mod[:, 0], mod[:, 1], g1, w_aug, tabs_x, mats, qg, kg, tm)
        qa_x, kat_x, va_x, gb_x, qc_x, kct_x, vc_x = _in_projection(
            ctx, modc[:, 0], modc[:, 1], g1, w_aug, tabs_c, mats, qg, kg, tmc)

        oa = _diff_attention(qa, [kat, kat_x], [va, va_x], lam_vecs, subln, lam_init, tq)
        ob = _conformer_conv(gb, *conv_args, tm)
        oc = _gqa_attention(qc, [kct, kct_x], [vc, vc_x], tq)

        j = i // 2
        if i % 2 == 0:
            x, h2 = _merge(oa, ob, oc, x, mod[:, 2], mod[:, 3], mod[:, 4], g2, w_o, None, tm)
            wg, wu, wd = ffn_gate[j].astype(BF16), ffn_up[j].astype(BF16), ffn_down[j].astype(BF16)
            x = _dense_ffn(h2.reshape(n, d), x.reshape(n, d), mod[:, 5], wg, wu, wd, tm).reshape(b, s, d)
        else:
            rwt = router_w[j].T.astype(F32)
            x, h2, logits_t = _merge(oa, ob, oc, x, mod[:, 2], mod[:, 3], mod[:, 4], g2, w_o, rwt, tm)
            block_rows = 1024 if n >= 8192 else 256
            n_blocks = (2 * n) // block_rows + N_EXPERTS
            n_blocks_pad = -(-n_blocks // LANES) * LANES
            dest, gates, blk_e = _route(logits_t, block_rows, n_blocks_pad, min(512, n))
            xb = _scatter_rows(h2, dest[0], dest[1], n_blocks * block_rows, min(256, n))
            yb = _expert_ffn(xb, blk_e[0, :n_blocks], moe_gate[j].astype(BF16), moe_up[j].astype(BF16),
                             moe_down[j].astype(BF16), block_rows, 512)
            y1, y2 = _gather_rows(yb, dest[0], dest[1], min(256, n))
            assert last
            x = _combine_final(x.reshape(n, d), y1, y2, gates, mod[:, 5], final_g.reshape(1, d), tm).reshape(b, s, d)

        if not last:
            oa_x = _diff_attention(qa_x, [kat_x], [va_x], lam_vecs, subln, lam_init, tqc)
            ob_x = _conformer_conv(gb_x, *conv_args, tmc)
            oc_x = _gqa_attention(qc_x, [kct_x], [vc_x], tqc)
            assert i % 2 == 0, "context tokens only ever pass through dense channel mixers"
            ctx, hc2 = _merge(oa_x, ob_x, oc_x, ctx, modc[:, 2], modc[:, 3], modc[:, 4], g2, w_o, None, tmc)
            ctx = _dense_ffn(hc2.reshape(b * sc, d), ctx.reshape(b * sc, d), modc[:, 5], wg, wu, wd,
                             tmc).reshape(b, sc, d)

    return x
```

```python
import functools
import math

import numpy as np
import jax
import jax.numpy as jnp
from jax import lax
from jax.experimental import pallas as pl
from jax.experimental.pallas import tpu as pltpu

F32 = jnp.float32
BF16 = jnp.bfloat16
I32 = jnp.int32

EPS = 1e-6
ROPE_THETA = 10000.0
GRID_W = 64

DIFF_HEADS = 4
DIFF_QK = 32
HEAD_V = 64
GQA_HEADS = 8
GQA_KV = 2
GQA_GROUP = GQA_HEADS // GQA_KV
CONV_K = 31
N_EXPERTS = 8
LOG2E = math.log2(math.e)

LANES = 128
MXU_DIM = 256
VMEM_LIMIT = 52 * 1024 * 1024
NEG_BIG = -1e30

C_QA, C_KA, C_VA, C_GB, C_QC, C_KC, C_VC, C_END = 0, 256, 512, 1024, 1536, 2048, 2176, 2432


def _cparams(semantics):
    return pltpu.CompilerParams(dimension_semantics=semantics, vmem_limit_bytes=VMEM_LIMIT)


def _dot(a, b):
    return jnp.dot(a, b, preferred_element_type=F32)


def _sigmoid(z):
    return 1.0 / (1.0 + jnp.exp(-z))


ROW_TILE = 8


def _store_row_tiles(ref, val, rows):
    for a in range(ROW_TILE):
        ref[pl.ds(a, rows, stride=ROW_TILE), :] = val[:, a * LANES:(a + 1) * LANES]


def _load_row_tiles(ref, rows):
    return jnp.concatenate([ref[pl.ds(a, rows, stride=ROW_TILE), :] for a in range(ROW_TILE)], axis=1)


def _mod_kernel(c_ref, w_ref, b_ref, o_ref):
    c = c_ref[...]
    s = c * _sigmoid(c)
    o_ref[...] = jnp.dot(s, w_ref[...], preferred_element_type=F32, precision=lax.Precision.HIGHEST) + b_ref[...]


def _ada_mod(cc, w, b):
    rows, d = cc.shape
    n = w.shape[1]
    tn = d
    return pl.pallas_call(
        _mod_kernel,
        out_shape=jax.ShapeDtypeStruct((rows, n), F32),
        grid=(n // tn,),
        in_specs=[pl.BlockSpec((rows, d), lambda j: (0, 0)),
                  pl.BlockSpec((d, tn), lambda j: (0, j)),
                  pl.BlockSpec((1, tn), lambda j: (0, j))],
        out_specs=pl.BlockSpec((rows, tn), lambda j: (0, j)),
        compiler_params=_cparams(("arbitrary",)),
        name="ada_mod",
    )(cc, w, b.reshape(1, n))


def _inproj_kernel(x_ref, shift_ref, scale_ref, g_ref, w_ref, cosa_ref, sina_ref, cosc_ref, sinc_ref,
                   pa_ref, pc_ref, hm_ref, qg_ref, kg_ref,
                   qa_o, kat_o, va_o, gb_o, qc_o, kct_o, vc_o, *, qa_scale, qc_scale):
    x = x_ref[0]
    ms = jnp.mean(x * x, axis=-1, keepdims=True)
    h = x * lax.rsqrt(ms + EPS) * g_ref[...]
    h = h * (1.0 + scale_ref[0]) + shift_ref[0]
    hb = h.astype(BF16)

    def proj(lo, hi):
        return _dot(hb, w_ref[:, lo:hi])

    def blockmat(y, m_ref):
        yb = y.astype(BF16)
        w = y.shape[1]
        if w == LANES:
            return _dot(yb, m_ref[:LANES, :LANES])
        return jnp.concatenate([_dot(yb[:, c:c + MXU_DIM], m_ref[...]) for c in range(0, w, MXU_DIM)], axis=1)

    def rope(y, cos, sin, p_ref):
        reps = y.shape[1] // LANES
        cos = jnp.tile(cos, (1, reps))
        sin = jnp.tile(sin, (1, reps))
        return y * cos + blockmat(y, p_ref) * sin

    def ones_col(width):
        lane = lax.broadcasted_iota(I32, (1, width), 1)
        return jnp.where(lane % LANES == HEAD_V, 1.0, 0.0).astype(F32)

    cosa, sina, cosc, sinc = cosa_ref[...], sina_ref[...], cosc_ref[...], sinc_ref[...]

    qa_o[0] = (rope(proj(C_QA, C_KA), cosa, sina, pa_ref) * qa_scale).astype(BF16)
    kat_o[0, 0] = rope(proj(C_KA, C_VA), cosa, sina, pa_ref).T.astype(BF16)
    va_o[0] = (proj(C_VA, C_GB) + ones_col(C_GB - C_VA)).astype(BF16)
    gb_o[0] = proj(C_GB, C_QC).astype(BF16)

    y = proj(C_QC, C_KC)
    yn = y * lax.rsqrt(blockmat(y * y, hm_ref) + EPS) * qg_ref[...]
    qc_o[0] = (rope(yn, cosc, sinc, pc_ref) * qc_scale).astype(BF16)

    y = proj(C_KC, C_VC)
    yn = y * lax.rsqrt(blockmat(y * y, hm_ref) + EPS) * kg_ref[...]
    kct_o[0, 0] = rope(yn, cosc, sinc, pc_ref).T.astype(BF16)

    vc_o[0] = (proj(C_VC, C_END) + ones_col(C_END - C_VC)).astype(BF16)


def _in_projection(x, shift, scale, g1, w_aug, tabs, mats, qg, kg, tm):
    b, s, d = x.shape
    nt = s // tm
    cosa, sina, cosc, sinc = tabs
    pa, pc, hm = mats
    row = lambda bb, i: (bb, i, 0)
    const2 = lambda bb, i: (0, 0)
    per_b = lambda bb, i: (bb, 0, 0)
    tab = lambda bb, i: (i, 0)
    kern = functools.partial(_inproj_kernel, qa_scale=DIFF_QK ** -0.5 * LOG2E, qc_scale=HEAD_V ** -0.5 * LOG2E)
    out_shape = (
        jax.ShapeDtypeStruct((b, s, 256), BF16),
        jax.ShapeDtypeStruct((b, nt, 256, tm), BF16),
        jax.ShapeDtypeStruct((b, s, 512), BF16),
        jax.ShapeDtypeStruct((b, s, 512), BF16),
        jax.ShapeDtypeStruct((b, s, 512), BF16),
        jax.ShapeDtypeStruct((b, nt, 128, tm), BF16),
        jax.ShapeDtypeStruct((b, s, 256), BF16),
    )
    out_specs = (
        pl.BlockSpec((1, tm, 256), row),
        pl.BlockSpec((1, 1, 256, tm), lambda bb, i: (bb, i, 0, 0)),
        pl.BlockSpec((1, tm, 512), row),
        pl.BlockSpec((1, tm, 512), row),
        pl.BlockSpec((1, tm, 512), row),
        pl.BlockSpec((1, 1, 128, tm), lambda bb, i: (bb, i, 0, 0)),
        pl.BlockSpec((1, tm, 256), row),
    )
    in_specs = [
        pl.BlockSpec((1, tm, d), row),
        pl.BlockSpec((1, 1, d), per_b),
        pl.BlockSpec((1, 1, d), per_b),
        pl.BlockSpec((1, d), const2),
        pl.BlockSpec((d, C_END), const2),
        pl.BlockSpec((tm, LANES), tab), pl.BlockSpec((tm, LANES), tab),
        pl.BlockSpec((tm, LANES), tab), pl.BlockSpec((tm, LANES), tab),
        pl.BlockSpec((MXU_DIM, MXU_DIM), const2), pl.BlockSpec((MXU_DIM, MXU_DIM), const2),
        pl.BlockSpec((MXU_DIM, MXU_DIM), const2),
        pl.BlockSpec((1, 512), const2), pl.BlockSpec((1, 128), const2),
    ]
    return pl.pallas_call(
        kern, out_shape=out_shape, grid=(b, nt), in_specs=in_specs, out_specs=out_specs,
        compiler_params=_cparams(("arbitrary", "arbitrary")), name="in_projection",
    )(x, shift, scale, g1, w_aug, cosa, sina, cosc, sinc, pa, pc, hm, qg, kg)


def _attention_sweeps(qms, k_slices, pv_groups, k_refs, v_refs, m_ref, acc_ref):
    r = qms[0].shape[0]

    def scores(j, kc):
        return _dot(qms[j], kc[k_slices[j], :])

    def sweep(running_max):
        acc_ref[...] = jnp.zeros(acc_ref.shape, F32)
        if running_max:
            m_ref[...] = jnp.full(m_ref.shape, NEG_BIG, F32)
        else:
            kc0 = k_refs[0][0, 0]
            for j in range(len(qms)):
                m0 = jnp.max(scores(j, kc0), axis=-1, keepdims=True)
                m_ref[j * r:(j + 1) * r, :] = jnp.broadcast_to(m0, (r, LANES))

        for k_ref, v_ref in zip(k_refs, v_refs):
            n_chunks, tk = k_ref.shape[1], k_ref.shape[3]

            def body(c, carry, k_ref=k_ref, v_ref=v_ref, tk=tk):
                kc = k_ref[0, c]
                vc = v_ref[0, pl.ds(pl.multiple_of(c * tk, tk), tk), :]
                for ids, v_lanes in pv_groups:
                    ps, alphas = [], []
                    for j in ids:
                        rows = slice(j * r, (j + 1) * r)
                        s = scores(j, kc)
                        m = m_ref[rows, :]
                        if running_max:
                            m_new = jnp.maximum(m, jnp.max(s, axis=-1, keepdims=True))
                            m_ref[rows, :] = m_new
                            alphas.append(jnp.exp2(m - m_new))
                            m = m_new
                        ps.append(jnp.exp2(s - jnp.tile(m, (1, tk // LANES))).astype(BF16))
                    rows = slice(ids[0] * r, (ids[-1] + 1) * r)
                    pv = _dot(jnp.concatenate(ps, axis=0), vc[:, v_lanes])
                    if running_max:
                        acc_ref[rows, :] = acc_ref[rows, :] * jnp.concatenate(alphas, axis=0) + pv
                    else:
                        acc_ref[rows, :] += pv
                return carry

            lax.fori_loop(0, n_chunks, body, 0, unroll=2 if n_chunks % 2 == 0 else 1)

    sweep(False)
    not_finite = jnp.sum(acc_ref[...] * 0.0)

    @pl.when(not_finite != 0.0)
    def _():
        sweep(True)


def _gqa_kernel(*refs, n_parts, tq):
    q_ref = refs[0]
    k_refs = refs[1:1 + n_parts]
    v_refs = refs[1 + n_parts:1 + 2 * n_parts]
    o_ref, m_ref, acc_ref = refs[1 + 2 * n_parts:]
    q = q_ref[0]
    qs = jnp.concatenate([q[:, HEAD_V * j:HEAD_V * (j + 1)] for j in range(GQA_GROUP)], axis=0)
    _attention_sweeps([qs], [slice(0, HEAD_V)], [((0,), slice(0, LANES))], k_refs, v_refs, m_ref, acc_ref)
    acc = acc_ref[...]
    o = acc[:, :HEAD_V] / acc[:, HEAD_V:HEAD_V + 1]
    for j in range(GQA_GROUP):
        o_ref[0, :, HEAD_V * j:HEAD_V * (j + 1)] = o[j * tq:(j + 1) * tq].astype(BF16)


def _gqa_attention(q, k_parts, v_parts, tq):
    b, sq, _ = q.shape
    n_parts = len(k_parts)
    in_specs = [pl.BlockSpec((1, tq, 256), lambda bb, g, i: (bb, i, g))]
    for kp in k_parts:
        in_specs.append(pl.BlockSpec((1, kp.shape[1], HEAD_V, kp.shape[3]), lambda bb, g, i: (bb, 0, g, 0)))
    for vp in v_parts:
        in_specs.append(pl.BlockSpec((1, vp.shape[1], LANES), lambda bb, g, i: (bb, 0, g)))
    return pl.pallas_call(
        functools.partial(_gqa_kernel, n_parts=n_parts, tq=tq),
        out_shape=jax.ShapeDtypeStruct((b, sq, 512), BF16),
        grid=(b, GQA_KV, sq // tq),
        in_specs=in_specs,
        out_specs=pl.BlockSpec((1, tq, 256), lambda bb, g, i: (bb, i, g)),
        scratch_shapes=[pltpu.VMEM((GQA_GROUP * tq, LANES), F32), pltpu.VMEM((GQA_GROUP * tq, LANES), F32)],
        compiler_params=_cparams(("arbitrary", "arbitrary", "arbitrary")),
        name="gqa_attention",
    )(q, *k_parts, *v_parts)


def _diff_kernel(*refs, n_parts, tq, lam_init):
    q_ref = refs[0]
    k_refs = refs[1:1 + n_parts]
    v_refs = refs[1 + n_parts:1 + 2 * n_parts]
    lam_ref, sg_ref, o_ref, m_ref, acc_ref = refs[1 + 2 * n_parts:]
    q = q_ref[0]
    qmaps = [q[:, DIFF_QK * j:DIFF_QK * (j + 1)] for j in range(4)]
    k_slices = [slice(DIFF_QK * j, DIFF_QK * (j + 1)) for j in range(4)]
    pv_groups = [((0, 1), slice(0, LANES)), ((2, 3), slice(LANES, 2 * LANES))]
    _attention_sweeps(qmaps, k_slices, pv_groups, k_refs, v_refs, m_ref, acc_ref)

    lv = lam_ref[...]
    lam = (jnp.exp(jnp.sum(lv[0:1] * lv[1:2], axis=-1, keepdims=True))
           - jnp.exp(jnp.sum(lv[2:3] * lv[3:4], axis=-1, keepdims=True)) + lam_init)
    acc = acc_ref[...]
    for hh in range(2):
        a0 = acc[(2 * hh) * tq:(2 * hh + 1) * tq]
        a1 = acc[(2 * hh + 1) * tq:(2 * hh + 2) * tq]
        o = a0[:, :HEAD_V] / a0[:, HEAD_V:HEAD_V + 1] - lam * (a1[:, :HEAD_V] / a1[:, HEAD_V:HEAD_V + 1])
        ms = jnp.mean(o * o, axis=-1, keepdims=True)
        on = o * lax.rsqrt(ms + EPS) * sg_ref[...] * (1.0 - lam_init)
        o_ref[0, :, HEAD_V * hh:HEAD_V * (hh + 1)] = on.astype(BF16)


def _diff_attention(q, k_parts, v_parts, lam_vecs, subln_g, lam_init, tq):
    b, sq, _ = q.shape
    n_parts = len(k_parts)
    in_specs = [pl.BlockSpec((1, tq, LANES), lambda bb, p, i: (bb, i, p))]
    for kp in k_parts:
        in_specs.append(pl.BlockSpec((1, kp.shape[1], LANES, kp.shape[3]), lambda bb, p, i: (bb, 0, p, 0)))
    for vp in v_parts:
        in_specs.append(pl.BlockSpec((1, vp.shape[1], 2 * LANES), lambda bb, p, i: (bb, 0, p)))
    in_specs.append(pl.BlockSpec((4, DIFF_QK), lambda bb, p, i: (0, 0)))
    in_specs.append(pl.BlockSpec((1, HEAD_V), lambda bb, p, i: (0, 0)))
    return pl.pallas_call(
        functools.partial(_diff_kernel, n_parts=n_parts, tq=tq, lam_init=lam_init),
        out_shape=jax.ShapeDtypeStruct((b, sq, 256), BF16),
        grid=(b, DIFF_HEADS // 2, sq // tq),
        in_specs=in_specs,
        out_specs=pl.BlockSpec((1, tq, LANES), lambda bb, p, i: (bb, i, p)),
        scratch_shapes=[pltpu.VMEM((4 * tq, LANES), F32), pltpu.VMEM((4 * tq, LANES), F32)],
        compiler_params=_cparams(("arbitrary", "arbitrary", "arbitrary")),
        name="diff_attention",
    )(q, *k_parts, *v_parts, lam_vecs, subln_g)


CONV_HALO = 16
CONV_ROWS = 64


def _conv_kernel(gb_ref, prev_ref, next_ref, w_ref, b_ref, lg_ref, lb_ref, o_ref, u_ref, *, tm):
    i = pl.program_id(1)
    last = pl.num_programs(1) - 1
    ch = w_ref.shape[1]

    def glu(z):
        z = z.astype(F32)
        return z[:, :ch] * _sigmoid(z[:, ch:])

    u_ref[CONV_HALO:CONV_HALO + tm, :] = glu(gb_ref[0])
    u_ref[0:CONV_HALO, :] = jnp.where(i > 0, glu(prev_ref[0]), 0.0)
    u_ref[CONV_HALO + tm:2 * CONV_HALO + tm, :] = jnp.where(i < last, glu(next_ref[0]), 0.0)

    off = CONV_HALO - CONV_K // 2
    for r0 in range(0, tm, CONV_ROWS):
        acc = jnp.zeros((CONV_ROWS, ch), F32)
        for j in range(CONV_K):
            acc = acc + u_ref[r0 + off + j:r0 + off + j + CONV_ROWS, :] * w_ref[j:j + 1, :]
        y = acc + b_ref[...]
        mu = jnp.mean(y, axis=-1, keepdims=True)
        yc = y - mu
        var = jnp.mean(yc * yc, axis=-1, keepdims=True)
        z = yc * lax.rsqrt(var + EPS) * lg_ref[...] + lb_ref[...]
        o_ref[0, r0:r0 + CONV_ROWS, :] = (z * _sigmoid(z)).astype(BF16)


def _conformer_conv(gb, w, bias, ln_g, ln_b, tm):
    b, s, two_ch = gb.shape
    ch = two_ch // 2
    hb = tm // CONV_HALO
    n_halo = s // CONV_HALO
    const2 = lambda bb, i: (0, 0)
    return pl.pallas_call(
        functools.partial(_conv_kernel, tm=tm),
        out_shape=jax.ShapeDtypeStruct((b, s, ch), BF16),
        grid=(b, s // tm),
        in_specs=[
            pl.BlockSpec((1, tm, two_ch), lambda bb, i: (bb, i, 0)),
            pl.BlockSpec((1, CONV_HALO, two_ch), lambda bb, i: (bb, jnp.maximum(i * hb - 1, 0), 0)),
            pl.BlockSpec((1, CONV_HALO, two_ch), lambda bb, i: (bb, jnp.minimum((i + 1) * hb, n_halo - 1), 0)),
            pl.BlockSpec((CONV_K, ch), const2), pl.BlockSpec((1, ch), const2),
            pl.BlockSpec((1, ch), const2), pl.BlockSpec((1, ch), const2),
        ],
        out_specs=pl.BlockSpec((1, tm, ch), lambda bb, i: (bb, i, 0)),
        scratch_shapes=[pltpu.VMEM((tm + 2 * CONV_HALO, ch), F32)],
        compiler_params=_cparams(("arbitrary", "arbitrary")),
        name="conformer_conv",
    )(gb, gb, gb, w, bias, ln_g, ln_b)


def _merge_kernel(*refs, with_router):
    oa_ref, ob_ref, oc_ref, x_ref, gate_ref, shift_ref, scale_ref, g2_ref, w_ref = refs[:9]
    if with_router:
        rw_ref, xo_ref, h2_ref, lg_ref = refs[9:]
    else:
        xo_ref, h2_ref = refs[9:]
    tm = x_ref.shape[1]
    wa = oa_ref.shape[2]
    wb = wa + ob_ref.shape[2]
    y = _dot(oa_ref[0], w_ref[0:wa, :]) + _dot(ob_ref[0], w_ref[wa:wb, :]) + _dot(oc_ref[0], w_ref[wb:, :])
    xn = x_ref[0] + gate_ref[0] * y
    xo_ref[0] = xn
    ms = jnp.mean(xn * xn, axis=-1, keepdims=True)
    h2 = xn * lax.rsqrt(ms + EPS) * g2_ref[...] * (1.0 + scale_ref[0]) + shift_ref[0]
    if with_router:
        _store_row_tiles(h2_ref, h2, tm)
        lg_ref[...] = lax.dot_general(rw_ref[...], h2, (((1,), (1,)), ((), ())),
                                      preferred_element_type=F32, precision=lax.Precision.HIGHEST)
    else:
        h2_ref[0] = h2.astype(BF16)


def _merge(oa, ob, oc, x, gate, shift, scale, g2, w_out, router_wt, tm):
    b, s, d = x.shape
    row = lambda bb, i: (bb, i, 0)
    per_b = lambda bb, i: (bb, 0, 0)
    const2 = lambda bb, i: (0, 0)
    nt = s // tm
    in_specs = [
        pl.BlockSpec((1, tm, oa.shape[2]), row), pl.BlockSpec((1, tm, ob.shape[2]), row),
        pl.BlockSpec((1, tm, oc.shape[2]), row), pl.BlockSpec((1, tm, d), row),
        pl.BlockSpec((1, 1, d), per_b), pl.BlockSpec((1, 1, d), per_b), pl.BlockSpec((1, 1, d), per_b),
        pl.BlockSpec((1, d), const2), pl.BlockSpec((d, d), const2),
    ]
    out_shape = [jax.ShapeDtypeStruct((b, s, d), F32)]
    out_specs = [pl.BlockSpec((1, tm, d), row)]
    args = [oa, ob, oc, x, gate, shift, scale, g2, w_out]
    with_router = router_wt is not None
    if not with_router:
        out_shape.append(jax.ShapeDtypeStruct((b, s, d), BF16))
        out_specs.append(pl.BlockSpec((1, tm, d), row))
    else:
        out_shape.append(jax.ShapeDtypeStruct((b * s * ROW_TILE, LANES), F32))
        out_specs.append(pl.BlockSpec((tm * ROW_TILE, LANES), lambda bb, i: (bb * nt + i, 0)))
        in_specs.append(pl.BlockSpec((N_EXPERTS, d), const2))
        out_shape.append(jax.ShapeDtypeStruct((N_EXPERTS, b * s), F32))
        out_specs.append(pl.BlockSpec((N_EXPERTS, tm), lambda bb, i: (0, bb * nt + i)))
        args.append(router_wt)
    return pl.pallas_call(
        functools.partial(_merge_kernel, with_router=with_router),
        out_shape=tuple(out_shape), grid=(b, nt), in_specs=in_specs, out_specs=tuple(out_specs),
        compiler_params=_cparams(("arbitrary", "arbitrary")), name="merge_heads",
    )(*args)


def _ffn_kernel(h_ref, x_ref, gate_ref, wg_ref, wu_ref, wd_ref, o_ref, *, tf):
    h = h_ref[...]
    ff = wg_ref.shape[1]
    acc = jnp.zeros(x_ref.shape, F32)
    for f in range(0, ff, tf):
        g = _dot(h, wg_ref[:, f:f + tf])
        u = _dot(h, wu_ref[:, f:f + tf])
        a = (g * _sigmoid(g) * u).astype(BF16)
        acc = acc + _dot(a, wd_ref[f:f + tf, :])
    o_ref[...] = x_ref[...] + gate_ref[0] * acc


def _dense_ffn(h2, x, gate, wg, wu, wd, tm):
    n, d = x.shape
    ff = wg.shape[1]
    s = n // gate.shape[0]
    row = lambda i: (i, 0)
    const2 = lambda i: (0, 0)
    resident = pl.Buffered(1)
    return pl.pallas_call(
        functools.partial(_ffn_kernel, tf=MXU_DIM),
        out_shape=jax.ShapeDtypeStruct((n, d), F32),
        grid=(n // tm,),
        in_specs=[
            pl.BlockSpec((tm, d), row), pl.BlockSpec((tm, d), row),
            pl.BlockSpec((1, 1, d), lambda i: ((i * tm) // s, 0, 0)),
            pl.BlockSpec((d, ff), const2, pipeline_mode=resident),
            pl.BlockSpec((d, ff), const2, pipeline_mode=resident),
            pl.BlockSpec((ff, d), const2, pipeline_mode=resident),
        ],
        out_specs=pl.BlockSpec((tm, d), row),
        compiler_params=_cparams(("arbitrary",)), name="dense_ffn",
    )(h2, x, gate, wg, wu, wd)


def _top2(lg):
    sub = lax.broadcasted_iota(I32, lg.shape, 0)
    l1 = jnp.max(lg, axis=0, keepdims=True)
    i1 = jnp.min(jnp.where(lg == l1, sub, N_EXPERTS), axis=0, keepdims=True)
    m1 = sub == i1
    lg2 = jnp.where(m1, -jnp.inf, lg)
    l2 = jnp.max(lg2, axis=0, keepdims=True)
    i2 = jnp.min(jnp.where(lg2 == l2, sub, N_EXPERTS), axis=0, keepdims=True)
    m2 = sub == i2
    return l1, l2, m1, m2


def _sublane_cumsum(x):
    sub = lax.broadcasted_iota(I32, x.shape, 0)
    for sh in (1, 2, 4):
        x = x + jnp.where(sub >= sh, pltpu.roll(x, sh, 0), 0.0)
    return x


def _route_kernel(lg_ref, tri_ref, dest_ref, gates_ref, be_ref, base_ref, start_ref, *, block_rows):
    phase = pl.program_id(0)
    j = pl.program_id(1)
    l1, l2, m1, m2 = _top2(lg_ref[...])
    e = jnp.where(m1 | m2, 1.0, 0.0).astype(F32)
    cnt = jnp.sum(e, axis=1, keepdims=True)

    @pl.when((phase == 0) & (j == 0))
    def _():
        base_ref[...] = jnp.zeros(base_ref.shape, F32)

    @pl.when((phase == 1) & (j == 0))
    def _():
        counts = base_ref[...]
        nblk = jnp.floor((counts + (block_rows - 1)) * (1.0 / block_rows))
        end_blk = _sublane_cumsum(nblk)
        start_ref[...] = (end_blk - nblk) * block_rows
        blk = lax.broadcasted_iota(I32, be_ref.shape, 1).astype(F32)
        owner = jnp.sum(jnp.where(end_blk[:, :1] <= blk, 1.0, 0.0), axis=0, keepdims=True)
        be_ref[...] = jnp.broadcast_to(jnp.minimum(owner, N_EXPERTS - 1.0), be_ref.shape).astype(I32)
        base_ref[...] = jnp.zeros(base_ref.shape, F32)

    @pl.when(phase == 1)
    def _():
        prefix = _dot(e.astype(BF16), tri_ref[...]) + base_ref[:, :1] + start_ref[:, :1]
        d1 = jnp.sum(jnp.where(m1, prefix, 0.0), axis=0, keepdims=True)
        d2 = jnp.sum(jnp.where(m2, prefix, 0.0), axis=0, keepdims=True)
        sub = lax.broadcasted_iota(I32, dest_ref.shape, 0)
        dest_ref[...] = jnp.where(sub == 0, d1, jnp.where(sub == 1, d2, 0.0)).astype(I32)
        ex = jnp.exp(l2 - l1)
        g1 = 1.0 / (1.0 + ex)
        g2 = ex / (1.0 + ex)
        half = lax.broadcasted_iota(I32, (LANES, lg_ref.shape[1]), 0) < LANES // 2
        gates_ref[...] = jnp.where(half, g1, g2).T

    base_ref[...] = base_ref[...] + cnt


def _route(logits_t, block_rows, n_blocks_pad, tr):
    n = logits_t.shape[1]
    tri = jnp.asarray(np.triu(np.ones((tr, tr), np.float32), k=1), BF16)
    return pl.pallas_call(
        functools.partial(_route_kernel, block_rows=block_rows),
        out_shape=(jax.ShapeDtypeStruct((N_EXPERTS, n), I32),
                   jax.ShapeDtypeStruct((n, LANES), F32),
                   jax.ShapeDtypeStruct((N_EXPERTS, n_blocks_pad), I32)),
        grid=(2, n // tr),
        in_specs=[pl.BlockSpec((N_EXPERTS, tr), lambda p, j: (0, j)),
                  pl.BlockSpec((tr, tr), lambda p, j: (0, 0))],
        out_specs=(pl.BlockSpec((N_EXPERTS, tr), lambda p, j: (0, j * p)),
                   pl.BlockSpec((tr, LANES), lambda p, j: (j * p, 0)),
                   pl.BlockSpec((N_EXPERTS, n_blocks_pad), lambda p, j: (0, 0))),
        scratch_shapes=[pltpu.VMEM((N_EXPERTS, LANES), F32), pltpu.VMEM((N_EXPERTS, LANES), F32)],
        compiler_params=_cparams(("arbitrary", "arbitrary")), name="moe_route",
    )(logits_t, tri)


def _row_copy(src_hbm, src_row, dst_hbm, dst_row, sem):
    src = pl.ds(pl.multiple_of(src_row * ROW_TILE, ROW_TILE), ROW_TILE)
    dst = pl.ds(pl.multiple_of(dst_row * ROW_TILE, ROW_TILE), ROW_TILE)
    return pltpu.make_async_copy(src_hbm.at[src], dst_hbm.at[dst], sem)


def _scatter_kernel(d1_ref, d2_ref, src_ref, init_hbm, out_hbm, sem, *, rows):
    del init_hbm

    def start(r, c):
        _row_copy(src_ref, r, out_hbm, d1_ref[0, 0, r], sem).start()
        _row_copy(src_ref, r, out_hbm, d2_ref[0, 0, r], sem).start()
        return c

    def wait(r, c):
        _row_copy(src_ref, 0, out_hbm, 0, sem).wait()
        _row_copy(src_ref, 0, out_hbm, 0, sem).wait()
        return c

    lax.fori_loop(0, rows, start, 0)
    lax.fori_loop(0, rows, wait, 0)


def _scatter_rows(src, d1, d2, total_rows, rows):
    n = d1.shape[0]
    idx_spec = pl.BlockSpec((1, 1, rows), lambda i: (i, 0, 0), memory_space=pltpu.SMEM)
    any_spec = pl.BlockSpec(memory_space=pl.ANY)
    return pl.pallas_call(
        functools.partial(_scatter_kernel, rows=rows),
        out_shape=jax.ShapeDtypeStruct((total_rows * ROW_TILE, LANES), src.dtype),
        grid=(n // rows,),
        in_specs=[idx_spec, idx_spec, pl.BlockSpec((rows * ROW_TILE, LANES), lambda i: (i, 0)), any_spec],
        out_specs=any_spec,
        scratch_shapes=[pltpu.SemaphoreType.DMA(())],
        input_output_aliases={3: 0},
        compiler_params=pltpu.CompilerParams(dimension_semantics=("arbitrary",), has_side_effects=True),
        name="moe_scatter_rows",
    )(d1.reshape(n // rows, 1, rows), d2.reshape(n // rows, 1, rows), src,
      jnp.zeros((total_rows * ROW_TILE, LANES), src.dtype))


def _expert_kernel(be_ref, x_ref, wg_ref, wu_ref, wd_ref, o_ref, xs_ref, acc_ref, *, block_rows):
    del be_ref
    f = pl.program_id(1)

    @pl.when(f == 0)
    def _():
        xs_ref[...] = _load_row_tiles(x_ref, block_rows).astype(BF16)
        acc_ref[...] = jnp.zeros(acc_ref.shape, F32)

    x = xs_ref[...]
    g = _dot(x, wg_ref[0])
    u = _dot(x, wu_ref[0])
    a = (g * _sigmoid(g) * u).astype(BF16)
    acc_ref[...] += _dot(a, wd_ref[0])

    @pl.when(f == pl.num_programs(1) - 1)
    def _():
        _store_row_tiles(o_ref, acc_ref[...], block_rows)


def _expert_ffn(xb, blk_expert, wg, wu, wd, block_rows, tf):
    d, ff = wg.shape[1], wg.shape[2]
    rows = xb.shape[0] // ROW_TILE
    grid_spec = pltpu.PrefetchScalarGridSpec(
        num_scalar_prefetch=1,
        grid=(rows // block_rows, ff // tf),
        in_specs=[
            pl.BlockSpec((block_rows * ROW_TILE, LANES), lambda i, f, be: (i, 0)),
            pl.BlockSpec((1, d, tf), lambda i, f, be: (be[i], 0, f)),
            pl.BlockSpec((1, d, tf), lambda i, f, be: (be[i], 0, f)),
            pl.BlockSpec((1, tf, d), lambda i, f, be: (be[i], f, 0)),
        ],
        out_specs=pl.BlockSpec((block_rows * ROW_TILE, LANES), lambda i, f, be: (i, 0)),
        scratch_shapes=[pltpu.VMEM((block_rows, d), BF16), pltpu.VMEM((block_rows, d), F32)],
    )
    return pl.pallas_call(
        functools.partial(_expert_kernel, block_rows=block_rows),
        out_shape=jax.ShapeDtypeStruct(xb.shape, F32), grid_spec=grid_spec,
        compiler_params=_cparams(("arbitrary", "arbitrary")), name="moe_expert_ffn",
    )(blk_expert, xb, wg, wu, wd)


def _combine_kernel(d1_ref, d2_ref, x_ref, yb_hbm, gates_ref, gate_ref, fg_ref, o_ref, y1_ref, y2_ref, sem):
    tm = x_ref.shape[0]

    def start(r, c):
        _row_copy(yb_hbm, d1_ref[0, 0, r], y1_ref, r, sem).start()
        _row_copy(yb_hbm, d2_ref[0, 0, r], y2_ref, r, sem).start()
        return c

    def wait(r, c):
        _row_copy(yb_hbm, 0, y1_ref, 0, sem).wait()
        _row_copy(yb_hbm, 0, y2_ref, 0, sem).wait()
        return c

    lax.fori_loop(0, tm, start, 0)
    lax.fori_loop(0, tm, wait, 0)

    gts = gates_ref[...]
    y = (gts[:, 0:1] * _load_row_tiles(y1_ref, tm)
         + gts[:, LANES // 2:LANES // 2 + 1] * _load_row_tiles(y2_ref, tm))
    xn = x_ref[...] + gate_ref[0] * y
    ms = jnp.mean(xn * xn, axis=-1, keepdims=True)
    o_ref[...] = xn * lax.rsqrt(ms + EPS) * fg_ref[...]


def _combine_final(x, yb, d1, d2, gates, gate, final_g, tm):
    n, d = x.shape
    s = n // gate.shape[0]
    row = lambda i: (i, 0)
    idx_spec = pl.BlockSpec((1, 1, tm), lambda i: (i, 0, 0), memory_space=pltpu.SMEM)
    return pl.pallas_call(
        _combine_kernel,
        out_shape=jax.ShapeDtypeStruct((n, d), F32),
        grid=(n // tm,),
        in_specs=[idx_spec, idx_spec,
                  pl.BlockSpec((tm, d), row),
                  pl.BlockSpec(memory_space=pl.ANY),
                  pl.BlockSpec((tm, LANES), row),
                  pl.BlockSpec((1, 1, d), lambda i: ((i * tm) // s, 0, 0)),
                  pl.BlockSpec((1, d), lambda i: (0, 0))],
        out_specs=pl.BlockSpec((tm, d), row),
        scratch_shapes=[pltpu.VMEM((tm * ROW_TILE, LANES), F32), pltpu.VMEM((tm * ROW_TILE, LANES), F32),
                        pltpu.SemaphoreType.DMA(())],
        compiler_params=_cparams(("arbitrary",)), name="moe_combine_final",
    )(d1.reshape(n // tm, 1, tm), d2.reshape(n // tm, 1, tm), x, yb, gates, gate, final_g)


def _rope_tables(s, dim):
    half = dim // 2
    t = jnp.arange(s)
    inv = 1.0 / (ROPE_THETA ** (jnp.arange(0, half, 2, dtype=F32) / half))
    ang_r = (t // GRID_W).astype(F32)[:, None] * inv
    ang_c = (t % GRID_W).astype(F32)[:, None] * inv
    ang = jnp.concatenate([ang_r, ang_r, ang_c, ang_c], axis=-1)
    reps = LANES // dim
    return jnp.tile(jnp.cos(ang), (1, reps)), jnp.tile(jnp.sin(ang), (1, reps))


def _rotate_matrix(dim):
    q = dim // 4
    p = np.zeros((MXU_DIM, MXU_DIM), np.float32)
    for j in range(MXU_DIM):
        if (j % (2 * q)) < q:
            p[j + q, j] = -1.0
        else:
            p[j - q, j] = 1.0
    return jnp.asarray(p, BF16)


def _head_mean_matrix():
    m = np.kron(np.eye(MXU_DIM // HEAD_V, dtype=np.float32), np.full((HEAD_V, HEAD_V), 1.0 / HEAD_V, np.float32))
    return jnp.asarray(m, BF16)


def _widen_values(w, heads):
    d = w.shape[0]
    w = w.reshape(d, heads, HEAD_V)
    return jnp.concatenate([w, jnp.zeros_like(w)], axis=-1).reshape(d, heads * LANES)


def _widen_in_proj(w):
    qa, ka, va, gb, qc, kc, vc = jnp.split(w, [256, 512, 768, 1280, 1792, 1920], axis=1)
    return jnp.concatenate([qa, ka, _widen_values(va, DIFF_HEADS), gb, qc, kc, _widen_values(vc, GQA_KV)],
                           axis=1).astype(BF16)


def kernel(x, c, ctx, c_ctx, ada_w, ada_b, norm1_g, norm2_g, w_in, w_out, lam_q1, lam_k1, lam_q2, lam_k2,
           diff_subln_g, conv_w, conv_b, conv_ln_g, conv_ln_b, q_norm_g, k_norm_g, ffn_gate, ffn_up, ffn_down,
           router_w, moe_gate, moe_up, moe_down, final_g):
    b, s, d = x.shape
    sc = ctx.shape[1]
    depth = ada_w.shape[0]
    n = b * s
    assert depth % 2 == 0, "the final RMSNorm is fused into the MoE combine of the last (odd) layer"

    tm = min(512, s)
    tmc = min(512, sc)
    tq = min(256, s)
    tqc = min(256, sc)

    tabs_x = _rope_tables(s, DIFF_QK) + _rope_tables(s, HEAD_V)
    ones_c, zeros_c = jnp.ones((sc, LANES), F32), jnp.zeros((sc, LANES), F32)
    tabs_c = (ones_c, zeros_c, ones_c, zeros_c)
    mats = (_rotate_matrix(DIFF_QK), _rotate_matrix(HEAD_V), _head_mean_matrix())

    cc = jnp.zeros((16, d), F32).at[:b].set(c).at[b].set(c_ctx)

    for i in range(depth):
        last = i == depth - 1
        lam_init = 0.8 - 0.6 * math.exp(-0.3 * i)
        mod_all = _ada_mod(cc, ada_w[i], ada_b[i])
        mod = mod_all[:b].reshape(b, 6, 1, d)
        modc = jnp.broadcast_to(mod_all[b].reshape(1, 6, 1, d), (b, 6, 1, d))

        w_aug = _widen_in_proj(w_in[i])
        g1 = norm1_g[i].reshape(1, d)
        qg = jnp.tile(q_norm_g[i], GQA_HEADS).reshape(1, -1)
        kg = jnp.tile(k_norm_g[i], GQA_KV).reshape(1, -1)
        lam_vecs = jnp.stack([lam_q1[i], lam_k1[i], lam_q2[i], lam_k2[i]]).astype(F32)
        subln = diff_subln_g[i].reshape(1, HEAD_V)
        conv_args = (conv_w[i], conv_b[i].reshape(1, -1), conv_ln_g[i].reshape(1, -1), conv_ln_b[i].reshape(1, -1))
        w_o = w_out[i].astype(BF16)
        g2 = norm2_g[i].reshape(1, d)

        qa, kat, va, gb, qc, kct, vc = _in_projection(x, mod[:, 0], mod[:, 1], g1, w_aug, tabs_x, mats, qg, kg, tm)
        qa_x, kat_x, va_x, gb_x, qc_x, kct_x, vc_x = _in_projection(
            ctx, modc[:, 0], modc[:, 1], g1, w_aug, tabs_c, mats, qg, kg, tmc)

        oa = _diff_attention(qa, [kat, kat_x], [va, va_x], lam_vecs, subln, lam_init, min(2 * tq, s))
        ob = _conformer_conv(gb, *conv_args, tm)
        oc = _gqa_attention(qc, [kct, kct_x], [vc, vc_x], tq)

        j = i // 2
        if i % 2 == 0:
            x, h2 = _merge(oa, ob, oc, x, mod[:, 2], mod[:, 3], mod[:, 4], g2, w_o, None, tm)
            wg, wu, wd = ffn_gate[j].astype(BF16), ffn_up[j].astype(BF16), ffn_down[j].astype(BF16)
            x = _dense_ffn(h2.reshape(n, d), x.reshape(n, d), mod[:, 5], wg, wu, wd, tm).reshape(b, s, d)
        else:
            rwt = router_w[j].T.astype(F32)
            x, h2, logits_t = _merge(oa, ob, oc, x, mod[:, 2], mod[:, 3], mod[:, 4], g2, w_o, rwt, tm)
            block_rows = 1024 if n >= 8192 else 256
            n_blocks = (2 * n) // block_rows + N_EXPERTS
            n_blocks_pad = -(-n_blocks // LANES) * LANES
            dest, gates, blk_e = _route(logits_t, block_rows, n_blocks_pad, min(512, n))
            xb = _scatter_rows(h2, dest[0], dest[1], n_blocks * block_rows, min(256, n))
            yb = _expert_ffn(xb, blk_e[0, :n_blocks], moe_gate[j].astype(BF16), moe_up[j].astype(BF16),
                             moe_down[j].astype(BF16), block_rows, 512)
            assert last
            x = _combine_final(x.reshape(n, d), yb, dest[0], dest[1], gates, mod[:, 5], final_g.reshape(1, d),
                               tm).reshape(b, s, d)

        if not last:
            oa_x = _diff_attention(qa_x, [kat_x], [va_x], lam_vecs, subln, lam_init, tqc)
            ob_x = _conformer_conv(gb_x, *conv_args, tmc)
            oc_x = _gqa_attention(qc_x, [kct_x], [vc_x], tqc)
            assert i % 2 == 0, "context tokens only ever pass through dense channel mixers"
            ctx, hc2 = _merge(oa_x, ob_x, oc_x, ctx, modc[:, 2], modc[:, 3], modc[:, 4], g2, w_o, None, tmc)
            ctx = _dense_ffn(hc2.reshape(b * sc, d), ctx.reshape(b * sc, d), modc[:, 5], wg, wu, wd,
                             tmc).reshape(b, sc, d)

    return x
```

```python
import functools
import math

import numpy as np
import jax
import jax.numpy as jnp
from jax import lax
from jax.experimental import pallas as pl
from jax.experimental.pallas import tpu as pltpu

F32 = jnp.float32
BF16 = jnp.bfloat16
I32 = jnp.int32

EPS = 1e-6
ROPE_THETA = 10000.0
GRID_W = 64

DIFF_HEADS = 4
DIFF_QK = 32
HEAD_V = 64
GQA_HEADS = 8
GQA_KV = 2
GQA_GROUP = GQA_HEADS // GQA_KV
CONV_K = 31
N_EXPERTS = 8
LOG2E = math.log2(math.e)

LANES = 128
MXU_DIM = 256
VMEM_LIMIT = 52 * 1024 * 1024
NEG_BIG = -1e30
ATTN_UNROLL = 4

C_QA, C_KA, C_VA, C_GB, C_QC, C_KC, C_VC, C_END = 0, 256, 512, 1024, 1536, 2048, 2176, 2432


def _cparams(semantics):
    return pltpu.CompilerParams(dimension_semantics=semantics, vmem_limit_bytes=VMEM_LIMIT)


def _dot(a, b):
    return jnp.dot(a, b, preferred_element_type=F32)


def _sigmoid(z):
    return 1.0 / (1.0 + jnp.exp(-z))


ROW_TILE = 8


def _store_row_tiles(ref, val, rows):
    for a in range(ROW_TILE):
        ref[pl.ds(a, rows, stride=ROW_TILE), :] = val[:, a * LANES:(a + 1) * LANES]


def _load_row_tiles(ref, rows):
    return jnp.concatenate([ref[pl.ds(a, rows, stride=ROW_TILE), :] for a in range(ROW_TILE)], axis=1)


def _mod_kernel(c_ref, w_ref, b_ref, o_ref):
    c = c_ref[...]
    s = c * _sigmoid(c)
    o_ref[...] = jnp.dot(s, w_ref[...], preferred_element_type=F32, precision=lax.Precision.HIGHEST) + b_ref[...]


def _ada_mod(cc, w, b):
    rows, d = cc.shape
    n = w.shape[1]
    tn = d
    return pl.pallas_call(
        _mod_kernel,
        out_shape=jax.ShapeDtypeStruct((rows, n), F32),
        grid=(n // tn,),
        in_specs=[pl.BlockSpec((rows, d), lambda j: (0, 0)),
                  pl.BlockSpec((d, tn), lambda j: (0, j)),
                  pl.BlockSpec((1, tn), lambda j: (0, j))],
        out_specs=pl.BlockSpec((rows, tn), lambda j: (0, j)),
        compiler_params=_cparams(("arbitrary",)),
        name="ada_mod",
    )(cc, w, b.reshape(1, n))


def _inproj_kernel(x_ref, shift_ref, scale_ref, g_ref, w_ref, cosa_ref, sina_ref, cosc_ref, sinc_ref,
                   pa_ref, pc_ref, hm_ref, qg_ref, kg_ref,
                   qa_o, kat_o, va_o, gb_o, qc_o, kct_o, vc_o, *, qa_scale, qc_scale):
    x = x_ref[0]
    ms = jnp.mean(x * x, axis=-1, keepdims=True)
    h = x * lax.rsqrt(ms + EPS) * g_ref[...]
    h = h * (1.0 + scale_ref[0]) + shift_ref[0]
    hb = h.astype(BF16)

    def proj(lo, hi):
        return _dot(hb, w_ref[:, lo:hi])

    def blockmat(y, m_ref):
        yb = y.astype(BF16)
        w = y.shape[1]
        if w == LANES:
            return _dot(yb, m_ref[:LANES, :LANES])
        return jnp.concatenate([_dot(yb[:, c:c + MXU_DIM], m_ref[...]) for c in range(0, w, MXU_DIM)], axis=1)

    def rope(y, cos, sin, p_ref):
        reps = y.shape[1] // LANES
        cos = jnp.tile(cos, (1, reps))
        sin = jnp.tile(sin, (1, reps))
        return y * cos + blockmat(y, p_ref) * sin

    def ones_col(width):
        lane = lax.broadcasted_iota(I32, (1, width), 1)
        return jnp.where(lane % LANES == HEAD_V, 1.0, 0.0).astype(F32)

    cosa, sina, cosc, sinc = cosa_ref[...], sina_ref[...], cosc_ref[...], sinc_ref[...]

    qa_o[0] = (rope(proj(C_QA, C_KA), cosa, sina, pa_ref) * qa_scale).astype(BF16)
    kat_o[0, 0] = rope(proj(C_KA, C_VA), cosa, sina, pa_ref).T.astype(BF16)
    va_o[0] = (proj(C_VA, C_GB) + ones_col(C_GB - C_VA)).astype(BF16)
    gb_o[0] = proj(C_GB, C_QC).astype(BF16)

    y = proj(C_QC, C_KC)
    yn = y * lax.rsqrt(blockmat(y * y, hm_ref) + EPS) * qg_ref[...]
    qc_o[0] = (rope(yn, cosc, sinc, pc_ref) * qc_scale).astype(BF16)

    y = proj(C_KC, C_VC)
    yn = y * lax.rsqrt(blockmat(y * y, hm_ref) + EPS) * kg_ref[...]
    kct_o[0, 0] = rope(yn, cosc, sinc, pc_ref).T.astype(BF16)

    vc_o[0] = (proj(C_VC, C_END) + ones_col(C_END - C_VC)).astype(BF16)


def _in_projection(x, shift, scale, g1, w_aug, tabs, mats, qg, kg, tm):
    b, s, d = x.shape
    nt = s // tm
    cosa, sina, cosc, sinc = tabs
    pa, pc, hm = mats
    row = lambda bb, i: (bb, i, 0)
    const2 = lambda bb, i: (0, 0)
    per_b = lambda bb, i: (bb, 0, 0)
    tab = lambda bb, i: (i, 0)
    kern = functools.partial(_inproj_kernel, qa_scale=DIFF_QK ** -0.5 * LOG2E, qc_scale=HEAD_V ** -0.5 * LOG2E)
    out_shape = (
        jax.ShapeDtypeStruct((b, s, 256), BF16),
        jax.ShapeDtypeStruct((b, nt, 256, tm), BF16),
        jax.ShapeDtypeStruct((b, s, 512), BF16),
        jax.ShapeDtypeStruct((b, s, 512), BF16),
        jax.ShapeDtypeStruct((b, s, 512), BF16),
        jax.ShapeDtypeStruct((b, nt, 128, tm), BF16),
        jax.ShapeDtypeStruct((b, s, 256), BF16),
    )
    out_specs = (
        pl.BlockSpec((1, tm, 256), row),
        pl.BlockSpec((1, 1, 256, tm), lambda bb, i: (bb, i, 0, 0)),
        pl.BlockSpec((1, tm, 512), row),
        pl.BlockSpec((1, tm, 512), row),
        pl.BlockSpec((1, tm, 512), row),
        pl.BlockSpec((1, 1, 128, tm), lambda bb, i: (bb, i, 0, 0)),
        pl.BlockSpec((1, tm, 256), row),
    )
    in_specs = [
        pl.BlockSpec((1, tm, d), row),
        pl.BlockSpec((1, 1, d), per_b),
        pl.BlockSpec((1, 1, d), per_b),
        pl.BlockSpec((1, d), const2),
        pl.BlockSpec((d, C_END), const2),
        pl.BlockSpec((tm, LANES), tab), pl.BlockSpec((tm, LANES), tab),
        pl.BlockSpec((tm, LANES), tab), pl.BlockSpec((tm, LANES), tab),
        pl.BlockSpec((MXU_DIM, MXU_DIM), const2), pl.BlockSpec((MXU_DIM, MXU_DIM), const2),
        pl.BlockSpec((MXU_DIM, MXU_DIM), const2),
        pl.BlockSpec((1, 512), const2), pl.BlockSpec((1, 128), const2),
    ]
    return pl.pallas_call(
        kern, out_shape=out_shape, grid=(b, nt), in_specs=in_specs, out_specs=out_specs,
        compiler_params=_cparams(("arbitrary", "arbitrary")), name="in_projection",
    )(x, shift, scale, g1, w_aug, cosa, sina, cosc, sinc, pa, pc, hm, qg, kg)


def _attention_sweeps(qms, k_slices, pv_groups, k_refs, v_refs, m_ref, acc_ref):
    r = qms[0].shape[0]

    def scores(j, kc):
        return _dot(qms[j], kc[k_slices[j], :])

    def sweep(running_max):
        acc_ref[...] = jnp.zeros(acc_ref.shape, F32)
        if running_max:
            m_ref[...] = jnp.full(m_ref.shape, NEG_BIG, F32)
        else:
            kc0 = k_refs[0][0, 0]
            for j in range(len(qms)):
                m0 = jnp.max(scores(j, kc0), axis=-1, keepdims=True)
                m_ref[j * r:(j + 1) * r, :] = jnp.broadcast_to(m0, (r, LANES))

        for k_ref, v_ref in zip(k_refs, v_refs):
            n_chunks, tk = k_ref.shape[1], k_ref.shape[3]

            def body(c, carry, k_ref=k_ref, v_ref=v_ref, tk=tk):
                kc = k_ref[0, c]
                vc = v_ref[0, pl.ds(pl.multiple_of(c * tk, tk), tk), :]
                for ids, v_lanes in pv_groups:
                    ps, alphas = [], []
                    for j in ids:
                        rows = slice(j * r, (j + 1) * r)
                        s = scores(j, kc)
                        m = m_ref[rows, :]
                        if running_max:
                            m_new = jnp.maximum(m, jnp.max(s, axis=-1, keepdims=True))
                            m_ref[rows, :] = m_new
                            alphas.append(jnp.exp2(m - m_new))
                            m = m_new
                        ps.append(jnp.exp2(s - jnp.tile(m, (1, tk // LANES))).astype(BF16))
                    rows = slice(ids[0] * r, (ids[-1] + 1) * r)
                    pv = _dot(jnp.concatenate(ps, axis=0), vc[:, v_lanes])
                    if running_max:
                        acc_ref[rows, :] = acc_ref[rows, :] * jnp.concatenate(alphas, axis=0) + pv
                    else:
                        acc_ref[rows, :] += pv
                return carry

            unroll = 1 if running_max else math.gcd(n_chunks, ATTN_UNROLL)
            lax.fori_loop(0, n_chunks, body, 0, unroll=unroll)

    sweep(False)
    not_finite = jnp.sum(acc_ref[...] * 0.0)

    @pl.when(not_finite != 0.0)
    def _():
        sweep(True)


def _gqa_kernel(*refs, n_parts, tq):
    q_ref = refs[0]
    k_refs = refs[1:1 + n_parts]
    v_refs = refs[1 + n_parts:1 + 2 * n_parts]
    o_ref, m_ref, acc_ref = refs[1 + 2 * n_parts:]
    q = q_ref[0]
    qs = jnp.concatenate([q[:, HEAD_V * j:HEAD_V * (j + 1)] for j in range(GQA_GROUP)], axis=0)
    _attention_sweeps([qs], [slice(0, HEAD_V)], [((0,), slice(0, LANES))], k_refs, v_refs, m_ref, acc_ref)
    acc = acc_ref[...]
    o = acc[:, :HEAD_V] / acc[:, HEAD_V:HEAD_V + 1]
    for j in range(GQA_GROUP):
        o_ref[0, :, HEAD_V * j:HEAD_V * (j + 1)] = o[j * tq:(j + 1) * tq].astype(BF16)


def _gqa_attention(q, k_parts, v_parts, tq):
    b, sq, _ = q.shape
    n_parts = len(k_parts)
    in_specs = [pl.BlockSpec((1, tq, 256), lambda bb, g, i: (bb, i, g))]
    for kp in k_parts:
        in_specs.append(pl.BlockSpec((1, kp.shape[1], HEAD_V, kp.shape[3]), lambda bb, g, i: (bb, 0, g, 0)))
    for vp in v_parts:
        in_specs.append(pl.BlockSpec((1, vp.shape[1], LANES), lambda bb, g, i: (bb, 0, g)))
    return pl.pallas_call(
        functools.partial(_gqa_kernel, n_parts=n_parts, tq=tq),
        out_shape=jax.ShapeDtypeStruct((b, sq, 512), BF16),
        grid=(b, GQA_KV, sq // tq),
        in_specs=in_specs,
        out_specs=pl.BlockSpec((1, tq, 256), lambda bb, g, i: (bb, i, g)),
        scratch_shapes=[pltpu.VMEM((GQA_GROUP * tq, LANES), F32), pltpu.VMEM((GQA_GROUP * tq, LANES), F32)],
        compiler_params=_cparams(("arbitrary", "arbitrary", "arbitrary")),
        name="gqa_attention",
    )(q, *k_parts, *v_parts)


def _diff_kernel(*refs, n_parts, tq, lam_init):
    q_ref = refs[0]
    k_refs = refs[1:1 + n_parts]
    v_refs = refs[1 + n_parts:1 + 2 * n_parts]
    lam_ref, sg_ref, o_ref, m_ref, acc_ref = refs[1 + 2 * n_parts:]
    q = q_ref[0]
    qmaps = [q[:, DIFF_QK * j:DIFF_QK * (j + 1)] for j in range(4)]
    k_slices = [slice(DIFF_QK * j, DIFF_QK * (j + 1)) for j in range(4)]
    pv_groups = [((0, 1), slice(0, LANES)), ((2, 3), slice(LANES, 2 * LANES))]
    _attention_sweeps(qmaps, k_slices, pv_groups, k_refs, v_refs, m_ref, acc_ref)

    lv = lam_ref[...]
    lam = (jnp.exp(jnp.sum(lv[0:1] * lv[1:2], axis=-1, keepdims=True))
           - jnp.exp(jnp.sum(lv[2:3] * lv[3:4], axis=-1, keepdims=True)) + lam_init)
    acc = acc_ref[...]
    for hh in range(2):
        a0 = acc[(2 * hh) * tq:(2 * hh + 1) * tq]
        a1 = acc[(2 * hh + 1) * tq:(2 * hh + 2) * tq]
        o = a0[:, :HEAD_V] / a0[:, HEAD_V:HEAD_V + 1] - lam * (a1[:, :HEAD_V] / a1[:, HEAD_V:HEAD_V + 1])
        ms = jnp.mean(o * o, axis=-1, keepdims=True)
        on = o * lax.rsqrt(ms + EPS) * sg_ref[...] * (1.0 - lam_init)
        o_ref[0, :, HEAD_V * hh:HEAD_V * (hh + 1)] = on.astype(BF16)


def _diff_attention(q, k_parts, v_parts, lam_vecs, subln_g, lam_init, tq):
    b, sq, _ = q.shape
    n_parts = len(k_parts)
    in_specs = [pl.BlockSpec((1, tq, LANES), lambda bb, p, i: (bb, i, p))]
    for kp in k_parts:
        in_specs.append(pl.BlockSpec((1, kp.shape[1], LANES, kp.shape[3]), lambda bb, p, i: (bb, 0, p, 0)))
    for vp in v_parts:
        in_specs.append(pl.BlockSpec((1, vp.shape[1], 2 * LANES), lambda bb, p, i: (bb, 0, p)))
    in_specs.append(pl.BlockSpec((4, DIFF_QK), lambda bb, p, i: (0, 0)))
    in_specs.append(pl.BlockSpec((1, HEAD_V), lambda bb, p, i: (0, 0)))
    return pl.pallas_call(
        functools.partial(_diff_kernel, n_parts=n_parts, tq=tq, lam_init=lam_init),
        out_shape=jax.ShapeDtypeStruct((b, sq, 256), BF16),
        grid=(b, DIFF_HEADS // 2, sq // tq),
        in_specs=in_specs,
        out_specs=pl.BlockSpec((1, tq, LANES), lambda bb, p, i: (bb, i, p)),
        scratch_shapes=[pltpu.VMEM((4 * tq, LANES), F32), pltpu.VMEM((4 * tq, LANES), F32)],
        compiler_params=_cparams(("arbitrary", "arbitrary", "arbitrary")),
        name="diff_attention",
    )(q, *k_parts, *v_parts, lam_vecs, subln_g)


CONV_HALO = 16
CONV_ROWS = 64


def _conv_kernel(gb_ref, prev_ref, next_ref, w_ref, b_ref, lg_ref, lb_ref, o_ref, u_ref, *, tm):
    i = pl.program_id(1)
    last = pl.num_programs(1) - 1
    ch = w_ref.shape[1]

    def glu(z):
        z = z.astype(F32)
        return z[:, :ch] * _sigmoid(z[:, ch:])

    u_ref[CONV_HALO:CONV_HALO + tm, :] = glu(gb_ref[0])
    u_ref[0:CONV_HALO, :] = jnp.where(i > 0, glu(prev_ref[0]), 0.0)
    u_ref[CONV_HALO + tm:2 * CONV_HALO + tm, :] = jnp.where(i < last, glu(next_ref[0]), 0.0)

    off = CONV_HALO - CONV_K // 2
    for r0 in range(0, tm, CONV_ROWS):
        acc = jnp.zeros((CONV_ROWS, ch), F32)
        for j in range(CONV_K):
            acc = acc + u_ref[r0 + off + j:r0 + off + j + CONV_ROWS, :] * w_ref[j:j + 1, :]
        y = acc + b_ref[...]
        mu = jnp.mean(y, axis=-1, keepdims=True)
        yc = y - mu
        var = jnp.mean(yc * yc, axis=-1, keepdims=True)
        z = yc * lax.rsqrt(var + EPS) * lg_ref[...] + lb_ref[...]
        o_ref[0, r0:r0 + CONV_ROWS, :] = (z * _sigmoid(z)).astype(BF16)


def _conformer_conv(gb, w, bias, ln_g, ln_b, tm):
    b, s, two_ch = gb.shape
    ch = two_ch // 2
    hb = tm // CONV_HALO
    n_halo = s // CONV_HALO
    const2 = lambda bb, i: (0, 0)
    return pl.pallas_call(
        functools.partial(_conv_kernel, tm=tm),
        out_shape=jax.ShapeDtypeStruct((b, s, ch), BF16),
        grid=(b, s // tm),
        in_specs=[
            pl.BlockSpec((1, tm, two_ch), lambda bb, i: (bb, i, 0)),
            pl.BlockSpec((1, CONV_HALO, two_ch), lambda bb, i: (bb, jnp.maximum(i * hb - 1, 0), 0)),
            pl.BlockSpec((1, CONV_HALO, two_ch), lambda bb, i: (bb, jnp.minimum((i + 1) * hb, n_halo - 1), 0)),
            pl.BlockSpec((CONV_K, ch), const2), pl.BlockSpec((1, ch), const2),
            pl.BlockSpec((1, ch), const2), pl.BlockSpec((1, ch), const2),
        ],
        out_specs=pl.BlockSpec((1, tm, ch), lambda bb, i: (bb, i, 0)),
        scratch_shapes=[pltpu.VMEM((tm + 2 * CONV_HALO, ch), F32)],
        compiler_params=_cparams(("arbitrary", "arbitrary")),
        name="conformer_conv",
    )(gb, gb, gb, w, bias, ln_g, ln_b)


def _merge_kernel(*refs, with_router):
    oa_ref, ob_ref, oc_ref, x_ref, gate_ref, shift_ref, scale_ref, g2_ref, w_ref = refs[:9]
    if with_router:
        rw_ref, xo_ref, h2_ref, lg_ref = refs[9:]
    else:
        xo_ref, h2_ref = refs[9:]
    tm = x_ref.shape[1]
    wa = oa_ref.shape[2]
    wb = wa + ob_ref.shape[2]
    y = _dot(oa_ref[0], w_ref[0:wa, :]) + _dot(ob_ref[0], w_ref[wa:wb, :]) + _dot(oc_ref[0], w_ref[wb:, :])
    xn = x_ref[0] + gate_ref[0] * y
    xo_ref[0] = xn
    ms = jnp.mean(xn * xn, axis=-1, keepdims=True)
    h2 = xn * lax.rsqrt(ms + EPS) * g2_ref[...] * (1.0 + scale_ref[0]) + shift_ref[0]
    if with_router:
        _store_row_tiles(h2_ref, h2, tm)
        lg_ref[...] = lax.dot_general(rw_ref[...], h2, (((1,), (1,)), ((), ())),
                                      preferred_element_type=F32, precision=lax.Precision.HIGHEST)
    else:
        h2_ref[0] = h2.astype(BF16)


def _merge(oa, ob, oc, x, gate, shift, scale, g2, w_out, router_wt, tm):
    b, s, d = x.shape
    row = lambda bb, i: (bb, i, 0)
    per_b = lambda bb, i: (bb, 0, 0)
    const2 = lambda bb, i: (0, 0)
    nt = s // tm
    in_specs = [
        pl.BlockSpec((1, tm, oa.shape[2]), row), pl.BlockSpec((1, tm, ob.shape[2]), row),
        pl.BlockSpec((1, tm, oc.shape[2]), row), pl.BlockSpec((1, tm, d), row),
        pl.BlockSpec((1, 1, d), per_b), pl.BlockSpec((1, 1, d), per_b), pl.BlockSpec((1, 1, d), per_b),
        pl.BlockSpec((1, d), const2), pl.BlockSpec((d, d), const2),
    ]
    out_shape = [jax.ShapeDtypeStruct((b, s, d), F32)]
    out_specs = [pl.BlockSpec((1, tm, d), row)]
    args = [oa, ob, oc, x, gate, shift, scale, g2, w_out]
    with_router = router_wt is not None
    if not with_router:
        out_shape.append(jax.ShapeDtypeStruct((b, s, d), BF16))
        out_specs.append(pl.BlockSpec((1, tm, d), row))
    else:
        out_shape.append(jax.ShapeDtypeStruct((b * s * ROW_TILE, LANES), F32))
        out_specs.append(pl.BlockSpec((tm * ROW_TILE, LANES), lambda bb, i: (bb * nt + i, 0)))
        in_specs.append(pl.BlockSpec((N_EXPERTS, d), const2))
        out_shape.append(jax.ShapeDtypeStruct((N_EXPERTS, b * s), F32))
        out_specs.append(pl.BlockSpec((N_EXPERTS, tm), lambda bb, i: (0, bb * nt + i)))
        args.append(router_wt)
    return pl.pallas_call(
        functools.partial(_merge_kernel, with_router=with_router),
        out_shape=tuple(out_shape), grid=(b, nt), in_specs=in_specs, out_specs=tuple(out_specs),
        compiler_params=_cparams(("arbitrary", "arbitrary")), name="merge_heads",
    )(*args)


def _ffn_kernel(h_ref, x_ref, gate_ref, wg_ref, wu_ref, wd_ref, o_ref, *, tf):
    h = h_ref[...]
    ff = wg_ref.shape[1]
    acc = jnp.zeros(x_ref.shape, F32)
    for f in range(0, ff, tf):
        g = _dot(h, wg_ref[:, f:f + tf])
        u = _dot(h, wu_ref[:, f:f + tf])
        a = (g * _sigmoid(g) * u).astype(BF16)
        acc = acc + _dot(a, wd_ref[f:f + tf, :])
    o_ref[...] = x_ref[...] + gate_ref[0] * acc


def _dense_ffn(h2, x, gate, wg, wu, wd, tm):
    n, d = x.shape
    ff = wg.shape[1]
    s = n // gate.shape[0]
    row = lambda i: (i, 0)
    const2 = lambda i: (0, 0)
    resident = pl.Buffered(1)
    return pl.pallas_call(
        functools.partial(_ffn_kernel, tf=MXU_DIM),
        out_shape=jax.ShapeDtypeStruct((n, d), F32),
        grid=(n // tm,),
        in_specs=[
            pl.BlockSpec((tm, d), row), pl.BlockSpec((tm, d), row),
            pl.BlockSpec((1, 1, d), lambda i: ((i * tm) // s, 0, 0)),
            pl.BlockSpec((d, ff), const2, pipeline_mode=resident),
            pl.BlockSpec((d, ff), const2, pipeline_mode=resident),
            pl.BlockSpec((ff, d), const2, pipeline_mode=resident),
        ],
        out_specs=pl.BlockSpec((tm, d), row),
        compiler_params=_cparams(("arbitrary",)), name="dense_ffn",
    )(h2, x, gate, wg, wu, wd)


def _top2(lg):
    sub = lax.broadcasted_iota(I32, lg.shape, 0)
    l1 = jnp.max(lg, axis=0, keepdims=True)
    i1 = jnp.min(jnp.where(lg == l1, sub, N_EXPERTS), axis=0, keepdims=True)
    m1 = sub == i1
    lg2 = jnp.where(m1, -jnp.inf, lg)
    l2 = jnp.max(lg2, axis=0, keepdims=True)
    i2 = jnp.min(jnp.where(lg2 == l2, sub, N_EXPERTS), axis=0, keepdims=True)
    m2 = sub == i2
    return l1, l2, m1, m2


def _sublane_cumsum(x):
    sub = lax.broadcasted_iota(I32, x.shape, 0)
    for sh in (1, 2, 4):
        x = x + jnp.where(sub >= sh, pltpu.roll(x, sh, 0), 0.0)
    return x


def _route_kernel(lg_ref, tri_ref, dest_ref, gates_ref, be_ref, base_ref, start_ref, *, block_rows):
    phase = pl.program_id(0)
    j = pl.program_id(1)
    l1, l2, m1, m2 = _top2(lg_ref[...])
    e = jnp.where(m1 | m2, 1.0, 0.0).astype(F32)
    cnt = jnp.sum(e, axis=1, keepdims=True)

    @pl.when((phase == 0) & (j == 0))
    def _():
        base_ref[...] = jnp.zeros(base_ref.shape, F32)

    @pl.when((phase == 1) & (j == 0))
    def _():
        counts = base_ref[...]
        nblk = jnp.floor((counts + (block_rows - 1)) * (1.0 / block_rows))
        end_blk = _sublane_cumsum(nblk)
        start_ref[...] = (end_blk - nblk) * block_rows
        blk = lax.broadcasted_iota(I32, be_ref.shape, 1).astype(F32)
        owner = jnp.sum(jnp.where(end_blk[:, :1] <= blk, 1.0, 0.0), axis=0, keepdims=True)
        be_ref[...] = jnp.broadcast_to(jnp.minimum(owner, N_EXPERTS - 1.0), be_ref.shape).astype(I32)
        base_ref[...] = jnp.zeros(base_ref.shape, F32)

    @pl.when(phase == 1)
    def _():
        prefix = _dot(e.astype(BF16), tri_ref[...]) + base_ref[:, :1] + start_ref[:, :1]
        d1 = jnp.sum(jnp.where(m1, prefix, 0.0), axis=0, keepdims=True)
        d2 = jnp.sum(jnp.where(m2, prefix, 0.0), axis=0, keepdims=True)
        sub = lax.broadcasted_iota(I32, dest_ref.shape, 0)
        dest_ref[...] = jnp.where(sub == 0, d1, jnp.where(sub == 1, d2, 0.0)).astype(I32)
        ex = jnp.exp(l2 - l1)
        g1 = 1.0 / (1.0 + ex)
        g2 = ex / (1.0 + ex)
        half = lax.broadcasted_iota(I32, (LANES, lg_ref.shape[1]), 0) < LANES // 2
        gates_ref[...] = jnp.where(half, g1, g2).T

    base_ref[...] = base_ref[...] + cnt


def _route(logits_t, block_rows, n_blocks_pad, tr):
    n = logits_t.shape[1]
    tri = jnp.asarray(np.triu(np.ones((tr, tr), np.float32), k=1), BF16)
    return pl.pallas_call(
        functools.partial(_route_kernel, block_rows=block_rows),
        out_shape=(jax.ShapeDtypeStruct((N_EXPERTS, n), I32),
                   jax.ShapeDtypeStruct((n, LANES), F32),
                   jax.ShapeDtypeStruct((N_EXPERTS, n_blocks_pad), I32)),
        grid=(2, n // tr),
        in_specs=[pl.BlockSpec((N_EXPERTS, tr), lambda p, j: (0, j)),
                  pl.BlockSpec((tr, tr), lambda p, j: (0, 0))],
        out_specs=(pl.BlockSpec((N_EXPERTS, tr), lambda p, j: (0, j * p)),
                   pl.BlockSpec((tr, LANES), lambda p, j: (j * p, 0)),
                   pl.BlockSpec((N_EXPERTS, n_blocks_pad), lambda p, j: (0, 0))),
        scratch_shapes=[pltpu.VMEM((N_EXPERTS, LANES), F32), pltpu.VMEM((N_EXPERTS, LANES), F32)],
        compiler_params=_cparams(("arbitrary", "arbitrary")), name="moe_route",
    )(logits_t, tri)


def _row_copy(src_hbm, src_row, dst_hbm, dst_row, sem):
    src = pl.ds(pl.multiple_of(src_row * ROW_TILE, ROW_TILE), ROW_TILE)
    dst = pl.ds(pl.multiple_of(dst_row * ROW_TILE, ROW_TILE), ROW_TILE)
    return pltpu.make_async_copy(src_hbm.at[src], dst_hbm.at[dst], sem)


def _scatter_kernel(d1_ref, d2_ref, src_ref, init_hbm, out_hbm, sem, *, rows):
    del init_hbm

    def start(r, c):
        _row_copy(src_ref, r, out_hbm, d1_ref[0, 0, r], sem).start()
        _row_copy(src_ref, r, out_hbm, d2_ref[0, 0, r], sem).start()
        return c

    def wait(r, c):
        _row_copy(src_ref, 0, out_hbm, 0, sem).wait()
        _row_copy(src_ref, 0, out_hbm, 0, sem).wait()
        return c

    lax.fori_loop(0, rows, start, 0)
    lax.fori_loop(0, rows, wait, 0)


def _scatter_rows(src, d1, d2, total_rows, rows):
    n = d1.shape[0]
    idx_spec = pl.BlockSpec((1, 1, rows), lambda i: (i, 0, 0), memory_space=pltpu.SMEM)
    any_spec = pl.BlockSpec(memory_space=pl.ANY)
    return pl.pallas_call(
        functools.partial(_scatter_kernel, rows=rows),
        out_shape=jax.ShapeDtypeStruct((total_rows * ROW_TILE, LANES), src.dtype),
        grid=(n // rows,),
        in_specs=[idx_spec, idx_spec, pl.BlockSpec((rows * ROW_TILE, LANES), lambda i: (i, 0)), any_spec],
        out_specs=any_spec,
        scratch_shapes=[pltpu.SemaphoreType.DMA(())],
        input_output_aliases={3: 0},
        compiler_params=pltpu.CompilerParams(dimension_semantics=("arbitrary",), has_side_effects=True),
        name="moe_scatter_rows",
    )(d1.reshape(n // rows, 1, rows), d2.reshape(n // rows, 1, rows), src,
      jnp.zeros((total_rows * ROW_TILE, LANES), src.dtype))


def _expert_kernel(be_ref, x_ref, wg_ref, wu_ref, wd_ref, o_ref, xs_ref, acc_ref, *, block_rows):
    del be_ref
    f = pl.program_id(1)

    @pl.when(f == 0)
    def _():
        xs_ref[...] = _load_row_tiles(x_ref, block_rows).astype(BF16)
        acc_ref[...] = jnp.zeros(acc_ref.shape, F32)

    x = xs_ref[...]
    g = _dot(x, wg_ref[0])
    u = _dot(x, wu_ref[0])
    a = (g * _sigmoid(g) * u).astype(BF16)
    acc_ref[...] += _dot(a, wd_ref[0])

    @pl.when(f == pl.num_programs(1) - 1)
    def _():
        _store_row_tiles(o_ref, acc_ref[...], block_rows)


def _expert_ffn(xb, blk_expert, wg, wu, wd, block_rows, tf):
    d, ff = wg.shape[1], wg.shape[2]
    rows = xb.shape[0] // ROW_TILE
    grid_spec = pltpu.PrefetchScalarGridSpec(
        num_scalar_prefetch=1,
        grid=(rows // block_rows, ff // tf),
        in_specs=[
            pl.BlockSpec((block_rows * ROW_TILE, LANES), lambda i, f, be: (i, 0)),
            pl.BlockSpec((1, d, tf), lambda i, f, be: (be[i], 0, f)),
            pl.BlockSpec((1, d, tf), lambda i, f, be: (be[i], 0, f)),
            pl.BlockSpec((1, tf, d), lambda i, f, be: (be[i], f, 0)),
        ],
        out_specs=pl.BlockSpec((block_rows * ROW_TILE, LANES), lambda i, f, be: (i, 0)),
        scratch_shapes=[pltpu.VMEM((block_rows, d), BF16), pltpu.VMEM((block_rows, d), F32)],
    )
    return pl.pallas_call(
        functools.partial(_expert_kernel, block_rows=block_rows),
        out_shape=jax.ShapeDtypeStruct(xb.shape, F32), grid_spec=grid_spec,
        compiler_params=_cparams(("arbitrary", "arbitrary")), name="moe_expert_ffn",
    )(blk_expert, xb, wg, wu, wd)


def _combine_kernel(d1_ref, d2_ref, x_ref, yb_hbm, gates_ref, gate_ref, fg_ref, o_ref, y1_ref, y2_ref, sem):
    tm = x_ref.shape[0]

    def start(r, c):
        _row_copy(yb_hbm, d1_ref[0, 0, r], y1_ref, r, sem).start()
        _row_copy(yb_hbm, d2_ref[0, 0, r], y2_ref, r, sem).start()
        return c

    def wait(r, c):
        _row_copy(yb_hbm, 0, y1_ref, 0, sem).wait()
        _row_copy(yb_hbm, 0, y2_ref, 0, sem).wait()
        return c

    lax.fori_loop(0, tm, start, 0)
    lax.fori_loop(0, tm, wait, 0)

    gts = gates_ref[...]
    y = (gts[:, 0:1] * _load_row_tiles(y1_ref, tm)
         + gts[:, LANES // 2:LANES // 2 + 1] * _load_row_tiles(y2_ref, tm))
    xn = x_ref[...] + gate_ref[0] * y
    ms = jnp.mean(xn * xn, axis=-1, keepdims=True)
    o_ref[...] = xn * lax.rsqrt(ms + EPS) * fg_ref[...]


def _combine_final(x, yb, d1, d2, gates, gate, final_g, tm):
    n, d = x.shape
    s = n // gate.shape[0]
    row = lambda i: (i, 0)
    idx_spec = pl.BlockSpec((1, 1, tm), lambda i: (i, 0, 0), memory_space=pltpu.SMEM)
    return pl.pallas_call(
        _combine_kernel,
        out_shape=jax.ShapeDtypeStruct((n, d), F32),
        grid=(n // tm,),
        in_specs=[idx_spec, idx_spec,
                  pl.BlockSpec((tm, d), row),
                  pl.BlockSpec(memory_space=pl.ANY),
                  pl.BlockSpec((tm, LANES), row),
                  pl.BlockSpec((1, 1, d), lambda i: ((i * tm) // s, 0, 0)),
                  pl.BlockSpec((1, d), lambda i: (0, 0))],
        out_specs=pl.BlockSpec((tm, d), row),
        scratch_shapes=[pltpu.VMEM((tm * ROW_TILE, LANES), F32), pltpu.VMEM((tm * ROW_TILE, LANES), F32),
                        pltpu.SemaphoreType.DMA(())],
        compiler_params=_cparams(("arbitrary",)), name="moe_combine_final",
    )(d1.reshape(n // tm, 1, tm), d2.reshape(n // tm, 1, tm), x, yb, gates, gate, final_g)


def _rope_tables(s, dim):
    half = dim // 2
    t = jnp.arange(s)
    inv = 1.0 / (ROPE_THETA ** (jnp.arange(0, half, 2, dtype=F32) / half))
    ang_r = (t // GRID_W).astype(F32)[:, None] * inv
    ang_c = (t % GRID_W).astype(F32)[:, None] * inv
    ang = jnp.concatenate([ang_r, ang_r, ang_c, ang_c], axis=-1)
    reps = LANES // dim
    return jnp.tile(jnp.cos(ang), (1, reps)), jnp.tile(jnp.sin(ang), (1, reps))


def _rotate_matrix(dim):
    q = dim // 4
    p = np.zeros((MXU_DIM, MXU_DIM), np.float32)
    for j in range(MXU_DIM):
        if (j % (2 * q)) < q:
            p[j + q, j] = -1.0
        else:
            p[j - q, j] = 1.0
    return jnp.asarray(p, BF16)


def _head_mean_matrix():
    m = np.kron(np.eye(MXU_DIM // HEAD_V, dtype=np.float32), np.full((HEAD_V, HEAD_V), 1.0 / HEAD_V, np.float32))
    return jnp.asarray(m, BF16)


def _widen_values(w, heads):
    d = w.shape[0]
    w = w.reshape(d, heads, HEAD_V)
    return jnp.concatenate([w, jnp.zeros_like(w)], axis=-1).reshape(d, heads * LANES)


def _widen_in_proj(w):
    qa, ka, va, gb, qc, kc, vc = jnp.split(w, [256, 512, 768, 1280, 1792, 1920], axis=1)
    return jnp.concatenate([qa, ka, _widen_values(va, DIFF_HEADS), gb, qc, kc, _widen_values(vc, GQA_KV)],
                           axis=1).astype(BF16)


def kernel(x, c, ctx, c_ctx, ada_w, ada_b, norm1_g, norm2_g, w_in, w_out, lam_q1, lam_k1, lam_q2, lam_k2,
           diff_subln_g, conv_w, conv_b, conv_ln_g, conv_ln_b, q_norm_g, k_norm_g, ffn_gate, ffn_up, ffn_down,
           router_w, moe_gate, moe_up, moe_down, final_g):
    b, s, d = x.shape
    sc = ctx.shape[1]
    depth = ada_w.shape[0]
    n = b * s
    assert depth % 2 == 0, "the final RMSNorm is fused into the MoE combine of the last (odd) layer"

    tm = min(512, s)
    tmc = min(512, sc)
    tq = min(512, s)
    tq_diff = min(1024, s)
    tqc = min(256, sc)

    tabs_x = _rope_tables(s, DIFF_QK) + _rope_tables(s, HEAD_V)
    ones_c, zeros_c = jnp.ones((sc, LANES), F32), jnp.zeros((sc, LANES), F32)
    tabs_c = (ones_c, zeros_c, ones_c, zeros_c)
    mats = (_rotate_matrix(DIFF_QK), _rotate_matrix(HEAD_V), _head_mean_matrix())

    cc = jnp.zeros((16, d), F32).at[:b].set(c).at[b].set(c_ctx)

    for i in range(depth):
        last = i == depth - 1
        lam_init = 0.8 - 0.6 * math.exp(-0.3 * i)
        mod_all = _ada_mod(cc, ada_w[i], ada_b[i])
        mod = mod_all[:b].reshape(b, 6, 1, d)
        modc = jnp.broadcast_to(mod_all[b].reshape(1, 6, 1, d), (b, 6, 1, d))

        w_aug = _widen_in_proj(w_in[i])
        g1 = norm1_g[i].reshape(1, d)
        qg = jnp.tile(q_norm_g[i], GQA_HEADS).reshape(1, -1)
        kg = jnp.tile(k_norm_g[i], GQA_KV).reshape(1, -1)
        lam_vecs = jnp.stack([lam_q1[i], lam_k1[i], lam_q2[i], lam_k2[i]]).astype(F32)
        subln = diff_subln_g[i].reshape(1, HEAD_V)
        conv_args = (conv_w[i], conv_b[i].reshape(1, -1), conv_ln_g[i].reshape(1, -1), conv_ln_b[i].reshape(1, -1))
        w_o = w_out[i].astype(BF16)
        g2 = norm2_g[i].reshape(1, d)

        qa, kat, va, gb, qc, kct, vc = _in_projection(x, mod[:, 0], mod[:, 1], g1, w_aug, tabs_x, mats, qg, kg, tm)
        qa_x, kat_x, va_x, gb_x, qc_x, kct_x, vc_x = _in_projection(
            ctx, modc[:, 0], modc[:, 1], g1, w_aug, tabs_c, mats, qg, kg, tmc)

        oa = _diff_attention(qa, [kat, kat_x], [va, va_x], lam_vecs, subln, lam_init, tq_diff)
        ob = _conformer_conv(gb, *conv_args, tm)
        oc = _gqa_attention(qc, [kct, kct_x], [vc, vc_x], tq)

        j = i // 2
        if i % 2 == 0:
            x, h2 = _merge(oa, ob, oc, x, mod[:, 2], mod[:, 3], mod[:, 4], g2, w_o, None, tm)
            wg, wu, wd = ffn_gate[j].astype(BF16), ffn_up[j].astype(BF16), ffn_down[j].astype(BF16)
            x = _dense_ffn(h2.reshape(n, d), x.reshape(n, d), mod[:, 5], wg, wu, wd, tm).reshape(b, s, d)
        else:
            rwt = router_w[j].T.astype(F32)
            x, h2, logits_t = _merge(oa, ob, oc, x, mod[:, 2], mod[:, 3], mod[:, 4], g2, w_o, rwt, tm)
            block_rows = 1024 if n >= 8192 else 256
            n_blocks = (2 * n) // block_rows + N_EXPERTS
            n_blocks_pad = -(-n_blocks // LANES) * LANES
            dest, gates, blk_e = _route(logits_t, block_rows, n_blocks_pad, min(512, n))
            xb = _scatter_rows(h2, dest[0], dest[1], n_blocks * block_rows, min(256, n))
            yb = _expert_ffn(xb, blk_e[0, :n_blocks], moe_gate[j].astype(BF16), moe_up[j].astype(BF16),
                             moe_down[j].astype(BF16), block_rows, 512)
            assert last
            x = _combine_final(x.reshape(n, d), yb, dest[0], dest[1], gates, mod[:, 5], final_g.reshape(1, d),
                               tm).reshape(b, s, d)

        if not last:
            oa_x = _diff_attention(qa_x, [kat_x], [va_x], lam_vecs, subln, lam_init, tqc)
            ob_x = _conformer_conv(gb_x, *conv_args, tmc)
            oc_x = _gqa_attention(qc_x, [kct_x], [vc_x], tqc)
            assert i % 2 == 0, "context tokens only ever pass through dense channel mixers"
            ctx, hc2 = _merge(oa_x, ob_x, oc_x, ctx, modc[:, 2], modc[:, 3], modc[:, 4], g2, w_o, None, tmc)
            ctx = _dense_ffn(hc2.reshape(b * sc, d), ctx.reshape(b * sc, d), modc[:, 5], wg, wu, wd,
                             tmc).reshape(b, sc, d)

    return x
```

```python
import functools
import math

import numpy as np
import jax
import jax.numpy as jnp
from jax import lax
from jax.experimental import pallas as pl
from jax.experimental.pallas import tpu as pltpu

F32 = jnp.float32
BF16 = jnp.bfloat16
I32 = jnp.int32

EPS = 1e-6
ROPE_THETA = 10000.0
GRID_W = 64

DIFF_HEADS = 4
DIFF_QK = 32
HEAD_V = 64
GQA_HEADS = 8
GQA_KV = 2
GQA_GROUP = GQA_HEADS // GQA_KV
CONV_K = 31
N_EXPERTS = 8
LOG2E = math.log2(math.e)

LANES = 128
MXU_DIM = 256
VMEM_LIMIT = 52 * 1024 * 1024
NEG_BIG = -1e30
ATTN_UNROLL = 4

C_QA, C_KA, C_VA, C_GB, C_QC, C_KC, C_VC, C_END = 0, 256, 512, 1024, 1536, 2048, 2176, 2432


def _cparams(semantics):
    return pltpu.CompilerParams(dimension_semantics=semantics, vmem_limit_bytes=VMEM_LIMIT)


def _dot(a, b):
    return jnp.dot(a, b, preferred_element_type=F32)


def _sigmoid(z):
    return 1.0 / (1.0 + jnp.exp(-z))


ROW_TILE = 8


def _store_row_tiles(ref, val, rows):
    for a in range(ROW_TILE):
        ref[pl.ds(a, rows, stride=ROW_TILE), :] = val[:, a * LANES:(a + 1) * LANES]


def _load_row_tiles(ref, rows):
    return jnp.concatenate([ref[pl.ds(a, rows, stride=ROW_TILE), :] for a in range(ROW_TILE)], axis=1)


def _mod_kernel(c_ref, w_ref, b_ref, o_ref):
    c = c_ref[...]
    s = c * _sigmoid(c)
    o_ref[...] = jnp.dot(s, w_ref[...], preferred_element_type=F32, precision=lax.Precision.HIGHEST) + b_ref[...]


def _ada_mod(cc, w, b):
    rows, d = cc.shape
    n = w.shape[1]
    tn = d
    return pl.pallas_call(
        _mod_kernel,
        out_shape=jax.ShapeDtypeStruct((rows, n), F32),
        grid=(n // tn,),
        in_specs=[pl.BlockSpec((rows, d), lambda j: (0, 0)),
                  pl.BlockSpec((d, tn), lambda j: (0, j)),
                  pl.BlockSpec((1, tn), lambda j: (0, j))],
        out_specs=pl.BlockSpec((rows, tn), lambda j: (0, j)),
        compiler_params=_cparams(("arbitrary",)),
        name="ada_mod",
    )(cc, w, b.reshape(1, n))


def _inproj_kernel(x_ref, shift_ref, scale_ref, g_ref, w_ref, cosa_ref, sina_ref, cosc_ref, sinc_ref,
                   pa_ref, pc_ref, hm_ref, qg_ref, kg_ref,
                   qa_o, kat_o, va_o, gb_o, qc_o, kct_o, vc_o, *, qa_scale, qc_scale):
    x = x_ref[0]
    ms = jnp.mean(x * x, axis=-1, keepdims=True)
    h = x * lax.rsqrt(ms + EPS) * g_ref[...]
    h = h * (1.0 + scale_ref[0]) + shift_ref[0]
    hb = h.astype(BF16)

    def proj(lo, hi):
        return _dot(hb, w_ref[:, lo:hi])

    def blockmat(y, m_ref):
        yb = y.astype(BF16)
        w = y.shape[1]
        if w == LANES:
            return _dot(yb, m_ref[:LANES, :LANES])
        return jnp.concatenate([_dot(yb[:, c:c + MXU_DIM], m_ref[...]) for c in range(0, w, MXU_DIM)], axis=1)

    def rope(y, cos, sin, p_ref):
        reps = y.shape[1] // LANES
        cos = jnp.tile(cos, (1, reps))
        sin = jnp.tile(sin, (1, reps))
        return y * cos + blockmat(y, p_ref) * sin

    def ones_col(width):
        lane = lax.broadcasted_iota(I32, (1, width), 1)
        return jnp.where(lane % LANES == HEAD_V, 1.0, 0.0).astype(F32)

    cosa, sina, cosc, sinc = cosa_ref[...], sina_ref[...], cosc_ref[...], sinc_ref[...]

    qa_o[0] = (rope(proj(C_QA, C_KA), cosa, sina, pa_ref) * qa_scale).astype(BF16)
    kat_o[0, 0] = rope(proj(C_KA, C_VA), cosa, sina, pa_ref).T.astype(BF16)
    va_o[0] = (proj(C_VA, C_GB) + ones_col(C_GB - C_VA)).astype(BF16)
    gb_o[0] = proj(C_GB, C_QC).astype(BF16)

    y = proj(C_QC, C_KC)
    yn = y * lax.rsqrt(blockmat(y * y, hm_ref) + EPS) * qg_ref[...]
    qc_o[0] = (rope(yn, cosc, sinc, pc_ref) * qc_scale).astype(BF16)

    y = proj(C_KC, C_VC)
    yn = y * lax.rsqrt(blockmat(y * y, hm_ref) + EPS) * kg_ref[...]
    kct_o[0, 0] = rope(yn, cosc, sinc, pc_ref).T.astype(BF16)

    vc_o[0] = (proj(C_VC, C_END) + ones_col(C_END - C_VC)).astype(BF16)


def _in_projection(x, shift, scale, g1, w_aug, tabs, mats, qg, kg, tm):
    b, s, d = x.shape
    nt = s // tm
    cosa, sina, cosc, sinc = tabs
    pa, pc, hm = mats
    row = lambda bb, i: (bb, i, 0)
    const2 = lambda bb, i: (0, 0)
    per_b = lambda bb, i: (bb, 0, 0)
    tab = lambda bb, i: (i, 0)
    kern = functools.partial(_inproj_kernel, qa_scale=DIFF_QK ** -0.5 * LOG2E, qc_scale=HEAD_V ** -0.5 * LOG2E)
    out_shape = (
        jax.ShapeDtypeStruct((b, s, 256), BF16),
        jax.ShapeDtypeStruct((b, nt, 256, tm), BF16),
        jax.ShapeDtypeStruct((b, s, 512), BF16),
        jax.ShapeDtypeStruct((b, s, 512), BF16),
        jax.ShapeDtypeStruct((b, s, 512), BF16),
        jax.ShapeDtypeStruct((b, nt, 128, tm), BF16),
        jax.ShapeDtypeStruct((b, s, 256), BF16),
    )
    out_specs = (
        pl.BlockSpec((1, tm, 256), row),
        pl.BlockSpec((1, 1, 256, tm), lambda bb, i: (bb, i, 0, 0)),
        pl.BlockSpec((1, tm, 512), row),
        pl.BlockSpec((1, tm, 512), row),
        pl.BlockSpec((1, tm, 512), row),
        pl.BlockSpec((1, 1, 128, tm), lambda bb, i: (bb, i, 0, 0)),
        pl.BlockSpec((1, tm, 256), row),
    )
    in_specs = [
        pl.BlockSpec((1, tm, d), row),
        pl.BlockSpec((1, 1, d), per_b),
        pl.BlockSpec((1, 1, d), per_b),
        pl.BlockSpec((1, d), const2),
        pl.BlockSpec((d, C_END), const2),
        pl.BlockSpec((tm, LANES), tab), pl.BlockSpec((tm, LANES), tab),
        pl.BlockSpec((tm, LANES), tab), pl.BlockSpec((tm, LANES), tab),
        pl.BlockSpec((MXU_DIM, MXU_DIM), const2), pl.BlockSpec((MXU_DIM, MXU_DIM), const2),
        pl.BlockSpec((MXU_DIM, MXU_DIM), const2),
        pl.BlockSpec((1, 512), const2), pl.BlockSpec((1, 128), const2),
    ]
    return pl.pallas_call(
        kern, out_shape=out_shape, grid=(b, nt), in_specs=in_specs, out_specs=out_specs,
        compiler_params=_cparams(("arbitrary", "arbitrary")), name="in_projection",
    )(x, shift, scale, g1, w_aug, cosa, sina, cosc, sinc, pa, pc, hm, qg, kg)


def _attention_sweeps(qms, k_slices, pv_groups, k_refs, v_refs, m_ref, acc_ref):
    r = qms[0].shape[0]

    def scores(j, kc):
        return _dot(qms[j], kc[k_slices[j], :])

    def sweep(running_max):
        acc_ref[...] = jnp.zeros(acc_ref.shape, F32)
        if running_max:
            m_ref[...] = jnp.full(m_ref.shape, NEG_BIG, F32)
        else:
            kc0 = k_refs[0][0, 0]
            for j in range(len(qms)):
                m0 = jnp.max(scores(j, kc0), axis=-1, keepdims=True)
                m_ref[j * r:(j + 1) * r, :] = jnp.broadcast_to(m0, (r, LANES))

        for k_ref, v_ref in zip(k_refs, v_refs):
            n_chunks, tk = k_ref.shape[1], k_ref.shape[3]

            def body(c, carry, k_ref=k_ref, v_ref=v_ref, tk=tk):
                kc = k_ref[0, c]
                vc = v_ref[0, pl.ds(pl.multiple_of(c * tk, tk), tk), :]
                for ids, v_lanes in pv_groups:
                    ps, alphas = [], []
                    for j in ids:
                        rows = slice(j * r, (j + 1) * r)
                        s = scores(j, kc)
                        m = m_ref[rows, :]
                        if running_max:
                            m_new = jnp.maximum(m, jnp.max(s, axis=-1, keepdims=True))
                            m_ref[rows, :] = m_new
                            alphas.append(jnp.exp2(m - m_new))
                            m = m_new
                        ps.append(jnp.exp2(s - jnp.tile(m, (1, tk // LANES))).astype(BF16))
                    rows = slice(ids[0] * r, (ids[-1] + 1) * r)
                    pv = _dot(jnp.concatenate(ps, axis=0), vc[:, v_lanes])
                    if running_max:
                        acc_ref[rows, :] = acc_ref[rows, :] * jnp.concatenate(alphas, axis=0) + pv
                    else:
                        acc_ref[rows, :] += pv
                return carry

            unroll = 1 if running_max else math.gcd(n_chunks, ATTN_UNROLL)
            lax.fori_loop(0, n_chunks, body, 0, unroll=unroll)

    sweep(False)
    not_finite = jnp.sum(acc_ref[...] * 0.0)

    @pl.when(not_finite != 0.0)
    def _():
        sweep(True)


def _gqa_kernel(*refs, n_parts, tq):
    q_ref = refs[0]
    k_refs = refs[1:1 + n_parts]
    v_refs = refs[1 + n_parts:1 + 2 * n_parts]
    o_ref, m_ref, acc_ref = refs[1 + 2 * n_parts:]
    q = q_ref[0]
    qs = jnp.concatenate([q[:, HEAD_V * j:HEAD_V * (j + 1)] for j in range(GQA_GROUP)], axis=0)
    _attention_sweeps([qs], [slice(0, HEAD_V)], [((0,), slice(0, LANES))], k_refs, v_refs, m_ref, acc_ref)
    acc = acc_ref[...]
    o = acc[:, :HEAD_V] / acc[:, HEAD_V:HEAD_V + 1]
    for j in range(GQA_GROUP):
        o_ref[0, :, HEAD_V * j:HEAD_V * (j + 1)] = o[j * tq:(j + 1) * tq].astype(BF16)


def _gqa_attention(q, k_parts, v_parts, tq):
    b, sq, _ = q.shape
    n_parts = len(k_parts)
    in_specs = [pl.BlockSpec((1, tq, 256), lambda bb, g, i: (bb, i, g))]
    for kp in k_parts:
        in_specs.append(pl.BlockSpec((1, kp.shape[1], HEAD_V, kp.shape[3]), lambda bb, g, i: (bb, 0, g, 0)))
    for vp in v_parts:
        in_specs.append(pl.BlockSpec((1, vp.shape[1], LANES), lambda bb, g, i: (bb, 0, g)))
    return pl.pallas_call(
        functools.partial(_gqa_kernel, n_parts=n_parts, tq=tq),
        out_shape=jax.ShapeDtypeStruct((b, sq, 512), BF16),
        grid=(b, GQA_KV, sq // tq),
        in_specs=in_specs,
        out_specs=pl.BlockSpec((1, tq, 256), lambda bb, g, i: (bb, i, g)),
        scratch_shapes=[pltpu.VMEM((GQA_GROUP * tq, LANES), F32), pltpu.VMEM((GQA_GROUP * tq, LANES), F32)],
        compiler_params=_cparams(("arbitrary", "arbitrary", "arbitrary")),
        name="gqa_attention",
    )(q, *k_parts, *v_parts)


def _diff_kernel(*refs, n_parts, tq, lam_init):
    q_ref = refs[0]
    k_refs = refs[1:1 + n_parts]
    v_refs = refs[1 + n_parts:1 + 2 * n_parts]
    lam_ref, sg_ref, o_ref, m_ref, acc_ref = refs[1 + 2 * n_parts:]
    q = q_ref[0]
    qmaps = [q[:, DIFF_QK * j:DIFF_QK * (j + 1)] for j in range(4)]
    k_slices = [slice(DIFF_QK * j, DIFF_QK * (j + 1)) for j in range(4)]
    pv_groups = [((0, 1), slice(0, LANES)), ((2, 3), slice(LANES, 2 * LANES))]
    _attention_sweeps(qmaps, k_slices, pv_groups, k_refs, v_refs, m_ref, acc_ref)

    lv = lam_ref[...]
    lam = (jnp.exp(jnp.sum(lv[0:1] * lv[1:2], axis=-1, keepdims=True))
           - jnp.exp(jnp.sum(lv[2:3] * lv[3:4], axis=-1, keepdims=True)) + lam_init)
    acc = acc_ref[...]
    for hh in range(2):
        a0 = acc[(2 * hh) * tq:(2 * hh + 1) * tq]
        a1 = acc[(2 * hh + 1) * tq:(2 * hh + 2) * tq]
        o = a0[:, :HEAD_V] / a0[:, HEAD_V:HEAD_V + 1] - lam * (a1[:, :HEAD_V] / a1[:, HEAD_V:HEAD_V + 1])
        ms = jnp.mean(o * o, axis=-1, keepdims=True)
        on = o * lax.rsqrt(ms + EPS) * sg_ref[...] * (1.0 - lam_init)
        o_ref[0, :, HEAD_V * hh:HEAD_V * (hh + 1)] = on.astype(BF16)


def _diff_attention(q, k_parts, v_parts, lam_vecs, subln_g, lam_init, tq):
    b, sq, _ = q.shape
    n_parts = len(k_parts)
    in_specs = [pl.BlockSpec((1, tq, LANES), lambda bb, p, i: (bb, i, p))]
    for kp in k_parts:
        in_specs.append(pl.BlockSpec((1, kp.shape[1], LANES, kp.shape[3]), lambda bb, p, i: (bb, 0, p, 0)))
    for vp in v_parts:
        in_specs.append(pl.BlockSpec((1, vp.shape[1], 2 * LANES), lambda bb, p, i: (bb, 0, p)))
    in_specs.append(pl.BlockSpec((4, DIFF_QK), lambda bb, p, i: (0, 0)))
    in_specs.append(pl.BlockSpec((1, HEAD_V), lambda bb, p, i: (0, 0)))
    return pl.pallas_call(
        functools.partial(_diff_kernel, n_parts=n_parts, tq=tq, lam_init=lam_init),
        out_shape=jax.ShapeDtypeStruct((b, sq, 256), BF16),
        grid=(b, DIFF_HEADS // 2, sq // tq),
        in_specs=in_specs,
        out_specs=pl.BlockSpec((1, tq, LANES), lambda bb, p, i: (bb, i, p)),
        scratch_shapes=[pltpu.VMEM((4 * tq, LANES), F32), pltpu.VMEM((4 * tq, LANES), F32)],
        compiler_params=_cparams(("arbitrary", "arbitrary", "arbitrary")),
        name="diff_attention",
    )(q, *k_parts, *v_parts, lam_vecs, subln_g)


CONV_HALO = 16
CONV_ROWS = 64


def _conv_kernel(gb_ref, prev_ref, next_ref, w_ref, b_ref, lg_ref, lb_ref, o_ref, u_ref, *, tm):
    i = pl.program_id(1)
    last = pl.num_programs(1) - 1
    ch = w_ref.shape[1]

    def glu(z):
        z = z.astype(F32)
        return z[:, :ch] * _sigmoid(z[:, ch:])

    u_ref[CONV_HALO:CONV_HALO + tm, :] = glu(gb_ref[0])
    u_ref[0:CONV_HALO, :] = jnp.where(i > 0, glu(prev_ref[0]), 0.0)
    u_ref[CONV_HALO + tm:2 * CONV_HALO + tm, :] = jnp.where(i < last, glu(next_ref[0]), 0.0)

    off = CONV_HALO - CONV_K // 2
    for r0 in range(0, tm, CONV_ROWS):
        acc = jnp.zeros((CONV_ROWS, ch), F32)
        for j in range(CONV_K):
            acc = acc + u_ref[r0 + off + j:r0 + off + j + CONV_ROWS, :] * w_ref[j:j + 1, :]
        y = acc + b_ref[...]
        mu = jnp.mean(y, axis=-1, keepdims=True)
        yc = y - mu
        var = jnp.mean(yc * yc, axis=-1, keepdims=True)
        z = yc * lax.rsqrt(var + EPS) * lg_ref[...] + lb_ref[...]
        o_ref[0, r0:r0 + CONV_ROWS, :] = (z * _sigmoid(z)).astype(BF16)


def _conformer_conv(gb, w, bias, ln_g, ln_b, tm):
    b, s, two_ch = gb.shape
    ch = two_ch // 2
    hb = tm // CONV_HALO
    n_halo = s // CONV_HALO
    const2 = lambda bb, i: (0, 0)
    return pl.pallas_call(
        functools.partial(_conv_kernel, tm=tm),
        out_shape=jax.ShapeDtypeStruct((b, s, ch), BF16),
        grid=(b, s // tm),
        in_specs=[
            pl.BlockSpec((1, tm, two_ch), lambda bb, i: (bb, i, 0)),
            pl.BlockSpec((1, CONV_HALO, two_ch), lambda bb, i: (bb, jnp.maximum(i * hb - 1, 0), 0)),
            pl.BlockSpec((1, CONV_HALO, two_ch), lambda bb, i: (bb, jnp.minimum((i + 1) * hb, n_halo - 1), 0)),
            pl.BlockSpec((CONV_K, ch), const2), pl.BlockSpec((1, ch), const2),
            pl.BlockSpec((1, ch), const2), pl.BlockSpec((1, ch), const2),
        ],
        out_specs=pl.BlockSpec((1, tm, ch), lambda bb, i: (bb, i, 0)),
        scratch_shapes=[pltpu.VMEM((tm + 2 * CONV_HALO, ch), F32)],
        compiler_params=_cparams(("arbitrary", "arbitrary")),
        name="conformer_conv",
    )(gb, gb, gb, w, bias, ln_g, ln_b)


def _merge_kernel(*refs, with_router):
    oa_ref, ob_ref, oc_ref, x_ref, gate_ref, shift_ref, scale_ref, g2_ref, w_ref = refs[:9]
    if with_router:
        rw_ref, xo_ref, h2_ref, lg_ref = refs[9:]
    else:
        xo_ref, h2_ref = refs[9:]
    tm = x_ref.shape[1]
    wa = oa_ref.shape[2]
    wb = wa + ob_ref.shape[2]
    y = _dot(oa_ref[0], w_ref[0:wa, :]) + _dot(ob_ref[0], w_ref[wa:wb, :]) + _dot(oc_ref[0], w_ref[wb:, :])
    xn = x_ref[0] + gate_ref[0] * y
    xo_ref[0] = xn
    ms = jnp.mean(xn * xn, axis=-1, keepdims=True)
    h2 = xn * lax.rsqrt(ms + EPS) * g2_ref[...] * (1.0 + scale_ref[0]) + shift_ref[0]
    if with_router:
        _store_row_tiles(h2_ref, h2, tm)
        lg_ref[...] = lax.dot_general(rw_ref[...], h2, (((1,), (1,)), ((), ())),
                                      preferred_element_type=F32, precision=lax.Precision.HIGHEST)
    else:
        h2_ref[0] = h2.astype(BF16)


def _merge(oa, ob, oc, x, gate, shift, scale, g2, w_out, router_wt, tm):
    b, s, d = x.shape
    row = lambda bb, i: (bb, i, 0)
    per_b = lambda bb, i: (bb, 0, 0)
    const2 = lambda bb, i: (0, 0)
    nt = s // tm
    in_specs = [
        pl.BlockSpec((1, tm, oa.shape[2]), row), pl.BlockSpec((1, tm, ob.shape[2]), row),
        pl.BlockSpec((1, tm, oc.shape[2]), row), pl.BlockSpec((1, tm, d), row),
        pl.BlockSpec((1, 1, d), per_b), pl.BlockSpec((1, 1, d), per_b), pl.BlockSpec((1, 1, d), per_b),
        pl.BlockSpec((1, d), const2), pl.BlockSpec((d, d), const2),
    ]
    out_shape = [jax.ShapeDtypeStruct((b, s, d), F32)]
    out_specs = [pl.BlockSpec((1, tm, d), row)]
    args = [oa, ob, oc, x, gate, shift, scale, g2, w_out]
    with_router = router_wt is not None
    if not with_router:
        out_shape.append(jax.ShapeDtypeStruct((b, s, d), BF16))
        out_specs.append(pl.BlockSpec((1, tm, d), row))
    else:
        out_shape.append(jax.ShapeDtypeStruct((b * s * ROW_TILE, LANES), F32))
        out_specs.append(pl.BlockSpec((tm * ROW_TILE, LANES), lambda bb, i: (bb * nt + i, 0)))
        in_specs.append(pl.BlockSpec((N_EXPERTS, d), const2))
        out_shape.append(jax.ShapeDtypeStruct((N_EXPERTS, b * s), F32))
        out_specs.append(pl.BlockSpec((N_EXPERTS, tm), lambda bb, i: (0, bb * nt + i)))
        args.append(router_wt)
    return pl.pallas_call(
        functools.partial(_merge_kernel, with_router=with_router),
        out_shape=tuple(out_shape), grid=(b, nt), in_specs=in_specs, out_specs=tuple(out_specs),
        compiler_params=_cparams(("arbitrary", "arbitrary")), name="merge_heads",
    )(*args)


def _ffn_kernel(h_ref, x_ref, gate_ref, wg_ref, wu_ref, wd_ref, o_ref, *, tf):
    h = h_ref[...]
    ff = wg_ref.shape[1]
    acc = jnp.zeros(x_ref.shape, F32)
    for f in range(0, ff, tf):
        g = _dot(h, wg_ref[:, f:f + tf])
        u = _dot(h, wu_ref[:, f:f + tf])
        a = (g * _sigmoid(g) * u).astype(BF16)
        acc = acc + _dot(a, wd_ref[f:f + tf, :])
    o_ref[...] = x_ref[...] + gate_ref[0] * acc


def _dense_ffn(h2, x, gate, wg, wu, wd, tm):
    n, d = x.shape
    ff = wg.shape[1]
    s = n // gate.shape[0]
    row = lambda i: (i, 0)
    const2 = lambda i: (0, 0)
    resident = pl.Buffered(1)
    return pl.pallas_call(
        functools.partial(_ffn_kernel, tf=MXU_DIM),
        out_shape=jax.ShapeDtypeStruct((n, d), F32),
        grid=(n // tm,),
        in_specs=[
            pl.BlockSpec((tm, d), row), pl.BlockSpec((tm, d), row),
            pl.BlockSpec((1, 1, d), lambda i: ((i * tm) // s, 0, 0)),
            pl.BlockSpec((d, ff), const2, pipeline_mode=resident),
            pl.BlockSpec((d, ff), const2, pipeline_mode=resident),
            pl.BlockSpec((ff, d), const2, pipeline_mode=resident),
        ],
        out_specs=pl.BlockSpec((tm, d), row),
        compiler_params=_cparams(("arbitrary",)), name="dense_ffn",
    )(h2, x, gate, wg, wu, wd)


def _top2(lg):
    sub = lax.broadcasted_iota(I32, lg.shape, 0)
    l1 = jnp.max(lg, axis=0, keepdims=True)
    i1 = jnp.min(jnp.where(lg == l1, sub, N_EXPERTS), axis=0, keepdims=True)
    m1 = sub == i1
    lg2 = jnp.where(m1, -jnp.inf, lg)
    l2 = jnp.max(lg2, axis=0, keepdims=True)
    i2 = jnp.min(jnp.where(lg2 == l2, sub, N_EXPERTS), axis=0, keepdims=True)
    m2 = sub == i2
    return l1, l2, m1, m2


def _sublane_cumsum(x):
    sub = lax.broadcasted_iota(I32, x.shape, 0)
    for sh in (1, 2, 4):
        x = x + jnp.where(sub >= sh, pltpu.roll(x, sh, 0), 0.0)
    return x


def _route_kernel(lg_ref, tri_ref, dest_ref, gates_ref, be_ref, base_ref, start_ref, *, block_rows):
    phase = pl.program_id(0)
    j = pl.program_id(1)
    l1, l2, m1, m2 = _top2(lg_ref[...])
    e = jnp.where(m1 | m2, 1.0, 0.0).astype(F32)
    cnt = jnp.sum(e, axis=1, keepdims=True)

    @pl.when((phase == 0) & (j == 0))
    def _():
        base_ref[...] = jnp.zeros(base_ref.shape, F32)

    @pl.when((phase == 1) & (j == 0))
    def _():
        counts = base_ref[...]
        nblk = jnp.floor((counts + (block_rows - 1)) * (1.0 / block_rows))
        end_blk = _sublane_cumsum(nblk)
        start_ref[...] = (end_blk - nblk) * block_rows
        blk = lax.broadcasted_iota(I32, be_ref.shape, 1).astype(F32)
        owner = jnp.sum(jnp.where(end_blk[:, :1] <= blk, 1.0, 0.0), axis=0, keepdims=True)
        be_ref[...] = jnp.broadcast_to(jnp.minimum(owner, N_EXPERTS - 1.0), be_ref.shape).astype(I32)
        base_ref[...] = jnp.zeros(base_ref.shape, F32)

    @pl.when(phase == 1)
    def _():
        prefix = _dot(e.astype(BF16), tri_ref[...]) + base_ref[:, :1] + start_ref[:, :1]
        d1 = jnp.sum(jnp.where(m1, prefix, 0.0), axis=0, keepdims=True)
        d2 = jnp.sum(jnp.where(m2, prefix, 0.0), axis=0, keepdims=True)
        sub = lax.broadcasted_iota(I32, dest_ref.shape, 0)
        dest_ref[...] = jnp.where(sub == 0, d1, jnp.where(sub == 1, d2, 0.0)).astype(I32)
        ex = jnp.exp(l2 - l1)
        g1 = 1.0 / (1.0 + ex)
        g2 = ex / (1.0 + ex)
        half = lax.broadcasted_iota(I32, (LANES, lg_ref.shape[1]), 0) < LANES // 2
        gates_ref[...] = jnp.where(half, g1, g2).T

    base_ref[...] = base_ref[...] + cnt


def _route(logits_t, block_rows, n_blocks_pad, tr):
    n = logits_t.shape[1]
    tri = jnp.asarray(np.triu(np.ones((tr, tr), np.float32), k=1), BF16)
    return pl.pallas_call(
        functools.partial(_route_kernel, block_rows=block_rows),
        out_shape=(jax.ShapeDtypeStruct((N_EXPERTS, n), I32),
                   jax.ShapeDtypeStruct((n, LANES), F32),
                   jax.ShapeDtypeStruct((N_EXPERTS, n_blocks_pad), I32)),
        grid=(2, n // tr),
        in_specs=[pl.BlockSpec((N_EXPERTS, tr), lambda p, j: (0, j)),
                  pl.BlockSpec((tr, tr), lambda p, j: (0, 0))],
        out_specs=(pl.BlockSpec((N_EXPERTS, tr), lambda p, j: (0, j * p)),
                   pl.BlockSpec((tr, LANES), lambda p, j: (j * p, 0)),
                   pl.BlockSpec((N_EXPERTS, n_blocks_pad), lambda p, j: (0, 0))),
        scratch_shapes=[pltpu.VMEM((N_EXPERTS, LANES), F32), pltpu.VMEM((N_EXPERTS, LANES), F32)],
        compiler_params=_cparams(("arbitrary", "arbitrary")), name="moe_route",
    )(logits_t, tri)


def _row_copy(src_hbm, src_row, dst_hbm, dst_row, sem):
    src = pl.ds(pl.multiple_of(src_row * ROW_TILE, ROW_TILE), ROW_TILE)
    dst = pl.ds(pl.multiple_of(dst_row * ROW_TILE, ROW_TILE), ROW_TILE)
    return pltpu.make_async_copy(src_hbm.at[src], dst_hbm.at[dst], sem)


def _scatter_kernel(d1_ref, d2_ref, src_ref, init_hbm, out_hbm, sem, *, rows):
    del init_hbm

    def start(r, c):
        _row_copy(src_ref, r, out_hbm, d1_ref[0, 0, r], sem).start(priority=0)
        _row_copy(src_ref, r, out_hbm, d2_ref[0, 0, r], sem).start(priority=1)
        return c

    def wait(r, c):
        _row_copy(src_ref, 0, out_hbm, 0, sem).wait()
        _row_copy(src_ref, 0, out_hbm, 0, sem).wait()
        return c

    lax.fori_loop(0, rows, start, 0)
    lax.fori_loop(0, rows, wait, 0)


def _scatter_rows(src, d1, d2, total_rows, rows):
    n = d1.shape[0]
    idx_spec = pl.BlockSpec((1, 1, rows), lambda i: (i, 0, 0), memory_space=pltpu.SMEM)
    any_spec = pl.BlockSpec(memory_space=pl.ANY)
    return pl.pallas_call(
        functools.partial(_scatter_kernel, rows=rows),
        out_shape=jax.ShapeDtypeStruct((total_rows * ROW_TILE, LANES), src.dtype),
        grid=(n // rows,),
        in_specs=[idx_spec, idx_spec, pl.BlockSpec((rows * ROW_TILE, LANES), lambda i: (i, 0)), any_spec],
        out_specs=any_spec,
        scratch_shapes=[pltpu.SemaphoreType.DMA(())],
        input_output_aliases={3: 0},
        compiler_params=pltpu.CompilerParams(dimension_semantics=("arbitrary",), has_side_effects=True),
        name="moe_scatter_rows",
    )(d1.reshape(n // rows, 1, rows), d2.reshape(n // rows, 1, rows), src,
      jnp.zeros((total_rows * ROW_TILE, LANES), src.dtype))


def _expert_kernel(be_ref, x_ref, wg_ref, wu_ref, wd_ref, o_ref, *, block_rows, tf):
    del be_ref
    x = _load_row_tiles(x_ref, block_rows).astype(BF16)
    ff = wg_ref.shape[2]
    acc = jnp.zeros((block_rows, wd_ref.shape[2]), F32)
    for f in range(0, ff, tf):
        g = _dot(x, wg_ref[0, :, f:f + tf])
        u = _dot(x, wu_ref[0, :, f:f + tf])
        a = (g * _sigmoid(g) * u).astype(BF16)
        acc = acc + _dot(a, wd_ref[0, f:f + tf, :])
    _store_row_tiles(o_ref, acc, block_rows)


def _expert_ffn(xb, blk_expert, wg, wu, wd, block_rows):
    d, ff = wg.shape[1], wg.shape[2]
    rows = xb.shape[0] // ROW_TILE
    resident = pl.Buffered(1)
    grid_spec = pltpu.PrefetchScalarGridSpec(
        num_scalar_prefetch=1,
        grid=(rows // block_rows,),
        in_specs=[
            pl.BlockSpec((block_rows * ROW_TILE, LANES), lambda i, be: (i, 0)),
            pl.BlockSpec((1, d, ff), lambda i, be: (be[i], 0, 0), pipeline_mode=resident),
            pl.BlockSpec((1, d, ff), lambda i, be: (be[i], 0, 0), pipeline_mode=resident),
            pl.BlockSpec((1, ff, d), lambda i, be: (be[i], 0, 0), pipeline_mode=resident),
        ],
        out_specs=pl.BlockSpec((block_rows * ROW_TILE, LANES), lambda i, be: (i, 0)),
    )
    return pl.pallas_call(
        functools.partial(_expert_kernel, block_rows=block_rows, tf=MXU_DIM),
        out_shape=jax.ShapeDtypeStruct(xb.shape, F32), grid_spec=grid_spec,
        compiler_params=_cparams(("arbitrary",)), name="moe_expert_ffn",
    )(blk_expert, xb, wg, wu, wd)


def _combine_kernel(d1_ref, d2_ref, x_ref, yb_hbm, gates_ref, gate_ref, fg_ref, o_ref, y1_ref, y2_ref, sem):
    tm = x_ref.shape[0]

    def start(r, c):
        _row_copy(yb_hbm, d1_ref[0, 0, r], y1_ref, r, sem).start(priority=0)
        _row_copy(yb_hbm, d2_ref[0, 0, r], y2_ref, r, sem).start(priority=1)
        return c

    def wait(r, c):
        _row_copy(yb_hbm, 0, y1_ref, 0, sem).wait()
        _row_copy(yb_hbm, 0, y2_ref, 0, sem).wait()
        return c

    lax.fori_loop(0, tm, start, 0)
    lax.fori_loop(0, tm, wait, 0)

    gts = gates_ref[...]
    y = (gts[:, 0:1] * _load_row_tiles(y1_ref, tm)
         + gts[:, LANES // 2:LANES // 2 + 1] * _load_row_tiles(y2_ref, tm))
    xn = x_ref[...] + gate_ref[0] * y
    ms = jnp.mean(xn * xn, axis=-1, keepdims=True)
    o_ref[...] = xn * lax.rsqrt(ms + EPS) * fg_ref[...]


def _combine_final(x, yb, d1, d2, gates, gate, final_g, tm):
    n, d = x.shape
    s = n // gate.shape[0]
    row = lambda i: (i, 0)
    idx_spec = pl.BlockSpec((1, 1, tm), lambda i: (i, 0, 0), memory_space=pltpu.SMEM)
    return pl.pallas_call(
        _combine_kernel,
        out_shape=jax.ShapeDtypeStruct((n, d), F32),
        grid=(n // tm,),
        in_specs=[idx_spec, idx_spec,
                  pl.BlockSpec((tm, d), row),
                  pl.BlockSpec(memory_space=pl.ANY),
                  pl.BlockSpec((tm, LANES), row),
                  pl.BlockSpec((1, 1, d), lambda i: ((i * tm) // s, 0, 0)),
                  pl.BlockSpec((1, d), lambda i: (0, 0))],
        out_specs=pl.BlockSpec((tm, d), row),
        scratch_shapes=[pltpu.VMEM((tm * ROW_TILE, LANES), F32), pltpu.VMEM((tm * ROW_TILE, LANES), F32),
                        pltpu.SemaphoreType.DMA(())],
        compiler_params=_cparams(("arbitrary",)), name="moe_combine_final",
    )(d1.reshape(n // tm, 1, tm), d2.reshape(n // tm, 1, tm), x, yb, gates, gate, final_g)


def _rope_tables(s, dim):
    half = dim // 2
    t = jnp.arange(s)
    inv = 1.0 / (ROPE_THETA ** (jnp.arange(0, half, 2, dtype=F32) / half))
    ang_r = (t // GRID_W).astype(F32)[:, None] * inv
    ang_c = (t % GRID_W).astype(F32)[:, None] * inv
    ang = jnp.concatenate([ang_r, ang_r, ang_c, ang_c], axis=-1)
    reps = LANES // dim
    return jnp.tile(jnp.cos(ang), (1, reps)), jnp.tile(jnp.sin(ang), (1, reps))


def _rotate_matrix(dim):
    q = dim // 4
    p = np.zeros((MXU_DIM, MXU_DIM), np.float32)
    for j in range(MXU_DIM):
        if (j % (2 * q)) < q:
            p[j + q, j] = -1.0
        else:
            p[j - q, j] = 1.0
    return jnp.asarray(p, BF16)


def _head_mean_matrix():
    m = np.kron(np.eye(MXU_DIM // HEAD_V, dtype=np.float32), np.full((HEAD_V, HEAD_V), 1.0 / HEAD_V, np.float32))
    return jnp.asarray(m, BF16)


def _widen_values(w, heads):
    d = w.shape[0]
    w = w.reshape(d, heads, HEAD_V)
    return jnp.concatenate([w, jnp.zeros_like(w)], axis=-1).reshape(d, heads * LANES)


def _widen_in_proj(w):
    qa, ka, va, gb, qc, kc, vc = jnp.split(w, [256, 512, 768, 1280, 1792, 1920], axis=1)
    return jnp.concatenate([qa, ka, _widen_values(va, DIFF_HEADS), gb, qc, kc, _widen_values(vc, GQA_KV)],
                           axis=1).astype(BF16)


def kernel(x, c, ctx, c_ctx, ada_w, ada_b, norm1_g, norm2_g, w_in, w_out, lam_q1, lam_k1, lam_q2, lam_k2,
           diff_subln_g, conv_w, conv_b, conv_ln_g, conv_ln_b, q_norm_g, k_norm_g, ffn_gate, ffn_up, ffn_down,
           router_w, moe_gate, moe_up, moe_down, final_g):
    b, s, d = x.shape
    sc = ctx.shape[1]
    depth = ada_w.shape[0]
    n = b * s
    assert depth % 2 == 0, "the final RMSNorm is fused into the MoE combine of the last (odd) layer"

    tm = min(512, s)
    tmc = min(512, sc)
    tq = min(512, s)
    tq_diff = min(1024, s)
    tqc = min(256, sc)

    tabs_x = _rope_tables(s, DIFF_QK) + _rope_tables(s, HEAD_V)
    ones_c, zeros_c = jnp.ones((sc, LANES), F32), jnp.zeros((sc, LANES), F32)
    tabs_c = (ones_c, zeros_c, ones_c, zeros_c)
    mats = (_rotate_matrix(DIFF_QK), _rotate_matrix(HEAD_V), _head_mean_matrix())

    cc = jnp.zeros((16, d), F32).at[:b].set(c).at[b].set(c_ctx)

    for i in range(depth):
        last = i == depth - 1
        lam_init = 0.8 - 0.6 * math.exp(-0.3 * i)
        mod_all = _ada_mod(cc, ada_w[i], ada_b[i])
        mod = mod_all[:b].reshape(b, 6, 1, d)
        modc = jnp.broadcast_to(mod_all[b].reshape(1, 6, 1, d), (b, 6, 1, d))

        w_aug = _widen_in_proj(w_in[i])
        g1 = norm1_g[i].reshape(1, d)
        qg = jnp.tile(q_norm_g[i], GQA_HEADS).reshape(1, -1)
        kg = jnp.tile(k_norm_g[i], GQA_KV).reshape(1, -1)
        lam_vecs = jnp.stack([lam_q1[i], lam_k1[i], lam_q2[i], lam_k2[i]]).astype(F32)
        subln = diff_subln_g[i].reshape(1, HEAD_V)
        conv_args = (conv_w[i], conv_b[i].reshape(1, -1), conv_ln_g[i].reshape(1, -1), conv_ln_b[i].reshape(1, -1))
        w_o = w_out[i].astype(BF16)
        g2 = norm2_g[i].reshape(1, d)

        qa, kat, va, gb, qc, kct, vc = _in_projection(x, mod[:, 0], mod[:, 1], g1, w_aug, tabs_x, mats, qg, kg, tm)
        qa_x, kat_x, va_x, gb_x, qc_x, kct_x, vc_x = _in_projection(
            ctx, modc[:, 0], modc[:, 1], g1, w_aug, tabs_c, mats, qg, kg, tmc)

        oa = _diff_attention(qa, [kat, kat_x], [va, va_x], lam_vecs, subln, lam_init, tq_diff)
        ob = _conformer_conv(gb, *conv_args, tm)
        oc = _gqa_attention(qc, [kct, kct_x], [vc, vc_x], tq)

        j = i // 2
        if i % 2 == 0:
            x, h2 = _merge(oa, ob, oc, x, mod[:, 2], mod[:, 3], mod[:, 4], g2, w_o, None, tm)
            wg, wu, wd = ffn_gate[j].astype(BF16), ffn_up[j].astype(BF16), ffn_down[j].astype(BF16)
            x = _dense_ffn(h2.reshape(n, d), x.reshape(n, d), mod[:, 5], wg, wu, wd, tm).reshape(b, s, d)
        else:
            rwt = router_w[j].T.astype(F32)
            x, h2, logits_t = _merge(oa, ob, oc, x, mod[:, 2], mod[:, 3], mod[:, 4], g2, w_o, rwt, tm)
            block_rows = 512 if n >= 8192 else 256
            n_blocks = (2 * n) // block_rows + N_EXPERTS
            n_blocks_pad = -(-n_blocks // LANES) * LANES
            dest, gates, blk_e = _route(logits_t, block_rows, n_blocks_pad, min(512, n))
            xb = _scatter_rows(h2, dest[0], dest[1], n_blocks * block_rows, min(256, n))
            yb = _expert_ffn(xb, blk_e[0, :n_blocks], moe_gate[j].astype(BF16), moe_up[j].astype(BF16),
                             moe_down[j].astype(BF16), block_rows)
            assert last
            x = _combine_final(x.reshape(n, d), yb, dest[0], dest[1], gates, mod[:, 5], final_g.reshape(1, d),
                               tm).reshape(b, s, d)

        if not last:
            oa_x = _diff_attention(qa_x, [kat_x], [va_x], lam_vecs, subln, lam_init, tqc)
            ob_x = _conformer_conv(gb_x, *conv_args, tmc)
            oc_x = _gqa_attention(qc_x, [kct_x], [vc_x], tqc)
            assert i % 2 == 0, "context tokens only ever pass through dense channel mixers"
            ctx, hc2 = _merge(oa_x, ob_x, oc_x, ctx, modc[:, 2], modc[:, 3], modc[:, 4], g2, w_o, None, tmc)
            ctx = _dense_ffn(hc2.reshape(b * sc, d), ctx.reshape(b * sc, d), modc[:, 5], wg, wu, wd,
                             tmc).reshape(b, sc, d)

    return x
```

```python
import functools
import math

import numpy as np
import jax
import jax.numpy as jnp
from jax import lax
from jax.experimental import pallas as pl
from jax.experimental.pallas import tpu as pltpu

F32 = jnp.float32
BF16 = jnp.bfloat16
I32 = jnp.int32

EPS = 1e-6
ROPE_THETA = 10000.0
GRID_W = 64

DIFF_HEADS = 4
DIFF_QK = 32
HEAD_V = 64
GQA_HEADS = 8
GQA_KV = 2
GQA_GROUP = GQA_HEADS // GQA_KV
CONV_K = 31
N_EXPERTS = 8
LOG2E = math.log2(math.e)

LANES = 128
SUBLANES = 8
MXU_DIM = 256
VMEM_LIMIT = 52 * 1024 * 1024
NEG_BIG = -1e30
ATTN_UNROLL = 4
DMA_UNROLL = 8

C_QA, C_KA, C_VA, C_GB, C_QC, C_KC, C_VC, C_END = 0, 256, 512, 1024, 1536, 2048, 2176, 2432


def _cparams(semantics):
    return pltpu.CompilerParams(dimension_semantics=semantics, vmem_limit_bytes=VMEM_LIMIT)


def _dot(a, b):
    return jnp.dot(a, b, preferred_element_type=F32)


def _sigmoid(z):
    return 1.0 / (1.0 + jnp.exp(-z))


ROW_TILE = 8


def _store_row_tiles(ref, val, rows):
    for a in range(ROW_TILE):
        ref[pl.ds(a, rows, stride=ROW_TILE), :] = val[:, a * LANES:(a + 1) * LANES]


def _load_row_tiles(ref, rows):
    return jnp.concatenate([ref[pl.ds(a, rows, stride=ROW_TILE), :] for a in range(ROW_TILE)], axis=1)


def _mod_kernel(c_ref, w_ref, b_ref, o_ref):
    c = c_ref[...]
    s = c * _sigmoid(c)
    o_ref[...] = jnp.dot(s, w_ref[...], preferred_element_type=F32, precision=lax.Precision.HIGHEST) + b_ref[...]


def _ada_mod(cc, w, b):
    rows, d = cc.shape
    n = w.shape[1]
    tn = d
    return pl.pallas_call(
        _mod_kernel,
        out_shape=jax.ShapeDtypeStruct((rows, n), F32),
        grid=(n // tn,),
        in_specs=[pl.BlockSpec((rows, d), lambda j: (0, 0)),
                  pl.BlockSpec((d, tn), lambda j: (0, j)),
                  pl.BlockSpec((1, tn), lambda j: (0, j))],
        out_specs=pl.BlockSpec((rows, tn), lambda j: (0, j)),
        compiler_params=_cparams(("arbitrary",)),
        name="ada_mod",
    )(cc, w, b.reshape(1, n))


def _inproj_kernel(x_ref, shift_ref, scale_ref, g_ref, w_ref, cosa_ref, sina_ref, cosc_ref, sinc_ref,
                   pa_ref, pc_ref, hm_ref, qg_ref, kg_ref,
                   qa_o, kat_o, va_o, gb_o, qc_o, kct_o, vc_o, *, qa_scale, qc_scale):
    x = x_ref[0]
    ms = jnp.mean(x * x, axis=-1, keepdims=True)
    h = x * lax.rsqrt(ms + EPS) * g_ref[...]
    h = h * (1.0 + scale_ref[0]) + shift_ref[0]
    hb = h.astype(BF16)

    def proj(lo, hi):
        return _dot(hb, w_ref[:, lo:hi])

    def blockmat(y, m_ref):
        yb = y.astype(BF16)
        w = y.shape[1]
        if w == LANES:
            return _dot(yb, m_ref[:LANES, :LANES])
        return jnp.concatenate([_dot(yb[:, c:c + MXU_DIM], m_ref[...]) for c in range(0, w, MXU_DIM)], axis=1)

    def rope(y, cos, sin, p_ref):
        reps = y.shape[1] // LANES
        cos = jnp.tile(cos, (1, reps))
        sin = jnp.tile(sin, (1, reps))
        return y * cos + blockmat(y, p_ref) * sin

    def ones_col(width):
        lane = lax.broadcasted_iota(I32, (1, width), 1)
        return jnp.where(lane % LANES == HEAD_V, 1.0, 0.0).astype(F32)

    cosa, sina, cosc, sinc = cosa_ref[...], sina_ref[...], cosc_ref[...], sinc_ref[...]

    qa_o[0] = (rope(proj(C_QA, C_KA), cosa, sina, pa_ref) * qa_scale).astype(BF16)
    kat_o[0, 0] = rope(proj(C_KA, C_VA), cosa, sina, pa_ref).T.astype(BF16)
    va_o[0] = (proj(C_VA, C_GB) + ones_col(C_GB - C_VA)).astype(BF16)
    gb_o[0] = proj(C_GB, C_QC).astype(BF16)

    y = proj(C_QC, C_KC)
    yn = y * lax.rsqrt(blockmat(y * y, hm_ref) + EPS) * qg_ref[...]
    qc_o[0] = (rope(yn, cosc, sinc, pc_ref) * qc_scale).astype(BF16)

    y = proj(C_KC, C_VC)
    yn = y * lax.rsqrt(blockmat(y * y, hm_ref) + EPS) * kg_ref[...]
    kct_o[0, 0] = rope(yn, cosc, sinc, pc_ref).T.astype(BF16)

    vc_o[0] = (proj(C_VC, C_END) + ones_col(C_END - C_VC)).astype(BF16)


def _in_projection(x, shift, scale, g1, w_aug, tabs, mats, qg, kg, tm):
    b, s, d = x.shape
    nt = s // tm
    cosa, sina, cosc, sinc = tabs
    pa, pc, hm = mats
    row = lambda bb, i: (bb, i, 0)
    const2 = lambda bb, i: (0, 0)
    per_b = lambda bb, i: (bb, 0, 0)
    tab = lambda bb, i: (i, 0)
    kern = functools.partial(_inproj_kernel, qa_scale=DIFF_QK ** -0.5 * LOG2E, qc_scale=HEAD_V ** -0.5 * LOG2E)
    out_shape = (
        jax.ShapeDtypeStruct((b, s, 256), BF16),
        jax.ShapeDtypeStruct((b, nt, 256, tm), BF16),
        jax.ShapeDtypeStruct((b, s, 512), BF16),
        jax.ShapeDtypeStruct((b, s, 512), BF16),
        jax.ShapeDtypeStruct((b, s, 512), BF16),
        jax.ShapeDtypeStruct((b, nt, 128, tm), BF16),
        jax.ShapeDtypeStruct((b, s, 256), BF16),
    )
    out_specs = (
        pl.BlockSpec((1, tm, 256), row),
        pl.BlockSpec((1, 1, 256, tm), lambda bb, i: (bb, i, 0, 0)),
        pl.BlockSpec((1, tm, 512), row),
        pl.BlockSpec((1, tm, 512), row),
        pl.BlockSpec((1, tm, 512), row),
        pl.BlockSpec((1, 1, 128, tm), lambda bb, i: (bb, i, 0, 0)),
        pl.BlockSpec((1, tm, 256), row),
    )
    in_specs = [
        pl.BlockSpec((1, tm, d), row),
        pl.BlockSpec((1, 1, d), per_b),
        pl.BlockSpec((1, 1, d), per_b),
        pl.BlockSpec((1, d), const2),
        pl.BlockSpec((d, C_END), const2),
        pl.BlockSpec((tm, LANES), tab), pl.BlockSpec((tm, LANES), tab),
        pl.BlockSpec((tm, LANES), tab), pl.BlockSpec((tm, LANES), tab),
        pl.BlockSpec((MXU_DIM, MXU_DIM), const2), pl.BlockSpec((MXU_DIM, MXU_DIM), const2),
        pl.BlockSpec((MXU_DIM, MXU_DIM), const2),
        pl.BlockSpec((1, 512), const2), pl.BlockSpec((1, 128), const2),
    ]
    return pl.pallas_call(
        kern, out_shape=out_shape, grid=(b, nt), in_specs=in_specs, out_specs=out_specs,
        compiler_params=_cparams(("arbitrary", "arbitrary")), name="in_projection",
    )(x, shift, scale, g1, w_aug, cosa, sina, cosc, sinc, pa, pc, hm, qg, kg)


def _attention_sweeps(qms, k_slices, pv_groups, k_refs, v_refs, m_ref, acc_ref):
    r = qms[0].shape[0]

    def scores(j, kc):
        return _dot(qms[j], kc[k_slices[j], :])

    def sweep(running_max):
        acc_ref[...] = jnp.zeros(acc_ref.shape, F32)
        if running_max:
            m_ref[...] = jnp.full(m_ref.shape, NEG_BIG, F32)
        else:
            kc0 = k_refs[0][0, 0][:, :MXU_DIM]
            for j in range(len(qms)):
                m0 = jnp.max(scores(j, kc0), axis=-1, keepdims=True)
                m_ref[j * r:(j + 1) * r, :] = jnp.broadcast_to(m0, (r, LANES))

        for k_ref, v_ref in zip(k_refs, v_refs):
            n_chunks, tk = k_ref.shape[1], k_ref.shape[3]

            def body(c, carry, k_ref=k_ref, v_ref=v_ref, tk=tk):
                kc = k_ref[0, c]
                vc = v_ref[0, pl.ds(pl.multiple_of(c * tk, tk), tk), :]
                for ids, v_lanes in pv_groups:
                    ps, alphas = [], []
                    for j in ids:
                        rows = slice(j * r, (j + 1) * r)
                        s = scores(j, kc)
                        m = m_ref[rows, :]
                        if running_max:
                            m_new = jnp.maximum(m, jnp.max(s, axis=-1, keepdims=True))
                            m_ref[rows, :] = m_new
                            alphas.append(jnp.exp2(m - m_new))
                            m = m_new
                        ps.append(jnp.exp2(s - jnp.tile(m, (1, tk // LANES))).astype(BF16))
                    rows = slice(ids[0] * r, (ids[-1] + 1) * r)
                    pv = _dot(jnp.concatenate(ps, axis=0), vc[:, v_lanes])
                    if running_max:
                        acc_ref[rows, :] = acc_ref[rows, :] * jnp.concatenate(alphas, axis=0) + pv
                    else:
                        acc_ref[rows, :] += pv
                return carry

            unroll = 1 if running_max else math.gcd(n_chunks, ATTN_UNROLL)
            lax.fori_loop(0, n_chunks, body, 0, unroll=unroll)

    sweep(False)
    not_finite = jnp.sum(acc_ref[...] * 0.0)

    @pl.when(not_finite != 0.0)
    def _():
        sweep(True)


def _gqa_kernel(*refs, n_parts, tq):
    q_ref = refs[0]
    k_refs = refs[1:1 + n_parts]
    v_refs = refs[1 + n_parts:1 + 2 * n_parts]
    o_ref, m_ref, acc_ref = refs[1 + 2 * n_parts:]
    q = q_ref[0]
    qs = jnp.concatenate([q[:, HEAD_V * j:HEAD_V * (j + 1)] for j in range(GQA_GROUP)], axis=0)
    _attention_sweeps([qs], [slice(0, HEAD_V)], [((0,), slice(0, LANES))], k_refs, v_refs, m_ref, acc_ref)
    acc = acc_ref[...]
    o = acc[:, :HEAD_V] / acc[:, HEAD_V:HEAD_V + 1]
    for j in range(GQA_GROUP):
        o_ref[0, :, HEAD_V * j:HEAD_V * (j + 1)] = o[j * tq:(j + 1) * tq].astype(BF16)


def _gqa_attention(q, k_parts, v_parts, tq):
    b, sq, _ = q.shape
    n_parts = len(k_parts)
    in_specs = [pl.BlockSpec((1, tq, 256), lambda bb, g, i: (bb, i, g))]
    for kp in k_parts:
        in_specs.append(pl.BlockSpec((1, kp.shape[1], HEAD_V, kp.shape[3]), lambda bb, g, i: (bb, 0, g, 0)))
    for vp in v_parts:
        in_specs.append(pl.BlockSpec((1, vp.shape[1], LANES), lambda bb, g, i: (bb, 0, g)))
    return pl.pallas_call(
        functools.partial(_gqa_kernel, n_parts=n_parts, tq=tq),
        out_shape=jax.ShapeDtypeStruct((b, sq, 512), BF16),
        grid=(b, GQA_KV, sq // tq),
        in_specs=in_specs,
        out_specs=pl.BlockSpec((1, tq, 256), lambda bb, g, i: (bb, i, g)),
        scratch_shapes=[pltpu.VMEM((GQA_GROUP * tq, LANES), F32), pltpu.VMEM((GQA_GROUP * tq, LANES), F32)],
        compiler_params=_cparams(("arbitrary", "arbitrary", "arbitrary")),
        name="gqa_attention",
    )(q, *k_parts, *v_parts)


def _diff_kernel(*refs, n_parts, tq, lam_init):
    q_ref = refs[0]
    k_refs = refs[1:1 + n_parts]
    v_refs = refs[1 + n_parts:1 + 2 * n_parts]
    lam_ref, sg_ref, o_ref, m_ref, acc_ref = refs[1 + 2 * n_parts:]
    q = q_ref[0]
    qmaps = [q[:, DIFF_QK * j:DIFF_QK * (j + 1)] for j in range(4)]
    k_slices = [slice(DIFF_QK * j, DIFF_QK * (j + 1)) for j in range(4)]
    pv_groups = [((0, 1), slice(0, LANES)), ((2, 3), slice(LANES, 2 * LANES))]
    _attention_sweeps(qmaps, k_slices, pv_groups, k_refs, v_refs, m_ref, acc_ref)

    lv = lam_ref[...]
    lam = (jnp.exp(jnp.sum(lv[0:1] * lv[1:2], axis=-1, keepdims=True))
           - jnp.exp(jnp.sum(lv[2:3] * lv[3:4], axis=-1, keepdims=True)) + lam_init)
    acc = acc_ref[...]
    for hh in range(2):
        a0 = acc[(2 * hh) * tq:(2 * hh + 1) * tq]
        a1 = acc[(2 * hh + 1) * tq:(2 * hh + 2) * tq]
        o = a0[:, :HEAD_V] / a0[:, HEAD_V:HEAD_V + 1] - lam * (a1[:, :HEAD_V] / a1[:, HEAD_V:HEAD_V + 1])
        ms = jnp.mean(o * o, axis=-1, keepdims=True)
        on = o * lax.rsqrt(ms + EPS) * sg_ref[...] * (1.0 - lam_init)
        o_ref[0, :, HEAD_V * hh:HEAD_V * (hh + 1)] = on.astype(BF16)


def _diff_attention(q, k_parts, v_parts, lam_vecs, subln_g, lam_init, tq):
    b, sq, _ = q.shape
    n_parts = len(k_parts)
    in_specs = [pl.BlockSpec((1, tq, LANES), lambda bb, p, i: (bb, i, p))]
    for kp in k_parts:
        in_specs.append(pl.BlockSpec((1, kp.shape[1], LANES, kp.shape[3]), lambda bb, p, i: (bb, 0, p, 0)))
    for vp in v_parts:
        in_specs.append(pl.BlockSpec((1, vp.shape[1], 2 * LANES), lambda bb, p, i: (bb, 0, p)))
    in_specs.append(pl.BlockSpec((4, DIFF_QK), lambda bb, p, i: (0, 0)))
    in_specs.append(pl.BlockSpec((1, HEAD_V), lambda bb, p, i: (0, 0)))
    return pl.pallas_call(
        functools.partial(_diff_kernel, n_parts=n_parts, tq=tq, lam_init=lam_init),
        out_shape=jax.ShapeDtypeStruct((b, sq, 256), BF16),
        grid=(b, DIFF_HEADS // 2, sq // tq),
        in_specs=in_specs,
        out_specs=pl.BlockSpec((1, tq, LANES), lambda bb, p, i: (bb, i, p)),
        scratch_shapes=[pltpu.VMEM((4 * tq, LANES), F32), pltpu.VMEM((4 * tq, LANES), F32)],
        compiler_params=_cparams(("arbitrary", "arbitrary", "arbitrary")),
        name="diff_attention",
    )(q, *k_parts, *v_parts, lam_vecs, subln_g)


CONV_HALO = 16
CONV_ROWS = 64


def _conv_kernel(gb_ref, prev_ref, next_ref, w_ref, b_ref, lg_ref, lb_ref, o_ref, u_ref, sh_ref, *, tm):
    i = pl.program_id(1)
    last = pl.num_programs(1) - 1
    ch = w_ref.shape[1]

    def glu(z):
        z = z.astype(F32)
        return z[:, :ch] * _sigmoid(z[:, ch:])

    u_ref[CONV_HALO:CONV_HALO + tm, :] = glu(gb_ref[0])
    u_ref[0:CONV_HALO, :] = jnp.where(i > 0, glu(prev_ref[0]), 0.0)
    u_ref[CONV_HALO + tm:2 * CONV_HALO + tm, :] = jnp.where(i < last, glu(next_ref[0]), 0.0)

    span = sh_ref.shape[1]
    for r in range(1, SUBLANES):
        sh_ref[r - 1] = u_ref[r:r + span, :]

    off = CONV_HALO - CONV_K // 2
    for r0 in range(0, tm, CONV_ROWS):
        acc = jnp.zeros((CONV_ROWS, ch), F32)
        for j in range(CONV_K):
            phase, base = (off + j) % SUBLANES, r0 + (off + j) // SUBLANES * SUBLANES
            taps = u_ref[base:base + CONV_ROWS, :] if phase == 0 else sh_ref[phase - 1, base:base + CONV_ROWS, :]
            acc = acc + taps * w_ref[j:j + 1, :]
        y = acc + b_ref[...]
        mu = jnp.mean(y, axis=-1, keepdims=True)
        yc = y - mu
        var = jnp.mean(yc * yc, axis=-1, keepdims=True)
        z = yc * lax.rsqrt(var + EPS) * lg_ref[...] + lb_ref[...]
        o_ref[0, r0:r0 + CONV_ROWS, :] = (z * _sigmoid(z)).astype(BF16)


def _conformer_conv(gb, w, bias, ln_g, ln_b, tm):
    b, s, two_ch = gb.shape
    ch = two_ch // 2
    hb = tm // CONV_HALO
    n_halo = s // CONV_HALO
    const2 = lambda bb, i: (0, 0)
    return pl.pallas_call(
        functools.partial(_conv_kernel, tm=tm),
        out_shape=jax.ShapeDtypeStruct((b, s, ch), BF16),
        grid=(b, s // tm),
        in_specs=[
            pl.BlockSpec((1, tm, two_ch), lambda bb, i: (bb, i, 0)),
            pl.BlockSpec((1, CONV_HALO, two_ch), lambda bb, i: (bb, jnp.maximum(i * hb - 1, 0), 0)),
            pl.BlockSpec((1, CONV_HALO, two_ch), lambda bb, i: (bb, jnp.minimum((i + 1) * hb, n_halo - 1), 0)),
            pl.BlockSpec((CONV_K, ch), const2), pl.BlockSpec((1, ch), const2),
            pl.BlockSpec((1, ch), const2), pl.BlockSpec((1, ch), const2),
        ],
        out_specs=pl.BlockSpec((1, tm, ch), lambda bb, i: (bb, i, 0)),
        scratch_shapes=[pltpu.VMEM((tm + 2 * CONV_HALO, ch), F32),
                        pltpu.VMEM((SUBLANES - 1, tm + 2 * CONV_HALO - SUBLANES, ch), F32)],
        compiler_params=_cparams(("arbitrary", "arbitrary")),
        name="conformer_conv",
    )(gb, gb, gb, w, bias, ln_g, ln_b)


def _merge_kernel(*refs, with_router):
    oa_ref, ob_ref, oc_ref, x_ref, gate_ref, shift_ref, scale_ref, g2_ref, w_ref = refs[:9]
    if with_router:
        rw_ref, xo_ref, h2_ref, lg_ref = refs[9:]
    else:
        xo_ref, h2_ref = refs[9:]
    tm = x_ref.shape[1]
    wa = oa_ref.shape[2]
    wb = wa + ob_ref.shape[2]
    y = _dot(oa_ref[0], w_ref[0:wa, :]) + _dot(ob_ref[0], w_ref[wa:wb, :]) + _dot(oc_ref[0], w_ref[wb:, :])
    xn = x_ref[0] + gate_ref[0] * y
    xo_ref[0] = xn
    ms = jnp.mean(xn * xn, axis=-1, keepdims=True)
    h2 = xn * lax.rsqrt(ms + EPS) * g2_ref[...] * (1.0 + scale_ref[0]) + shift_ref[0]
    if with_router:
        _store_row_tiles(h2_ref, h2, tm)
        lg_ref[...] = lax.dot_general(rw_ref[...], h2, (((1,), (1,)), ((), ())),
                                      preferred_element_type=F32, precision=lax.Precision.HIGHEST)
    else:
        h2_ref[0] = h2.astype(BF16)


def _merge(oa, ob, oc, x, gate, shift, scale, g2, w_out, router_wt, tm):
    b, s, d = x.shape
    row = lambda bb, i: (bb, i, 0)
    per_b = lambda bb, i: (bb, 0, 0)
    const2 = lambda bb, i: (0, 0)
    nt = s // tm
    in_specs = [
        pl.BlockSpec((1, tm, oa.shape[2]), row), pl.BlockSpec((1, tm, ob.shape[2]), row),
        pl.BlockSpec((1, tm, oc.shape[2]), row), pl.BlockSpec((1, tm, d), row),
        pl.BlockSpec((1, 1, d), per_b), pl.BlockSpec((1, 1, d), per_b), pl.BlockSpec((1, 1, d), per_b),
        pl.BlockSpec((1, d), const2), pl.BlockSpec((d, d), const2),
    ]
    out_shape = [jax.ShapeDtypeStruct((b, s, d), F32)]
    out_specs = [pl.BlockSpec((1, tm, d), row)]
    args = [oa, ob, oc, x, gate, shift, scale, g2, w_out]
    with_router = router_wt is not None
    if not with_router:
        out_shape.append(jax.ShapeDtypeStruct((b, s, d), BF16))
        out_specs.append(pl.BlockSpec((1, tm, d), row))
    else:
        out_shape.append(jax.ShapeDtypeStruct((b * s * ROW_TILE, LANES), F32))
        out_specs.append(pl.BlockSpec((tm * ROW_TILE, LANES), lambda bb, i: (bb * nt + i, 0)))
        in_specs.append(pl.BlockSpec((N_EXPERTS, d), const2))
        out_shape.append(jax.ShapeDtypeStruct((N_EXPERTS, b * s), F32))
        out_specs.append(pl.BlockSpec((N_EXPERTS, tm), lambda bb, i: (0, bb * nt + i)))
        args.append(router_wt)
    return pl.pallas_call(
        functools.partial(_merge_kernel, with_router=with_router),
        out_shape=tuple(out_shape), grid=(b, nt), in_specs=in_specs, out_specs=tuple(out_specs),
        compiler_params=_cparams(("arbitrary", "arbitrary")), name="merge_heads",
    )(*args)


def _ffn_kernel(h_ref, x_ref, gate_ref, wg_ref, wu_ref, wd_ref, o_ref, *, tf):
    h = h_ref[...]
    ff = wg_ref.shape[1]
    acc = jnp.zeros(x_ref.shape, F32)
    for f in range(0, ff, tf):
        g = _dot(h, wg_ref[:, f:f + tf])
        u = _dot(h, wu_ref[:, f:f + tf])
        a = (g * _sigmoid(g) * u).astype(BF16)
        acc = acc + _dot(a, wd_ref[f:f + tf, :])
    o_ref[...] = x_ref[...] + gate_ref[0] * acc


def _dense_ffn(h2, x, gate, wg, wu, wd, tm):
    n, d = x.shape
    ff = wg.shape[1]
    s = n // gate.shape[0]
    row = lambda i: (i, 0)
    const2 = lambda i: (0, 0)
    resident = pl.Buffered(1)
    return pl.pallas_call(
        functools.partial(_ffn_kernel, tf=MXU_DIM),
        out_shape=jax.ShapeDtypeStruct((n, d), F32),
        grid=(n // tm,),
        in_specs=[
            pl.BlockSpec((tm, d), row), pl.BlockSpec((tm, d), row),
            pl.BlockSpec((1, 1, d), lambda i: ((i * tm) // s, 0, 0)),
            pl.BlockSpec((d, ff), const2, pipeline_mode=resident),
            pl.BlockSpec((d, ff), const2, pipeline_mode=resident),
            pl.BlockSpec((ff, d), const2, pipeline_mode=resident),
        ],
        out_specs=pl.BlockSpec((tm, d), row),
        compiler_params=_cparams(("arbitrary",)), name="dense_ffn",
    )(h2, x, gate, wg, wu, wd)


def _top2(lg):
    sub = lax.broadcasted_iota(I32, lg.shape, 0)
    l1 = jnp.max(lg, axis=0, keepdims=True)
    i1 = jnp.min(jnp.where(lg == l1, sub, N_EXPERTS), axis=0, keepdims=True)
    m1 = sub == i1
    lg2 = jnp.where(m1, -jnp.inf, lg)
    l2 = jnp.max(lg2, axis=0, keepdims=True)
    i2 = jnp.min(jnp.where(lg2 == l2, sub, N_EXPERTS), axis=0, keepdims=True)
    m2 = sub == i2
    return l1, l2, m1, m2


def _sublane_cumsum(x):
    sub = lax.broadcasted_iota(I32, x.shape, 0)
    for sh in (1, 2, 4):
        x = x + jnp.where(sub >= sh, pltpu.roll(x, sh, 0), 0.0)
    return x


def _route_kernel(lg_ref, tri_ref, dest_ref, gates_ref, be_ref, base_ref, start_ref, *, block_rows):
    phase = pl.program_id(0)
    j = pl.program_id(1)
    l1, l2, m1, m2 = _top2(lg_ref[...])
    e = jnp.where(m1 | m2, 1.0, 0.0).astype(F32)
    cnt = jnp.sum(e, axis=1, keepdims=True)

    @pl.when((phase == 0) & (j == 0))
    def _():
        base_ref[...] = jnp.zeros(base_ref.shape, F32)

    @pl.when((phase == 1) & (j == 0))
    def _():
        counts = base_ref[...]
        nblk = jnp.floor((counts + (block_rows - 1)) * (1.0 / block_rows))
        end_blk = _sublane_cumsum(nblk)
        start_ref[...] = (end_blk - nblk) * block_rows
        blk = lax.broadcasted_iota(I32, be_ref.shape, 1).astype(F32)
        owner = jnp.sum(jnp.where(end_blk[:, :1] <= blk, 1.0, 0.0), axis=0, keepdims=True)
        be_ref[...] = jnp.broadcast_to(jnp.minimum(owner, N_EXPERTS - 1.0), be_ref.shape).astype(I32)
        base_ref[...] = jnp.zeros(base_ref.shape, F32)

    @pl.when(phase == 1)
    def _():
        prefix = _dot(e.astype(BF16), tri_ref[...]) + base_ref[:, :1] + start_ref[:, :1]
        d1 = jnp.sum(jnp.where(m1, prefix, 0.0), axis=0, keepdims=True)
        d2 = jnp.sum(jnp.where(m2, prefix, 0.0), axis=0, keepdims=True)
        sub = lax.broadcasted_iota(I32, dest_ref.shape, 0)
        dest_ref[...] = jnp.where(sub == 0, d1, jnp.where(sub == 1, d2, 0.0)).astype(I32)
        ex = jnp.exp(l2 - l1)
        g1 = 1.0 / (1.0 + ex)
        g2 = ex / (1.0 + ex)
        half = lax.broadcasted_iota(I32, (LANES, lg_ref.shape[1]), 0) < LANES // 2
        gates_ref[...] = jnp.where(half, g1, g2).T

    base_ref[...] = base_ref[...] + cnt


def _route(logits_t, block_rows, n_blocks_pad, tr):
    n = logits_t.shape[1]
    tri = jnp.asarray(np.triu(np.ones((tr, tr), np.float32), k=1), BF16)
    return pl.pallas_call(
        functools.partial(_route_kernel, block_rows=block_rows),
        out_shape=(jax.ShapeDtypeStruct((N_EXPERTS, n), I32),
                   jax.ShapeDtypeStruct((n, LANES), F32),
                   jax.ShapeDtypeStruct((N_EXPERTS, n_blocks_pad), I32)),
        grid=(2, n // tr),
        in_specs=[pl.BlockSpec((N_EXPERTS, tr), lambda p, j: (0, j)),
                  pl.BlockSpec((tr, tr), lambda p, j: (0, 0))],
        out_specs=(pl.BlockSpec((N_EXPERTS, tr), lambda p, j: (0, j * p)),
                   pl.BlockSpec((tr, LANES), lambda p, j: (j * p, 0)),
                   pl.BlockSpec((N_EXPERTS, n_blocks_pad), lambda p, j: (0, 0))),
        scratch_shapes=[pltpu.VMEM((N_EXPERTS, LANES), F32), pltpu.VMEM((N_EXPERTS, LANES), F32)],
        compiler_params=_cparams(("arbitrary", "arbitrary")), name="moe_route",
    )(logits_t, tri)


def _row_copy(src_hbm, src_row, dst_hbm, dst_row, sem):
    src = pl.ds(pl.multiple_of(src_row * ROW_TILE, ROW_TILE), ROW_TILE)
    dst = pl.ds(pl.multiple_of(dst_row * ROW_TILE, ROW_TILE), ROW_TILE)
    return pltpu.make_async_copy(src_hbm.at[src], dst_hbm.at[dst], sem)


def _scatter_kernel(d1_ref, d2_ref, src_ref, init_hbm, out_hbm, sem, *, rows):
    del init_hbm

    def start(r, c):
        _row_copy(src_ref, r, out_hbm, d1_ref[0, 0, r], sem).start(priority=0)
        _row_copy(src_ref, r, out_hbm, d2_ref[0, 0, r], sem).start(priority=1)
        return c

    lax.fori_loop(0, rows, start, 0, unroll=DMA_UNROLL)
    for _ in range(2):
        pltpu.make_async_copy(src_ref, out_hbm.at[pl.ds(0, rows * ROW_TILE)], sem).wait()


def _scatter_rows(src, d1, d2, total_rows, rows):
    n = d1.shape[0]
    idx_spec = pl.BlockSpec((1, 1, rows), lambda i: (i, 0, 0), memory_space=pltpu.SMEM)
    any_spec = pl.BlockSpec(memory_space=pl.ANY)
    return pl.pallas_call(
        functools.partial(_scatter_kernel, rows=rows),
        out_shape=jax.ShapeDtypeStruct((total_rows * ROW_TILE, LANES), src.dtype),
        grid=(n // rows,),
        in_specs=[idx_spec, idx_spec, pl.BlockSpec((rows * ROW_TILE, LANES), lambda i: (i, 0)), any_spec],
        out_specs=any_spec,
        scratch_shapes=[pltpu.SemaphoreType.DMA(())],
        input_output_aliases={3: 0},
        compiler_params=pltpu.CompilerParams(dimension_semantics=("arbitrary",), has_side_effects=True),
        name="moe_scatter_rows",
    )(d1.reshape(n // rows, 1, rows), d2.reshape(n // rows, 1, rows), src,
      jnp.zeros((total_rows * ROW_TILE, LANES), src.dtype))


def _expert_kernel(be_ref, x_ref, wg_ref, wu_ref, wd_ref, o_ref, *, block_rows, tf):
    del be_ref
    x = _load_row_tiles(x_ref, block_rows).astype(BF16)
    ff = wg_ref.shape[2]
    acc = jnp.zeros((block_rows, wd_ref.shape[2]), F32)
    for f in range(0, ff, tf):
        g = _dot(x, wg_ref[0, :, f:f + tf])
        u = _dot(x, wu_ref[0, :, f:f + tf])
        a = (g * _sigmoid(g) * u).astype(BF16)
        acc = acc + _dot(a, wd_ref[0, f:f + tf, :])
    _store_row_tiles(o_ref, acc, block_rows)


def _expert_ffn(xb, blk_expert, wg, wu, wd, block_rows):
    d, ff = wg.shape[1], wg.shape[2]
    rows = xb.shape[0] // ROW_TILE
    resident = pl.Buffered(1)
    grid_spec = pltpu.PrefetchScalarGridSpec(
        num_scalar_prefetch=1,
        grid=(rows // block_rows,),
        in_specs=[
            pl.BlockSpec((block_rows * ROW_TILE, LANES), lambda i, be: (i, 0)),
            pl.BlockSpec((1, d, ff), lambda i, be: (be[i], 0, 0), pipeline_mode=resident),
            pl.BlockSpec((1, d, ff), lambda i, be: (be[i], 0, 0), pipeline_mode=resident),
            pl.BlockSpec((1, ff, d), lambda i, be: (be[i], 0, 0), pipeline_mode=resident),
        ],
        out_specs=pl.BlockSpec((block_rows * ROW_TILE, LANES), lambda i, be: (i, 0)),
    )
    return pl.pallas_call(
        functools.partial(_expert_kernel, block_rows=block_rows, tf=MXU_DIM),
        out_shape=jax.ShapeDtypeStruct(xb.shape, F32), grid_spec=grid_spec,
        compiler_params=_cparams(("arbitrary",)), name="moe_expert_ffn",
    )(blk_expert, xb, wg, wu, wd)


def _combine_kernel(d1_ref, d2_ref, x_ref, yb_hbm, gates_ref, gate_ref, fg_ref, o_ref, y1_ref, y2_ref, sem):
    tm = x_ref.shape[0]

    def start(r, c):
        _row_copy(yb_hbm, d1_ref[0, 0, r], y1_ref, r, sem).start(priority=0)
        _row_copy(yb_hbm, d2_ref[0, 0, r], y2_ref, r, sem).start(priority=1)
        return c

    lax.fori_loop(0, tm, start, 0, unroll=DMA_UNROLL)
    for y_ref in (y1_ref, y2_ref):
        pltpu.make_async_copy(yb_hbm.at[pl.ds(0, tm * ROW_TILE)], y_ref, sem).wait()

    gts = gates_ref[...]
    y = (gts[:, 0:1] * _load_row_tiles(y1_ref, tm)
         + gts[:, LANES // 2:LANES // 2 + 1] * _load_row_tiles(y2_ref, tm))
    xn = x_ref[...] + gate_ref[0] * y
    ms = jnp.mean(xn * xn, axis=-1, keepdims=True)
    o_ref[...] = xn * lax.rsqrt(ms + EPS) * fg_ref[...]


def _combine_final(x, yb, d1, d2, gates, gate, final_g, tm):
    n, d = x.shape
    s = n // gate.shape[0]
    row = lambda i: (i, 0)
    idx_spec = pl.BlockSpec((1, 1, tm), lambda i: (i, 0, 0), memory_space=pltpu.SMEM)
    return pl.pallas_call(
        _combine_kernel,
        out_shape=jax.ShapeDtypeStruct((n, d), F32),
        grid=(n // tm,),
        in_specs=[idx_spec, idx_spec,
                  pl.BlockSpec((tm, d), row),
                  pl.BlockSpec(memory_space=pl.ANY),
                  pl.BlockSpec((tm, LANES), row),
                  pl.BlockSpec((1, 1, d), lambda i: ((i * tm) // s, 0, 0)),
                  pl.BlockSpec((1, d), lambda i: (0, 0))],
        out_specs=pl.BlockSpec((tm, d), row),
        scratch_shapes=[pltpu.VMEM((tm * ROW_TILE, LANES), F32), pltpu.VMEM((tm * ROW_TILE, LANES), F32),
                        pltpu.SemaphoreType.DMA(())],
        compiler_params=_cparams(("arbitrary",)), name="moe_combine_final",
    )(d1.reshape(n // tm, 1, tm), d2.reshape(n // tm, 1, tm), x, yb, gates, gate, final_g)


def _rope_tables(s, dim):
    half = dim // 2
    t = jnp.arange(s)
    inv = 1.0 / (ROPE_THETA ** (jnp.arange(0, half, 2, dtype=F32) / half))
    ang_r = (t // GRID_W).astype(F32)[:, None] * inv
    ang_c = (t % GRID_W).astype(F32)[:, None] * inv
    ang = jnp.concatenate([ang_r, ang_r, ang_c, ang_c], axis=-1)
    reps = LANES // dim
    return jnp.tile(jnp.cos(ang), (1, reps)), jnp.tile(jnp.sin(ang), (1, reps))


def _rotate_matrix(dim):
    q = dim // 4
    p = np.zeros((MXU_DIM, MXU_DIM), np.float32)
    for j in range(MXU_DIM):
        if (j % (2 * q)) < q:
            p[j + q, j] = -1.0
        else:
            p[j - q, j] = 1.0
    return jnp.asarray(p, BF16)


def _head_mean_matrix():
    m = np.kron(np.eye(MXU_DIM // HEAD_V, dtype=np.float32), np.full((HEAD_V, HEAD_V), 1.0 / HEAD_V, np.float32))
    return jnp.asarray(m, BF16)


def _widen_values(w, heads):
    d = w.shape[0]
    w = w.reshape(d, heads, HEAD_V)
    return jnp.concatenate([w, jnp.zeros_like(w)], axis=-1).reshape(d, heads * LANES)


def _widen_in_proj(w):
    qa, ka, va, gb, qc, kc, vc = jnp.split(w, [256, 512, 768, 1280, 1792, 1920], axis=1)
    return jnp.concatenate([qa, ka, _widen_values(va, DIFF_HEADS), gb, qc, kc, _widen_values(vc, GQA_KV)],
                           axis=1).astype(BF16)


def kernel(x, c, ctx, c_ctx, ada_w, ada_b, norm1_g, norm2_g, w_in, w_out, lam_q1, lam_k1, lam_q2, lam_k2,
           diff_subln_g, conv_w, conv_b, conv_ln_g, conv_ln_b, q_norm_g, k_norm_g, ffn_gate, ffn_up, ffn_down,
           router_w, moe_gate, moe_up, moe_down, final_g):
    b, s, d = x.shape
    sc = ctx.shape[1]
    depth = ada_w.shape[0]
    n = b * s
    assert depth % 2 == 0, "the final RMSNorm is fused into the MoE combine of the last (odd) layer"

    tm = min(512, s)
    tmc = min(512, sc)
    tq = min(512, s)
    tq_diff = min(1024, s)
    tqc = min(256, sc)

    tabs_x = _rope_tables(s, DIFF_QK) + _rope_tables(s, HEAD_V)
    ones_c, zeros_c = jnp.ones((sc, LANES), F32), jnp.zeros((sc, LANES), F32)
    tabs_c = (ones_c, zeros_c, ones_c, zeros_c)
    mats = (_rotate_matrix(DIFF_QK), _rotate_matrix(HEAD_V), _head_mean_matrix())

    cc = jnp.zeros((16, d), F32).at[:b].set(c).at[b].set(c_ctx)

    for i in range(depth):
        last = i == depth - 1
        lam_init = 0.8 - 0.6 * math.exp(-0.3 * i)
        mod_all = _ada_mod(cc, ada_w[i], ada_b[i])
        mod = mod_all[:b].reshape(b, 6, 1, d)
        modc = jnp.broadcast_to(mod_all[b].reshape(1, 6, 1, d), (b, 6, 1, d))

        w_aug = _widen_in_proj(w_in[i])
        g1 = norm1_g[i].reshape(1, d)
        qg = jnp.tile(q_norm_g[i], GQA_HEADS).reshape(1, -1)
        kg = jnp.tile(k_norm_g[i], GQA_KV).reshape(1, -1)
        lam_vecs = jnp.stack([lam_q1[i], lam_k1[i], lam_q2[i], lam_k2[i]]).astype(F32)
        subln = diff_subln_g[i].reshape(1, HEAD_V)
        conv_args = (conv_w[i], conv_b[i].reshape(1, -1), conv_ln_g[i].reshape(1, -1), conv_ln_b[i].reshape(1, -1))
        w_o = w_out[i].astype(BF16)
        g2 = norm2_g[i].reshape(1, d)

        qa, kat, va, gb, qc, kct, vc = _in_projection(x, mod[:, 0], mod[:, 1], g1, w_aug, tabs_x, mats, qg, kg, tm)
        qa_x, kat_x, va_x, gb_x, qc_x, kct_x, vc_x = _in_projection(
            ctx, modc[:, 0], modc[:, 1], g1, w_aug, tabs_c, mats, qg, kg, tmc)

        oa = _diff_attention(qa, [kat, kat_x], [va, va_x], lam_vecs, subln, lam_init, tq_diff)
        ob = _conformer_conv(gb, *conv_args, tm)
        oc = _gqa_attention(qc, [kct, kct_x], [vc, vc_x], tq)

        j = i // 2
        if i % 2 == 0:
            x, h2 = _merge(oa, ob, oc, x, mod[:, 2], mod[:, 3], mod[:, 4], g2, w_o, None, tm)
            wg, wu, wd = ffn_gate[j].astype(BF16), ffn_up[j].astype(BF16), ffn_down[j].astype(BF16)
            x = _dense_ffn(h2.reshape(n, d), x.reshape(n, d), mod[:, 5], wg, wu, wd, tm).reshape(b, s, d)
        else:
            rwt = router_w[j].T.astype(F32)
            x, h2, logits_t = _merge(oa, ob, oc, x, mod[:, 2], mod[:, 3], mod[:, 4], g2, w_o, rwt, tm)
            block_rows = 512 if n >= 8192 else 256
            n_blocks = (2 * n) // block_rows + N_EXPERTS
            n_blocks_pad = -(-n_blocks // LANES) * LANES
            dest, gates, blk_e = _route(logits_t, block_rows, n_blocks_pad, min(512, n))
            xb = _scatter_rows(h2, dest[0], dest[1], n_blocks * block_rows, min(256, n))
            yb = _expert_ffn(xb, blk_e[0, :n_blocks], moe_gate[j].astype(BF16), moe_up[j].astype(BF16),
                             moe_down[j].astype(BF16), block_rows)
            assert last
            x = _combine_final(x.reshape(n, d), yb, dest[0], dest[1], gates, mod[:, 5], final_g.reshape(1, d),
                               tm).reshape(b, s, d)

        if not last:
            oa_x = _diff_attention(qa_x, [kat_x], [va_x], lam_vecs, subln, lam_init, tqc)
            ob_x = _conformer_conv(gb_x, *conv_args, tmc)
            oc_x = _gqa_attention(qc_x, [kct_x], [vc_x], tqc)
            assert i % 2 == 0, "context tokens only ever pass through dense channel mixers"
            ctx, hc2 = _merge(oa_x, ob_x, oc_x, ctx, modc[:, 2], modc[:, 3], modc[:, 4], g2, w_o, None, tmc)
            ctx = _dense_ffn(hc2.reshape(b * sc, d), ctx.reshape(b * sc, d), modc[:, 5], wg, wu, wd,
                             tmc).reshape(b, sc, d)

    return x
```

```python
import functools
import math

import numpy as np
import jax
import jax.numpy as jnp
from jax import lax
from jax.experimental import pallas as pl
from jax.experimental.pallas import tpu as pltpu

F32 = jnp.float32
BF16 = jnp.bfloat16
I32 = jnp.int32

EPS = 1e-6
ROPE_THETA = 10000.0
GRID_W = 64

DIFF_HEADS = 4
DIFF_QK = 32
HEAD_V = 64
GQA_HEADS = 8
GQA_KV = 2
GQA_GROUP = GQA_HEADS // GQA_KV
CONV_K = 31
N_EXPERTS = 8
LOG2E = math.log2(math.e)

LANES = 128
SUBLANES = 8
MXU_DIM = 256
VMEM_LIMIT = 52 * 1024 * 1024
NEG_BIG = -1e30
P_DTYPE = jnp.float8_e4m3fn
P_MAX_EXP = 8.5
P_HEADROOM_FIXED = 4.0
P_HEADROOM_RUNNING = 8.0
ATTN_UNROLL = 4
DMA_UNROLL = 8

C_QA, C_KA, C_VA, C_GB, C_QC, C_KC, C_VC, C_END = 0, 256, 512, 1024, 1536, 2048, 2176, 2432


def _cparams(semantics):
    return pltpu.CompilerParams(dimension_semantics=semantics, vmem_limit_bytes=VMEM_LIMIT)


def _dot(a, b):
    return jnp.dot(a, b, preferred_element_type=F32)


def _sigmoid(z):
    return 1.0 / (1.0 + jnp.exp(-z))


ROW_TILE = 8


def _store_row_tiles(ref, val, rows):
    for a in range(ROW_TILE):
        ref[pl.ds(a, rows, stride=ROW_TILE), :] = val[:, a * LANES:(a + 1) * LANES]


def _load_row_tiles(ref, rows):
    return jnp.concatenate([ref[pl.ds(a, rows, stride=ROW_TILE), :] for a in range(ROW_TILE)], axis=1)


def _mod_kernel(c_ref, w_ref, b_ref, o_ref):
    c = c_ref[...]
    s = c * _sigmoid(c)
    o_ref[...] = jnp.dot(s, w_ref[...], preferred_element_type=F32, precision=lax.Precision.HIGHEST) + b_ref[...]


def _ada_mod(cc, w, b):
    rows, d = cc.shape
    n = w.shape[1]
    tn = d
    return pl.pallas_call(
        _mod_kernel,
        out_shape=jax.ShapeDtypeStruct((rows, n), F32),
        grid=(n // tn,),
        in_specs=[pl.BlockSpec((rows, d), lambda j: (0, 0)),
                  pl.BlockSpec((d, tn), lambda j: (0, j)),
                  pl.BlockSpec((1, tn), lambda j: (0, j))],
        out_specs=pl.BlockSpec((rows, tn), lambda j: (0, j)),
        compiler_params=_cparams(("arbitrary",)),
        name="ada_mod",
    )(cc, w, b.reshape(1, n))


def _inproj_kernel(x_ref, shift_ref, scale_ref, g_ref, w_ref, cosa_ref, sina_ref, cosc_ref, sinc_ref,
                   pa_ref, pc_ref, hm_ref, qg_ref, kg_ref,
                   qa_o, kat_o, va_o, gb_o, qc_o, kct_o, vc_o, *, qa_scale, qc_scale):
    x = x_ref[0]
    ms = jnp.mean(x * x, axis=-1, keepdims=True)
    h = x * lax.rsqrt(ms + EPS) * g_ref[...]
    h = h * (1.0 + scale_ref[0]) + shift_ref[0]
    hb = h.astype(BF16)

    def proj(lo, hi):
        return _dot(hb, w_ref[:, lo:hi])

    def blockmat(y, m_ref):
        yb = y.astype(BF16)
        w = y.shape[1]
        if w == LANES:
            return _dot(yb, m_ref[:LANES, :LANES])
        return jnp.concatenate([_dot(yb[:, c:c + MXU_DIM], m_ref[...]) for c in range(0, w, MXU_DIM)], axis=1)

    def rope(y, cos, sin, p_ref):
        reps = y.shape[1] // LANES
        cos = jnp.tile(cos, (1, reps))
        sin = jnp.tile(sin, (1, reps))
        return y * cos + blockmat(y, p_ref) * sin

    def ones_col(width):
        lane = lax.broadcasted_iota(I32, (1, width), 1)
        return jnp.where(lane % LANES == HEAD_V, 1.0, 0.0).astype(F32)

    cosa, sina, cosc, sinc = cosa_ref[...], sina_ref[...], cosc_ref[...], sinc_ref[...]

    qa_o[0] = (rope(proj(C_QA, C_KA), cosa, sina, pa_ref) * qa_scale).astype(BF16)
    kat_o[0, 0] = rope(proj(C_KA, C_VA), cosa, sina, pa_ref).T.astype(BF16)
    va_o[0] = (proj(C_VA, C_GB) + ones_col(C_GB - C_VA)).astype(P_DTYPE)
    gb_o[0] = proj(C_GB, C_QC).astype(BF16)

    y = proj(C_QC, C_KC)
    yn = y * lax.rsqrt(blockmat(y * y, hm_ref) + EPS) * qg_ref[...]
    qc_o[0] = (rope(yn, cosc, sinc, pc_ref) * qc_scale).astype(BF16)

    y = proj(C_KC, C_VC)
    yn = y * lax.rsqrt(blockmat(y * y, hm_ref) + EPS) * kg_ref[...]
    kct_o[0, 0] = rope(yn, cosc, sinc, pc_ref).T.astype(BF16)

    vc_o[0] = (proj(C_VC, C_END) + ones_col(C_END - C_VC)).astype(P_DTYPE)


def _in_projection(x, shift, scale, g1, w_aug, tabs, mats, qg, kg, tm):
    b, s, d = x.shape
    nt = s // tm
    cosa, sina, cosc, sinc = tabs
    pa, pc, hm = mats
    row = lambda bb, i: (bb, i, 0)
    const2 = lambda bb, i: (0, 0)
    per_b = lambda bb, i: (bb, 0, 0)
    tab = lambda bb, i: (i, 0)
    kern = functools.partial(_inproj_kernel, qa_scale=DIFF_QK ** -0.5 * LOG2E, qc_scale=HEAD_V ** -0.5 * LOG2E)
    out_shape = (
        jax.ShapeDtypeStruct((b, s, 256), BF16),
        jax.ShapeDtypeStruct((b, nt, 256, tm), BF16),
        jax.ShapeDtypeStruct((b, s, 512), P_DTYPE),
        jax.ShapeDtypeStruct((b, s, 512), BF16),
        jax.ShapeDtypeStruct((b, s, 512), BF16),
        jax.ShapeDtypeStruct((b, nt, 128, tm), BF16),
        jax.ShapeDtypeStruct((b, s, 256), P_DTYPE),
    )
    out_specs = (
        pl.BlockSpec((1, tm, 256), row),
        pl.BlockSpec((1, 1, 256, tm), lambda bb, i: (bb, i, 0, 0)),
        pl.BlockSpec((1, tm, 512), row),
        pl.BlockSpec((1, tm, 512), row),
        pl.BlockSpec((1, tm, 512), row),
        pl.BlockSpec((1, 1, 128, tm), lambda bb, i: (bb, i, 0, 0)),
        pl.BlockSpec((1, tm, 256), row),
    )
    in_specs = [
        pl.BlockSpec((1, tm, d), row),
        pl.BlockSpec((1, 1, d), per_b),
        pl.BlockSpec((1, 1, d), per_b),
        pl.BlockSpec((1, d), const2),
        pl.BlockSpec((d, C_END), const2),
        pl.BlockSpec((tm, LANES), tab), pl.BlockSpec((tm, LANES), tab),
        pl.BlockSpec((tm, LANES), tab), pl.BlockSpec((tm, LANES), tab),
        pl.BlockSpec((MXU_DIM, MXU_DIM), const2), pl.BlockSpec((MXU_DIM, MXU_DIM), const2),
        pl.BlockSpec((MXU_DIM, MXU_DIM), const2),
        pl.BlockSpec((1, 512), const2), pl.BlockSpec((1, 128), const2),
    ]
    return pl.pallas_call(
        kern, out_shape=out_shape, grid=(b, nt), in_specs=in_specs, out_specs=out_specs,
        compiler_params=_cparams(("arbitrary", "arbitrary")), name="in_projection",
    )(x, shift, scale, g1, w_aug, cosa, sina, cosc, sinc, pa, pc, hm, qg, kg)


def _attention_sweeps(qms, k_slices, pv_groups, k_refs, v_refs, m_ref, smax_ref, acc_ref):
    r = qms[0].shape[0]

    def scores(j, kc):
        return _dot(qms[j], kc[k_slices[j], :])

    def sweep(running_max):
        acc_ref[...] = jnp.zeros(acc_ref.shape, F32)
        if running_max:
            m_ref[...] = jnp.full(m_ref.shape, NEG_BIG, F32)
        else:
            smax_ref[...] = jnp.full(smax_ref.shape, NEG_BIG, F32)
            kc0 = k_refs[0][0, 0][:, :MXU_DIM]
            for j in range(len(qms)):
                m0 = jnp.max(scores(j, kc0), axis=-1, keepdims=True)
                m_ref[j * r:(j + 1) * r, :] = jnp.broadcast_to(m0 - P_HEADROOM_FIXED, (r, LANES))

        for k_ref, v_ref in zip(k_refs, v_refs):
            n_chunks, tk = k_ref.shape[1], k_ref.shape[3]

            def body(c, carry, k_ref=k_ref, v_ref=v_ref, tk=tk):
                kc = k_ref[0, c]
                vc = v_ref[0, pl.ds(pl.multiple_of(c * tk, tk), tk), :]
                for ids, v_lanes in pv_groups:
                    ps, alphas = [], []
                    for j in ids:
                        rows = slice(j * r, (j + 1) * r)
                        s = scores(j, kc)
                        m = m_ref[rows, :]
                        if running_max:
                            m_new = jnp.maximum(m, jnp.max(s, axis=-1, keepdims=True) - P_HEADROOM_RUNNING)
                            m_ref[rows, :] = m_new
                            alphas.append(jnp.exp2(m - m_new))
                            m = m_new
                        else:
                            cm = functools.reduce(jnp.maximum, [s[:, l:l + LANES] for l in range(0, tk, LANES)])
                            smax_ref[rows, :] = jnp.maximum(smax_ref[rows, :], cm)
                        ps.append(jnp.exp2(s - jnp.tile(m, (1, tk // LANES))).astype(P_DTYPE))
                    rows = slice(ids[0] * r, (ids[-1] + 1) * r)
                    pv = _dot(jnp.concatenate(ps, axis=0), vc[:, v_lanes])
                    if running_max:
                        acc_ref[rows, :] = acc_ref[rows, :] * jnp.concatenate(alphas, axis=0) + pv
                    else:
                        acc_ref[rows, :] += pv
                return carry

            unroll = 1 if running_max else math.gcd(n_chunks, ATTN_UNROLL)
            lax.fori_loop(0, n_chunks, body, 0, unroll=unroll)

    sweep(False)
    top_exp = jnp.max(smax_ref[...] - m_ref[...])

    @pl.when(jnp.logical_not(top_exp <= P_MAX_EXP))
    def _():
        sweep(True)


def _gqa_kernel(*refs, n_parts, tq):
    q_ref = refs[0]
    k_refs = refs[1:1 + n_parts]
    v_refs = refs[1 + n_parts:1 + 2 * n_parts]
    o_ref, m_ref, smax_ref, acc_ref = refs[1 + 2 * n_parts:]
    q = q_ref[0]
    qs = jnp.concatenate([q[:, HEAD_V * j:HEAD_V * (j + 1)] for j in range(GQA_GROUP)], axis=0)
    _attention_sweeps([qs], [slice(0, HEAD_V)], [((0,), slice(0, LANES))], k_refs, v_refs,
                      m_ref, smax_ref, acc_ref)
    acc = acc_ref[...]
    o = acc[:, :HEAD_V] / acc[:, HEAD_V:HEAD_V + 1]
    for j in range(GQA_GROUP):
        o_ref[0, :, HEAD_V * j:HEAD_V * (j + 1)] = o[j * tq:(j + 1) * tq].astype(BF16)


def _gqa_attention(q, k_parts, v_parts, tq):
    b, sq, _ = q.shape
    n_parts = len(k_parts)
    in_specs = [pl.BlockSpec((1, tq, 256), lambda bb, g, i: (bb, i, g))]
    for kp in k_parts:
        in_specs.append(pl.BlockSpec((1, kp.shape[1], HEAD_V, kp.shape[3]), lambda bb, g, i: (bb, 0, g, 0)))
    for vp in v_parts:
        in_specs.append(pl.BlockSpec((1, vp.shape[1], LANES), lambda bb, g, i: (bb, 0, g)))
    return pl.pallas_call(
        functools.partial(_gqa_kernel, n_parts=n_parts, tq=tq),
        out_shape=jax.ShapeDtypeStruct((b, sq, 512), BF16),
        grid=(b, GQA_KV, sq // tq),
        in_specs=in_specs,
        out_specs=pl.BlockSpec((1, tq, 256), lambda bb, g, i: (bb, i, g)),
        scratch_shapes=[pltpu.VMEM((GQA_GROUP * tq, LANES), F32)] * 3,
        compiler_params=_cparams(("arbitrary", "arbitrary", "arbitrary")),
        name="gqa_attention",
    )(q, *k_parts, *v_parts)


def _diff_kernel(*refs, n_parts, tq, lam_init):
    q_ref = refs[0]
    k_refs = refs[1:1 + n_parts]
    v_refs = refs[1 + n_parts:1 + 2 * n_parts]
    lam_ref, sg_ref, o_ref, m_ref, smax_ref, acc_ref = refs[1 + 2 * n_parts:]
    q = q_ref[0]
    qmaps = [q[:, DIFF_QK * j:DIFF_QK * (j + 1)] for j in range(4)]
    k_slices = [slice(DIFF_QK * j, DIFF_QK * (j + 1)) for j in range(4)]
    pv_groups = [((0, 1), slice(0, LANES)), ((2, 3), slice(LANES, 2 * LANES))]
    _attention_sweeps(qmaps, k_slices, pv_groups, k_refs, v_refs, m_ref, smax_ref, acc_ref)

    lv = lam_ref[...]
    lam = (jnp.exp(jnp.sum(lv[0:1] * lv[1:2], axis=-1, keepdims=True))
           - jnp.exp(jnp.sum(lv[2:3] * lv[3:4], axis=-1, keepdims=True)) + lam_init)
    acc = acc_ref[...]
    for hh in range(2):
        a0 = acc[(2 * hh) * tq:(2 * hh + 1) * tq]
        a1 = acc[(2 * hh + 1) * tq:(2 * hh + 2) * tq]
        o = a0[:, :HEAD_V] / a0[:, HEAD_V:HEAD_V + 1] - lam * (a1[:, :HEAD_V] / a1[:, HEAD_V:HEAD_V + 1])
        ms = jnp.mean(o * o, axis=-1, keepdims=True)
        on = o * lax.rsqrt(ms + EPS) * sg_ref[...] * (1.0 - lam_init)
        o_ref[0, :, HEAD_V * hh:HEAD_V * (hh + 1)] = on.astype(BF16)


def _diff_attention(q, k_parts, v_parts, lam_vecs, subln_g, lam_init, tq):
    b, sq, _ = q.shape
    n_parts = len(k_parts)
    in_specs = [pl.BlockSpec((1, tq, LANES), lambda bb, p, i: (bb, i, p))]
    for kp in k_parts:
        in_specs.append(pl.BlockSpec((1, kp.shape[1], LANES, kp.shape[3]), lambda bb, p, i: (bb, 0, p, 0)))
    for vp in v_parts:
        in_specs.append(pl.BlockSpec((1, vp.shape[1], 2 * LANES), lambda bb, p, i: (bb, 0, p)))
    in_specs.append(pl.BlockSpec((4, DIFF_QK), lambda bb, p, i: (0, 0)))
    in_specs.append(pl.BlockSpec((1, HEAD_V), lambda bb, p, i: (0, 0)))
    return pl.pallas_call(
        functools.partial(_diff_kernel, n_parts=n_parts, tq=tq, lam_init=lam_init),
        out_shape=jax.ShapeDtypeStruct((b, sq, 256), BF16),
        grid=(b, DIFF_HEADS // 2, sq // tq),
        in_specs=in_specs,
        out_specs=pl.BlockSpec((1, tq, LANES), lambda bb, p, i: (bb, i, p)),
        scratch_shapes=[pltpu.VMEM((4 * tq, LANES), F32)] * 3,
        compiler_params=_cparams(("arbitrary", "arbitrary", "arbitrary")),
        name="diff_attention",
    )(q, *k_parts, *v_parts, lam_vecs, subln_g)


CONV_HALO = 16
CONV_ROWS = 64


def _conv_kernel(gb_ref, prev_ref, next_ref, w_ref, b_ref, lg_ref, lb_ref, o_ref, u_ref, sh_ref, *, tm):
    i = pl.program_id(1)
    last = pl.num_programs(1) - 1
    ch = w_ref.shape[1]

    def glu(z):
        z = z.astype(F32)
        return z[:, :ch] * _sigmoid(z[:, ch:])

    u_ref[CONV_HALO:CONV_HALO + tm, :] = glu(gb_ref[0])
    u_ref[0:CONV_HALO, :] = jnp.where(i > 0, glu(prev_ref[0]), 0.0)
    u_ref[CONV_HALO + tm:2 * CONV_HALO + tm, :] = jnp.where(i < last, glu(next_ref[0]), 0.0)

    span = sh_ref.shape[1]
    for r in range(1, SUBLANES):
        sh_ref[r - 1] = u_ref[r:r + span, :]

    off = CONV_HALO - CONV_K // 2
    for r0 in range(0, tm, CONV_ROWS):
        acc = jnp.zeros((CONV_ROWS, ch), F32)
        for j in range(CONV_K):
            phase, base = (off + j) % SUBLANES, r0 + (off + j) // SUBLANES * SUBLANES
            taps = u_ref[base:base + CONV_ROWS, :] if phase == 0 else sh_ref[phase - 1, base:base + CONV_ROWS, :]
            acc = acc + taps * w_ref[j:j + 1, :]
        y = acc + b_ref[...]
        mu = jnp.mean(y, axis=-1, keepdims=True)
        yc = y - mu
        var = jnp.mean(yc * yc, axis=-1, keepdims=True)
        z = yc * lax.rsqrt(var + EPS) * lg_ref[...] + lb_ref[...]
        o_ref[0, r0:r0 + CONV_ROWS, :] = (z * _sigmoid(z)).astype(BF16)


def _conformer_conv(gb, w, bias, ln_g, ln_b, tm):
    b, s, two_ch = gb.shape
    ch = two_ch // 2
    hb = tm // CONV_HALO
    n_halo = s // CONV_HALO
    const2 = lambda bb, i: (0, 0)
    return pl.pallas_call(
        functools.partial(_conv_kernel, tm=tm),
        out_shape=jax.ShapeDtypeStruct((b, s, ch), BF16),
        grid=(b, s // tm),
        in_specs=[
            pl.BlockSpec((1, tm, two_ch), lambda bb, i: (bb, i, 0)),
            pl.BlockSpec((1, CONV_HALO, two_ch), lambda bb, i: (bb, jnp.maximum(i * hb - 1, 0), 0)),
            pl.BlockSpec((1, CONV_HALO, two_ch), lambda bb, i: (bb, jnp.minimum((i + 1) * hb, n_halo - 1), 0)),
            pl.BlockSpec((CONV_K, ch), const2), pl.BlockSpec((1, ch), const2),
            pl.BlockSpec((1, ch), const2), pl.BlockSpec((1, ch), const2),
        ],
        out_specs=pl.BlockSpec((1, tm, ch), lambda bb, i: (bb, i, 0)),
        scratch_shapes=[pltpu.VMEM((tm + 2 * CONV_HALO, ch), F32),
                        pltpu.VMEM((SUBLANES - 1, tm + 2 * CONV_HALO - SUBLANES, ch), F32)],
        compiler_params=_cparams(("arbitrary", "arbitrary")),
        name="conformer_conv",
    )(gb, gb, gb, w, bias, ln_g, ln_b)


def _merge_kernel(*refs, with_router):
    oa_ref, ob_ref, oc_ref, x_ref, gate_ref, shift_ref, scale_ref, g2_ref, w_ref = refs[:9]
    if with_router:
        rw_ref, xo_ref, h2_ref, lg_ref = refs[9:]
    else:
        xo_ref, h2_ref = refs[9:]
    tm = x_ref.shape[1]
    wa = oa_ref.shape[2]
    wb = wa + ob_ref.shape[2]
    y = _dot(oa_ref[0], w_ref[0:wa, :]) + _dot(ob_ref[0], w_ref[wa:wb, :]) + _dot(oc_ref[0], w_ref[wb:, :])
    xn = x_ref[0] + gate_ref[0] * y
    xo_ref[0] = xn
    ms = jnp.mean(xn * xn, axis=-1, keepdims=True)
    h2 = xn * lax.rsqrt(ms + EPS) * g2_ref[...] * (1.0 + scale_ref[0]) + shift_ref[0]
    if with_router:
        _store_row_tiles(h2_ref, h2, tm)
        lg_ref[...] = lax.dot_general(rw_ref[...], h2, (((1,), (1,)), ((), ())),
                                      preferred_element_type=F32, precision=lax.Precision.HIGHEST)
    else:
        h2_ref[0] = h2.astype(BF16)


def _merge(oa, ob, oc, x, gate, shift, scale, g2, w_out, router_wt, tm):
    b, s, d = x.shape
    row = lambda bb, i: (bb, i, 0)
    per_b = lambda bb, i: (bb, 0, 0)
    const2 = lambda bb, i: (0, 0)
    nt = s // tm
    in_specs = [
        pl.BlockSpec((1, tm, oa.shape[2]), row), pl.BlockSpec((1, tm, ob.shape[2]), row),
        pl.BlockSpec((1, tm, oc.shape[2]), row), pl.BlockSpec((1, tm, d), row),
        pl.BlockSpec((1, 1, d), per_b), pl.BlockSpec((1, 1, d), per_b), pl.BlockSpec((1, 1, d), per_b),
        pl.BlockSpec((1, d), const2), pl.BlockSpec((d, d), const2),
    ]
    out_shape = [jax.ShapeDtypeStruct((b, s, d), F32)]
    out_specs = [pl.BlockSpec((1, tm, d), row)]
    args = [oa, ob, oc, x, gate, shift, scale, g2, w_out]
    with_router = router_wt is not None
    if not with_router:
        out_shape.append(jax.ShapeDtypeStruct((b, s, d), BF16))
        out_specs.append(pl.BlockSpec((1, tm, d), row))
    else:
        out_shape.append(jax.ShapeDtypeStruct((b * s * ROW_TILE, LANES), F32))
        out_specs.append(pl.BlockSpec((tm * ROW_TILE, LANES), lambda bb, i: (bb * nt + i, 0)))
        in_specs.append(pl.BlockSpec((N_EXPERTS, d), const2))
        out_shape.append(jax.ShapeDtypeStruct((N_EXPERTS, b * s), F32))
        out_specs.append(pl.BlockSpec((N_EXPERTS, tm), lambda bb, i: (0, bb * nt + i)))
        args.append(router_wt)
    return pl.pallas_call(
        functools.partial(_merge_kernel, with_router=with_router),
        out_shape=tuple(out_shape), grid=(b, nt), in_specs=in_specs, out_specs=tuple(out_specs),
        compiler_params=_cparams(("arbitrary", "arbitrary")), name="merge_heads",
    )(*args)


def _ffn_kernel(h_ref, x_ref, gate_ref, wg_ref, wu_ref, wd_ref, o_ref, *, tf):
    h = h_ref[...]
    ff = wg_ref.shape[1]
    acc = jnp.zeros(x_ref.shape, F32)
    for f in range(0, ff, tf):
        g = _dot(h, wg_ref[:, f:f + tf])
        u = _dot(h, wu_ref[:, f:f + tf])
        a = (g * _sigmoid(g) * u).astype(BF16)
        acc = acc + _dot(a, wd_ref[f:f + tf, :])
    o_ref[...] = x_ref[...] + gate_ref[0] * acc


def _dense_ffn(h2, x, gate, wg, wu, wd, tm):
    n, d = x.shape
    ff = wg.shape[1]
    s = n // gate.shape[0]
    row = lambda i: (i, 0)
    const2 = lambda i: (0, 0)
    resident = pl.Buffered(1)
    return pl.pallas_call(
        functools.partial(_ffn_kernel, tf=MXU_DIM),
        out_shape=jax.ShapeDtypeStruct((n, d), F32),
        grid=(n // tm,),
        in_specs=[
            pl.BlockSpec((tm, d), row), pl.BlockSpec((tm, d), row),
            pl.BlockSpec((1, 1, d), lambda i: ((i * tm) // s, 0, 0)),
            pl.BlockSpec((d, ff), const2, pipeline_mode=resident),
            pl.BlockSpec((d, ff), const2, pipeline_mode=resident),
            pl.BlockSpec((ff, d), const2, pipeline_mode=resident),
        ],
        out_specs=pl.BlockSpec((tm, d), row),
        compiler_params=_cparams(("arbitrary",)), name="dense_ffn",
    )(h2, x, gate, wg, wu, wd)


def _top2(lg):
    sub = lax.broadcasted_iota(I32, lg.shape, 0)
    l1 = jnp.max(lg, axis=0, keepdims=True)
    i1 = jnp.min(jnp.where(lg == l1, sub, N_EXPERTS), axis=0, keepdims=True)
    m1 = sub == i1
    lg2 = jnp.where(m1, -jnp.inf, lg)
    l2 = jnp.max(lg2, axis=0, keepdims=True)
    i2 = jnp.min(jnp.where(lg2 == l2, sub, N_EXPERTS), axis=0, keepdims=True)
    m2 = sub == i2
    return l1, l2, m1, m2


def _sublane_cumsum(x):
    sub = lax.broadcasted_iota(I32, x.shape, 0)
    for sh in (1, 2, 4):
        x = x + jnp.where(sub >= sh, pltpu.roll(x, sh, 0), 0.0)
    return x


def _route_kernel(lg_ref, tri_ref, dest_ref, gates_ref, be_ref, base_ref, start_ref, *, block_rows):
    phase = pl.program_id(0)
    j = pl.program_id(1)
    l1, l2, m1, m2 = _top2(lg_ref[...])
    e = jnp.where(m1 | m2, 1.0, 0.0).astype(F32)
    cnt = jnp.sum(e, axis=1, keepdims=True)

    @pl.when((phase == 0) & (j == 0))
    def _():
        base_ref[...] = jnp.zeros(base_ref.shape, F32)

    @pl.when((phase == 1) & (j == 0))
    def _():
        counts = base_ref[...]
        nblk = jnp.floor((counts + (block_rows - 1)) * (1.0 / block_rows))
        end_blk = _sublane_cumsum(nblk)
        start_ref[...] = (end_blk - nblk) * block_rows
        blk = lax.broadcasted_iota(I32, be_ref.shape, 1).astype(F32)
        owner = jnp.sum(jnp.where(end_blk[:, :1] <= blk, 1.0, 0.0), axis=0, keepdims=True)
        be_ref[...] = jnp.broadcast_to(jnp.minimum(owner, N_EXPERTS - 1.0), be_ref.shape).astype(I32)
        base_ref[...] = jnp.zeros(base_ref.shape, F32)

    @pl.when(phase == 1)
    def _():
        prefix = _dot(e.astype(BF16), tri_ref[...]) + base_ref[:, :1] + start_ref[:, :1]
        d1 = jnp.sum(jnp.where(m1, prefix, 0.0), axis=0, keepdims=True)
        d2 = jnp.sum(jnp.where(m2, prefix, 0.0), axis=0, keepdims=True)
        sub = lax.broadcasted_iota(I32, dest_ref.shape, 0)
        dest_ref[...] = jnp.where(sub == 0, d1, jnp.where(sub == 1, d2, 0.0)).astype(I32)
        ex = jnp.exp(l2 - l1)
        g1 = 1.0 / (1.0 + ex)
        g2 = ex / (1.0 + ex)
        half = lax.broadcasted_iota(I32, (LANES, lg_ref.shape[1]), 0) < LANES // 2
        gates_ref[...] = jnp.where(half, g1, g2).T

    base_ref[...] = base_ref[...] + cnt


def _route(logits_t, block_rows, n_blocks_pad, tr):
    n = logits_t.shape[1]
    tri = jnp.asarray(np.triu(np.ones((tr, tr), np.float32), k=1), BF16)
    return pl.pallas_call(
        functools.partial(_route_kernel, block_rows=block_rows),
        out_shape=(jax.ShapeDtypeStruct((N_EXPERTS, n), I32),
                   jax.ShapeDtypeStruct((n, LANES), F32),
                   jax.ShapeDtypeStruct((N_EXPERTS, n_blocks_pad), I32)),
        grid=(2, n // tr),
        in_specs=[pl.BlockSpec((N_EXPERTS, tr), lambda p, j: (0, j)),
                  pl.BlockSpec((tr, tr), lambda p, j: (0, 0))],
        out_specs=(pl.BlockSpec((N_EXPERTS, tr), lambda p, j: (0, j * p)),
                   pl.BlockSpec((tr, LANES), lambda p, j: (j * p, 0)),
                   pl.BlockSpec((N_EXPERTS, n_blocks_pad), lambda p, j: (0, 0))),
        scratch_shapes=[pltpu.VMEM((N_EXPERTS, LANES), F32), pltpu.VMEM((N_EXPERTS, LANES), F32)],
        compiler_params=_cparams(("arbitrary", "arbitrary")), name="moe_route",
    )(logits_t, tri)


def _row_copy(src_hbm, src_row, dst_hbm, dst_row, sem):
    src = pl.ds(pl.multiple_of(src_row * ROW_TILE, ROW_TILE), ROW_TILE)
    dst = pl.ds(pl.multiple_of(dst_row * ROW_TILE, ROW_TILE), ROW_TILE)
    return pltpu.make_async_copy(src_hbm.at[src], dst_hbm.at[dst], sem)


def _scatter_kernel(d1_ref, d2_ref, src_ref, init_hbm, out_hbm, sem, *, rows):
    del init_hbm

    def start(r, c):
        _row_copy(src_ref, r, out_hbm, d1_ref[0, 0, r], sem).start(priority=0)
        _row_copy(src_ref, r, out_hbm, d2_ref[0, 0, r], sem).start(priority=1)
        return c

    lax.fori_loop(0, rows, start, 0, unroll=DMA_UNROLL)
    for _ in range(2):
        pltpu.make_async_copy(src_ref, out_hbm.at[pl.ds(0, rows * ROW_TILE)], sem).wait()


def _scatter_rows(src, d1, d2, total_rows, rows):
    n = d1.shape[0]
    idx_spec = pl.BlockSpec((1, 1, rows), lambda i: (i, 0, 0), memory_space=pltpu.SMEM)
    any_spec = pl.BlockSpec(memory_space=pl.ANY)
    return pl.pallas_call(
        functools.partial(_scatter_kernel, rows=rows),
        out_shape=jax.ShapeDtypeStruct((total_rows * ROW_TILE, LANES), src.dtype),
        grid=(n // rows,),
        in_specs=[idx_spec, idx_spec, pl.BlockSpec((rows * ROW_TILE, LANES), lambda i: (i, 0)), any_spec],
        out_specs=any_spec,
        scratch_shapes=[pltpu.SemaphoreType.DMA(())],
        input_output_aliases={3: 0},
        compiler_params=pltpu.CompilerParams(dimension_semantics=("arbitrary",), has_side_effects=True),
        name="moe_scatter_rows",
    )(d1.reshape(n // rows, 1, rows), d2.reshape(n // rows, 1, rows), src,
      jnp.zeros((total_rows * ROW_TILE, LANES), src.dtype))


def _expert_kernel(be_ref, x_ref, wg_ref, wu_ref, wd_ref, o_ref, *, block_rows, tf):
    del be_ref
    x = _load_row_tiles(x_ref, block_rows).astype(BF16)
    ff = wg_ref.shape[2]
    acc = jnp.zeros((block_rows, wd_ref.shape[2]), F32)
    for f in range(0, ff, tf):
        g = _dot(x, wg_ref[0, :, f:f + tf])
        u = _dot(x, wu_ref[0, :, f:f + tf])
        a = (g * _sigmoid(g) * u).astype(BF16)
        acc = acc + _dot(a, wd_ref[0, f:f + tf, :])
    _store_row_tiles(o_ref, acc, block_rows)


def _expert_ffn(xb, blk_expert, wg, wu, wd, block_rows):
    d, ff = wg.shape[1], wg.shape[2]
    rows = xb.shape[0] // ROW_TILE
    resident = pl.Buffered(1)
    grid_spec = pltpu.PrefetchScalarGridSpec(
        num_scalar_prefetch=1,
        grid=(rows // block_rows,),
        in_specs=[
            pl.BlockSpec((block_rows * ROW_TILE, LANES), lambda i, be: (i, 0)),
            pl.BlockSpec((1, d, ff), lambda i, be: (be[i], 0, 0), pipeline_mode=resident),
            pl.BlockSpec((1, d, ff), lambda i, be: (be[i], 0, 0), pipeline_mode=resident),
            pl.BlockSpec((1, ff, d), lambda i, be: (be[i], 0, 0), pipeline_mode=resident),
        ],
        out_specs=pl.BlockSpec((block_rows * ROW_TILE, LANES), lambda i, be: (i, 0)),
    )
    return pl.pallas_call(
        functools.partial(_expert_kernel, block_rows=block_rows, tf=MXU_DIM),
        out_shape=jax.ShapeDtypeStruct(xb.shape, F32), grid_spec=grid_spec,
        compiler_params=_cparams(("arbitrary",)), name="moe_expert_ffn",
    )(blk_expert, xb, wg, wu, wd)


def _combine_kernel(d1_ref, d2_ref, x_ref, yb_hbm, gates_ref, gate_ref, fg_ref, o_ref, y1_ref, y2_ref, sem):
    tm = x_ref.shape[0]

    def start(r, c):
        _row_copy(yb_hbm, d1_ref[0, 0, r], y1_ref, r, sem).start(priority=0)
        _row_copy(yb_hbm, d2_ref[0, 0, r], y2_ref, r, sem).start(priority=1)
        return c

    lax.fori_loop(0, tm, start, 0, unroll=DMA_UNROLL)
    for y_ref in (y1_ref, y2_ref):
        pltpu.make_async_copy(yb_hbm.at[pl.ds(0, tm * ROW_TILE)], y_ref, sem).wait()

    gts = gates_ref[...]
    y = (gts[:, 0:1] * _load_row_tiles(y1_ref, tm)
         + gts[:, LANES // 2:LANES // 2 + 1] * _load_row_tiles(y2_ref, tm))
    xn = x_ref[...] + gate_ref[0] * y
    ms = jnp.mean(xn * xn, axis=-1, keepdims=True)
    o_ref[...] = xn * lax.rsqrt(ms + EPS) * fg_ref[...]


def _combine_final(x, yb, d1, d2, gates, gate, final_g, tm):
    n, d = x.shape
    s = n // gate.shape[0]
    row = lambda i: (i, 0)
    idx_spec = pl.BlockSpec((1, 1, tm), lambda i: (i, 0, 0), memory_space=pltpu.SMEM)
    return pl.pallas_call(
        _combine_kernel,
        out_shape=jax.ShapeDtypeStruct((n, d), F32),
        grid=(n // tm,),
        in_specs=[idx_spec, idx_spec,
                  pl.BlockSpec((tm, d), row),
                  pl.BlockSpec(memory_space=pl.ANY),
                  pl.BlockSpec((tm, LANES), row),
                  pl.BlockSpec((1, 1, d), lambda i: ((i * tm) // s, 0, 0)),
                  pl.BlockSpec((1, d), lambda i: (0, 0))],
        out_specs=pl.BlockSpec((tm, d), row),
        scratch_shapes=[pltpu.VMEM((tm * ROW_TILE, LANES), F32), pltpu.VMEM((tm * ROW_TILE, LANES), F32),
                        pltpu.SemaphoreType.DMA(())],
        compiler_params=_cparams(("arbitrary",)), name="moe_combine_final",
    )(d1.reshape(n // tm, 1, tm), d2.reshape(n // tm, 1, tm), x, yb, gates, gate, final_g)


def _rope_tables(s, dim):
    half = dim // 2
    t = jnp.arange(s)
    inv = 1.0 / (ROPE_THETA ** (jnp.arange(0, half, 2, dtype=F32) / half))
    ang_r = (t // GRID_W).astype(F32)[:, None] * inv
    ang_c = (t % GRID_W).astype(F32)[:, None] * inv
    ang = jnp.concatenate([ang_r, ang_r, ang_c, ang_c], axis=-1)
    reps = LANES // dim
    return jnp.tile(jnp.cos(ang), (1, reps)), jnp.tile(jnp.sin(ang), (1, reps))


def _rotate_matrix(dim):
    q = dim // 4
    p = np.zeros((MXU_DIM, MXU_DIM), np.float32)
    for j in range(MXU_DIM):
        if (j % (2 * q)) < q:
            p[j + q, j] = -1.0
        else:
            p[j - q, j] = 1.0
    return jnp.asarray(p, BF16)


def _head_mean_matrix():
    m = np.kron(np.eye(MXU_DIM // HEAD_V, dtype=np.float32), np.full((HEAD_V, HEAD_V), 1.0 / HEAD_V, np.float32))
    return jnp.asarray(m, BF16)


def _widen_values(w, heads):
    d = w.shape[0]
    w = w.reshape(d, heads, HEAD_V)
    return jnp.concatenate([w, jnp.zeros_like(w)], axis=-1).reshape(d, heads * LANES)


def _widen_in_proj(w):
    qa, ka, va, gb, qc, kc, vc = jnp.split(w, [256, 512, 768, 1280, 1792, 1920], axis=1)
    return jnp.concatenate([qa, ka, _widen_values(va, DIFF_HEADS), gb, qc, kc, _widen_values(vc, GQA_KV)],
                           axis=1).astype(BF16)


def kernel(x, c, ctx, c_ctx, ada_w, ada_b, norm1_g, norm2_g, w_in, w_out, lam_q1, lam_k1, lam_q2, lam_k2,
           diff_subln_g, conv_w, conv_b, conv_ln_g, conv_ln_b, q_norm_g, k_norm_g, ffn_gate, ffn_up, ffn_down,
           router_w, moe_gate, moe_up, moe_down, final_g):
    b, s, d = x.shape
    sc = ctx.shape[1]
    depth = ada_w.shape[0]
    n = b * s
    assert depth % 2 == 0, "the final RMSNorm is fused into the MoE combine of the last (odd) layer"

    tm = min(512, s)
    tmc = min(512, sc)
    tq = min(512, s)
    tq_diff = min(1024, s)
    tqc = min(256, sc)

    tabs_x = _rope_tables(s, DIFF_QK) + _rope_tables(s, HEAD_V)
    ones_c, zeros_c = jnp.ones((sc, LANES), F32), jnp.zeros((sc, LANES), F32)
    tabs_c = (ones_c, zeros_c, ones_c, zeros_c)
    mats = (_rotate_matrix(DIFF_QK), _rotate_matrix(HEAD_V), _head_mean_matrix())

    cc = jnp.zeros((16, d), F32).at[:b].set(c).at[b].set(c_ctx)

    for i in range(depth):
        last = i == depth - 1
        lam_init = 0.8 - 0.6 * math.exp(-0.3 * i)
        mod_all = _ada_mod(cc, ada_w[i], ada_b[i])
        mod = mod_all[:b].reshape(b, 6, 1, d)
        modc = jnp.broadcast_to(mod_all[b].reshape(1, 6, 1, d), (b, 6, 1, d))

        w_aug = _widen_in_proj(w_in[i])
        g1 = norm1_g[i].reshape(1, d)
        qg = jnp.tile(q_norm_g[i], GQA_HEADS).reshape(1, -1)
        kg = jnp.tile(k_norm_g[i], GQA_KV).reshape(1, -1)
        lam_vecs = jnp.stack([lam_q1[i], lam_k1[i], lam_q2[i], lam_k2[i]]).astype(F32)
        subln = diff_subln_g[i].reshape(1, HEAD_V)
        conv_args = (conv_w[i], conv_b[i].reshape(1, -1), conv_ln_g[i].reshape(1, -1), conv_ln_b[i].reshape(1, -1))
        w_o = w_out[i].astype(BF16)
        g2 = norm2_g[i].reshape(1, d)

        qa, kat, va, gb, qc, kct, vc = _in_projection(x, mod[:, 0], mod[:, 1], g1, w_aug, tabs_x, mats, qg, kg, tm)
        qa_x, kat_x, va_x, gb_x, qc_x, kct_x, vc_x = _in_projection(
            ctx, modc[:, 0], modc[:, 1], g1, w_aug, tabs_c, mats, qg, kg, tmc)

        oa = _diff_attention(qa, [kat, kat_x], [va, va_x], lam_vecs, subln, lam_init, tq_diff)
        ob = _conformer_conv(gb, *conv_args, tm)
        oc = _gqa_attention(qc, [kct, kct_x], [vc, vc_x], tq)

        j = i // 2
        if i % 2 == 0:
            x, h2 = _merge(oa, ob, oc, x, mod[:, 2], mod[:, 3], mod[:, 4], g2, w_o, None, tm)
            wg, wu, wd = ffn_gate[j].astype(BF16), ffn_up[j].astype(BF16), ffn_down[j].astype(BF16)
            x = _dense_ffn(h2.reshape(n, d), x.reshape(n, d), mod[:, 5], wg, wu, wd, tm).reshape(b, s, d)
        else:
            rwt = router_w[j].T.astype(F32)
            x, h2, logits_t = _merge(oa, ob, oc, x, mod[:, 2], mod[:, 3], mod[:, 4], g2, w_o, rwt, tm)
            block_rows = 512 if n >= 8192 else 256
            n_blocks = (2 * n) // block_rows + N_EXPERTS
            n_blocks_pad = -(-n_blocks // LANES) * LANES
            dest, gates, blk_e = _route(logits_t, block_rows, n_blocks_pad, min(512, n))
            xb = _scatter_rows(h2, dest[0], dest[1], n_blocks * block_rows, min(256, n))
            yb = _expert_ffn(xb, blk_e[0, :n_blocks], moe_gate[j].astype(BF16), moe_up[j].astype(BF16),
                             moe_down[j].astype(BF16), block_rows)
            assert last
            x = _combine_final(x.reshape(n, d), yb, dest[0], dest[1], gates, mod[:, 5], final_g.reshape(1, d),
                               tm).reshape(b, s, d)

        if not last:
            oa_x = _diff_attention(qa_x, [kat_x], [va_x], lam_vecs, subln, lam_init, tqc)
            ob_x = _conformer_conv(gb_x, *conv_args, tmc)
            oc_x = _gqa_attention(qc_x, [kct_x], [vc_x], tqc)
            assert i % 2 == 0, "context tokens only ever pass through dense channel mixers"
            ctx, hc2 = _merge(oa_x, ob_x, oc_x, ctx, modc[:, 2], modc[:, 3], modc[:, 4], g2, w_o, None, tmc)
            ctx = _dense_ffn(hc2.reshape(b * sc, d), ctx.reshape(b * sc, d), modc[:, 5], wg, wu, wd,
                             tmc).reshape(b, sc, d)

    return x
```

```python
import functools
import math

import numpy as np
import jax
import jax.numpy as jnp
from jax import lax
from jax.experimental import pallas as pl
from jax.experimental.pallas import tpu as pltpu

F32 = jnp.float32
BF16 = jnp.bfloat16
I32 = jnp.int32

EPS = 1e-6
ROPE_THETA = 10000.0
GRID_W = 64

DIFF_HEADS = 4
DIFF_QK = 32
HEAD_V = 64
GQA_HEADS = 8
GQA_KV = 2
GQA_GROUP = GQA_HEADS // GQA_KV
CONV_K = 31
N_EXPERTS = 8
LOG2E = math.log2(math.e)

LANES = 128
SUBLANES = 8
MXU_DIM = 256
VMEM_LIMIT = 52 * 1024 * 1024
NEG_BIG = -1e30
P_DTYPE = jnp.float8_e4m3fn
P_MAX_EXP = 8.5
P_HEADROOM_FIXED = 0.0
P_HEADROOM_RUNNING = 8.0
ATTN_UNROLL = 4
DMA_UNROLL = 8

C_QA, C_KA, C_VA, C_GB, C_QC, C_KC, C_VC, C_END = 0, 256, 512, 1024, 1536, 2048, 2176, 2432


def _cparams(semantics):
    return pltpu.CompilerParams(dimension_semantics=semantics, vmem_limit_bytes=VMEM_LIMIT)


def _dot(a, b):
    return jnp.dot(a, b, preferred_element_type=F32)


def _sigmoid(z):
    return 1.0 / (1.0 + jnp.exp(-z))


ROW_TILE = 8


def _store_row_tiles(ref, val, rows):
    for a in range(ROW_TILE):
        ref[pl.ds(a, rows, stride=ROW_TILE), :] = val[:, a * LANES:(a + 1) * LANES]


def _load_row_tiles(ref, rows):
    return jnp.concatenate([ref[pl.ds(a, rows, stride=ROW_TILE), :] for a in range(ROW_TILE)], axis=1)


def _mod_kernel(c_ref, w_ref, b_ref, o_ref):
    c = c_ref[...]
    s = c * _sigmoid(c)
    o_ref[...] = jnp.dot(s, w_ref[...], preferred_element_type=F32, precision=lax.Precision.HIGHEST) + b_ref[...]


def _ada_mod(cc, w, b):
    rows, d = cc.shape
    n = w.shape[1]
    tn = d
    return pl.pallas_call(
        _mod_kernel,
        out_shape=jax.ShapeDtypeStruct((rows, n), F32),
        grid=(n // tn,),
        in_specs=[pl.BlockSpec((rows, d), lambda j: (0, 0)),
                  pl.BlockSpec((d, tn), lambda j: (0, j)),
                  pl.BlockSpec((1, tn), lambda j: (0, j))],
        out_specs=pl.BlockSpec((rows, tn), lambda j: (0, j)),
        compiler_params=_cparams(("arbitrary",)),
        name="ada_mod",
    )(cc, w, b.reshape(1, n))


def _inproj_kernel(x_ref, shift_ref, scale_ref, g_ref, w_ref, cosa_ref, sina_ref, cosc_ref, sinc_ref,
                   pa_ref, pc_ref, hm_ref, qg_ref, kg_ref,
                   qa_o, kat_o, va_o, gb_o, qc_o, kct_o, vc_o, *, qa_scale, qc_scale):
    x = x_ref[0]
    ms = jnp.mean(x * x, axis=-1, keepdims=True)
    h = x * lax.rsqrt(ms + EPS) * g_ref[...]
    h = h * (1.0 + scale_ref[0]) + shift_ref[0]
    hb = h.astype(BF16)

    def proj(lo, hi):
        return _dot(hb, w_ref[:, lo:hi])

    def blockmat(y, m_ref):
        yb = y.astype(BF16)
        w = y.shape[1]
        if w == LANES:
            return _dot(yb, m_ref[:LANES, :LANES])
        return jnp.concatenate([_dot(yb[:, c:c + MXU_DIM], m_ref[...]) for c in range(0, w, MXU_DIM)], axis=1)

    def rope(y, cos, sin, p_ref):
        reps = y.shape[1] // LANES
        cos = jnp.tile(cos, (1, reps))
        sin = jnp.tile(sin, (1, reps))
        return y * cos + blockmat(y, p_ref) * sin

    def ones_col(width):
        lane = lax.broadcasted_iota(I32, (1, width), 1)
        return jnp.where(lane % LANES == HEAD_V, 1.0, 0.0).astype(F32)

    cosa, sina, cosc, sinc = cosa_ref[...], sina_ref[...], cosc_ref[...], sinc_ref[...]

    qa_o[0] = (rope(proj(C_QA, C_KA), cosa, sina, pa_ref) * qa_scale).astype(BF16)
    kat_o[0, 0] = rope(proj(C_KA, C_VA), cosa, sina, pa_ref).T.astype(BF16)
    va_o[0] = (proj(C_VA, C_GB) + ones_col(C_GB - C_VA)).astype(P_DTYPE)
    gb_o[0] = proj(C_GB, C_QC).astype(BF16)

    y = proj(C_QC, C_KC)
    yn = y * lax.rsqrt(blockmat(y * y, hm_ref) + EPS) * qg_ref[...]
    qc_o[0] = (rope(yn, cosc, sinc, pc_ref) * qc_scale).astype(BF16)

    y = proj(C_KC, C_VC)
    yn = y * lax.rsqrt(blockmat(y * y, hm_ref) + EPS) * kg_ref[...]
    kct_o[0, 0] = rope(yn, cosc, sinc, pc_ref).T.astype(BF16)

    vc_o[0] = (proj(C_VC, C_END) + ones_col(C_END - C_VC)).astype(P_DTYPE)


def _in_projection(x, shift, scale, g1, w_aug, tabs, mats, qg, kg, tm):
    b, s, d = x.shape
    nt = s // tm
    cosa, sina, cosc, sinc = tabs
    pa, pc, hm = mats
    row = lambda bb, i: (bb, i, 0)
    const2 = lambda bb, i: (0, 0)
    per_b = lambda bb, i: (bb, 0, 0)
    tab = lambda bb, i: (i, 0)
    kern = functools.partial(_inproj_kernel, qa_scale=DIFF_QK ** -0.5 * LOG2E, qc_scale=HEAD_V ** -0.5 * LOG2E)
    out_shape = (
        jax.ShapeDtypeStruct((b, s, 256), BF16),
        jax.ShapeDtypeStruct((b, nt, 256, tm), BF16),
        jax.ShapeDtypeStruct((b, s, 512), P_DTYPE),
        jax.ShapeDtypeStruct((b, s, 512), BF16),
        jax.ShapeDtypeStruct((b, s, 512), BF16),
        jax.ShapeDtypeStruct((b, nt, 128, tm), BF16),
        jax.ShapeDtypeStruct((b, s, 256), P_DTYPE),
    )
    out_specs = (
        pl.BlockSpec((1, tm, 256), row),
        pl.BlockSpec((1, 1, 256, tm), lambda bb, i: (bb, i, 0, 0)),
        pl.BlockSpec((1, tm, 512), row),
        pl.BlockSpec((1, tm, 512), row),
        pl.BlockSpec((1, tm, 512), row),
        pl.BlockSpec((1, 1, 128, tm), lambda bb, i: (bb, i, 0, 0)),
        pl.BlockSpec((1, tm, 256), row),
    )
    in_specs = [
        pl.BlockSpec((1, tm, d), row),
        pl.BlockSpec((1, 1, d), per_b),
        pl.BlockSpec((1, 1, d), per_b),
        pl.BlockSpec((1, d), const2),
        pl.BlockSpec((d, C_END), const2),
        pl.BlockSpec((tm, LANES), tab), pl.BlockSpec((tm, LANES), tab),
        pl.BlockSpec((tm, LANES), tab), pl.BlockSpec((tm, LANES), tab),
        pl.BlockSpec((MXU_DIM, MXU_DIM), const2), pl.BlockSpec((MXU_DIM, MXU_DIM), const2),
        pl.BlockSpec((MXU_DIM, MXU_DIM), const2),
        pl.BlockSpec((1, 512), const2), pl.BlockSpec((1, 128), const2),
    ]
    return pl.pallas_call(
        kern, out_shape=out_shape, grid=(b, nt), in_specs=in_specs, out_specs=out_specs,
        compiler_params=_cparams(("arbitrary", "arbitrary")), name="in_projection",
    )(x, shift, scale, g1, w_aug, cosa, sina, cosc, sinc, pa, pc, hm, qg, kg)


def _attention_sweeps(qms, k_slices, pv_groups, k_refs, v_refs, m_ref, smax_ref, acc_ref):
    r = qms[0].shape[0]

    def scores(j, kc):
        return _dot(qms[j], kc[k_slices[j], :])

    def sweep(running_max):
        acc_ref[...] = jnp.zeros(acc_ref.shape, F32)
        if running_max:
            m_ref[...] = jnp.full(m_ref.shape, NEG_BIG, F32)
        else:
            smax_ref[...] = jnp.full(smax_ref.shape, NEG_BIG, F32)
            kc0 = k_refs[0][0, 0][:, :MXU_DIM]
            for j in range(len(qms)):
                m0 = jnp.max(scores(j, kc0), axis=-1, keepdims=True)
                m_ref[j * r:(j + 1) * r, :] = jnp.broadcast_to(m0 - P_HEADROOM_FIXED, (r, LANES))

        for k_ref, v_ref in zip(k_refs, v_refs):
            n_chunks, tk = k_ref.shape[1], k_ref.shape[3]

            def body(c, carry, k_ref=k_ref, v_ref=v_ref, tk=tk):
                kc = k_ref[0, c]
                vc = v_ref[0, pl.ds(pl.multiple_of(c * tk, tk), tk), :]
                for ids, v_lanes in pv_groups:
                    ps, alphas = [], []
                    for j in ids:
                        rows = slice(j * r, (j + 1) * r)
                        s = scores(j, kc)
                        m = m_ref[rows, :]
                        if running_max:
                            m_new = jnp.maximum(m, jnp.max(s, axis=-1, keepdims=True) - P_HEADROOM_RUNNING)
                            m_ref[rows, :] = m_new
                            alphas.append(jnp.exp2(m - m_new))
                            m = m_new
                        else:
                            cm = functools.reduce(jnp.maximum, [s[:, l:l + LANES] for l in range(0, tk, LANES)])
                            smax_ref[rows, :] = jnp.maximum(smax_ref[rows, :], cm)
                        d = (s - jnp.tile(m, (1, tk // LANES))).astype(BF16)
                        ps.append(jnp.exp2(d).astype(P_DTYPE))
                    rows = slice(ids[0] * r, (ids[-1] + 1) * r)
                    pv = _dot(jnp.concatenate(ps, axis=0), vc[:, v_lanes])
                    if running_max:
                        acc_ref[rows, :] = acc_ref[rows, :] * jnp.concatenate(alphas, axis=0) + pv
                    else:
                        acc_ref[rows, :] += pv
                return carry

            unroll = 1 if running_max else math.gcd(n_chunks, ATTN_UNROLL)
            lax.fori_loop(0, n_chunks, body, 0, unroll=unroll)

    sweep(False)
    top_exp = jnp.max(smax_ref[...] - m_ref[...])

    @pl.when(jnp.logical_not(top_exp <= P_MAX_EXP))
    def _():
        sweep(True)


def _gqa_kernel(*refs, n_parts, tq):
    q_ref = refs[0]
    k_refs = refs[1:1 + n_parts]
    v_refs = refs[1 + n_parts:1 + 2 * n_parts]
    o_ref, m_ref, smax_ref, acc_ref = refs[1 + 2 * n_parts:]
    q = q_ref[0]
    qs = jnp.concatenate([q[:, HEAD_V * j:HEAD_V * (j + 1)] for j in range(GQA_GROUP)], axis=0)
    _attention_sweeps([qs], [slice(0, HEAD_V)], [((0,), slice(0, LANES))], k_refs, v_refs,
                      m_ref, smax_ref, acc_ref)
    acc = acc_ref[...]
    o = acc[:, :HEAD_V] / acc[:, HEAD_V:HEAD_V + 1]
    for j in range(GQA_GROUP):
        o_ref[0, :, HEAD_V * j:HEAD_V * (j + 1)] = o[j * tq:(j + 1) * tq].astype(BF16)


def _gqa_attention(q, k_parts, v_parts, tq):
    b, sq, _ = q.shape
    n_parts = len(k_parts)
    in_specs = [pl.BlockSpec((1, tq, 256), lambda bb, g, i: (bb, i, g))]
    for kp in k_parts:
        in_specs.append(pl.BlockSpec((1, kp.shape[1], HEAD_V, kp.shape[3]), lambda bb, g, i: (bb, 0, g, 0)))
    for vp in v_parts:
        in_specs.append(pl.BlockSpec((1, vp.shape[1], LANES), lambda bb, g, i: (bb, 0, g)))
    return pl.pallas_call(
        functools.partial(_gqa_kernel, n_parts=n_parts, tq=tq),
        out_shape=jax.ShapeDtypeStruct((b, sq, 512), BF16),
        grid=(b, GQA_KV, sq // tq),
        in_specs=in_specs,
        out_specs=pl.BlockSpec((1, tq, 256), lambda bb, g, i: (bb, i, g)),
        scratch_shapes=[pltpu.VMEM((GQA_GROUP * tq, LANES), F32)] * 3,
        compiler_params=_cparams(("arbitrary", "arbitrary", "arbitrary")),
        name="gqa_attention",
    )(q, *k_parts, *v_parts)


def _diff_kernel(*refs, n_parts, tq, lam_init):
    q_ref = refs[0]
    k_refs = refs[1:1 + n_parts]
    v_refs = refs[1 + n_parts:1 + 2 * n_parts]
    lam_ref, sg_ref, o_ref, m_ref, smax_ref, acc_ref = refs[1 + 2 * n_parts:]
    q = q_ref[0]
    qmaps = [q[:, DIFF_QK * j:DIFF_QK * (j + 1)] for j in range(4)]
    k_slices = [slice(DIFF_QK * j, DIFF_QK * (j + 1)) for j in range(4)]
    pv_groups = [((0, 1), slice(0, LANES)), ((2, 3), slice(LANES, 2 * LANES))]
    _attention_sweeps(qmaps, k_slices, pv_groups, k_refs, v_refs, m_ref, smax_ref, acc_ref)

    lv = lam_ref[...]
    lam = (jnp.exp(jnp.sum(lv[0:1] * lv[1:2], axis=-1, keepdims=True))
           - jnp.exp(jnp.sum(lv[2:3] * lv[3:4], axis=-1, keepdims=True)) + lam_init)
    acc = acc_ref[...]
    for hh in range(2):
        a0 = acc[(2 * hh) * tq:(2 * hh + 1) * tq]
        a1 = acc[(2 * hh + 1) * tq:(2 * hh + 2) * tq]
        o = a0[:, :HEAD_V] / a0[:, HEAD_V:HEAD_V + 1] - lam * (a1[:, :HEAD_V] / a1[:, HEAD_V:HEAD_V + 1])
        ms = jnp.mean(o * o, axis=-1, keepdims=True)
        on = o * lax.rsqrt(ms + EPS) * sg_ref[...] * (1.0 - lam_init)
        o_ref[0, :, HEAD_V * hh:HEAD_V * (hh + 1)] = on.astype(BF16)


def _diff_attention(q, k_parts, v_parts, lam_vecs, subln_g, lam_init, tq):
    b, sq, _ = q.shape
    n_parts = len(k_parts)
    in_specs = [pl.BlockSpec((1, tq, LANES), lambda bb, p, i: (bb, i, p))]
    for kp in k_parts:
        in_specs.append(pl.BlockSpec((1, kp.shape[1], LANES, kp.shape[3]), lambda bb, p, i: (bb, 0, p, 0)))
    for vp in v_parts:
        in_specs.append(pl.BlockSpec((1, vp.shape[1], 2 * LANES), lambda bb, p, i: (bb, 0, p)))
    in_specs.append(pl.BlockSpec((4, DIFF_QK), lambda bb, p, i: (0, 0)))
    in_specs.append(pl.BlockSpec((1, HEAD_V), lambda bb, p, i: (0, 0)))
    return pl.pallas_call(
        functools.partial(_diff_kernel, n_parts=n_parts, tq=tq, lam_init=lam_init),
        out_shape=jax.ShapeDtypeStruct((b, sq, 256), BF16),
        grid=(b, DIFF_HEADS // 2, sq // tq),
        in_specs=in_specs,
        out_specs=pl.BlockSpec((1, tq, LANES), lambda bb, p, i: (bb, i, p)),
        scratch_shapes=[pltpu.VMEM((4 * tq, LANES), F32)] * 3,
        compiler_params=_cparams(("arbitrary", "arbitrary", "arbitrary")),
        name="diff_attention",
    )(q, *k_parts, *v_parts, lam_vecs, subln_g)


CONV_HALO = 16
CONV_ROWS = 64


def _conv_kernel(gb_ref, prev_ref, next_ref, w_ref, b_ref, lg_ref, lb_ref, o_ref, u_ref, sh_ref, *, tm):
    i = pl.program_id(1)
    last = pl.num_programs(1) - 1
    ch = w_ref.shape[1]

    def glu(z):
        z = z.astype(F32)
        return z[:, :ch] * _sigmoid(z[:, ch:])

    u_ref[CONV_HALO:CONV_HALO + tm, :] = glu(gb_ref[0])
    u_ref[0:CONV_HALO, :] = jnp.where(i > 0, glu(prev_ref[0]), 0.0)
    u_ref[CONV_HALO + tm:2 * CONV_HALO + tm, :] = jnp.where(i < last, glu(next_ref[0]), 0.0)

    span = sh_ref.shape[1]
    for r in range(1, SUBLANES):
        sh_ref[r - 1] = u_ref[r:r + span, :]

    off = CONV_HALO - CONV_K // 2
    for r0 in range(0, tm, CONV_ROWS):
        acc = jnp.zeros((CONV_ROWS, ch), F32)
        for j in range(CONV_K):
            phase, base = (off + j) % SUBLANES, r0 + (off + j) // SUBLANES * SUBLANES
            taps = u_ref[base:base + CONV_ROWS, :] if phase == 0 else sh_ref[phase - 1, base:base + CONV_ROWS, :]
            acc = acc + taps * w_ref[j:j + 1, :]
        y = acc + b_ref[...]
        mu = jnp.mean(y, axis=-1, keepdims=True)
        yc = y - mu
        var = jnp.mean(yc * yc, axis=-1, keepdims=True)
        z = yc * lax.rsqrt(var + EPS) * lg_ref[...] + lb_ref[...]
        o_ref[0, r0:r0 + CONV_ROWS, :] = (z * _sigmoid(z)).astype(BF16)


def _conformer_conv(gb, w, bias, ln_g, ln_b, tm):
    b, s, two_ch = gb.shape
    ch = two_ch // 2
    hb = tm // CONV_HALO
    n_halo = s // CONV_HALO
    const2 = lambda bb, i: (0, 0)
    return pl.pallas_call(
        functools.partial(_conv_kernel, tm=tm),
        out_shape=jax.ShapeDtypeStruct((b, s, ch), BF16),
        grid=(b, s // tm),
        in_specs=[
            pl.BlockSpec((1, tm, two_ch), lambda bb, i: (bb, i, 0)),
            pl.BlockSpec((1, CONV_HALO, two_ch), lambda bb, i: (bb, jnp.maximum(i * hb - 1, 0), 0)),
            pl.BlockSpec((1, CONV_HALO, two_ch), lambda bb, i: (bb, jnp.minimum((i + 1) * hb, n_halo - 1), 0)),
            pl.BlockSpec((CONV_K, ch), const2), pl.BlockSpec((1, ch), const2),
            pl.BlockSpec((1, ch), const2), pl.BlockSpec((1, ch), const2),
        ],
        out_specs=pl.BlockSpec((1, tm, ch), lambda bb, i: (bb, i, 0)),
        scratch_shapes=[pltpu.VMEM((tm + 2 * CONV_HALO, ch), F32),
                        pltpu.VMEM((SUBLANES - 1, tm + 2 * CONV_HALO - SUBLANES, ch), F32)],
        compiler_params=_cparams(("arbitrary", "arbitrary")),
        name="conformer_conv",
    )(gb, gb, gb, w, bias, ln_g, ln_b)


def _merge_kernel(*refs, with_router):
    oa_ref, ob_ref, oc_ref, x_ref, gate_ref, shift_ref, scale_ref, g2_ref, w_ref = refs[:9]
    if with_router:
        rw_ref, xo_ref, h2_ref, lg_ref = refs[9:]
    else:
        xo_ref, h2_ref = refs[9:]
    tm = x_ref.shape[1]
    wa = oa_ref.shape[2]
    wb = wa + ob_ref.shape[2]
    y = _dot(oa_ref[0], w_ref[0:wa, :]) + _dot(ob_ref[0], w_ref[wa:wb, :]) + _dot(oc_ref[0], w_ref[wb:, :])
    xn = x_ref[0] + gate_ref[0] * y
    xo_ref[0] = xn
    ms = jnp.mean(xn * xn, axis=-1, keepdims=True)
    h2 = xn * lax.rsqrt(ms + EPS) * g2_ref[...] * (1.0 + scale_ref[0]) + shift_ref[0]
    if with_router:
        _store_row_tiles(h2_ref, h2, tm)
        lg_ref[...] = lax.dot_general(rw_ref[...], h2, (((1,), (1,)), ((), ())),
                                      preferred_element_type=F32, precision=lax.Precision.HIGHEST)
    else:
        h2_ref[0] = h2.astype(BF16)


def _merge(oa, ob, oc, x, gate, shift, scale, g2, w_out, router_wt, tm):
    b, s, d = x.shape
    row = lambda bb, i: (bb, i, 0)
    per_b = lambda bb, i: (bb, 0, 0)
    const2 = lambda bb, i: (0, 0)
    nt = s // tm
    in_specs = [
        pl.BlockSpec((1, tm, oa.shape[2]), row), pl.BlockSpec((1, tm, ob.shape[2]), row),
        pl.BlockSpec((1, tm, oc.shape[2]), row), pl.BlockSpec((1, tm, d), row),
        pl.BlockSpec((1, 1, d), per_b), pl.BlockSpec((1, 1, d), per_b), pl.BlockSpec((1, 1, d), per_b),
        pl.BlockSpec((1, d), const2), pl.BlockSpec((d, d), const2),
    ]
    out_shape = [jax.ShapeDtypeStruct((b, s, d), F32)]
    out_specs = [pl.BlockSpec((1, tm, d), row)]
    args = [oa, ob, oc, x, gate, shift, scale, g2, w_out]
    with_router = router_wt is not None
    if not with_router:
        out_shape.append(jax.ShapeDtypeStruct((b, s, d), BF16))
        out_specs.append(pl.BlockSpec((1, tm, d), row))
    else:
        out_shape.append(jax.ShapeDtypeStruct((b * s * ROW_TILE, LANES), F32))
        out_specs.append(pl.BlockSpec((tm * ROW_TILE, LANES), lambda bb, i: (bb * nt + i, 0)))
        in_specs.append(pl.BlockSpec((N_EXPERTS, d), const2))
        out_shape.append(jax.ShapeDtypeStruct((N_EXPERTS, b * s), F32))
        out_specs.append(pl.BlockSpec((N_EXPERTS, tm), lambda bb, i: (0, bb * nt + i)))
        args.append(router_wt)
    return pl.pallas_call(
        functools.partial(_merge_kernel, with_router=with_router),
        out_shape=tuple(out_shape), grid=(b, nt), in_specs=in_specs, out_specs=tuple(out_specs),
        compiler_params=_cparams(("arbitrary", "arbitrary")), name="merge_heads",
    )(*args)


def _ffn_kernel(h_ref, x_ref, gate_ref, wg_ref, wu_ref, wd_ref, o_ref, *, tf):
    h = h_ref[...]
    ff = wg_ref.shape[1]
    acc = jnp.zeros(x_ref.shape, F32)
    for f in range(0, ff, tf):
        g = _dot(h, wg_ref[:, f:f + tf])
        u = _dot(h, wu_ref[:, f:f + tf])
        a = (g * _sigmoid(g) * u).astype(BF16)
        acc = acc + _dot(a, wd_ref[f:f + tf, :])
    o_ref[...] = x_ref[...] + gate_ref[0] * acc


def _dense_ffn(h2, x, gate, wg, wu, wd, tm):
    n, d = x.shape
    ff = wg.shape[1]
    s = n // gate.shape[0]
    row = lambda i: (i, 0)
    const2 = lambda i: (0, 0)
    resident = pl.Buffered(1)
    return pl.pallas_call(
        functools.partial(_ffn_kernel, tf=MXU_DIM),
        out_shape=jax.ShapeDtypeStruct((n, d), F32),
        grid=(n // tm,),
        in_specs=[
            pl.BlockSpec((tm, d), row), pl.BlockSpec((tm, d), row),
            pl.BlockSpec((1, 1, d), lambda i: ((i * tm) // s, 0, 0)),
            pl.BlockSpec((d, ff), const2, pipeline_mode=resident),
            pl.BlockSpec((d, ff), const2, pipeline_mode=resident),
            pl.BlockSpec((ff, d), const2, pipeline_mode=resident),
        ],
        out_specs=pl.BlockSpec((tm, d), row),
        compiler_params=_cparams(("arbitrary",)), name="dense_ffn",
    )(h2, x, gate, wg, wu, wd)


def _top2(lg):
    sub = lax.broadcasted_iota(I32, lg.shape, 0)
    l1 = jnp.max(lg, axis=0, keepdims=True)
    i1 = jnp.min(jnp.where(lg == l1, sub, N_EXPERTS), axis=0, keepdims=True)
    m1 = sub == i1
    lg2 = jnp.where(m1, -jnp.inf, lg)
    l2 = jnp.max(lg2, axis=0, keepdims=True)
    i2 = jnp.min(jnp.where(lg2 == l2, sub, N_EXPERTS), axis=0, keepdims=True)
    m2 = sub == i2
    return l1, l2, m1, m2


def _sublane_cumsum(x):
    sub = lax.broadcasted_iota(I32, x.shape, 0)
    for sh in (1, 2, 4):
        x = x + jnp.where(sub >= sh, pltpu.roll(x, sh, 0), 0.0)
    return x


def _route_kernel(lg_ref, tri_ref, dest_ref, gates_ref, be_ref, base_ref, start_ref, *, block_rows):
    phase = pl.program_id(0)
    j = pl.program_id(1)
    l1, l2, m1, m2 = _top2(lg_ref[...])
    e = jnp.where(m1 | m2, 1.0, 0.0).astype(F32)
    cnt = jnp.sum(e, axis=1, keepdims=True)

    @pl.when((phase == 0) & (j == 0))
    def _():
        base_ref[...] = jnp.zeros(base_ref.shape, F32)

    @pl.when((phase == 1) & (j == 0))
    def _():
        counts = base_ref[...]
        nblk = jnp.floor((counts + (block_rows - 1)) * (1.0 / block_rows))
        end_blk = _sublane_cumsum(nblk)
        start_ref[...] = (end_blk - nblk) * block_rows
        blk = lax.broadcasted_iota(I32, be_ref.shape, 1).astype(F32)
        owner = jnp.sum(jnp.where(end_blk[:, :1] <= blk, 1.0, 0.0), axis=0, keepdims=True)
        be_ref[...] = jnp.broadcast_to(jnp.minimum(owner, N_EXPERTS - 1.0), be_ref.shape).astype(I32)
        base_ref[...] = jnp.zeros(base_ref.shape, F32)

    @pl.when(phase == 1)
    def _():
        prefix = _dot(e.astype(BF16), tri_ref[...]) + base_ref[:, :1] + start_ref[:, :1]
        d1 = jnp.sum(jnp.where(m1, prefix, 0.0), axis=0, keepdims=True)
        d2 = jnp.sum(jnp.where(m2, prefix, 0.0), axis=0, keepdims=True)
        sub = lax.broadcasted_iota(I32, dest_ref.shape, 0)
        dest_ref[...] = jnp.where(sub == 0, d1, jnp.where(sub == 1, d2, 0.0)).astype(I32)
        ex = jnp.exp(l2 - l1)
        g1 = 1.0 / (1.0 + ex)
        g2 = ex / (1.0 + ex)
        half = lax.broadcasted_iota(I32, (LANES, lg_ref.shape[1]), 0) < LANES // 2
        gates_ref[...] = jnp.where(half, g1, g2).T

    base_ref[...] = base_ref[...] + cnt


def _route(logits_t, block_rows, n_blocks_pad, tr):
    n = logits_t.shape[1]
    tri = jnp.asarray(np.triu(np.ones((tr, tr), np.float32), k=1), BF16)
    return pl.pallas_call(
        functools.partial(_route_kernel, block_rows=block_rows),
        out_shape=(jax.ShapeDtypeStruct((N_EXPERTS, n), I32),
                   jax.ShapeDtypeStruct((n, LANES), F32),
                   jax.ShapeDtypeStruct((N_EXPERTS, n_blocks_pad), I32)),
        grid=(2, n // tr),
        in_specs=[pl.BlockSpec((N_EXPERTS, tr), lambda p, j: (0, j)),
                  pl.BlockSpec((tr, tr), lambda p, j: (0, 0))],
        out_specs=(pl.BlockSpec((N_EXPERTS, tr), lambda p, j: (0, j * p)),
                   pl.BlockSpec((tr, LANES), lambda p, j: (j * p, 0)),
                   pl.BlockSpec((N_EXPERTS, n_blocks_pad), lambda p, j: (0, 0))),
        scratch_shapes=[pltpu.VMEM((N_EXPERTS, LANES), F32), pltpu.VMEM((N_EXPERTS, LANES), F32)],
        compiler_params=_cparams(("arbitrary", "arbitrary")), name="moe_route",
    )(logits_t, tri)


def _row_copy(src_hbm, src_row, dst_hbm, dst_row, sem):
    src = pl.ds(pl.multiple_of(src_row * ROW_TILE, ROW_TILE), ROW_TILE)
    dst = pl.ds(pl.multiple_of(dst_row * ROW_TILE, ROW_TILE), ROW_TILE)
    return pltpu.make_async_copy(src_hbm.at[src], dst_hbm.at[dst], sem)


def _scatter_kernel(d1_ref, d2_ref, src_ref, init_hbm, out_hbm, sem, *, rows):
    del init_hbm

    def start(r, c):
        _row_copy(src_ref, r, out_hbm, d1_ref[0, 0, r], sem).start(priority=0)
        _row_copy(src_ref, r, out_hbm, d2_ref[0, 0, r], sem).start(priority=1)
        return c

    lax.fori_loop(0, rows, start, 0, unroll=DMA_UNROLL)
    for _ in range(2):
        pltpu.make_async_copy(src_ref, out_hbm.at[pl.ds(0, rows * ROW_TILE)], sem).wait()


def _scatter_rows(src, d1, d2, total_rows, rows):
    n = d1.shape[0]
    idx_spec = pl.BlockSpec((1, 1, rows), lambda i: (i, 0, 0), memory_space=pltpu.SMEM)
    any_spec = pl.BlockSpec(memory_space=pl.ANY)
    return pl.pallas_call(
        functools.partial(_scatter_kernel, rows=rows),
        out_shape=jax.ShapeDtypeStruct((total_rows * ROW_TILE, LANES), src.dtype),
        grid=(n // rows,),
        in_specs=[idx_spec, idx_spec, pl.BlockSpec((rows * ROW_TILE, LANES), lambda i: (i, 0)), any_spec],
        out_specs=any_spec,
        scratch_shapes=[pltpu.SemaphoreType.DMA(())],
        input_output_aliases={3: 0},
        compiler_params=pltpu.CompilerParams(dimension_semantics=("arbitrary",), has_side_effects=True),
        name="moe_scatter_rows",
    )(d1.reshape(n // rows, 1, rows), d2.reshape(n // rows, 1, rows), src,
      jnp.zeros((total_rows * ROW_TILE, LANES), src.dtype))


def _expert_kernel(be_ref, x_ref, wg_ref, wu_ref, wd_ref, o_ref, *, block_rows, tf):
    del be_ref
    x = _load_row_tiles(x_ref, block_rows).astype(BF16)
    ff = wg_ref.shape[2]
    acc = jnp.zeros((block_rows, wd_ref.shape[2]), F32)
    for f in range(0, ff, tf):
        g = _dot(x, wg_ref[0, :, f:f + tf])
        u = _dot(x, wu_ref[0, :, f:f + tf])
        a = (g * _sigmoid(g) * u).astype(BF16)
        acc = acc + _dot(a, wd_ref[0, f:f + tf, :])
    _store_row_tiles(o_ref, acc, block_rows)


def _expert_ffn(xb, blk_expert, wg, wu, wd, block_rows):
    d, ff = wg.shape[1], wg.shape[2]
    rows = xb.shape[0] // ROW_TILE
    resident = pl.Buffered(1)
    grid_spec = pltpu.PrefetchScalarGridSpec(
        num_scalar_prefetch=1,
        grid=(rows // block_rows,),
        in_specs=[
            pl.BlockSpec((block_rows * ROW_TILE, LANES), lambda i, be: (i, 0)),
            pl.BlockSpec((1, d, ff), lambda i, be: (be[i], 0, 0), pipeline_mode=resident),
            pl.BlockSpec((1, d, ff), lambda i, be: (be[i], 0, 0), pipeline_mode=resident),
            pl.BlockSpec((1, ff, d), lambda i, be: (be[i], 0, 0), pipeline_mode=resident),
        ],
        out_specs=pl.BlockSpec((block_rows * ROW_TILE, LANES), lambda i, be: (i, 0)),
    )
    return pl.pallas_call(
        functools.partial(_expert_kernel, block_rows=block_rows, tf=MXU_DIM),
        out_shape=jax.ShapeDtypeStruct(xb.shape, F32), grid_spec=grid_spec,
        compiler_params=_cparams(("arbitrary",)), name="moe_expert_ffn",
    )(blk_expert, xb, wg, wu, wd)


def _combine_kernel(d1_ref, d2_ref, x_ref, yb_hbm, gates_ref, gate_ref, fg_ref, o_ref, y1_ref, y2_ref, sem):
    tm = x_ref.shape[0]

    def start(r, c):
        _row_copy(yb_hbm, d1_ref[0, 0, r], y1_ref, r, sem).start(priority=0)
        _row_copy(yb_hbm, d2_ref[0, 0, r], y2_ref, r, sem).start(priority=1)
        return c

    lax.fori_loop(0, tm, start, 0, unroll=DMA_UNROLL)
    for y_ref in (y1_ref, y2_ref):
        pltpu.make_async_copy(yb_hbm.at[pl.ds(0, tm * ROW_TILE)], y_ref, sem).wait()

    gts = gates_ref[...]
    y = (gts[:, 0:1] * _load_row_tiles(y1_ref, tm)
         + gts[:, LANES // 2:LANES // 2 + 1] * _load_row_tiles(y2_ref, tm))
    xn = x_ref[...] + gate_ref[0] * y
    ms = jnp.mean(xn * xn, axis=-1, keepdims=True)
    o_ref[...] = xn * lax.rsqrt(ms + EPS) * fg_ref[...]


def _combine_final(x, yb, d1, d2, gates, gate, final_g, tm):
    n, d = x.shape
    s = n // gate.shape[0]
    row = lambda i: (i, 0)
    idx_spec = pl.BlockSpec((1, 1, tm), lambda i: (i, 0, 0), memory_space=pltpu.SMEM)
    return pl.pallas_call(
        _combine_kernel,
        out_shape=jax.ShapeDtypeStruct((n, d), F32),
        grid=(n // tm,),
        in_specs=[idx_spec, idx_spec,
                  pl.BlockSpec((tm, d), row),
                  pl.BlockSpec(memory_space=pl.ANY),
                  pl.BlockSpec((tm, LANES), row),
                  pl.BlockSpec((1, 1, d), lambda i: ((i * tm) // s, 0, 0)),
                  pl.BlockSpec((1, d), lambda i: (0, 0))],
        out_specs=pl.BlockSpec((tm, d), row),
        scratch_shapes=[pltpu.VMEM((tm * ROW_TILE, LANES), F32), pltpu.VMEM((tm * ROW_TILE, LANES), F32),
                        pltpu.SemaphoreType.DMA(())],
        compiler_params=_cparams(("arbitrary",)), name="moe_combine_final",
    )(d1.reshape(n // tm, 1, tm), d2.reshape(n // tm, 1, tm), x, yb, gates, gate, final_g)


def _rope_tables(s, dim):
    half = dim // 2
    t = jnp.arange(s)
    inv = 1.0 / (ROPE_THETA ** (jnp.arange(0, half, 2, dtype=F32) / half))
    ang_r = (t // GRID_W).astype(F32)[:, None] * inv
    ang_c = (t % GRID_W).astype(F32)[:, None] * inv
    ang = jnp.concatenate([ang_r, ang_r, ang_c, ang_c], axis=-1)
    reps = LANES // dim
    return jnp.tile(jnp.cos(ang), (1, reps)), jnp.tile(jnp.sin(ang), (1, reps))


def _rotate_matrix(dim):
    q = dim // 4
    p = np.zeros((MXU_DIM, MXU_DIM), np.float32)
    for j in range(MXU_DIM):
        if (j % (2 * q)) < q:
            p[j + q, j] = -1.0
        else:
            p[j - q, j] = 1.0
    return jnp.asarray(p, BF16)


def _head_mean_matrix():
    m = np.kron(np.eye(MXU_DIM // HEAD_V, dtype=np.float32), np.full((HEAD_V, HEAD_V), 1.0 / HEAD_V, np.float32))
    return jnp.asarray(m, BF16)


def _widen_values(w, heads):
    d = w.shape[0]
    w = w.reshape(d, heads, HEAD_V)
    return jnp.concatenate([w, jnp.zeros_like(w)], axis=-1).reshape(d, heads * LANES)


def _widen_in_proj(w):
    qa, ka, va, gb, qc, kc, vc = jnp.split(w, [256, 512, 768, 1280, 1792, 1920], axis=1)
    return jnp.concatenate([qa, ka, _widen_values(va, DIFF_HEADS), gb, qc, kc, _widen_values(vc, GQA_KV)],
                           axis=1).astype(BF16)


def kernel(x, c, ctx, c_ctx, ada_w, ada_b, norm1_g, norm2_g, w_in, w_out, lam_q1, lam_k1, lam_q2, lam_k2,
           diff_subln_g, conv_w, conv_b, conv_ln_g, conv_ln_b, q_norm_g, k_norm_g, ffn_gate, ffn_up, ffn_down,
           router_w, moe_gate, moe_up, moe_down, final_g):
    b, s, d = x.shape
    sc = ctx.shape[1]
    depth = ada_w.shape[0]
    n = b * s
    assert depth % 2 == 0, "the final RMSNorm is fused into the MoE combine of the last (odd) layer"

    tm = min(512, s)
    tmc = min(512, sc)
    tq = min(512, s)
    tq_diff = min(1024, s)
    tqc = min(256, sc)

    tabs_x = _rope_tables(s, DIFF_QK) + _rope_tables(s, HEAD_V)
    ones_c, zeros_c = jnp.ones((sc, LANES), F32), jnp.zeros((sc, LANES), F32)
    tabs_c = (ones_c, zeros_c, ones_c, zeros_c)
    mats = (_rotate_matrix(DIFF_QK), _rotate_matrix(HEAD_V), _head_mean_matrix())

    cc = jnp.zeros((16, d), F32).at[:b].set(c).at[b].set(c_ctx)

    for i in range(depth):
        last = i == depth - 1
        lam_init = 0.8 - 0.6 * math.exp(-0.3 * i)
        mod_all = _ada_mod(cc, ada_w[i], ada_b[i])
        mod = mod_all[:b].reshape(b, 6, 1, d)
        modc = jnp.broadcast_to(mod_all[b].reshape(1, 6, 1, d), (b, 6, 1, d))

        w_aug = _widen_in_proj(w_in[i])
        g1 = norm1_g[i].reshape(1, d)
        qg = jnp.tile(q_norm_g[i], GQA_HEADS).reshape(1, -1)
        kg = jnp.tile(k_norm_g[i], GQA_KV).reshape(1, -1)
        lam_vecs = jnp.stack([lam_q1[i], lam_k1[i], lam_q2[i], lam_k2[i]]).astype(F32)
        subln = diff_subln_g[i].reshape(1, HEAD_V)
        conv_args = (conv_w[i], conv_b[i].reshape(1, -1), conv_ln_g[i].reshape(1, -1), conv_ln_b[i].reshape(1, -1))
        w_o = w_out[i].astype(BF16)
        g2 = norm2_g[i].reshape(1, d)

        qa, kat, va, gb, qc, kct, vc = _in_projection(x, mod[:, 0], mod[:, 1], g1, w_aug, tabs_x, mats, qg, kg, tm)
        qa_x, kat_x, va_x, gb_x, qc_x, kct_x, vc_x = _in_projection(
            ctx, modc[:, 0], modc[:, 1], g1, w_aug, tabs_c, mats, qg, kg, tmc)

        oa = _diff_attention(qa, [kat, kat_x], [va, va_x], lam_vecs, subln, lam_init, tq_diff)
        ob = _conformer_conv(gb, *conv_args, tm)
        oc = _gqa_attention(qc, [kct, kct_x], [vc, vc_x], tq)

        j = i // 2
        if i % 2 == 0:
            x, h2 = _merge(oa, ob, oc, x, mod[:, 2], mod[:, 3], mod[:, 4], g2, w_o, None, tm)
            wg, wu, wd = ffn_gate[j].astype(BF16), ffn_up[j].astype(BF16), ffn_down[j].astype(BF16)
            x = _dense_ffn(h2.reshape(n, d), x.reshape(n, d), mod[:, 5], wg, wu, wd, tm).reshape(b, s, d)
        else:
            rwt = router_w[j].T.astype(F32)
            x, h2, logits_t = _merge(oa, ob, oc, x, mod[:, 2], mod[:, 3], mod[:, 4], g2, w_o, rwt, tm)
            block_rows = 512 if n >= 8192 else 256
            n_blocks = (2 * n) // block_rows + N_EXPERTS
            n_blocks_pad = -(-n_blocks // LANES) * LANES
            dest, gates, blk_e = _route(logits_t, block_rows, n_blocks_pad, min(512, n))
            xb = _scatter_rows(h2, dest[0], dest[1], n_blocks * block_rows, min(256, n))
            yb = _expert_ffn(xb, blk_e[0, :n_blocks], moe_gate[j].astype(BF16), moe_up[j].astype(BF16),
                             moe_down[j].astype(BF16), block_rows)
            assert last
            x = _combine_final(x.reshape(n, d), yb, dest[0], dest[1], gates, mod[:, 5], final_g.reshape(1, d),
                               tm).reshape(b, s, d)

        if not last:
            oa_x = _diff_attention(qa_x, [kat_x], [va_x], lam_vecs, subln, lam_init, tqc)
            ob_x = _conformer_conv(gb_x, *conv_args, tmc)
            oc_x = _gqa_attention(qc_x, [kct_x], [vc_x], tqc)
            assert i % 2 == 0, "context tokens only ever pass through dense channel mixers"
            ctx, hc2 = _merge(oa_x, ob_x, oc_x, ctx, modc[:, 2], modc[:, 3], modc[:, 4], g2, w_o, None, tmc)
            ctx = _dense_ffn(hc2.reshape(b * sc, d), ctx.reshape(b * sc, d), modc[:, 5], wg, wu, wd,
                             tmc).reshape(b, sc, d)

    return x
```

```python
import functools
import math

import numpy as np
import jax
import jax.numpy as jnp
from jax import lax
from jax.experimental import pallas as pl
from jax.experimental.pallas import tpu as pltpu

F32 = jnp.float32
BF16 = jnp.bfloat16
I32 = jnp.int32

EPS = 1e-6
ROPE_THETA = 10000.0
GRID_W = 64

DIFF_HEADS = 4
DIFF_QK = 32
HEAD_V = 64
GQA_HEADS = 8
GQA_KV = 2
GQA_GROUP = GQA_HEADS // GQA_KV
CONV_K = 31
N_EXPERTS = 8
LOG2E = math.log2(math.e)

LANES = 128
SUBLANES = 8
MXU_DIM = 256
VMEM_LIMIT = 52 * 1024 * 1024
NEG_BIG = -1e30
V_DTYPE = jnp.float8_e4m3fn
P_E4M3 = (jnp.float8_e4m3fn, 8.5)
P_E5M2 = (jnp.float8_e5m2, 15.5)
P_HEADROOM_RUNNING = 8.0
ATTN_UNROLL = 4
DMA_UNROLL = 8

C_QA, C_KA, C_VA, C_GB, C_QC, C_KC, C_VC, C_END = 0, 256, 512, 1024, 1536, 2048, 2176, 2432


def _cparams(semantics):
    return pltpu.CompilerParams(dimension_semantics=semantics, vmem_limit_bytes=VMEM_LIMIT)


def _dot(a, b):
    return jnp.dot(a, b, preferred_element_type=F32)


def _sigmoid(z):
    return 1.0 / (1.0 + jnp.exp(-z))


ROW_TILE = 8


def _store_row_tiles(ref, val, rows):
    for a in range(ROW_TILE):
        ref[pl.ds(a, rows, stride=ROW_TILE), :] = val[:, a * LANES:(a + 1) * LANES]


def _load_row_tiles(ref, rows):
    return jnp.concatenate([ref[pl.ds(a, rows, stride=ROW_TILE), :] for a in range(ROW_TILE)], axis=1)


def _mod_kernel(c_ref, w_ref, b_ref, o_ref):
    c = c_ref[...]
    s = c * _sigmoid(c)
    o_ref[...] = jnp.dot(s, w_ref[...], preferred_element_type=F32, precision=lax.Precision.HIGHEST) + b_ref[...]


def _ada_mod(cc, w, b):
    rows, d = cc.shape
    n = w.shape[1]
    tn = d
    return pl.pallas_call(
        _mod_kernel,
        out_shape=jax.ShapeDtypeStruct((rows, n), F32),
        grid=(n // tn,),
        in_specs=[pl.BlockSpec((rows, d), lambda j: (0, 0)),
                  pl.BlockSpec((d, tn), lambda j: (0, j)),
                  pl.BlockSpec((1, tn), lambda j: (0, j))],
        out_specs=pl.BlockSpec((rows, tn), lambda j: (0, j)),
        compiler_params=_cparams(("arbitrary",)),
        name="ada_mod",
    )(cc, w, b.reshape(1, n))


def _inproj_kernel(x_ref, shift_ref, scale_ref, g_ref, w_ref, cosa_ref, sina_ref, cosc_ref, sinc_ref,
                   pa_ref, pc_ref, hm_ref, qg_ref, kg_ref,
                   qa_o, kat_o, va_o, gb_o, qc_o, kct_o, vc_o, *, qa_scale, qc_scale):
    x = x_ref[0]
    ms = jnp.mean(x * x, axis=-1, keepdims=True)
    h = x * lax.rsqrt(ms + EPS) * g_ref[...]
    h = h * (1.0 + scale_ref[0]) + shift_ref[0]
    hb = h.astype(BF16)

    def proj(lo, hi):
        return _dot(hb, w_ref[:, lo:hi])

    def blockmat(y, m_ref):
        yb = y.astype(BF16)
        w = y.shape[1]
        if w == LANES:
            return _dot(yb, m_ref[:LANES, :LANES])
        return jnp.concatenate([_dot(yb[:, c:c + MXU_DIM], m_ref[...]) for c in range(0, w, MXU_DIM)], axis=1)

    def rope(y, cos, sin, p_ref):
        reps = y.shape[1] // LANES
        cos = jnp.tile(cos, (1, reps))
        sin = jnp.tile(sin, (1, reps))
        return y * cos + blockmat(y, p_ref) * sin

    def ones_col(width):
        lane = lax.broadcasted_iota(I32, (1, width), 1)
        return jnp.where(lane % LANES == HEAD_V, 1.0, 0.0).astype(F32)

    cosa, sina, cosc, sinc = cosa_ref[...], sina_ref[...], cosc_ref[...], sinc_ref[...]

    qa_o[0] = (rope(proj(C_QA, C_KA), cosa, sina, pa_ref) * qa_scale).astype(BF16)
    kat_o[0, 0] = rope(proj(C_KA, C_VA), cosa, sina, pa_ref).T.astype(BF16)
    va_o[0] = (proj(C_VA, C_GB) + ones_col(C_GB - C_VA)).astype(V_DTYPE)
    gb_o[0] = proj(C_GB, C_QC).astype(BF16)

    y = proj(C_QC, C_KC)
    yn = y * lax.rsqrt(blockmat(y * y, hm_ref) + EPS) * qg_ref[...]
    qc_o[0] = (rope(yn, cosc, sinc, pc_ref) * qc_scale).astype(BF16)

    y = proj(C_KC, C_VC)
    yn = y * lax.rsqrt(blockmat(y * y, hm_ref) + EPS) * kg_ref[...]
    kct_o[0, 0] = rope(yn, cosc, sinc, pc_ref).T.astype(BF16)

    vc_o[0] = (proj(C_VC, C_END) + ones_col(C_END - C_VC)).astype(V_DTYPE)


def _in_projection(x, shift, scale, g1, w_aug, tabs, mats, qg, kg, tm):
    b, s, d = x.shape
    nt = s // tm
    cosa, sina, cosc, sinc = tabs
    pa, pc, hm = mats
    row = lambda bb, i: (bb, i, 0)
    const2 = lambda bb, i: (0, 0)
    per_b = lambda bb, i: (bb, 0, 0)
    tab = lambda bb, i: (i, 0)
    kern = functools.partial(_inproj_kernel, qa_scale=DIFF_QK ** -0.5 * LOG2E, qc_scale=HEAD_V ** -0.5 * LOG2E)
    out_shape = (
        jax.ShapeDtypeStruct((b, s, 256), BF16),
        jax.ShapeDtypeStruct((b, nt, 256, tm), BF16),
        jax.ShapeDtypeStruct((b, s, 512), V_DTYPE),
        jax.ShapeDtypeStruct((b, s, 512), BF16),
        jax.ShapeDtypeStruct((b, s, 512), BF16),
        jax.ShapeDtypeStruct((b, nt, 128, tm), BF16),
        jax.ShapeDtypeStruct((b, s, 256), V_DTYPE),
    )
    out_specs = (
        pl.BlockSpec((1, tm, 256), row),
        pl.BlockSpec((1, 1, 256, tm), lambda bb, i: (bb, i, 0, 0)),
        pl.BlockSpec((1, tm, 512), row),
        pl.BlockSpec((1, tm, 512), row),
        pl.BlockSpec((1, tm, 512), row),
        pl.BlockSpec((1, 1, 128, tm), lambda bb, i: (bb, i, 0, 0)),
        pl.BlockSpec((1, tm, 256), row),
    )
    in_specs = [
        pl.BlockSpec((1, tm, d), row),
        pl.BlockSpec((1, 1, d), per_b),
        pl.BlockSpec((1, 1, d), per_b),
        pl.BlockSpec((1, d), const2),
        pl.BlockSpec((d, C_END), const2),
        pl.BlockSpec((tm, LANES), tab), pl.BlockSpec((tm, LANES), tab),
        pl.BlockSpec((tm, LANES), tab), pl.BlockSpec((tm, LANES), tab),
        pl.BlockSpec((MXU_DIM, MXU_DIM), const2), pl.BlockSpec((MXU_DIM, MXU_DIM), const2),
        pl.BlockSpec((MXU_DIM, MXU_DIM), const2),
        pl.BlockSpec((1, 512), const2), pl.BlockSpec((1, 128), const2),
    ]
    return pl.pallas_call(
        kern, out_shape=out_shape, grid=(b, nt), in_specs=in_specs, out_specs=out_specs,
        compiler_params=_cparams(("arbitrary", "arbitrary")), name="in_projection",
    )(x, shift, scale, g1, w_aug, cosa, sina, cosc, sinc, pa, pc, hm, qg, kg)


def _attention_sweeps(qms, k_slices, pv_groups, k_refs, v_refs, m_ref, smax_ref, acc_ref, p_format):
    r = qms[0].shape[0]
    p_dtype, p_max_exp = p_format

    def scores(j, kc):
        return _dot(qms[j], kc[k_slices[j], :])

    def sweep(running_max):
        acc_ref[...] = jnp.zeros(acc_ref.shape, F32)
        if running_max:
            m_ref[...] = jnp.full(m_ref.shape, NEG_BIG, F32)
        else:
            smax_ref[...] = jnp.full(smax_ref.shape, NEG_BIG, F32)
            kc0 = k_refs[0][0, 0][:, :MXU_DIM]
            for j in range(len(qms)):
                m0 = jnp.max(scores(j, kc0), axis=-1, keepdims=True)
                m_ref[j * r:(j + 1) * r, :] = jnp.broadcast_to(m0, (r, LANES))

        for k_ref, v_ref in zip(k_refs, v_refs):
            n_chunks, tk = k_ref.shape[1], k_ref.shape[3]

            def body(c, carry, k_ref=k_ref, v_ref=v_ref, tk=tk):
                kc = k_ref[0, c]
                vc = v_ref[0, pl.ds(pl.multiple_of(c * tk, tk), tk), :]
                for ids, v_lanes in pv_groups:
                    ps, alphas = [], []
                    for j in ids:
                        rows = slice(j * r, (j + 1) * r)
                        s = scores(j, kc)
                        m = m_ref[rows, :]
                        if running_max:
                            m_new = jnp.maximum(m, jnp.max(s, axis=-1, keepdims=True) - P_HEADROOM_RUNNING)
                            m_ref[rows, :] = m_new
                            alphas.append(jnp.exp2(m - m_new))
                            m = m_new
                        else:
                            cm = functools.reduce(jnp.maximum, [s[:, l:l + LANES] for l in range(0, tk, LANES)])
                            smax_ref[rows, :] = jnp.maximum(smax_ref[rows, :], cm)
                        d = (s - jnp.tile(m, (1, tk // LANES))).astype(BF16)
                        ps.append(jnp.exp2(d).astype(p_dtype))
                    rows = slice(ids[0] * r, (ids[-1] + 1) * r)
                    pv = _dot(jnp.concatenate(ps, axis=0), vc[:, v_lanes])
                    if running_max:
                        acc_ref[rows, :] = acc_ref[rows, :] * jnp.concatenate(alphas, axis=0) + pv
                    else:
                        acc_ref[rows, :] += pv
                return carry

            unroll = 1 if running_max else math.gcd(n_chunks, ATTN_UNROLL)
            lax.fori_loop(0, n_chunks, body, 0, unroll=unroll)

    sweep(False)
    top_exp = jnp.max(smax_ref[...] - m_ref[...])

    @pl.when(jnp.logical_not(top_exp <= p_max_exp))
    def _():
        sweep(True)


def _gqa_kernel(*refs, n_parts, tq):
    q_ref = refs[0]
    k_refs = refs[1:1 + n_parts]
    v_refs = refs[1 + n_parts:1 + 2 * n_parts]
    o_ref, m_ref, smax_ref, acc_ref = refs[1 + 2 * n_parts:]
    q = q_ref[0]
    qs = jnp.concatenate([q[:, HEAD_V * j:HEAD_V * (j + 1)] for j in range(GQA_GROUP)], axis=0)
    _attention_sweeps([qs], [slice(0, HEAD_V)], [((0,), slice(0, LANES))], k_refs, v_refs,
                      m_ref, smax_ref, acc_ref, P_E4M3)
    acc = acc_ref[...]
    o = acc[:, :HEAD_V] / acc[:, HEAD_V:HEAD_V + 1]
    for j in range(GQA_GROUP):
        o_ref[0, :, HEAD_V * j:HEAD_V * (j + 1)] = o[j * tq:(j + 1) * tq].astype(BF16)


def _gqa_attention(q, k_parts, v_parts, tq):
    b, sq, _ = q.shape
    n_parts = len(k_parts)
    in_specs = [pl.BlockSpec((1, tq, 256), lambda bb, g, i: (bb, i, g))]
    for kp in k_parts:
        in_specs.append(pl.BlockSpec((1, kp.shape[1], HEAD_V, kp.shape[3]), lambda bb, g, i: (bb, 0, g, 0)))
    for vp in v_parts:
        in_specs.append(pl.BlockSpec((1, vp.shape[1], LANES), lambda bb, g, i: (bb, 0, g)))
    return pl.pallas_call(
        functools.partial(_gqa_kernel, n_parts=n_parts, tq=tq),
        out_shape=jax.ShapeDtypeStruct((b, sq, 512), BF16),
        grid=(b, GQA_KV, sq // tq),
        in_specs=in_specs,
        out_specs=pl.BlockSpec((1, tq, 256), lambda bb, g, i: (bb, i, g)),
        scratch_shapes=[pltpu.VMEM((GQA_GROUP * tq, LANES), F32)] * 3,
        compiler_params=_cparams(("arbitrary", "arbitrary", "arbitrary")),
        name="gqa_attention",
    )(q, *k_parts, *v_parts)


def _diff_kernel(*refs, n_parts, tq, lam_init):
    q_ref = refs[0]
    k_refs = refs[1:1 + n_parts]
    v_refs = refs[1 + n_parts:1 + 2 * n_parts]
    lam_ref, sg_ref, o_ref, m_ref, smax_ref, acc_ref = refs[1 + 2 * n_parts:]
    q = q_ref[0]
    qmaps = [q[:, DIFF_QK * j:DIFF_QK * (j + 1)] for j in range(4)]
    k_slices = [slice(DIFF_QK * j, DIFF_QK * (j + 1)) for j in range(4)]
    pv_groups = [((0, 1), slice(0, LANES)), ((2, 3), slice(LANES, 2 * LANES))]
    _attention_sweeps(qmaps, k_slices, pv_groups, k_refs, v_refs, m_ref, smax_ref, acc_ref, P_E5M2)

    lv = lam_ref[...]
    lam = (jnp.exp(jnp.sum(lv[0:1] * lv[1:2], axis=-1, keepdims=True))
           - jnp.exp(jnp.sum(lv[2:3] * lv[3:4], axis=-1, keepdims=True)) + lam_init)
    acc = acc_ref[...]
    for hh in range(2):
        a0 = acc[(2 * hh) * tq:(2 * hh + 1) * tq]
        a1 = acc[(2 * hh + 1) * tq:(2 * hh + 2) * tq]
        o = a0[:, :HEAD_V] / a0[:, HEAD_V:HEAD_V + 1] - lam * (a1[:, :HEAD_V] / a1[:, HEAD_V:HEAD_V + 1])
        ms = jnp.mean(o * o, axis=-1, keepdims=True)
        on = o * lax.rsqrt(ms + EPS) * sg_ref[...] * (1.0 - lam_init)
        o_ref[0, :, HEAD_V * hh:HEAD_V * (hh + 1)] = on.astype(BF16)


def _diff_attention(q, k_parts, v_parts, lam_vecs, subln_g, lam_init, tq):
    b, sq, _ = q.shape
    n_parts = len(k_parts)
    in_specs = [pl.BlockSpec((1, tq, LANES), lambda bb, p, i: (bb, i, p))]
    for kp in k_parts:
        in_specs.append(pl.BlockSpec((1, kp.shape[1], LANES, kp.shape[3]), lambda bb, p, i: (bb, 0, p, 0)))
    for vp in v_parts:
        in_specs.append(pl.BlockSpec((1, vp.shape[1], 2 * LANES), lambda bb, p, i: (bb, 0, p)))
    in_specs.append(pl.BlockSpec((4, DIFF_QK), lambda bb, p, i: (0, 0)))
    in_specs.append(pl.BlockSpec((1, HEAD_V), lambda bb, p, i: (0, 0)))
    return pl.pallas_call(
        functools.partial(_diff_kernel, n_parts=n_parts, tq=tq, lam_init=lam_init),
        out_shape=jax.ShapeDtypeStruct((b, sq, 256), BF16),
        grid=(b, DIFF_HEADS // 2, sq // tq),
        in_specs=in_specs,
        out_specs=pl.BlockSpec((1, tq, LANES), lambda bb, p, i: (bb, i, p)),
        scratch_shapes=[pltpu.VMEM((4 * tq, LANES), F32)] * 3,
        compiler_params=_cparams(("arbitrary", "arbitrary", "arbitrary")),
        name="diff_attention",
    )(q, *k_parts, *v_parts, lam_vecs, subln_g)


CONV_HALO = 16
CONV_ROWS = 64


def _conv_kernel(gb_ref, prev_ref, next_ref, w_ref, b_ref, lg_ref, lb_ref, o_ref, u_ref, sh_ref, *, tm):
    i = pl.program_id(1)
    last = pl.num_programs(1) - 1
    ch = w_ref.shape[1]

    def glu(z):
        z = z.astype(F32)
        return z[:, :ch] * _sigmoid(z[:, ch:])

    u_ref[CONV_HALO:CONV_HALO + tm, :] = glu(gb_ref[0])
    u_ref[0:CONV_HALO, :] = jnp.where(i > 0, glu(prev_ref[0]), 0.0)
    u_ref[CONV_HALO + tm:2 * CONV_HALO + tm, :] = jnp.where(i < last, glu(next_ref[0]), 0.0)

    span = sh_ref.shape[1]
    for r in range(1, SUBLANES):
        sh_ref[r - 1] = u_ref[r:r + span, :]

    off = CONV_HALO - CONV_K // 2
    for r0 in range(0, tm, CONV_ROWS):
        acc = jnp.zeros((CONV_ROWS, ch), F32)
        for j in range(CONV_K):
            phase, base = (off + j) % SUBLANES, r0 + (off + j) // SUBLANES * SUBLANES
            taps = u_ref[base:base + CONV_ROWS, :] if phase == 0 else sh_ref[phase - 1, base:base + CONV_ROWS, :]
            acc = acc + taps * w_ref[j:j + 1, :]
        y = acc + b_ref[...]
        mu = jnp.mean(y, axis=-1, keepdims=True)
        yc = y - mu
        var = jnp.mean(yc * yc, axis=-1, keepdims=True)
        z = yc * lax.rsqrt(var + EPS) * lg_ref[...] + lb_ref[...]
        o_ref[0, r0:r0 + CONV_ROWS, :] = (z * _sigmoid(z)).astype(BF16)


def _conformer_conv(gb, w, bias, ln_g, ln_b, tm):
    b, s, two_ch = gb.shape
    ch = two_ch // 2
    hb = tm // CONV_HALO
    n_halo = s // CONV_HALO
    const2 = lambda bb, i: (0, 0)
    return pl.pallas_call(
        functools.partial(_conv_kernel, tm=tm),
        out_shape=jax.ShapeDtypeStruct((b, s, ch), BF16),
        grid=(b, s // tm),
        in_specs=[
            pl.BlockSpec((1, tm, two_ch), lambda bb, i: (bb, i, 0)),
            pl.BlockSpec((1, CONV_HALO, two_ch), lambda bb, i: (bb, jnp.maximum(i * hb - 1, 0), 0)),
            pl.BlockSpec((1, CONV_HALO, two_ch), lambda bb, i: (bb, jnp.minimum((i + 1) * hb, n_halo - 1), 0)),
            pl.BlockSpec((CONV_K, ch), const2), pl.BlockSpec((1, ch), const2),
            pl.BlockSpec((1, ch), const2), pl.BlockSpec((1, ch), const2),
        ],
        out_specs=pl.BlockSpec((1, tm, ch), lambda bb, i: (bb, i, 0)),
        scratch_shapes=[pltpu.VMEM((tm + 2 * CONV_HALO, ch), F32),
                        pltpu.VMEM((SUBLANES - 1, tm + 2 * CONV_HALO - SUBLANES, ch), F32)],
        compiler_params=_cparams(("arbitrary", "arbitrary")),
        name="conformer_conv",
    )(gb, gb, gb, w, bias, ln_g, ln_b)


def _merge_kernel(*refs, with_router):
    oa_ref, ob_ref, oc_ref, x_ref, gate_ref, shift_ref, scale_ref, g2_ref, w_ref = refs[:9]
    if with_router:
        rw_ref, xo_ref, h2_ref, lg_ref = refs[9:]
    else:
        xo_ref, h2_ref = refs[9:]
    tm = x_ref.shape[1]
    wa = oa_ref.shape[2]
    wb = wa + ob_ref.shape[2]
    y = _dot(oa_ref[0], w_ref[0:wa, :]) + _dot(ob_ref[0], w_ref[wa:wb, :]) + _dot(oc_ref[0], w_ref[wb:, :])
    xn = x_ref[0] + gate_ref[0] * y
    xo_ref[0] = xn
    ms = jnp.mean(xn * xn, axis=-1, keepdims=True)
    h2 = xn * lax.rsqrt(ms + EPS) * g2_ref[...] * (1.0 + scale_ref[0]) + shift_ref[0]
    if with_router:
        _store_row_tiles(h2_ref, h2, tm)
        lg_ref[...] = lax.dot_general(rw_ref[...], h2, (((1,), (1,)), ((), ())),
                                      preferred_element_type=F32, precision=lax.Precision.HIGHEST)
    else:
        h2_ref[0] = h2.astype(BF16)


def _merge(oa, ob, oc, x, gate, shift, scale, g2, w_out, router_wt, tm):
    b, s, d = x.shape
    row = lambda bb, i: (bb, i, 0)
    per_b = lambda bb, i: (bb, 0, 0)
    const2 = lambda bb, i: (0, 0)
    nt = s // tm
    in_specs = [
        pl.BlockSpec((1, tm, oa.shape[2]), row), pl.BlockSpec((1, tm, ob.shape[2]), row),
        pl.BlockSpec((1, tm, oc.shape[2]), row), pl.BlockSpec((1, tm, d), row),
        pl.BlockSpec((1, 1, d), per_b), pl.BlockSpec((1, 1, d), per_b), pl.BlockSpec((1, 1, d), per_b),
        pl.BlockSpec((1, d), const2), pl.BlockSpec((d, d), const2),
    ]
    out_shape = [jax.ShapeDtypeStruct((b, s, d), F32)]
    out_specs = [pl.BlockSpec((1, tm, d), row)]
    args = [oa, ob, oc, x, gate, shift, scale, g2, w_out]
    with_router = router_wt is not None
    if not with_router:
        out_shape.append(jax.ShapeDtypeStruct((b, s, d), BF16))
        out_specs.append(pl.BlockSpec((1, tm, d), row))
    else:
        out_shape.append(jax.ShapeDtypeStruct((b * s * ROW_TILE, LANES), F32))
        out_specs.append(pl.BlockSpec((tm * ROW_TILE, LANES), lambda bb, i: (bb * nt + i, 0)))
        in_specs.append(pl.BlockSpec((N_EXPERTS, d), const2))
        out_shape.append(jax.ShapeDtypeStruct((N_EXPERTS, b * s), F32))
        out_specs.append(pl.BlockSpec((N_EXPERTS, tm), lambda bb, i: (0, bb * nt + i)))
        args.append(router_wt)
    return pl.pallas_call(
        functools.partial(_merge_kernel, with_router=with_router),
        out_shape=tuple(out_shape), grid=(b, nt), in_specs=in_specs, out_specs=tuple(out_specs),
        compiler_params=_cparams(("arbitrary", "arbitrary")), name="merge_heads",
    )(*args)


def _ffn_kernel(h_ref, x_ref, gate_ref, wg_ref, wu_ref, wd_ref, o_ref, *, tf):
    h = h_ref[...]
    ff = wg_ref.shape[1]
    acc = jnp.zeros(x_ref.shape, F32)
    for f in range(0, ff, tf):
        g = _dot(h, wg_ref[:, f:f + tf])
        u = _dot(h, wu_ref[:, f:f + tf])
        a = (g * _sigmoid(g) * u).astype(BF16)
        acc = acc + _dot(a, wd_ref[f:f + tf, :])
    o_ref[...] = x_ref[...] + gate_ref[0] * acc


def _dense_ffn(h2, x, gate, wg, wu, wd, tm):
    n, d = x.shape
    ff = wg.shape[1]
    s = n // gate.shape[0]
    row = lambda i: (i, 0)
    const2 = lambda i: (0, 0)
    resident = pl.Buffered(1)
    return pl.pallas_call(
        functools.partial(_ffn_kernel, tf=MXU_DIM),
        out_shape=jax.ShapeDtypeStruct((n, d), F32),
        grid=(n // tm,),
        in_specs=[
            pl.BlockSpec((tm, d), row), pl.BlockSpec((tm, d), row),
            pl.BlockSpec((1, 1, d), lambda i: ((i * tm) // s, 0, 0)),
            pl.BlockSpec((d, ff), const2, pipeline_mode=resident),
            pl.BlockSpec((d, ff), const2, pipeline_mode=resident),
            pl.BlockSpec((ff, d), const2, pipeline_mode=resident),
        ],
        out_specs=pl.BlockSpec((tm, d), row),
        compiler_params=_cparams(("arbitrary",)), name="dense_ffn",
    )(h2, x, gate, wg, wu, wd)


def _top2(lg):
    sub = lax.broadcasted_iota(I32, lg.shape, 0)
    l1 = jnp.max(lg, axis=0, keepdims=True)
    i1 = jnp.min(jnp.where(lg == l1, sub, N_EXPERTS), axis=0, keepdims=True)
    m1 = sub == i1
    lg2 = jnp.where(m1, -jnp.inf, lg)
    l2 = jnp.max(lg2, axis=0, keepdims=True)
    i2 = jnp.min(jnp.where(lg2 == l2, sub, N_EXPERTS), axis=0, keepdims=True)
    m2 = sub == i2
    return l1, l2, m1, m2


def _sublane_cumsum(x):
    sub = lax.broadcasted_iota(I32, x.shape, 0)
    for sh in (1, 2, 4):
        x = x + jnp.where(sub >= sh, pltpu.roll(x, sh, 0), 0.0)
    return x


def _route_kernel(lg_ref, tri_ref, dest_ref, gates_ref, be_ref, base_ref, start_ref, *, block_rows):
    phase = pl.program_id(0)
    j = pl.program_id(1)
    l1, l2, m1, m2 = _top2(lg_ref[...])
    e = jnp.where(m1 | m2, 1.0, 0.0).astype(F32)
    cnt = jnp.sum(e, axis=1, keepdims=True)

    @pl.when((phase == 0) & (j == 0))
    def _():
        base_ref[...] = jnp.zeros(base_ref.shape, F32)

    @pl.when((phase == 1) & (j == 0))
    def _():
        counts = base_ref[...]
        nblk = jnp.floor((counts + (block_rows - 1)) * (1.0 / block_rows))
        end_blk = _sublane_cumsum(nblk)
        start_ref[...] = (end_blk - nblk) * block_rows
        blk = lax.broadcasted_iota(I32, be_ref.shape, 1).astype(F32)
        owner = jnp.sum(jnp.where(end_blk[:, :1] <= blk, 1.0, 0.0), axis=0, keepdims=True)
        be_ref[...] = jnp.broadcast_to(jnp.minimum(owner, N_EXPERTS - 1.0), be_ref.shape).astype(I32)
        base_ref[...] = jnp.zeros(base_ref.shape, F32)

    @pl.when(phase == 1)
    def _():
        prefix = _dot(e.astype(BF16), tri_ref[...]) + base_ref[:, :1] + start_ref[:, :1]
        d1 = jnp.sum(jnp.where(m1, prefix, 0.0), axis=0, keepdims=True)
        d2 = jnp.sum(jnp.where(m2, prefix, 0.0), axis=0, keepdims=True)
        sub = lax.broadcasted_iota(I32, dest_ref.shape, 0)
        dest_ref[...] = jnp.where(sub == 0, d1, jnp.where(sub == 1, d2, 0.0)).astype(I32)
        ex = jnp.exp(l2 - l1)
        g1 = 1.0 / (1.0 + ex)
        g2 = ex / (1.0 + ex)
        half = lax.broadcasted_iota(I32, (LANES, lg_ref.shape[1]), 0) < LANES // 2
        gates_ref[...] = jnp.where(half, g1, g2).T

    base_ref[...] = base_ref[...] + cnt


def _route(logits_t, block_rows, n_blocks_pad, tr):
    n = logits_t.shape[1]
    tri = jnp.asarray(np.triu(np.ones((tr, tr), np.float32), k=1), BF16)
    return pl.pallas_call(
        functools.partial(_route_kernel, block_rows=block_rows),
        out_shape=(jax.ShapeDtypeStruct((N_EXPERTS, n), I32),
                   jax.ShapeDtypeStruct((n, LANES), F32),
                   jax.ShapeDtypeStruct((N_EXPERTS, n_blocks_pad), I32)),
        grid=(2, n // tr),
        in_specs=[pl.BlockSpec((N_EXPERTS, tr), lambda p, j: (0, j)),
                  pl.BlockSpec((tr, tr), lambda p, j: (0, 0))],
        out_specs=(pl.BlockSpec((N_EXPERTS, tr), lambda p, j: (0, j * p)),
                   pl.BlockSpec((tr, LANES), lambda p, j: (j * p, 0)),
                   pl.BlockSpec((N_EXPERTS, n_blocks_pad), lambda p, j: (0, 0))),
        scratch_shapes=[pltpu.VMEM((N_EXPERTS, LANES), F32), pltpu.VMEM((N_EXPERTS, LANES), F32)],
        compiler_params=_cparams(("arbitrary", "arbitrary")), name="moe_route",
    )(logits_t, tri)


def _row_copy(src_hbm, src_row, dst_hbm, dst_row, sem):
    src = pl.ds(pl.multiple_of(src_row * ROW_TILE, ROW_TILE), ROW_TILE)
    dst = pl.ds(pl.multiple_of(dst_row * ROW_TILE, ROW_TILE), ROW_TILE)
    return pltpu.make_async_copy(src_hbm.at[src], dst_hbm.at[dst], sem)


def _scatter_kernel(d1_ref, d2_ref, src_ref, init_hbm, out_hbm, sem, *, rows):
    del init_hbm

    def start(r, c):
        _row_copy(src_ref, r, out_hbm, d1_ref[0, 0, r], sem).start(priority=0)
        _row_copy(src_ref, r, out_hbm, d2_ref[0, 0, r], sem).start(priority=1)
        return c

    lax.fori_loop(0, rows, start, 0, unroll=DMA_UNROLL)
    for _ in range(2):
        pltpu.make_async_copy(src_ref, out_hbm.at[pl.ds(0, rows * ROW_TILE)], sem).wait()


def _scatter_rows(src, d1, d2, total_rows, rows):
    n = d1.shape[0]
    idx_spec = pl.BlockSpec((1, 1, rows), lambda i: (i, 0, 0), memory_space=pltpu.SMEM)
    any_spec = pl.BlockSpec(memory_space=pl.ANY)
    return pl.pallas_call(
        functools.partial(_scatter_kernel, rows=rows),
        out_shape=jax.ShapeDtypeStruct((total_rows * ROW_TILE, LANES), src.dtype),
        grid=(n // rows,),
        in_specs=[idx_spec, idx_spec, pl.BlockSpec((rows * ROW_TILE, LANES), lambda i: (i, 0)), any_spec],
        out_specs=any_spec,
        scratch_shapes=[pltpu.SemaphoreType.DMA(())],
        input_output_aliases={3: 0},
        compiler_params=pltpu.CompilerParams(dimension_semantics=("arbitrary",), has_side_effects=True),
        name="moe_scatter_rows",
    )(d1.reshape(n // rows, 1, rows), d2.reshape(n // rows, 1, rows), src,
      jnp.zeros((total_rows * ROW_TILE, LANES), src.dtype))


def _expert_kernel(be_ref, x_ref, wg_ref, wu_ref, wd_ref, o_ref, *, block_rows, tf):
    del be_ref
    x = _load_row_tiles(x_ref, block_rows).astype(BF16)
    ff = wg_ref.shape[2]
    acc = jnp.zeros((block_rows, wd_ref.shape[2]), F32)
    for f in range(0, ff, tf):
        g = _dot(x, wg_ref[0, :, f:f + tf])
        u = _dot(x, wu_ref[0, :, f:f + tf])
        a = (g * _sigmoid(g) * u).astype(BF16)
        acc = acc + _dot(a, wd_ref[0, f:f + tf, :])
    _store_row_tiles(o_ref, acc, block_rows)


def _expert_ffn(xb, blk_expert, wg, wu, wd, block_rows):
    d, ff = wg.shape[1], wg.shape[2]
    rows = xb.shape[0] // ROW_TILE
    resident = pl.Buffered(1)
    grid_spec = pltpu.PrefetchScalarGridSpec(
        num_scalar_prefetch=1,
        grid=(rows // block_rows,),
        in_specs=[
            pl.BlockSpec((block_rows * ROW_TILE, LANES), lambda i, be: (i, 0)),
            pl.BlockSpec((1, d, ff), lambda i, be: (be[i], 0, 0), pipeline_mode=resident),
            pl.BlockSpec((1, d, ff), lambda i, be: (be[i], 0, 0), pipeline_mode=resident),
            pl.BlockSpec((1, ff, d), lambda i, be: (be[i], 0, 0), pipeline_mode=resident),
        ],
        out_specs=pl.BlockSpec((block_rows * ROW_TILE, LANES), lambda i, be: (i, 0)),
    )
    return pl.pallas_call(
        functools.partial(_expert_kernel, block_rows=block_rows, tf=MXU_DIM),
        out_shape=jax.ShapeDtypeStruct(xb.shape, F32), grid_spec=grid_spec,
        compiler_params=_cparams(("arbitrary",)), name="moe_expert_ffn",
    )(blk_expert, xb, wg, wu, wd)


def _combine_kernel(d1_ref, d2_ref, x_ref, yb_hbm, gates_ref, gate_ref, fg_ref, o_ref, y1_ref, y2_ref, sem):
    tm = x_ref.shape[0]

    def start(r, c):
        _row_copy(yb_hbm, d1_ref[0, 0, r], y1_ref, r, sem).start(priority=0)
        _row_copy(yb_hbm, d2_ref[0, 0, r], y2_ref, r, sem).start(priority=1)
        return c

    lax.fori_loop(0, tm, start, 0, unroll=DMA_UNROLL)
    for y_ref in (y1_ref, y2_ref):
        pltpu.make_async_copy(yb_hbm.at[pl.ds(0, tm * ROW_TILE)], y_ref, sem).wait()

    gts = gates_ref[...]
    y = (gts[:, 0:1] * _load_row_tiles(y1_ref, tm)
         + gts[:, LANES // 2:LANES // 2 + 1] * _load_row_tiles(y2_ref, tm))
    xn = x_ref[...] + gate_ref[0] * y
    ms = jnp.mean(xn * xn, axis=-1, keepdims=True)
    o_ref[...] = xn * lax.rsqrt(ms + EPS) * fg_ref[...]


def _combine_final(x, yb, d1, d2, gates, gate, final_g, tm):
    n, d = x.shape
    s = n // gate.shape[0]
    row = lambda i: (i, 0)
    idx_spec = pl.BlockSpec((1, 1, tm), lambda i: (i, 0, 0), memory_space=pltpu.SMEM)
    return pl.pallas_call(
        _combine_kernel,
        out_shape=jax.ShapeDtypeStruct((n, d), F32),
        grid=(n // tm,),
        in_specs=[idx_spec, idx_spec,
                  pl.BlockSpec((tm, d), row),
                  pl.BlockSpec(memory_space=pl.ANY),
                  pl.BlockSpec((tm, LANES), row),
                  pl.BlockSpec((1, 1, d), lambda i: ((i * tm) // s, 0, 0)),
                  pl.BlockSpec((1, d), lambda i: (0, 0))],
        out_specs=pl.BlockSpec((tm, d), row),
        scratch_shapes=[pltpu.VMEM((tm * ROW_TILE, LANES), F32), pltpu.VMEM((tm * ROW_TILE, LANES), F32),
                        pltpu.SemaphoreType.DMA(())],
        compiler_params=_cparams(("arbitrary",)), name="moe_combine_final",
    )(d1.reshape(n // tm, 1, tm), d2.reshape(n // tm, 1, tm), x, yb, gates, gate, final_g)


def _rope_tables(s, dim):
    half = dim // 2
    t = jnp.arange(s)
    inv = 1.0 / (ROPE_THETA ** (jnp.arange(0, half, 2, dtype=F32) / half))
    ang_r = (t // GRID_W).astype(F32)[:, None] * inv
    ang_c = (t % GRID_W).astype(F32)[:, None] * inv
    ang = jnp.concatenate([ang_r, ang_r, ang_c, ang_c], axis=-1)
    reps = LANES // dim
    return jnp.tile(jnp.cos(ang), (1, reps)), jnp.tile(jnp.sin(ang), (1, reps))


def _rotate_matrix(dim):
    q = dim // 4
    p = np.zeros((MXU_DIM, MXU_DIM), np.float32)
    for j in range(MXU_DIM):
        if (j % (2 * q)) < q:
            p[j + q, j] = -1.0
        else:
            p[j - q, j] = 1.0
    return jnp.asarray(p, BF16)


def _head_mean_matrix():
    m = np.kron(np.eye(MXU_DIM // HEAD_V, dtype=np.float32), np.full((HEAD_V, HEAD_V), 1.0 / HEAD_V, np.float32))
    return jnp.asarray(m, BF16)


def _widen_values(w, heads):
    d = w.shape[0]
    w = w.reshape(d, heads, HEAD_V)
    return jnp.concatenate([w, jnp.zeros_like(w)], axis=-1).reshape(d, heads * LANES)


def _widen_in_proj(w):
    qa, ka, va, gb, qc, kc, vc = jnp.split(w, [256, 512, 768, 1280, 1792, 1920], axis=1)
    return jnp.concatenate([qa, ka, _widen_values(va, DIFF_HEADS), gb, qc, kc, _widen_values(vc, GQA_KV)],
                           axis=1).astype(BF16)


def kernel(x, c, ctx, c_ctx, ada_w, ada_b, norm1_g, norm2_g, w_in, w_out, lam_q1, lam_k1, lam_q2, lam_k2,
           diff_subln_g, conv_w, conv_b, conv_ln_g, conv_ln_b, q_norm_g, k_norm_g, ffn_gate, ffn_up, ffn_down,
           router_w, moe_gate, moe_up, moe_down, final_g):
    b, s, d = x.shape
    sc = ctx.shape[1]
    depth = ada_w.shape[0]
    n = b * s
    assert depth % 2 == 0, "the final RMSNorm is fused into the MoE combine of the last (odd) layer"

    tm = min(512, s)
    tmc = min(512, sc)
    tq = min(512, s)
    tq_diff = min(1024, s)
    tqc = min(256, sc)

    tabs_x = _rope_tables(s, DIFF_QK) + _rope_tables(s, HEAD_V)
    ones_c, zeros_c = jnp.ones((sc, LANES), F32), jnp.zeros((sc, LANES), F32)
    tabs_c = (ones_c, zeros_c, ones_c, zeros_c)
    mats = (_rotate_matrix(DIFF_QK), _rotate_matrix(HEAD_V), _head_mean_matrix())

    cc = jnp.zeros((16, d), F32).at[:b].set(c).at[b].set(c_ctx)

    for i in range(depth):
        last = i == depth - 1
        lam_init = 0.8 - 0.6 * math.exp(-0.3 * i)
        mod_all = _ada_mod(cc, ada_w[i], ada_b[i])
        mod = mod_all[:b].reshape(b, 6, 1, d)
        modc = jnp.broadcast_to(mod_all[b].reshape(1, 6, 1, d), (b, 6, 1, d))

        w_aug = _widen_in_proj(w_in[i])
        g1 = norm1_g[i].reshape(1, d)
        qg = jnp.tile(q_norm_g[i], GQA_HEADS).reshape(1, -1)
        kg = jnp.tile(k_norm_g[i], GQA_KV).reshape(1, -1)
        lam_vecs = jnp.stack([lam_q1[i], lam_k1[i], lam_q2[i], lam_k2[i]]).astype(F32)
        subln = diff_subln_g[i].reshape(1, HEAD_V)
        conv_args = (conv_w[i], conv_b[i].reshape(1, -1), conv_ln_g[i].reshape(1, -1), conv_ln_b[i].reshape(1, -1))
        w_o = w_out[i].astype(BF16)
        g2 = norm2_g[i].reshape(1, d)

        qa, kat, va, gb, qc, kct, vc = _in_projection(x, mod[:, 0], mod[:, 1], g1, w_aug, tabs_x, mats, qg, kg, tm)
        qa_x, kat_x, va_x, gb_x, qc_x, kct_x, vc_x = _in_projection(
            ctx, modc[:, 0], modc[:, 1], g1, w_aug, tabs_c, mats, qg, kg, tmc)

        oa = _diff_attention(qa, [kat, kat_x], [va, va_x], lam_vecs, subln, lam_init, tq_diff)
        ob = _conformer_conv(gb, *conv_args, tm)
        oc = _gqa_attention(qc, [kct, kct_x], [vc, vc_x], tq)

        j = i // 2
        if i % 2 == 0:
            x, h2 = _merge(oa, ob, oc, x, mod[:, 2], mod[:, 3], mod[:, 4], g2, w_o, None, tm)
            wg, wu, wd = ffn_gate[j].astype(BF16), ffn_up[j].astype(BF16), ffn_down[j].astype(BF16)
            x = _dense_ffn(h2.reshape(n, d), x.reshape(n, d), mod[:, 5], wg, wu, wd, tm).reshape(b, s, d)
        else:
            rwt = router_w[j].T.astype(F32)
            x, h2, logits_t = _merge(oa, ob, oc, x, mod[:, 2], mod[:, 3], mod[:, 4], g2, w_o, rwt, tm)
            block_rows = 512 if n >= 8192 else 256
            n_blocks = (2 * n) // block_rows + N_EXPERTS
            n_blocks_pad = -(-n_blocks // LANES) * LANES
            dest, gates, blk_e = _route(logits_t, block_rows, n_blocks_pad, min(512, n))
            xb = _scatter_rows(h2, dest[0], dest[1], n_blocks * block_rows, min(256, n))
            yb = _expert_ffn(xb, blk_e[0, :n_blocks], moe_gate[j].astype(BF16), moe_up[j].astype(BF16),
                             moe_down[j].astype(BF16), block_rows)
            assert last
            x = _combine_final(x.reshape(n, d), yb, dest[0], dest[1], gates, mod[:, 5], final_g.reshape(1, d),
                               tm).reshape(b, s, d)

        if not last:
            oa_x = _diff_attention(qa_x, [kat_x], [va_x], lam_vecs, subln, lam_init, tqc)
            ob_x = _conformer_conv(gb_x, *conv_args, tmc)
            oc_x = _gqa_attention(qc_x, [kct_x], [vc_x], tqc)
            assert i % 2 == 0, "context tokens only ever pass through dense channel mixers"
            ctx, hc2 = _merge(oa_x, ob_x, oc_x, ctx, modc[:, 2], modc[:, 3], modc[:, 4], g2, w_o, None, tmc)
            ctx = _dense_ffn(hc2.reshape(b * sc, d), ctx.reshape(b * sc, d), modc[:, 5], wg, wu, wd,
                             tmc).reshape(b, sc, d)

    return x
```

```python
import functools
import math

import numpy as np
import jax
import jax.numpy as jnp
from jax import lax
from jax.experimental import pallas as pl
from jax.experimental.pallas import tpu as pltpu

F32 = jnp.float32
BF16 = jnp.bfloat16
I32 = jnp.int32

EPS = 1e-6
ROPE_THETA = 10000.0
GRID_W = 64

DIFF_HEADS = 4
DIFF_QK = 32
HEAD_V = 64
GQA_HEADS = 8
GQA_KV = 2
GQA_GROUP = GQA_HEADS // GQA_KV
CONV_K = 31
N_EXPERTS = 8
LOG2E = math.log2(math.e)

LANES = 128
SUBLANES = 8
MXU_DIM = 256
VMEM_LIMIT = 52 * 1024 * 1024
NEG_BIG = -1e30
V_DTYPE = jnp.float8_e4m3fn
QK_DTYPE = jnp.float8_e4m3fn
P_E4M3 = (jnp.float8_e4m3fn, 8.5)
P_E5M2 = (jnp.float8_e5m2, 15.5)
P_HEADROOM_RUNNING = 8.0
ATTN_UNROLL = 4
DMA_UNROLL = 8

C_QA, C_KA, C_VA, C_GB, C_QC, C_KC, C_VC, C_END = 0, 256, 512, 1024, 1536, 2048, 2176, 2432


def _cparams(semantics):
    return pltpu.CompilerParams(dimension_semantics=semantics, vmem_limit_bytes=VMEM_LIMIT)


def _dot(a, b):
    return jnp.dot(a, b, preferred_element_type=F32)


def _sigmoid(z):
    return 1.0 / (1.0 + jnp.exp(-z))


ROW_TILE = 8


def _store_row_tiles(ref, val, rows):
    for a in range(ROW_TILE):
        ref[pl.ds(a, rows, stride=ROW_TILE), :] = val[:, a * LANES:(a + 1) * LANES]


def _load_row_tiles(ref, rows):
    return jnp.concatenate([ref[pl.ds(a, rows, stride=ROW_TILE), :] for a in range(ROW_TILE)], axis=1)


def _mod_kernel(c_ref, w_ref, b_ref, o_ref):
    c = c_ref[...]
    s = c * _sigmoid(c)
    o_ref[...] = jnp.dot(s, w_ref[...], preferred_element_type=F32, precision=lax.Precision.HIGHEST) + b_ref[...]


def _ada_mod(cc, w, b):
    rows, d = cc.shape
    n = w.shape[1]
    tn = d
    return pl.pallas_call(
        _mod_kernel,
        out_shape=jax.ShapeDtypeStruct((rows, n), F32),
        grid=(n // tn,),
        in_specs=[pl.BlockSpec((rows, d), lambda j: (0, 0)),
                  pl.BlockSpec((d, tn), lambda j: (0, j)),
                  pl.BlockSpec((1, tn), lambda j: (0, j))],
        out_specs=pl.BlockSpec((rows, tn), lambda j: (0, j)),
        compiler_params=_cparams(("arbitrary",)),
        name="ada_mod",
    )(cc, w, b.reshape(1, n))


def _inproj_kernel(x_ref, shift_ref, scale_ref, g_ref, w_ref, cosa_ref, sina_ref, cosc_ref, sinc_ref,
                   pa_ref, pc_ref, hm_ref, qg_ref, kg_ref,
                   qa_o, kat_o, va_o, gb_o, qc_o, kct_o, vc_o, *, qa_scale, qc_scale):
    x = x_ref[0]
    ms = jnp.mean(x * x, axis=-1, keepdims=True)
    h = x * lax.rsqrt(ms + EPS) * g_ref[...]
    h = h * (1.0 + scale_ref[0]) + shift_ref[0]
    hb = h.astype(BF16)

    def proj(lo, hi):
        return _dot(hb, w_ref[:, lo:hi])

    def blockmat(y, m_ref):
        yb = y.astype(BF16)
        w = y.shape[1]
        if w == LANES:
            return _dot(yb, m_ref[:LANES, :LANES])
        return jnp.concatenate([_dot(yb[:, c:c + MXU_DIM], m_ref[...]) for c in range(0, w, MXU_DIM)], axis=1)

    def rope(y, cos, sin, p_ref):
        reps = y.shape[1] // LANES
        cos = jnp.tile(cos, (1, reps))
        sin = jnp.tile(sin, (1, reps))
        return y * cos + blockmat(y, p_ref) * sin

    def ones_col(width):
        lane = lax.broadcasted_iota(I32, (1, width), 1)
        return jnp.where(lane % LANES == HEAD_V, 1.0, 0.0).astype(F32)

    cosa, sina, cosc, sinc = cosa_ref[...], sina_ref[...], cosc_ref[...], sinc_ref[...]

    qa_o[0] = (rope(proj(C_QA, C_KA), cosa, sina, pa_ref) * qa_scale).astype(QK_DTYPE)
    kat_o[0, 0] = (rope(proj(C_KA, C_VA), cosa, sina, pa_ref) * qa_scale).T.astype(QK_DTYPE)
    va_o[0] = (proj(C_VA, C_GB) + ones_col(C_GB - C_VA)).astype(V_DTYPE)
    gb_o[0] = proj(C_GB, C_QC).astype(BF16)

    y = proj(C_QC, C_KC)
    yn = y * lax.rsqrt(blockmat(y * y, hm_ref) + EPS) * qg_ref[...]
    qc_o[0] = (rope(yn, cosc, sinc, pc_ref) * qc_scale).astype(QK_DTYPE)

    y = proj(C_KC, C_VC)
    yn = y * lax.rsqrt(blockmat(y * y, hm_ref) + EPS) * kg_ref[...]
    kct_o[0, 0] = (rope(yn, cosc, sinc, pc_ref) * qc_scale).T.astype(QK_DTYPE)

    vc_o[0] = (proj(C_VC, C_END) + ones_col(C_END - C_VC)).astype(V_DTYPE)


def _in_projection(x, shift, scale, g1, w_aug, tabs, mats, qg, kg, tm):
    b, s, d = x.shape
    nt = s // tm
    cosa, sina, cosc, sinc = tabs
    pa, pc, hm = mats
    row = lambda bb, i: (bb, i, 0)
    const2 = lambda bb, i: (0, 0)
    per_b = lambda bb, i: (bb, 0, 0)
    tab = lambda bb, i: (i, 0)
    kern = functools.partial(_inproj_kernel, qa_scale=math.sqrt(DIFF_QK ** -0.5 * LOG2E),
                             qc_scale=math.sqrt(HEAD_V ** -0.5 * LOG2E))
    out_shape = (
        jax.ShapeDtypeStruct((b, s, 256), QK_DTYPE),
        jax.ShapeDtypeStruct((b, nt, 256, tm), QK_DTYPE),
        jax.ShapeDtypeStruct((b, s, 512), V_DTYPE),
        jax.ShapeDtypeStruct((b, s, 512), BF16),
        jax.ShapeDtypeStruct((b, s, 512), QK_DTYPE),
        jax.ShapeDtypeStruct((b, nt, 128, tm), QK_DTYPE),
        jax.ShapeDtypeStruct((b, s, 256), V_DTYPE),
    )
    out_specs = (
        pl.BlockSpec((1, tm, 256), row),
        pl.BlockSpec((1, 1, 256, tm), lambda bb, i: (bb, i, 0, 0)),
        pl.BlockSpec((1, tm, 512), row),
        pl.BlockSpec((1, tm, 512), row),
        pl.BlockSpec((1, tm, 512), row),
        pl.BlockSpec((1, 1, 128, tm), lambda bb, i: (bb, i, 0, 0)),
        pl.BlockSpec((1, tm, 256), row),
    )
    in_specs = [
        pl.BlockSpec((1, tm, d), row),
        pl.BlockSpec((1, 1, d), per_b),
        pl.BlockSpec((1, 1, d), per_b),
        pl.BlockSpec((1, d), const2),
        pl.BlockSpec((d, C_END), const2),
        pl.BlockSpec((tm, LANES), tab), pl.BlockSpec((tm, LANES), tab),
        pl.BlockSpec((tm, LANES), tab), pl.BlockSpec((tm, LANES), tab),
        pl.BlockSpec((MXU_DIM, MXU_DIM), const2), pl.BlockSpec((MXU_DIM, MXU_DIM), const2),
        pl.BlockSpec((MXU_DIM, MXU_DIM), const2),
        pl.BlockSpec((1, 512), const2), pl.BlockSpec((1, 128), const2),
    ]
    return pl.pallas_call(
        kern, out_shape=out_shape, grid=(b, nt), in_specs=in_specs, out_specs=out_specs,
        compiler_params=_cparams(("arbitrary", "arbitrary")), name="in_projection",
    )(x, shift, scale, g1, w_aug, cosa, sina, cosc, sinc, pa, pc, hm, qg, kg)


def _attention_sweeps(qms, k_slices, pv_groups, k_refs, v_refs, m_ref, smax_ref, acc_ref, p_format):
    r = qms[0].shape[0]
    p_dtype, p_max_exp = p_format

    def scores(j, kc):
        return _dot(qms[j], kc[k_slices[j], :])

    def sweep(running_max):
        acc_ref[...] = jnp.zeros(acc_ref.shape, F32)
        if running_max:
            m_ref[...] = jnp.full(m_ref.shape, NEG_BIG, F32)
        else:
            smax_ref[...] = jnp.full(smax_ref.shape, NEG_BIG, F32)
            kc0 = k_refs[0][0, 0][:, :MXU_DIM]
            for j in range(len(qms)):
                m0 = jnp.max(scores(j, kc0), axis=-1, keepdims=True)
                m_ref[j * r:(j + 1) * r, :] = jnp.broadcast_to(m0, (r, LANES))

        for k_ref, v_ref in zip(k_refs, v_refs):
            n_chunks, tk = k_ref.shape[1], k_ref.shape[3]

            def body(c, carry, k_ref=k_ref, v_ref=v_ref, tk=tk):
                kc = k_ref[0, c]
                vc = v_ref[0, pl.ds(pl.multiple_of(c * tk, tk), tk), :]
                for ids, v_lanes in pv_groups:
                    ps, alphas = [], []
                    for j in ids:
                        rows = slice(j * r, (j + 1) * r)
                        s = scores(j, kc)
                        m = m_ref[rows, :]
                        if running_max:
                            m_new = jnp.maximum(m, jnp.max(s, axis=-1, keepdims=True) - P_HEADROOM_RUNNING)
                            m_ref[rows, :] = m_new
                            alphas.append(jnp.exp2(m - m_new))
                            m = m_new
                        else:
                            cm = functools.reduce(jnp.maximum, [s[:, l:l + LANES] for l in range(0, tk, LANES)])
                            smax_ref[rows, :] = jnp.maximum(smax_ref[rows, :], cm)
                        d = (s - jnp.tile(m, (1, tk // LANES))).astype(BF16)
                        ps.append(jnp.exp2(d).astype(p_dtype))
                    rows = slice(ids[0] * r, (ids[-1] + 1) * r)
                    pv = _dot(jnp.concatenate(ps, axis=0), vc[:, v_lanes])
                    if running_max:
                        acc_ref[rows, :] = acc_ref[rows, :] * jnp.concatenate(alphas, axis=0) + pv
                    else:
                        acc_ref[rows, :] += pv
                return carry

            unroll = 1 if running_max else math.gcd(n_chunks, ATTN_UNROLL)
            lax.fori_loop(0, n_chunks, body, 0, unroll=unroll)

    sweep(False)
    top_exp = jnp.max(smax_ref[...] - m_ref[...])

    @pl.when(jnp.logical_not(top_exp <= p_max_exp))
    def _():
        sweep(True)


def _gqa_kernel(*refs, n_parts, tq):
    q_ref = refs[0]
    k_refs = refs[1:1 + n_parts]
    v_refs = refs[1 + n_parts:1 + 2 * n_parts]
    o_ref, m_ref, smax_ref, acc_ref = refs[1 + 2 * n_parts:]
    q = q_ref[0]
    qs = jnp.concatenate([q[:, HEAD_V * j:HEAD_V * (j + 1)] for j in range(GQA_GROUP)], axis=0)
    _attention_sweeps([qs], [slice(0, HEAD_V)], [((0,), slice(0, LANES))], k_refs, v_refs,
                      m_ref, smax_ref, acc_ref, P_E4M3)
    acc = acc_ref[...]
    o = acc[:, :HEAD_V] / acc[:, HEAD_V:HEAD_V + 1]
    for j in range(GQA_GROUP):
        o_ref[0, :, HEAD_V * j:HEAD_V * (j + 1)] = o[j * tq:(j + 1) * tq].astype(BF16)


def _gqa_attention(q, k_parts, v_parts, tq):
    b, sq, _ = q.shape
    n_parts = len(k_parts)
    in_specs = [pl.BlockSpec((1, tq, 256), lambda bb, g, i: (bb, i, g))]
    for kp in k_parts:
        in_specs.append(pl.BlockSpec((1, kp.shape[1], HEAD_V, kp.shape[3]), lambda bb, g, i: (bb, 0, g, 0)))
    for vp in v_parts:
        in_specs.append(pl.BlockSpec((1, vp.shape[1], LANES), lambda bb, g, i: (bb, 0, g)))
    return pl.pallas_call(
        functools.partial(_gqa_kernel, n_parts=n_parts, tq=tq),
        out_shape=jax.ShapeDtypeStruct((b, sq, 512), BF16),
        grid=(b, GQA_KV, sq // tq),
        in_specs=in_specs,
        out_specs=pl.BlockSpec((1, tq, 256), lambda bb, g, i: (bb, i, g)),
        scratch_shapes=[pltpu.VMEM((GQA_GROUP * tq, LANES), F32)] * 3,
        compiler_params=_cparams(("arbitrary", "arbitrary", "arbitrary")),
        name="gqa_attention",
    )(q, *k_parts, *v_parts)


def _diff_kernel(*refs, n_parts, tq, lam_init):
    q_ref = refs[0]
    k_refs = refs[1:1 + n_parts]
    v_refs = refs[1 + n_parts:1 + 2 * n_parts]
    lam_ref, sg_ref, o_ref, m_ref, smax_ref, acc_ref = refs[1 + 2 * n_parts:]
    q = q_ref[0]
    qmaps = [q[:, DIFF_QK * j:DIFF_QK * (j + 1)] for j in range(4)]
    k_slices = [slice(DIFF_QK * j, DIFF_QK * (j + 1)) for j in range(4)]
    pv_groups = [((0, 1), slice(0, LANES)), ((2, 3), slice(LANES, 2 * LANES))]
    _attention_sweeps(qmaps, k_slices, pv_groups, k_refs, v_refs, m_ref, smax_ref, acc_ref, P_E5M2)

    lv = lam_ref[...]
    lam = (jnp.exp(jnp.sum(lv[0:1] * lv[1:2], axis=-1, keepdims=True))
           - jnp.exp(jnp.sum(lv[2:3] * lv[3:4], axis=-1, keepdims=True)) + lam_init)
    acc = acc_ref[...]
    for hh in range(2):
        a0 = acc[(2 * hh) * tq:(2 * hh + 1) * tq]
        a1 = acc[(2 * hh + 1) * tq:(2 * hh + 2) * tq]
        o = a0[:, :HEAD_V] / a0[:, HEAD_V:HEAD_V + 1] - lam * (a1[:, :HEAD_V] / a1[:, HEAD_V:HEAD_V + 1])
        ms = jnp.mean(o * o, axis=-1, keepdims=True)
        on = o * lax.rsqrt(ms + EPS) * sg_ref[...] * (1.0 - lam_init)
        o_ref[0, :, HEAD_V * hh:HEAD_V * (hh + 1)] = on.astype(BF16)


def _diff_attention(q, k_parts, v_parts, lam_vecs, subln_g, lam_init, tq):
    b, sq, _ = q.shape
    n_parts = len(k_parts)
    in_specs = [pl.BlockSpec((1, tq, LANES), lambda bb, p, i: (bb, i, p))]
    for kp in k_parts:
        in_specs.append(pl.BlockSpec((1, kp.shape[1], LANES, kp.shape[3]), lambda bb, p, i: (bb, 0, p, 0)))
    for vp in v_parts:
        in_specs.append(pl.BlockSpec((1, vp.shape[1], 2 * LANES), lambda bb, p, i: (bb, 0, p)))
    in_specs.append(pl.BlockSpec((4, DIFF_QK), lambda bb, p, i: (0, 0)))
    in_specs.append(pl.BlockSpec((1, HEAD_V), lambda bb, p, i: (0, 0)))
    return pl.pallas_call(
        functools.partial(_diff_kernel, n_parts=n_parts, tq=tq, lam_init=lam_init),
        out_shape=jax.ShapeDtypeStruct((b, sq, 256), BF16),
        grid=(b, DIFF_HEADS // 2, sq // tq),
        in_specs=in_specs,
        out_specs=pl.BlockSpec((1, tq, LANES), lambda bb, p, i: (bb, i, p)),
        scratch_shapes=[pltpu.VMEM((4 * tq, LANES), F32)] * 3,
        compiler_params=_cparams(("arbitrary", "arbitrary", "arbitrary")),
        name="diff_attention",
    )(q, *k_parts, *v_parts, lam_vecs, subln_g)


CONV_HALO = 16
CONV_ROWS = 64


def _conv_kernel(gb_ref, prev_ref, next_ref, w_ref, b_ref, lg_ref, lb_ref, o_ref, u_ref, sh_ref, *, tm):
    i = pl.program_id(1)
    last = pl.num_programs(1) - 1
    ch = w_ref.shape[1]

    def glu(z):
        z = z.astype(F32)
        return z[:, :ch] * _sigmoid(z[:, ch:])

    u_ref[CONV_HALO:CONV_HALO + tm, :] = glu(gb_ref[0])
    u_ref[0:CONV_HALO, :] = jnp.where(i > 0, glu(prev_ref[0]), 0.0)
    u_ref[CONV_HALO + tm:2 * CONV_HALO + tm, :] = jnp.where(i < last, glu(next_ref[0]), 0.0)

    span = sh_ref.shape[1]
    for r in range(1, SUBLANES):
        sh_ref[r - 1] = u_ref[r:r + span, :]

    off = CONV_HALO - CONV_K // 2
    for r0 in range(0, tm, CONV_ROWS):
        acc = jnp.zeros((CONV_ROWS, ch), F32)
        for j in range(CONV_K):
            phase, base = (off + j) % SUBLANES, r0 + (off + j) // SUBLANES * SUBLANES
            taps = u_ref[base:base + CONV_ROWS, :] if phase == 0 else sh_ref[phase - 1, base:base + CONV_ROWS, :]
            acc = acc + taps * w_ref[j:j + 1, :]
        y = acc + b_ref[...]
        mu = jnp.mean(y, axis=-1, keepdims=True)
        yc = y - mu
        var = jnp.mean(yc * yc, axis=-1, keepdims=True)
        z = yc * lax.rsqrt(var + EPS) * lg_ref[...] + lb_ref[...]
        o_ref[0, r0:r0 + CONV_ROWS, :] = (z * _sigmoid(z)).astype(BF16)


def _conformer_conv(gb, w, bias, ln_g, ln_b, tm):
    b, s, two_ch = gb.shape
    ch = two_ch // 2
    hb = tm // CONV_HALO
    n_halo = s // CONV_HALO
    const2 = lambda bb, i: (0, 0)
    return pl.pallas_call(
        functools.partial(_conv_kernel, tm=tm),
        out_shape=jax.ShapeDtypeStruct((b, s, ch), BF16),
        grid=(b, s // tm),
        in_specs=[
            pl.BlockSpec((1, tm, two_ch), lambda bb, i: (bb, i, 0)),
            pl.BlockSpec((1, CONV_HALO, two_ch), lambda bb, i: (bb, jnp.maximum(i * hb - 1, 0), 0)),
            pl.BlockSpec((1, CONV_HALO, two_ch), lambda bb, i: (bb, jnp.minimum((i + 1) * hb, n_halo - 1), 0)),
            pl.BlockSpec((CONV_K, ch), const2), pl.BlockSpec((1, ch), const2),
            pl.BlockSpec((1, ch), const2), pl.BlockSpec((1, ch), const2),
        ],
        out_specs=pl.BlockSpec((1, tm, ch), lambda bb, i: (bb, i, 0)),
        scratch_shapes=[pltpu.VMEM((tm + 2 * CONV_HALO, ch), F32),
                        pltpu.VMEM((SUBLANES - 1, tm + 2 * CONV_HALO - SUBLANES, ch), F32)],
        compiler_params=_cparams(("arbitrary", "arbitrary")),
        name="conformer_conv",
    )(gb, gb, gb, w, bias, ln_g, ln_b)


def _merge_kernel(*refs, with_router):
    oa_ref, ob_ref, oc_ref, x_ref, gate_ref, shift_ref, scale_ref, g2_ref, w_ref = refs[:9]
    if with_router:
        rw_ref, xo_ref, h2_ref, lg_ref = refs[9:]
    else:
        xo_ref, h2_ref = refs[9:]
    tm = x_ref.shape[1]
    wa = oa_ref.shape[2]
    wb = wa + ob_ref.shape[2]
    y = _dot(oa_ref[0], w_ref[0:wa, :]) + _dot(ob_ref[0], w_ref[wa:wb, :]) + _dot(oc_ref[0], w_ref[wb:, :])
    xn = x_ref[0] + gate_ref[0] * y
    xo_ref[0] = xn
    ms = jnp.mean(xn * xn, axis=-1, keepdims=True)
    h2 = xn * lax.rsqrt(ms + EPS) * g2_ref[...] * (1.0 + scale_ref[0]) + shift_ref[0]
    if with_router:
        _store_row_tiles(h2_ref, h2, tm)
        lg_ref[...] = lax.dot_general(rw_ref[...], h2, (((1,), (1,)), ((), ())),
                                      preferred_element_type=F32, precision=lax.Precision.HIGHEST)
    else:
        h2_ref[0] = h2.astype(BF16)


def _merge(oa, ob, oc, x, gate, shift, scale, g2, w_out, router_wt, tm):
    b, s, d = x.shape
    row = lambda bb, i: (bb, i, 0)
    per_b = lambda bb, i: (bb, 0, 0)
    const2 = lambda bb, i: (0, 0)
    nt = s // tm
    in_specs = [
        pl.BlockSpec((1, tm, oa.shape[2]), row), pl.BlockSpec((1, tm, ob.shape[2]), row),
        pl.BlockSpec((1, tm, oc.shape[2]), row), pl.BlockSpec((1, tm, d), row),
        pl.BlockSpec((1, 1, d), per_b), pl.BlockSpec((1, 1, d), per_b), pl.BlockSpec((1, 1, d), per_b),
        pl.BlockSpec((1, d), const2), pl.BlockSpec((d, d), const2),
    ]
    out_shape = [jax.ShapeDtypeStruct((b, s, d), F32)]
    out_specs = [pl.BlockSpec((1, tm, d), row)]
    args = [oa, ob, oc, x, gate, shift, scale, g2, w_out]
    with_router = router_wt is not None
    if not with_router:
        out_shape.append(jax.ShapeDtypeStruct((b, s, d), BF16))
        out_specs.append(pl.BlockSpec((1, tm, d), row))
    else:
        out_shape.append(jax.ShapeDtypeStruct((b * s * ROW_TILE, LANES), F32))
        out_specs.append(pl.BlockSpec((tm * ROW_TILE, LANES), lambda bb, i: (bb * nt + i, 0)))
        in_specs.append(pl.BlockSpec((N_EXPERTS, d), const2))
        out_shape.append(jax.ShapeDtypeStruct((N_EXPERTS, b * s), F32))
        out_specs.append(pl.BlockSpec((N_EXPERTS, tm), lambda bb, i: (0, bb * nt + i)))
        args.append(router_wt)
    return pl.pallas_call(
        functools.partial(_merge_kernel, with_router=with_router),
        out_shape=tuple(out_shape), grid=(b, nt), in_specs=in_specs, out_specs=tuple(out_specs),
        compiler_params=_cparams(("arbitrary", "arbitrary")), name="merge_heads",
    )(*args)


def _ffn_kernel(h_ref, x_ref, gate_ref, wg_ref, wu_ref, wd_ref, o_ref, *, tf):
    h = h_ref[...]
    ff = wg_ref.shape[1]
    acc = jnp.zeros(x_ref.shape, F32)
    for f in range(0, ff, tf):
        g = _dot(h, wg_ref[:, f:f + tf])
        u = _dot(h, wu_ref[:, f:f + tf])
        a = (g * _sigmoid(g) * u).astype(BF16)
        acc = acc + _dot(a, wd_ref[f:f + tf, :])
    o_ref[...] = x_ref[...] + gate_ref[0] * acc


def _dense_ffn(h2, x, gate, wg, wu, wd, tm):
    n, d = x.shape
    ff = wg.shape[1]
    s = n // gate.shape[0]
    row = lambda i: (i, 0)
    const2 = lambda i: (0, 0)
    resident = pl.Buffered(1)
    return pl.pallas_call(
        functools.partial(_ffn_kernel, tf=MXU_DIM),
        out_shape=jax.ShapeDtypeStruct((n, d), F32),
        grid=(n // tm,),
        in_specs=[
            pl.BlockSpec((tm, d), row), pl.BlockSpec((tm, d), row),
            pl.BlockSpec((1, 1, d), lambda i: ((i * tm) // s, 0, 0)),
            pl.BlockSpec((d, ff), const2, pipeline_mode=resident),
            pl.BlockSpec((d, ff), const2, pipeline_mode=resident),
            pl.BlockSpec((ff, d), const2, pipeline_mode=resident),
        ],
        out_specs=pl.BlockSpec((tm, d), row),
        compiler_params=_cparams(("arbitrary",)), name="dense_ffn",
    )(h2, x, gate, wg, wu, wd)


def _top2(lg):
    sub = lax.broadcasted_iota(I32, lg.shape, 0)
    l1 = jnp.max(lg, axis=0, keepdims=True)
    i1 = jnp.min(jnp.where(lg == l1, sub, N_EXPERTS), axis=0, keepdims=True)
    m1 = sub == i1
    lg2 = jnp.where(m1, -jnp.inf, lg)
    l2 = jnp.max(lg2, axis=0, keepdims=True)
    i2 = jnp.min(jnp.where(lg2 == l2, sub, N_EXPERTS), axis=0, keepdims=True)
    m2 = sub == i2
    return l1, l2, m1, m2


def _sublane_cumsum(x):
    sub = lax.broadcasted_iota(I32, x.shape, 0)
    for sh in (1, 2, 4):
        x = x + jnp.where(sub >= sh, pltpu.roll(x, sh, 0), 0.0)
    return x


def _route_kernel(lg_ref, tri_ref, dest_ref, gates_ref, be_ref, base_ref, start_ref, *, block_rows):
    phase = pl.program_id(0)
    j = pl.program_id(1)
    l1, l2, m1, m2 = _top2(lg_ref[...])
    e = jnp.where(m1 | m2, 1.0, 0.0).astype(F32)
    cnt = jnp.sum(e, axis=1, keepdims=True)

    @pl.when((phase == 0) & (j == 0))
    def _():
        base_ref[...] = jnp.zeros(base_ref.shape, F32)

    @pl.when((phase == 1) & (j == 0))
    def _():
        counts = base_ref[...]
        nblk = jnp.floor((counts + (block_rows - 1)) * (1.0 / block_rows))
        end_blk = _sublane_cumsum(nblk)
        start_ref[...] = (end_blk - nblk) * block_rows
        blk = lax.broadcasted_iota(I32, be_ref.shape, 1).astype(F32)
        owner = jnp.sum(jnp.where(end_blk[:, :1] <= blk, 1.0, 0.0), axis=0, keepdims=True)
        be_ref[...] = jnp.broadcast_to(jnp.minimum(owner, N_EXPERTS - 1.0), be_ref.shape).astype(I32)
        base_ref[...] = jnp.zeros(base_ref.shape, F32)

    @pl.when(phase == 1)
    def _():
        prefix = _dot(e.astype(BF16), tri_ref[...]) + base_ref[:, :1] + start_ref[:, :1]
        d1 = jnp.sum(jnp.where(m1, prefix, 0.0), axis=0, keepdims=True)
        d2 = jnp.sum(jnp.where(m2, prefix, 0.0), axis=0, keepdims=True)
        sub = lax.broadcasted_iota(I32, dest_ref.shape, 0)
        dest_ref[...] = jnp.where(sub == 0, d1, jnp.where(sub == 1, d2, 0.0)).astype(I32)
        ex = jnp.exp(l2 - l1)
        g1 = 1.0 / (1.0 + ex)
        g2 = ex / (1.0 + ex)
        half = lax.broadcasted_iota(I32, (LANES, lg_ref.shape[1]), 0) < LANES // 2
        gates_ref[...] = jnp.where(half, g1, g2).T

    base_ref[...] = base_ref[...] + cnt


def _route(logits_t, block_rows, n_blocks_pad, tr):
    n = logits_t.shape[1]
    tri = jnp.asarray(np.triu(np.ones((tr, tr), np.float32), k=1), BF16)
    return pl.pallas_call(
        functools.partial(_route_kernel, block_rows=block_rows),
        out_shape=(jax.ShapeDtypeStruct((N_EXPERTS, n), I32),
                   jax.ShapeDtypeStruct((n, LANES), F32),
                   jax.ShapeDtypeStruct((N_EXPERTS, n_blocks_pad), I32)),
        grid=(2, n // tr),
        in_specs=[pl.BlockSpec((N_EXPERTS, tr), lambda p, j: (0, j)),
                  pl.BlockSpec((tr, tr), lambda p, j: (0, 0))],
        out_specs=(pl.BlockSpec((N_EXPERTS, tr), lambda p, j: (0, j * p)),
                   pl.BlockSpec((tr, LANES), lambda p, j: (j * p, 0)),
                   pl.BlockSpec((N_EXPERTS, n_blocks_pad), lambda p, j: (0, 0))),
        scratch_shapes=[pltpu.VMEM((N_EXPERTS, LANES), F32), pltpu.VMEM((N_EXPERTS, LANES), F32)],
        compiler_params=_cparams(("arbitrary", "arbitrary")), name="moe_route",
    )(logits_t, tri)


def _row_copy(src_hbm, src_row, dst_hbm, dst_row, sem):
    src = pl.ds(pl.multiple_of(src_row * ROW_TILE, ROW_TILE), ROW_TILE)
    dst = pl.ds(pl.multiple_of(dst_row * ROW_TILE, ROW_TILE), ROW_TILE)
    return pltpu.make_async_copy(src_hbm.at[src], dst_hbm.at[dst], sem)


def _scatter_kernel(d1_ref, d2_ref, src_ref, init_hbm, out_hbm, sem, *, rows):
    del init_hbm

    def start(r, c):
        _row_copy(src_ref, r, out_hbm, d1_ref[0, 0, r], sem).start(priority=0)
        _row_copy(src_ref, r, out_hbm, d2_ref[0, 0, r], sem).start(priority=1)
        return c

    lax.fori_loop(0, rows, start, 0, unroll=DMA_UNROLL)
    for _ in range(2):
        pltpu.make_async_copy(src_ref, out_hbm.at[pl.ds(0, rows * ROW_TILE)], sem).wait()


def _scatter_rows(src, d1, d2, total_rows, rows):
    n = d1.shape[0]
    idx_spec = pl.BlockSpec((1, 1, rows), lambda i: (i, 0, 0), memory_space=pltpu.SMEM)
    any_spec = pl.BlockSpec(memory_space=pl.ANY)
    return pl.pallas_call(
        functools.partial(_scatter_kernel, rows=rows),
        out_shape=jax.ShapeDtypeStruct((total_rows * ROW_TILE, LANES), src.dtype),
        grid=(n // rows,),
        in_specs=[idx_spec, idx_spec, pl.BlockSpec((rows * ROW_TILE, LANES), lambda i: (i, 0)), any_spec],
        out_specs=any_spec,
        scratch_shapes=[pltpu.SemaphoreType.DMA(())],
        input_output_aliases={3: 0},
        compiler_params=pltpu.CompilerParams(dimension_semantics=("arbitrary",), has_side_effects=True),
        name="moe_scatter_rows",
    )(d1.reshape(n // rows, 1, rows), d2.reshape(n // rows, 1, rows), src,
      jnp.zeros((total_rows * ROW_TILE, LANES), src.dtype))


def _expert_kernel(be_ref, x_ref, wg_ref, wu_ref, wd_ref, o_ref, *, block_rows, tf):
    del be_ref
    x = _load_row_tiles(x_ref, block_rows).astype(BF16)
    ff = wg_ref.shape[2]
    acc = jnp.zeros((block_rows, wd_ref.shape[2]), F32)
    for f in range(0, ff, tf):
        g = _dot(x, wg_ref[0, :, f:f + tf])
        u = _dot(x, wu_ref[0, :, f:f + tf])
        a = (g * _sigmoid(g) * u).astype(BF16)
        acc = acc + _dot(a, wd_ref[0, f:f + tf, :])
    _store_row_tiles(o_ref, acc, block_rows)


def _expert_ffn(xb, blk_expert, wg, wu, wd, block_rows):
    d, ff = wg.shape[1], wg.shape[2]
    rows = xb.shape[0] // ROW_TILE
    resident = pl.Buffered(1)
    grid_spec = pltpu.PrefetchScalarGridSpec(
        num_scalar_prefetch=1,
        grid=(rows // block_rows,),
        in_specs=[
            pl.BlockSpec((block_rows * ROW_TILE, LANES), lambda i, be: (i, 0)),
            pl.BlockSpec((1, d, ff), lambda i, be: (be[i], 0, 0), pipeline_mode=resident),
            pl.BlockSpec((1, d, ff), lambda i, be: (be[i], 0, 0), pipeline_mode=resident),
            pl.BlockSpec((1, ff, d), lambda i, be: (be[i], 0, 0), pipeline_mode=resident),
        ],
        out_specs=pl.BlockSpec((block_rows * ROW_TILE, LANES), lambda i, be: (i, 0)),
    )
    return pl.pallas_call(
        functools.partial(_expert_kernel, block_rows=block_rows, tf=MXU_DIM),
        out_shape=jax.ShapeDtypeStruct(xb.shape, F32), grid_spec=grid_spec,
        compiler_params=_cparams(("arbitrary",)), name="moe_expert_ffn",
    )(blk_expert, xb, wg, wu, wd)


def _combine_kernel(d1_ref, d2_ref, x_ref, yb_hbm, gates_ref, gate_ref, fg_ref, o_ref, y1_ref, y2_ref, sem):
    tm = x_ref.shape[0]

    def start(r, c):
        _row_copy(yb_hbm, d1_ref[0, 0, r], y1_ref, r, sem).start(priority=0)
        _row_copy(yb_hbm, d2_ref[0, 0, r], y2_ref, r, sem).start(priority=1)
        return c

    lax.fori_loop(0, tm, start, 0, unroll=DMA_UNROLL)
    for y_ref in (y1_ref, y2_ref):
        pltpu.make_async_copy(yb_hbm.at[pl.ds(0, tm * ROW_TILE)], y_ref, sem).wait()

    gts = gates_ref[...]
    y = (gts[:, 0:1] * _load_row_tiles(y1_ref, tm)
         + gts[:, LANES // 2:LANES // 2 + 1] * _load_row_tiles(y2_ref, tm))
    xn = x_ref[...] + gate_ref[0] * y
    ms = jnp.mean(xn * xn, axis=-1, keepdims=True)
    o_ref[...] = xn * lax.rsqrt(ms + EPS) * fg_ref[...]


def _combine_final(x, yb, d1, d2, gates, gate, final_g, tm):
    n, d = x.shape
    s = n // gate.shape[0]
    row = lambda i: (i, 0)
    idx_spec = pl.BlockSpec((1, 1, tm), lambda i: (i, 0, 0), memory_space=pltpu.SMEM)
    return pl.pallas_call(
        _combine_kernel,
        out_shape=jax.ShapeDtypeStruct((n, d), F32),
        grid=(n // tm,),
        in_specs=[idx_spec, idx_spec,
                  pl.BlockSpec((tm, d), row),
                  pl.BlockSpec(memory_space=pl.ANY),
                  pl.BlockSpec((tm, LANES), row),
                  pl.BlockSpec((1, 1, d), lambda i: ((i * tm) // s, 0, 0)),
                  pl.BlockSpec((1, d), lambda i: (0, 0))],
        out_specs=pl.BlockSpec((tm, d), row),
        scratch_shapes=[pltpu.VMEM((tm * ROW_TILE, LANES), F32), pltpu.VMEM((tm * ROW_TILE, LANES), F32),
                        pltpu.SemaphoreType.DMA(())],
        compiler_params=_cparams(("arbitrary",)), name="moe_combine_final",
    )(d1.reshape(n // tm, 1, tm), d2.reshape(n // tm, 1, tm), x, yb, gates, gate, final_g)


def _rope_tables(s, dim):
    half = dim // 2
    t = jnp.arange(s)
    inv = 1.0 / (ROPE_THETA ** (jnp.arange(0, half, 2, dtype=F32) / half))
    ang_r = (t // GRID_W).astype(F32)[:, None] * inv
    ang_c = (t % GRID_W).astype(F32)[:, None] * inv
    ang = jnp.concatenate([ang_r, ang_r, ang_c, ang_c], axis=-1)
    reps = LANES // dim
    return jnp.tile(jnp.cos(ang), (1, reps)), jnp.tile(jnp.sin(ang), (1, reps))


def _rotate_matrix(dim):
    q = dim // 4
    p = np.zeros((MXU_DIM, MXU_DIM), np.float32)
    for j in range(MXU_DIM):
        if (j % (2 * q)) < q:
            p[j + q, j] = -1.0
        else:
            p[j - q, j] = 1.0
    return jnp.asarray(p, BF16)


def _head_mean_matrix():
    m = np.kron(np.eye(MXU_DIM // HEAD_V, dtype=np.float32), np.full((HEAD_V, HEAD_V), 1.0 / HEAD_V, np.float32))
    return jnp.asarray(m, BF16)


def _widen_values(w, heads):
    d = w.shape[0]
    w = w.reshape(d, heads, HEAD_V)
    return jnp.concatenate([w, jnp.zeros_like(w)], axis=-1).reshape(d, heads * LANES)


def _widen_in_proj(w):
    qa, ka, va, gb, qc, kc, vc = jnp.split(w, [256, 512, 768, 1280, 1792, 1920], axis=1)
    return jnp.concatenate([qa, ka, _widen_values(va, DIFF_HEADS), gb, qc, kc, _widen_values(vc, GQA_KV)],
                           axis=1).astype(BF16)


def kernel(x, c, ctx, c_ctx, ada_w, ada_b, norm1_g, norm2_g, w_in, w_out, lam_q1, lam_k1, lam_q2, lam_k2,
           diff_subln_g, conv_w, conv_b, conv_ln_g, conv_ln_b, q_norm_g, k_norm_g, ffn_gate, ffn_up, ffn_down,
           router_w, moe_gate, moe_up, moe_down, final_g):
    b, s, d = x.shape
    sc = ctx.shape[1]
    depth = ada_w.shape[0]
    n = b * s
    assert depth % 2 == 0, "the final RMSNorm is fused into the MoE combine of the last (odd) layer"

    tm = min(512, s)
    tmc = min(512, sc)
    tq = min(512, s)
    tq_diff = min(1024, s)
    tqc = min(256, sc)

    tabs_x = _rope_tables(s, DIFF_QK) + _rope_tables(s, HEAD_V)
    ones_c, zeros_c = jnp.ones((sc, LANES), F32), jnp.zeros((sc, LANES), F32)
    tabs_c = (ones_c, zeros_c, ones_c, zeros_c)
    mats = (_rotate_matrix(DIFF_QK), _rotate_matrix(HEAD_V), _head_mean_matrix())

    cc = jnp.zeros((16, d), F32).at[:b].set(c).at[b].set(c_ctx)

    for i in range(depth):
        last = i == depth - 1
        lam_init = 0.8 - 0.6 * math.exp(-0.3 * i)
        mod_all = _ada_mod(cc, ada_w[i], ada_b[i])
        mod = mod_all[:b].reshape(b, 6, 1, d)
        modc = jnp.broadcast_to(mod_all[b].reshape(1, 6, 1, d), (b, 6, 1, d))

        w_aug = _widen_in_proj(w_in[i])
        g1 = norm1_g[i].reshape(1, d)
        qg = jnp.tile(q_norm_g[i], GQA_HEADS).reshape(1, -1)
        kg = jnp.tile(k_norm_g[i], GQA_KV).reshape(1, -1)
        lam_vecs = jnp.stack([lam_q1[i], lam_k1[i], lam_q2[i], lam_k2[i]]).astype(F32)
        subln = diff_subln_g[i].reshape(1, HEAD_V)
        conv_args = (conv_w[i], conv_b[i].reshape(1, -1), conv_ln_g[i].reshape(1, -1), conv_ln_b[i].reshape(1, -1))
        w_o = w_out[i].astype(BF16)
        g2 = norm2_g[i].reshape(1, d)

        qa, kat, va, gb, qc, kct, vc = _in_projection(x, mod[:, 0], mod[:, 1], g1, w_aug, tabs_x, mats, qg, kg, tm)
        qa_x, kat_x, va_x, gb_x, qc_x, kct_x, vc_x = _in_projection(
            ctx, modc[:, 0], modc[:, 1], g1, w_aug, tabs_c, mats, qg, kg, tmc)

        oa = _diff_attention(qa, [kat, kat_x], [va, va_x], lam_vecs, subln, lam_init, tq_diff)
        ob = _conformer_conv(gb, *conv_args, tm)
        oc = _gqa_attention(qc, [kct, kct_x], [vc, vc_x], tq)

        j = i // 2
        if i % 2 == 0:
            x, h2 = _merge(oa, ob, oc, x, mod[:, 2], mod[:, 3], mod[:, 4], g2, w_o, None, tm)
            wg, wu, wd = ffn_gate[j].astype(BF16), ffn_up[j].astype(BF16), ffn_down[j].astype(BF16)
            x = _dense_ffn(h2.reshape(n, d), x.reshape(n, d), mod[:, 5], wg, wu, wd, tm).reshape(b, s, d)
        else:
            rwt = router_w[j].T.astype(F32)
            x, h2, logits_t = _merge(oa, ob, oc, x, mod[:, 2], mod[:, 3], mod[:, 4], g2, w_o, rwt, tm)
            block_rows = 512 if n >= 8192 else 256
            n_blocks = (2 * n) // block_rows + N_EXPERTS
            n_blocks_pad = -(-n_blocks // LANES) * LANES
            dest, gates, blk_e = _route(logits_t, block_rows, n_blocks_pad, min(512, n))
            xb = _scatter_rows(h2, dest[0], dest[1], n_blocks * block_rows, min(256, n))
            yb = _expert_ffn(xb, blk_e[0, :n_blocks], moe_gate[j].astype(BF16), moe_up[j].astype(BF16),
                             moe_down[j].astype(BF16), block_rows)
            assert last
            x = _combine_final(x.reshape(n, d), yb, dest[0], dest[1], gates, mod[:, 5], final_g.reshape(1, d),
                               tm).reshape(b, s, d)

        if not last:
            oa_x = _diff_attention(qa_x, [kat_x], [va_x], lam_vecs, subln, lam_init, tqc)
            ob_x = _conformer_conv(gb_x, *conv_args, tmc)
            oc_x = _gqa_attention(qc_x, [kct_x], [vc_x], tqc)
            assert i % 2 == 0, "context tokens only ever pass through dense channel mixers"
            ctx, hc2 = _merge(oa_x, ob_x, oc_x, ctx, modc[:, 2], modc[:, 3], modc[:, 4], g2, w_o, None, tmc)
            ctx = _dense_ffn(hc2.reshape(b * sc, d), ctx.reshape(b * sc, d), modc[:, 5], wg, wu, wd,
                             tmc).reshape(b, sc, d)

    return x
```

```python
import functools
import math

import numpy as np
import jax
import jax.numpy as jnp
from jax import lax
from jax.experimental import pallas as pl
from jax.experimental.pallas import tpu as pltpu

F32 = jnp.float32
BF16 = jnp.bfloat16
I32 = jnp.int32

EPS = 1e-6
ROPE_THETA = 10000.0
GRID_W = 64

DIFF_HEADS = 4
DIFF_QK = 32
HEAD_V = 64
GQA_HEADS = 8
GQA_KV = 2
GQA_GROUP = GQA_HEADS // GQA_KV
CONV_K = 31
N_EXPERTS = 8
LOG2E = math.log2(math.e)

LANES = 128
SUBLANES = 8
MXU_DIM = 256
VMEM_LIMIT = 52 * 1024 * 1024
NEG_BIG = -1e30
V_DTYPE = jnp.float8_e4m3fn
QK_DTYPE = jnp.float8_e4m3fn
P_E4M3 = (jnp.float8_e4m3fn, 8.5)
P_E5M2 = (jnp.float8_e5m2, 15.5)
P_HEADROOM_RUNNING = 8.0
ATTN_UNROLL = 4
DMA_UNROLL = 8

C_QA, C_KA, C_VA, C_GB, C_QC, C_KC, C_VC, C_END = 0, 256, 512, 1024, 1536, 2048, 2176, 2432


def _cparams(semantics):
    return pltpu.CompilerParams(dimension_semantics=semantics, vmem_limit_bytes=VMEM_LIMIT)


def _dot(a, b):
    return jnp.dot(a, b, preferred_element_type=F32)


def _sigmoid(z):
    return 1.0 / (1.0 + jnp.exp(-z))


ROW_TILE = 8


def _store_row_tiles(ref, val, rows):
    for a in range(ROW_TILE):
        ref[pl.ds(a, rows, stride=ROW_TILE), :] = val[:, a * LANES:(a + 1) * LANES]


def _load_row_tiles(ref, rows):
    return jnp.concatenate([ref[pl.ds(a, rows, stride=ROW_TILE), :] for a in range(ROW_TILE)], axis=1)


def _mod_kernel(c_ref, w_ref, b_ref, o_ref):
    c = c_ref[...]
    s = c * _sigmoid(c)
    o_ref[...] = jnp.dot(s, w_ref[...], preferred_element_type=F32, precision=lax.Precision.HIGHEST) + b_ref[...]


def _ada_mod(cc, w, b):
    rows, d = cc.shape
    n = w.shape[1]
    tn = d
    return pl.pallas_call(
        _mod_kernel,
        out_shape=jax.ShapeDtypeStruct((rows, n), F32),
        grid=(n // tn,),
        in_specs=[pl.BlockSpec((rows, d), lambda j: (0, 0)),
                  pl.BlockSpec((d, tn), lambda j: (0, j)),
                  pl.BlockSpec((1, tn), lambda j: (0, j))],
        out_specs=pl.BlockSpec((rows, tn), lambda j: (0, j)),
        compiler_params=_cparams(("arbitrary",)),
        name="ada_mod",
    )(cc, w, b.reshape(1, n))


def _inproj_kernel(x_ref, shift_ref, scale_ref, g_ref, w_ref, cosa_ref, sina_ref, cosc_ref, sinc_ref,
                   pa_ref, pc_ref, hm_ref, qg_ref, kg_ref,
                   qa_o, kat_o, va_o, gb_o, qc_o, kct_o, vc_o, amax_o):
    x = x_ref[0]
    ms = jnp.mean(x * x, axis=-1, keepdims=True)
    h = x * lax.rsqrt(ms + EPS) * g_ref[...]
    h = h * (1.0 + scale_ref[0]) + shift_ref[0]
    hb = h.astype(BF16)

    def proj(lo, hi):
        return _dot(hb, w_ref[:, lo:hi])

    def blockmat(y, m_ref):
        yb = y.astype(BF16)
        w = y.shape[1]
        if w == LANES:
            return _dot(yb, m_ref[:LANES, :LANES])
        return jnp.concatenate([_dot(yb[:, c:c + MXU_DIM], m_ref[...]) for c in range(0, w, MXU_DIM)], axis=1)

    def rope(y, cos, sin, p_ref):
        reps = y.shape[1] // LANES
        cos = jnp.tile(cos, (1, reps))
        sin = jnp.tile(sin, (1, reps))
        return y * cos + blockmat(y, p_ref) * sin

    def ones_col(width):
        lane = lax.broadcasted_iota(I32, (1, width), 1)
        return jnp.where(lane % LANES == HEAD_V, 1.0, 0.0).astype(F32)

    cosa, sina, cosc, sinc = cosa_ref[...], sina_ref[...], cosc_ref[...], sinc_ref[...]

    stat_w = amax_o.shape[3]

    def col_amax(y):
        cm = jnp.max(jnp.abs(y), axis=0, keepdims=True)
        pad = stat_w - y.shape[1]
        return cm if pad == 0 else jnp.concatenate([cm, jnp.zeros((1, pad), F32)], axis=1)

    qa = rope(proj(C_QA, C_KA), cosa, sina, pa_ref)
    qa_o[0] = qa.astype(BF16)
    ka = rope(proj(C_KA, C_VA), cosa, sina, pa_ref)
    kat_o[0, 0] = ka.T.astype(BF16)
    va = proj(C_VA, C_GB)
    va_o[0] = (va + ones_col(C_GB - C_VA)).astype(BF16)
    gb_o[0] = proj(C_GB, C_QC).astype(BF16)

    y = proj(C_QC, C_KC)
    yn = y * lax.rsqrt(blockmat(y * y, hm_ref) + EPS) * qg_ref[...]
    qc = rope(yn, cosc, sinc, pc_ref)
    qc_o[0] = qc.astype(BF16)

    y = proj(C_KC, C_VC)
    yn = y * lax.rsqrt(blockmat(y * y, hm_ref) + EPS) * kg_ref[...]
    kc = rope(yn, cosc, sinc, pc_ref)
    kct_o[0, 0] = kc.T.astype(BF16)

    vc = proj(C_VC, C_END)
    vc_o[0] = (vc + ones_col(C_END - C_VC)).astype(BF16)

    stats = [col_amax(t) for t in (qa, ka, va, qc, kc, vc)]
    stats += [jnp.zeros((1, stat_w), F32)] * (SUBLANES - len(stats))
    amax_o[0, 0] = jnp.concatenate(stats, axis=0)


def _in_projection(x, shift, scale, g1, w_aug, tabs, mats, qg, kg, tm):
    b, s, d = x.shape
    nt = s // tm
    cosa, sina, cosc, sinc = tabs
    pa, pc, hm = mats
    row = lambda bb, i: (bb, i, 0)
    const2 = lambda bb, i: (0, 0)
    per_b = lambda bb, i: (bb, 0, 0)
    tab = lambda bb, i: (i, 0)
    out_shape = (
        jax.ShapeDtypeStruct((b, s, 256), BF16),
        jax.ShapeDtypeStruct((b, nt, 256, tm), BF16),
        jax.ShapeDtypeStruct((b, s, 512), BF16),
        jax.ShapeDtypeStruct((b, s, 512), BF16),
        jax.ShapeDtypeStruct((b, s, 512), BF16),
        jax.ShapeDtypeStruct((b, nt, 128, tm), BF16),
        jax.ShapeDtypeStruct((b, s, 256), BF16),
        jax.ShapeDtypeStruct((b, nt, SUBLANES, 512), F32),
    )
    out_specs = (
        pl.BlockSpec((1, tm, 256), row),
        pl.BlockSpec((1, 1, 256, tm), lambda bb, i: (bb, i, 0, 0)),
        pl.BlockSpec((1, tm, 512), row),
        pl.BlockSpec((1, tm, 512), row),
        pl.BlockSpec((1, tm, 512), row),
        pl.BlockSpec((1, 1, 128, tm), lambda bb, i: (bb, i, 0, 0)),
        pl.BlockSpec((1, tm, 256), row),
        pl.BlockSpec((1, 1, SUBLANES, 512), lambda bb, i: (bb, i, 0, 0)),
    )
    in_specs = [
        pl.BlockSpec((1, tm, d), row),
        pl.BlockSpec((1, 1, d), per_b),
        pl.BlockSpec((1, 1, d), per_b),
        pl.BlockSpec((1, d), const2),
        pl.BlockSpec((d, C_END), const2),
        pl.BlockSpec((tm, LANES), tab), pl.BlockSpec((tm, LANES), tab),
        pl.BlockSpec((tm, LANES), tab), pl.BlockSpec((tm, LANES), tab),
        pl.BlockSpec((MXU_DIM, MXU_DIM), const2), pl.BlockSpec((MXU_DIM, MXU_DIM), const2),
        pl.BlockSpec((MXU_DIM, MXU_DIM), const2),
        pl.BlockSpec((1, 512), const2), pl.BlockSpec((1, 128), const2),
    ]
    return pl.pallas_call(
        _inproj_kernel, out_shape=out_shape, grid=(b, nt), in_specs=in_specs, out_specs=out_specs,
        compiler_params=_cparams(("arbitrary", "arbitrary")), name="in_projection",
    )(x, shift, scale, g1, w_aug, cosa, sina, cosc, sinc, pa, pc, hm, qg, kg)


SCL_Q, SCL_K, SCL_V, SCL_V_INV = 0, 1, 2, 3


LO_GAIN = 16.0
V_ROW_CHUNK = 512


def _split_fp8(x, dtype):
    hi = x.astype(dtype)
    return hi, ((x - hi.astype(F32)) * LO_GAIN).astype(dtype)


def _stack_qk(x, other_side, dtype, axis, width):
    hi, lo = _split_fp8(x, dtype)
    hi_small = (hi.astype(F32) * (1.0 / LO_GAIN)).astype(dtype)
    parts = [hi, lo, hi_small] if other_side else [hi, hi_small, lo]
    pad = width - 3 * x.shape[axis]
    if pad:
        pad_shape = tuple(pad if a == axis else n for a, n in enumerate(x.shape))
        parts.append(jnp.zeros(pad_shape, dtype))
    return jnp.concatenate(parts, axis=axis)


def _quantize_kv(k_refs, v_refs, k8_refs, v8_refs, scl_ref, dk):
    sk = scl_ref[0, SCL_K:SCL_K + 1, 0:1]
    sv = scl_ref[0, SCL_V:SCL_V + 1, 0:1]
    for k_ref, k8_ref in zip(k_refs, k8_refs):
        n_maps = k_ref.shape[2] // dk
        stack = k8_ref.shape[1] // n_maps

        def k_body(c, carry, k_ref=k_ref, k8_ref=k8_ref, n_maps=n_maps, stack=stack):
            kf = k_ref[0, c].astype(F32) * sk
            k8_ref[c] = jnp.concatenate(
                [_stack_qk(kf[dk * j:dk * (j + 1), :], True, k8_ref.dtype, 0, stack) for j in range(n_maps)], axis=0)
            return carry

        lax.fori_loop(0, k_ref.shape[1], k_body, 0)

    for v_ref, v8_ref in zip(v_refs, v8_refs):
        rows = min(V_ROW_CHUNK, v_ref.shape[1])

        def v_body(c, carry, v_ref=v_ref, v8_ref=v8_ref, rows=rows):
            sl = pl.ds(pl.multiple_of(c * rows, rows), rows)
            vf = v_ref[0, sl, :].astype(F32)
            groups = []
            for g in range(v_ref.shape[2] // LANES):
                vs = vf[:, LANES * g:LANES * g + HEAD_V] * sv
                hi = vs.astype(v8_ref.dtype)
                lo = (vs - hi.astype(F32)).astype(v8_ref.dtype)
                ones = vf[:, LANES * g + HEAD_V:LANES * (g + 1)].astype(v8_ref.dtype)
                groups += [hi, ones, lo, jnp.zeros((rows, HEAD_V), v8_ref.dtype)]
            v8_ref[sl, :] = jnp.concatenate(groups, axis=1)
            return carry

        lax.fori_loop(0, v_ref.shape[1] // rows, v_body, 0)


def _attention_sweeps(qms, k_slices, pv_groups, k_refs, v_refs, m_ref, smax_ref, acc_ref, p_format):
    r = qms[0].shape[0]
    p_dtype, p_max_exp = p_format

    def scores(j, kc):
        return _dot(qms[j], kc[k_slices[j], :])

    def sweep(running_max):
        acc_ref[...] = jnp.zeros(acc_ref.shape, F32)
        if running_max:
            m_ref[...] = jnp.full(m_ref.shape, NEG_BIG, F32)
        else:
            smax_ref[...] = jnp.full(smax_ref.shape, NEG_BIG, F32)
            kc0 = k_refs[0][0][:, :MXU_DIM]
            for j in range(len(qms)):
                m0 = jnp.max(scores(j, kc0), axis=-1, keepdims=True)
                m_ref[j * r:(j + 1) * r, :] = jnp.broadcast_to(m0, (r, LANES))

        for k_ref, v_ref in zip(k_refs, v_refs):
            n_chunks, tk = k_ref.shape[0], k_ref.shape[2]

            def body(c, carry, k_ref=k_ref, v_ref=v_ref, tk=tk):
                kc = k_ref[c]
                vc = v_ref[pl.ds(pl.multiple_of(c * tk, tk), tk), :]
                for ids, v_lanes in pv_groups:
                    ps, alphas = [], []
                    for j in ids:
                        rows = slice(j * r, (j + 1) * r)
                        s = scores(j, kc)
                        m = m_ref[rows, :]
                        if running_max:
                            m_new = jnp.maximum(m, jnp.max(s, axis=-1, keepdims=True) - P_HEADROOM_RUNNING)
                            m_ref[rows, :] = m_new
                            alphas.append(jnp.exp2(m - m_new))
                            m = m_new
                        else:
                            cm = functools.reduce(jnp.maximum, [s[:, l:l + LANES] for l in range(0, tk, LANES)])
                            smax_ref[rows, :] = jnp.maximum(smax_ref[rows, :], cm)
                        d = (s - jnp.tile(m, (1, tk // LANES))).astype(BF16)
                        ps.append(jnp.exp2(d).astype(p_dtype))
                    rows = slice(ids[0] * r, (ids[-1] + 1) * r)
                    pv = _dot(jnp.concatenate(ps, axis=0), vc[:, v_lanes])
                    pv = pv[:, :LANES] + pv[:, LANES:]
                    if running_max:
                        acc_ref[rows, :] = acc_ref[rows, :] * jnp.concatenate(alphas, axis=0) + pv
                    else:
                        acc_ref[rows, :] += pv
                return carry

            unroll = 1 if running_max else math.gcd(n_chunks, ATTN_UNROLL)
            lax.fori_loop(0, n_chunks, body, 0, unroll=unroll)

    sweep(False)
    top_exp = jnp.max(smax_ref[...] - m_ref[...])

    @pl.when(jnp.logical_not(top_exp <= p_max_exp))
    def _():
        sweep(True)


def _gqa_kernel(*refs, n_parts, tq):
    q_ref = refs[0]
    k_refs = refs[1:1 + n_parts]
    v_refs = refs[1 + n_parts:1 + 2 * n_parts]
    scl_ref, o_ref, m_ref, smax_ref, acc_ref = refs[1 + 2 * n_parts:6 + 2 * n_parts]
    k8_refs = refs[6 + 2 * n_parts:6 + 3 * n_parts]
    v8_refs = refs[6 + 3 * n_parts:]

    @pl.when(pl.program_id(2) == 0)
    def _():
        _quantize_kv(k_refs, v_refs, k8_refs, v8_refs, scl_ref, HEAD_V)

    stack = k8_refs[0].shape[1]
    q = q_ref[0].astype(F32) * scl_ref[0, SCL_Q:SCL_Q + 1, 0:1]
    qs = jnp.concatenate([_stack_qk(q[:, HEAD_V * j:HEAD_V * (j + 1)], False, QK_DTYPE, 1, stack)
                          for j in range(GQA_GROUP)], axis=0)
    _attention_sweeps([qs], [slice(0, stack)], [((0,), slice(0, 2 * LANES))], k8_refs, v8_refs,
                      m_ref, smax_ref, acc_ref, P_E4M3)
    o = _attention_output(acc_ref[...], scl_ref)
    for j in range(GQA_GROUP):
        o_ref[0, :, HEAD_V * j:HEAD_V * (j + 1)] = o[j * tq:(j + 1) * tq].astype(BF16)


def _attention_output(acc, scl_ref):
    return acc[:, :HEAD_V] * scl_ref[0, SCL_V_INV:SCL_V_INV + 1, 0:1] / acc[:, HEAD_V:HEAD_V + 1]


def _stack_height(dk):
    return max(4 * dk, LANES)


def _fp8_kv_scratch(k_blocks, v_blocks, dk):
    ks = [pltpu.VMEM((kb[1], kb[2] // dk * _stack_height(dk), kb[3]), QK_DTYPE) for kb in k_blocks]
    vs = [pltpu.VMEM((vb[1], 2 * vb[2]), V_DTYPE) for vb in v_blocks]
    return ks + vs


def _gqa_attention(q, k_parts, v_parts, scales, tq):
    b, sq, _ = q.shape
    n_parts = len(k_parts)
    k_blocks = [(1, kp.shape[1], HEAD_V, kp.shape[3]) for kp in k_parts]
    v_blocks = [(1, vp.shape[1], LANES) for vp in v_parts]
    in_specs = [pl.BlockSpec((1, tq, 256), lambda bb, g, i: (bb, i, g))]
    in_specs += [pl.BlockSpec(kb, lambda bb, g, i: (bb, 0, g, 0)) for kb in k_blocks]
    in_specs += [pl.BlockSpec(vb, lambda bb, g, i: (bb, 0, g)) for vb in v_blocks]
    in_specs.append(pl.BlockSpec((1, SUBLANES, LANES), lambda bb, g, i: (bb, 0, 0)))
    return pl.pallas_call(
        functools.partial(_gqa_kernel, n_parts=n_parts, tq=tq),
        out_shape=jax.ShapeDtypeStruct((b, sq, 512), BF16),
        grid=(b, GQA_KV, sq // tq),
        in_specs=in_specs,
        out_specs=pl.BlockSpec((1, tq, 256), lambda bb, g, i: (bb, i, g)),
        scratch_shapes=[pltpu.VMEM((GQA_GROUP * tq, LANES), F32)] * 3 + _fp8_kv_scratch(k_blocks, v_blocks, HEAD_V),
        compiler_params=_cparams(("arbitrary", "arbitrary", "arbitrary")),
        name="gqa_attention",
    )(q, *k_parts, *v_parts, scales)


def _diff_kernel(*refs, n_parts, tq, lam_init):
    q_ref = refs[0]
    k_refs = refs[1:1 + n_parts]
    v_refs = refs[1 + n_parts:1 + 2 * n_parts]
    scl_ref, lam_ref, sg_ref, o_ref, m_ref, smax_ref, acc_ref = refs[1 + 2 * n_parts:8 + 2 * n_parts]
    k8_refs = refs[8 + 2 * n_parts:8 + 3 * n_parts]
    v8_refs = refs[8 + 3 * n_parts:]

    @pl.when(pl.program_id(2) == 0)
    def _():
        _quantize_kv(k_refs, v_refs, k8_refs, v8_refs, scl_ref, DIFF_QK)

    stack = k8_refs[0].shape[1] // 4
    q = q_ref[0].astype(F32) * scl_ref[0, SCL_Q:SCL_Q + 1, 0:1]
    qmaps = [_stack_qk(q[:, DIFF_QK * j:DIFF_QK * (j + 1)], False, QK_DTYPE, 1, stack) for j in range(4)]
    k_slices = [slice(stack * j, stack * (j + 1)) for j in range(4)]
    pv_groups = [((0, 1), slice(0, 2 * LANES)), ((2, 3), slice(2 * LANES, 4 * LANES))]
    _attention_sweeps(qmaps, k_slices, pv_groups, k8_refs, v8_refs, m_ref, smax_ref, acc_ref, P_E5M2)

    lv = lam_ref[...]
    lam = (jnp.exp(jnp.sum(lv[0:1] * lv[1:2], axis=-1, keepdims=True))
           - jnp.exp(jnp.sum(lv[2:3] * lv[3:4], axis=-1, keepdims=True)) + lam_init)
    outs = _attention_output(acc_ref[...], scl_ref)
    for hh in range(2):
        o = outs[(2 * hh) * tq:(2 * hh + 1) * tq] - lam * outs[(2 * hh + 1) * tq:(2 * hh + 2) * tq]
        ms = jnp.mean(o * o, axis=-1, keepdims=True)
        on = o * lax.rsqrt(ms + EPS) * sg_ref[...] * (1.0 - lam_init)
        o_ref[0, :, HEAD_V * hh:HEAD_V * (hh + 1)] = on.astype(BF16)


def _diff_attention(q, k_parts, v_parts, scales, lam_vecs, subln_g, lam_init, tq):
    b, sq, _ = q.shape
    n_parts = len(k_parts)
    k_blocks = [(1, kp.shape[1], LANES, kp.shape[3]) for kp in k_parts]
    v_blocks = [(1, vp.shape[1], 2 * LANES) for vp in v_parts]
    in_specs = [pl.BlockSpec((1, tq, LANES), lambda bb, p, i: (bb, i, p))]
    in_specs += [pl.BlockSpec(kb, lambda bb, p, i: (bb, 0, p, 0)) for kb in k_blocks]
    in_specs += [pl.BlockSpec(vb, lambda bb, p, i: (bb, 0, p)) for vb in v_blocks]
    in_specs.append(pl.BlockSpec((1, SUBLANES, LANES), lambda bb, p, i: (bb, 0, 0)))
    in_specs.append(pl.BlockSpec((4, DIFF_QK), lambda bb, p, i: (0, 0)))
    in_specs.append(pl.BlockSpec((1, HEAD_V), lambda bb, p, i: (0, 0)))
    return pl.pallas_call(
        functools.partial(_diff_kernel, n_parts=n_parts, tq=tq, lam_init=lam_init),
        out_shape=jax.ShapeDtypeStruct((b, sq, 256), BF16),
        grid=(b, DIFF_HEADS // 2, sq // tq),
        in_specs=in_specs,
        out_specs=pl.BlockSpec((1, tq, LANES), lambda bb, p, i: (bb, i, p)),
        scratch_shapes=[pltpu.VMEM((4 * tq, LANES), F32)] * 3 + _fp8_kv_scratch(k_blocks, v_blocks, DIFF_QK),
        compiler_params=_cparams(("arbitrary", "arbitrary", "arbitrary")),
        name="diff_attention",
    )(q, *k_parts, *v_parts, scales, lam_vecs, subln_g)


CONV_HALO = 16
CONV_ROWS = 64


def _conv_kernel(gb_ref, prev_ref, next_ref, w_ref, b_ref, lg_ref, lb_ref, o_ref, u_ref, sh_ref, *, tm):
    i = pl.program_id(1)
    last = pl.num_programs(1) - 1
    ch = w_ref.shape[1]

    def glu(z):
        z = z.astype(F32)
        return z[:, :ch] * _sigmoid(z[:, ch:])

    u_ref[CONV_HALO:CONV_HALO + tm, :] = glu(gb_ref[0])
    u_ref[0:CONV_HALO, :] = jnp.where(i > 0, glu(prev_ref[0]), 0.0)
    u_ref[CONV_HALO + tm:2 * CONV_HALO + tm, :] = jnp.where(i < last, glu(next_ref[0]), 0.0)

    span = sh_ref.shape[1]
    for r in range(1, SUBLANES):
        sh_ref[r - 1] = u_ref[r:r + span, :]

    off = CONV_HALO - CONV_K // 2
    for r0 in range(0, tm, CONV_ROWS):
        acc = jnp.zeros((CONV_ROWS, ch), F32)
        for j in range(CONV_K):
            phase, base = (off + j) % SUBLANES, r0 + (off + j) // SUBLANES * SUBLANES
            taps = u_ref[base:base + CONV_ROWS, :] if phase == 0 else sh_ref[phase - 1, base:base + CONV_ROWS, :]
            acc = acc + taps * w_ref[j:j + 1, :]
        y = acc + b_ref[...]
        mu = jnp.mean(y, axis=-1, keepdims=True)
        yc = y - mu
        var = jnp.mean(yc * yc, axis=-1, keepdims=True)
        z = yc * lax.rsqrt(var + EPS) * lg_ref[...] + lb_ref[...]
        o_ref[0, r0:r0 + CONV_ROWS, :] = (z * _sigmoid(z)).astype(BF16)


def _conformer_conv(gb, w, bias, ln_g, ln_b, tm):
    b, s, two_ch = gb.shape
    ch = two_ch // 2
    hb = tm // CONV_HALO
    n_halo = s // CONV_HALO
    const2 = lambda bb, i: (0, 0)
    return pl.pallas_call(
        functools.partial(_conv_kernel, tm=tm),
        out_shape=jax.ShapeDtypeStruct((b, s, ch), BF16),
        grid=(b, s // tm),
        in_specs=[
            pl.BlockSpec((1, tm, two_ch), lambda bb, i: (bb, i, 0)),
            pl.BlockSpec((1, CONV_HALO, two_ch), lambda bb, i: (bb, jnp.maximum(i * hb - 1, 0), 0)),
            pl.BlockSpec((1, CONV_HALO, two_ch), lambda bb, i: (bb, jnp.minimum((i + 1) * hb, n_halo - 1), 0)),
            pl.BlockSpec((CONV_K, ch), const2), pl.BlockSpec((1, ch), const2),
            pl.BlockSpec((1, ch), const2), pl.BlockSpec((1, ch), const2),
        ],
        out_specs=pl.BlockSpec((1, tm, ch), lambda bb, i: (bb, i, 0)),
        scratch_shapes=[pltpu.VMEM((tm + 2 * CONV_HALO, ch), F32),
                        pltpu.VMEM((SUBLANES - 1, tm + 2 * CONV_HALO - SUBLANES, ch), F32)],
        compiler_params=_cparams(("arbitrary", "arbitrary")),
        name="conformer_conv",
    )(gb, gb, gb, w, bias, ln_g, ln_b)


def _merge_kernel(*refs, with_router):
    oa_ref, ob_ref, oc_ref, x_ref, gate_ref, shift_ref, scale_ref, g2_ref, w_ref = refs[:9]
    if with_router:
        rw_ref, xo_ref, h2_ref, lg_ref = refs[9:]
    else:
        xo_ref, h2_ref = refs[9:]
    tm = x_ref.shape[1]
    wa = oa_ref.shape[2]
    wb = wa + ob_ref.shape[2]
    y = _dot(oa_ref[0], w_ref[0:wa, :]) + _dot(ob_ref[0], w_ref[wa:wb, :]) + _dot(oc_ref[0], w_ref[wb:, :])
    xn = x_ref[0] + gate_ref[0] * y
    xo_ref[0] = xn
    ms = jnp.mean(xn * xn, axis=-1, keepdims=True)
    h2 = xn * lax.rsqrt(ms + EPS) * g2_ref[...] * (1.0 + scale_ref[0]) + shift_ref[0]
    if with_router:
        _store_row_tiles(h2_ref, h2, tm)
        lg_ref[...] = lax.dot_general(rw_ref[...], h2, (((1,), (1,)), ((), ())),
                                      preferred_element_type=F32, precision=lax.Precision.HIGHEST)
    else:
        h2_ref[0] = h2.astype(BF16)


def _merge(oa, ob, oc, x, gate, shift, scale, g2, w_out, router_wt, tm):
    b, s, d = x.shape
    row = lambda bb, i: (bb, i, 0)
    per_b = lambda bb, i: (bb, 0, 0)
    const2 = lambda bb, i: (0, 0)
    nt = s // tm
    in_specs = [
        pl.BlockSpec((1, tm, oa.shape[2]), row), pl.BlockSpec((1, tm, ob.shape[2]), row),
        pl.BlockSpec((1, tm, oc.shape[2]), row), pl.BlockSpec((1, tm, d), row),
        pl.BlockSpec((1, 1, d), per_b), pl.BlockSpec((1, 1, d), per_b), pl.BlockSpec((1, 1, d), per_b),
        pl.BlockSpec((1, d), const2), pl.BlockSpec((d, d), const2),
    ]
    out_shape = [jax.ShapeDtypeStruct((b, s, d), F32)]
    out_specs = [pl.BlockSpec((1, tm, d), row)]
    args = [oa, ob, oc, x, gate, shift, scale, g2, w_out]
    with_router = router_wt is not None
    if not with_router:
        out_shape.append(jax.ShapeDtypeStruct((b, s, d), BF16))
        out_specs.append(pl.BlockSpec((1, tm, d), row))
    else:
        out_shape.append(jax.ShapeDtypeStruct((b * s * ROW_TILE, LANES), F32))
        out_specs.append(pl.BlockSpec((tm * ROW_TILE, LANES), lambda bb, i: (bb * nt + i, 0)))
        in_specs.append(pl.BlockSpec((N_EXPERTS, d), const2))
        out_shape.append(jax.ShapeDtypeStruct((N_EXPERTS, b * s), F32))
        out_specs.append(pl.BlockSpec((N_EXPERTS, tm), lambda bb, i: (0, bb * nt + i)))
        args.append(router_wt)
    return pl.pallas_call(
        functools.partial(_merge_kernel, with_router=with_router),
        out_shape=tuple(out_shape), grid=(b, nt), in_specs=in_specs, out_specs=tuple(out_specs),
        compiler_params=_cparams(("arbitrary", "arbitrary")), name="merge_heads",
    )(*args)


def _ffn_kernel(h_ref, x_ref, gate_ref, wg_ref, wu_ref, wd_ref, o_ref, *, tf):
    h = h_ref[...]
    ff = wg_ref.shape[1]
    acc = jnp.zeros(x_ref.shape, F32)
    for f in range(0, ff, tf):
        g = _dot(h, wg_ref[:, f:f + tf])
        u = _dot(h, wu_ref[:, f:f + tf])
        a = (g * _sigmoid(g) * u).astype(BF16)
        acc = acc + _dot(a, wd_ref[f:f + tf, :])
    o_ref[...] = x_ref[...] + gate_ref[0] * acc


def _dense_ffn(h2, x, gate, wg, wu, wd, tm):
    n, d = x.shape
    ff = wg.shape[1]
    s = n // gate.shape[0]
    row = lambda i: (i, 0)
    const2 = lambda i: (0, 0)
    resident = pl.Buffered(1)
    return pl.pallas_call(
        functools.partial(_ffn_kernel, tf=MXU_DIM),
        out_shape=jax.ShapeDtypeStruct((n, d), F32),
        grid=(n // tm,),
        in_specs=[
            pl.BlockSpec((tm, d), row), pl.BlockSpec((tm, d), row),
            pl.BlockSpec((1, 1, d), lambda i: ((i * tm) // s, 0, 0)),
            pl.BlockSpec((d, ff), const2, pipeline_mode=resident),
            pl.BlockSpec((d, ff), const2, pipeline_mode=resident),
            pl.BlockSpec((ff, d), const2, pipeline_mode=resident),
        ],
        out_specs=pl.BlockSpec((tm, d), row),
        compiler_params=_cparams(("arbitrary",)), name="dense_ffn",
    )(h2, x, gate, wg, wu, wd)


def _top2(lg):
    sub = lax.broadcasted_iota(I32, lg.shape, 0)
    l1 = jnp.max(lg, axis=0, keepdims=True)
    i1 = jnp.min(jnp.where(lg == l1, sub, N_EXPERTS), axis=0, keepdims=True)
    m1 = sub == i1
    lg2 = jnp.where(m1, -jnp.inf, lg)
    l2 = jnp.max(lg2, axis=0, keepdims=True)
    i2 = jnp.min(jnp.where(lg2 == l2, sub, N_EXPERTS), axis=0, keepdims=True)
    m2 = sub == i2
    return l1, l2, m1, m2


def _sublane_cumsum(x):
    sub = lax.broadcasted_iota(I32, x.shape, 0)
    for sh in (1, 2, 4):
        x = x + jnp.where(sub >= sh, pltpu.roll(x, sh, 0), 0.0)
    return x


def _route_kernel(lg_ref, tri_ref, dest_ref, gates_ref, be_ref, base_ref, start_ref, *, block_rows):
    phase = pl.program_id(0)
    j = pl.program_id(1)
    l1, l2, m1, m2 = _top2(lg_ref[...])
    e = jnp.where(m1 | m2, 1.0, 0.0).astype(F32)
    cnt = jnp.sum(e, axis=1, keepdims=True)

    @pl.when((phase == 0) & (j == 0))
    def _():
        base_ref[...] = jnp.zeros(base_ref.shape, F32)

    @pl.when((phase == 1) & (j == 0))
    def _():
        counts = base_ref[...]
        nblk = jnp.floor((counts + (block_rows - 1)) * (1.0 / block_rows))
        end_blk = _sublane_cumsum(nblk)
        start_ref[...] = (end_blk - nblk) * block_rows
        blk = lax.broadcasted_iota(I32, be_ref.shape, 1).astype(F32)
        owner = jnp.sum(jnp.where(end_blk[:, :1] <= blk, 1.0, 0.0), axis=0, keepdims=True)
        be_ref[...] = jnp.broadcast_to(jnp.minimum(owner, N_EXPERTS - 1.0), be_ref.shape).astype(I32)
        base_ref[...] = jnp.zeros(base_ref.shape, F32)

    @pl.when(phase == 1)
    def _():
        prefix = _dot(e.astype(BF16), tri_ref[...]) + base_ref[:, :1] + start_ref[:, :1]
        d1 = jnp.sum(jnp.where(m1, prefix, 0.0), axis=0, keepdims=True)
        d2 = jnp.sum(jnp.where(m2, prefix, 0.0), axis=0, keepdims=True)
        sub = lax.broadcasted_iota(I32, dest_ref.shape, 0)
        dest_ref[...] = jnp.where(sub == 0, d1, jnp.where(sub == 1, d2, 0.0)).astype(I32)
        ex = jnp.exp(l2 - l1)
        g1 = 1.0 / (1.0 + ex)
        g2 = ex / (1.0 + ex)
        half = lax.broadcasted_iota(I32, (LANES, lg_ref.shape[1]), 0) < LANES // 2
        gates_ref[...] = jnp.where(half, g1, g2).T

    base_ref[...] = base_ref[...] + cnt


def _route(logits_t, block_rows, n_blocks_pad, tr):
    n = logits_t.shape[1]
    tri = jnp.asarray(np.triu(np.ones((tr, tr), np.float32), k=1), BF16)
    return pl.pallas_call(
        functools.partial(_route_kernel, block_rows=block_rows),
        out_shape=(jax.ShapeDtypeStruct((N_EXPERTS, n), I32),
                   jax.ShapeDtypeStruct((n, LANES), F32),
                   jax.ShapeDtypeStruct((N_EXPERTS, n_blocks_pad), I32)),
        grid=(2, n // tr),
        in_specs=[pl.BlockSpec((N_EXPERTS, tr), lambda p, j: (0, j)),
                  pl.BlockSpec((tr, tr), lambda p, j: (0, 0))],
        out_specs=(pl.BlockSpec((N_EXPERTS, tr), lambda p, j: (0, j * p)),
                   pl.BlockSpec((tr, LANES), lambda p, j: (j * p, 0)),
                   pl.BlockSpec((N_EXPERTS, n_blocks_pad), lambda p, j: (0, 0))),
        scratch_shapes=[pltpu.VMEM((N_EXPERTS, LANES), F32), pltpu.VMEM((N_EXPERTS, LANES), F32)],
        compiler_params=_cparams(("arbitrary", "arbitrary")), name="moe_route",
    )(logits_t, tri)


def _row_copy(src_hbm, src_row, dst_hbm, dst_row, sem):
    src = pl.ds(pl.multiple_of(src_row * ROW_TILE, ROW_TILE), ROW_TILE)
    dst = pl.ds(pl.multiple_of(dst_row * ROW_TILE, ROW_TILE), ROW_TILE)
    return pltpu.make_async_copy(src_hbm.at[src], dst_hbm.at[dst], sem)


def _scatter_kernel(d1_ref, d2_ref, src_ref, init_hbm, out_hbm, sem, *, rows):
    del init_hbm

    def start(r, c):
        _row_copy(src_ref, r, out_hbm, d1_ref[0, 0, r], sem).start(priority=0)
        _row_copy(src_ref, r, out_hbm, d2_ref[0, 0, r], sem).start(priority=1)
        return c

    lax.fori_loop(0, rows, start, 0, unroll=DMA_UNROLL)
    for _ in range(2):
        pltpu.make_async_copy(src_ref, out_hbm.at[pl.ds(0, rows * ROW_TILE)], sem).wait()


def _scatter_rows(src, d1, d2, total_rows, rows):
    n = d1.shape[0]
    idx_spec = pl.BlockSpec((1, 1, rows), lambda i: (i, 0, 0), memory_space=pltpu.SMEM)
    any_spec = pl.BlockSpec(memory_space=pl.ANY)
    return pl.pallas_call(
        functools.partial(_scatter_kernel, rows=rows),
        out_shape=jax.ShapeDtypeStruct((total_rows * ROW_TILE, LANES), src.dtype),
        grid=(n // rows,),
        in_specs=[idx_spec, idx_spec, pl.BlockSpec((rows * ROW_TILE, LANES), lambda i: (i, 0)), any_spec],
        out_specs=any_spec,
        scratch_shapes=[pltpu.SemaphoreType.DMA(())],
        input_output_aliases={3: 0},
        compiler_params=pltpu.CompilerParams(dimension_semantics=("arbitrary",), has_side_effects=True),
        name="moe_scatter_rows",
    )(d1.reshape(n // rows, 1, rows), d2.reshape(n // rows, 1, rows), src,
      jnp.zeros((total_rows * ROW_TILE, LANES), src.dtype))


def _expert_kernel(be_ref, x_ref, wg_ref, wu_ref, wd_ref, o_ref, *, block_rows, tf):
    del be_ref
    x = _load_row_tiles(x_ref, block_rows).astype(BF16)
    ff = wg_ref.shape[2]
    acc = jnp.zeros((block_rows, wd_ref.shape[2]), F32)
    for f in range(0, ff, tf):
        g = _dot(x, wg_ref[0, :, f:f + tf])
        u = _dot(x, wu_ref[0, :, f:f + tf])
        a = (g * _sigmoid(g) * u).astype(BF16)
        acc = acc + _dot(a, wd_ref[0, f:f + tf, :])
    _store_row_tiles(o_ref, acc, block_rows)


def _expert_ffn(xb, blk_expert, wg, wu, wd, block_rows):
    d, ff = wg.shape[1], wg.shape[2]
    rows = xb.shape[0] // ROW_TILE
    resident = pl.Buffered(1)
    grid_spec = pltpu.PrefetchScalarGridSpec(
        num_scalar_prefetch=1,
        grid=(rows // block_rows,),
        in_specs=[
            pl.BlockSpec((block_rows * ROW_TILE, LANES), lambda i, be: (i, 0)),
            pl.BlockSpec((1, d, ff), lambda i, be: (be[i], 0, 0), pipeline_mode=resident),
            pl.BlockSpec((1, d, ff), lambda i, be: (be[i], 0, 0), pipeline_mode=resident),
            pl.BlockSpec((1, ff, d), lambda i, be: (be[i], 0, 0), pipeline_mode=resident),
        ],
        out_specs=pl.BlockSpec((block_rows * ROW_TILE, LANES), lambda i, be: (i, 0)),
    )
    return pl.pallas_call(
        functools.partial(_expert_kernel, block_rows=block_rows, tf=MXU_DIM),
        out_shape=jax.ShapeDtypeStruct(xb.shape, F32), grid_spec=grid_spec,
        compiler_params=_cparams(("arbitrary",)), name="moe_expert_ffn",
    )(blk_expert, xb, wg, wu, wd)


def _combine_kernel(d1_ref, d2_ref, x_ref, yb_hbm, gates_ref, gate_ref, fg_ref, o_ref, y1_ref, y2_ref, sem):
    tm = x_ref.shape[0]

    def start(r, c):
        _row_copy(yb_hbm, d1_ref[0, 0, r], y1_ref, r, sem).start(priority=0)
        _row_copy(yb_hbm, d2_ref[0, 0, r], y2_ref, r, sem).start(priority=1)
        return c

    lax.fori_loop(0, tm, start, 0, unroll=DMA_UNROLL)
    for y_ref in (y1_ref, y2_ref):
        pltpu.make_async_copy(yb_hbm.at[pl.ds(0, tm * ROW_TILE)], y_ref, sem).wait()

    gts = gates_ref[...]
    y = (gts[:, 0:1] * _load_row_tiles(y1_ref, tm)
         + gts[:, LANES // 2:LANES // 2 + 1] * _load_row_tiles(y2_ref, tm))
    xn = x_ref[...] + gate_ref[0] * y
    ms = jnp.mean(xn * xn, axis=-1, keepdims=True)
    o_ref[...] = xn * lax.rsqrt(ms + EPS) * fg_ref[...]


def _combine_final(x, yb, d1, d2, gates, gate, final_g, tm):
    n, d = x.shape
    s = n // gate.shape[0]
    row = lambda i: (i, 0)
    idx_spec = pl.BlockSpec((1, 1, tm), lambda i: (i, 0, 0), memory_space=pltpu.SMEM)
    return pl.pallas_call(
        _combine_kernel,
        out_shape=jax.ShapeDtypeStruct((n, d), F32),
        grid=(n // tm,),
        in_specs=[idx_spec, idx_spec,
                  pl.BlockSpec((tm, d), row),
                  pl.BlockSpec(memory_space=pl.ANY),
                  pl.BlockSpec((tm, LANES), row),
                  pl.BlockSpec((1, 1, d), lambda i: ((i * tm) // s, 0, 0)),
                  pl.BlockSpec((1, d), lambda i: (0, 0))],
        out_specs=pl.BlockSpec((tm, d), row),
        scratch_shapes=[pltpu.VMEM((tm * ROW_TILE, LANES), F32), pltpu.VMEM((tm * ROW_TILE, LANES), F32),
                        pltpu.SemaphoreType.DMA(())],
        compiler_params=_cparams(("arbitrary",)), name="moe_combine_final",
    )(d1.reshape(n // tm, 1, tm), d2.reshape(n // tm, 1, tm), x, yb, gates, gate, final_g)


AMAX_QA, AMAX_KA, AMAX_VA, AMAX_QC, AMAX_KC, AMAX_VC = range(6)
FP8_TARGET_MAX = 256.0


def _fp8_scales(q_max, k_max, v_max, head_dim):
    tiny = jnp.finfo(F32).tiny
    c = head_dim ** -0.5 * LOG2E
    ratio = jnp.where((q_max > 0) & (k_max > 0), c * k_max / jnp.maximum(q_max, tiny), 1.0)
    sq = jnp.exp2(jnp.round(0.5 * jnp.log2(ratio)))
    sk = c / sq
    sv = jnp.where(v_max > 0, jnp.exp2(jnp.floor(jnp.log2(FP8_TARGET_MAX / jnp.maximum(v_max, tiny)))), 1.0)
    rows = jnp.stack([sq, sk, sv, 1.0 / sv] + [jnp.zeros_like(sq)] * (SUBLANES - 4), axis=1)
    return jnp.broadcast_to(rows[:, :, None], rows.shape + (LANES,)).astype(F32)


def _rope_tables(s, dim):
    half = dim // 2
    t = jnp.arange(s)
    inv = 1.0 / (ROPE_THETA ** (jnp.arange(0, half, 2, dtype=F32) / half))
    ang_r = (t // GRID_W).astype(F32)[:, None] * inv
    ang_c = (t % GRID_W).astype(F32)[:, None] * inv
    ang = jnp.concatenate([ang_r, ang_r, ang_c, ang_c], axis=-1)
    reps = LANES // dim
    return jnp.tile(jnp.cos(ang), (1, reps)), jnp.tile(jnp.sin(ang), (1, reps))


def _rotate_matrix(dim):
    q = dim // 4
    p = np.zeros((MXU_DIM, MXU_DIM), np.float32)
    for j in range(MXU_DIM):
        if (j % (2 * q)) < q:
            p[j + q, j] = -1.0
        else:
            p[j - q, j] = 1.0
    return jnp.asarray(p, BF16)


def _head_mean_matrix():
    m = np.kron(np.eye(MXU_DIM // HEAD_V, dtype=np.float32), np.full((HEAD_V, HEAD_V), 1.0 / HEAD_V, np.float32))
    return jnp.asarray(m, BF16)


def _widen_values(w, heads):
    d = w.shape[0]
    w = w.reshape(d, heads, HEAD_V)
    return jnp.concatenate([w, jnp.zeros_like(w)], axis=-1).reshape(d, heads * LANES)


def _widen_in_proj(w):
    qa, ka, va, gb, qc, kc, vc = jnp.split(w, [256, 512, 768, 1280, 1792, 1920], axis=1)
    return jnp.concatenate([qa, ka, _widen_values(va, DIFF_HEADS), gb, qc, kc, _widen_values(vc, GQA_KV)],
                           axis=1).astype(BF16)


def kernel(x, c, ctx, c_ctx, ada_w, ada_b, norm1_g, norm2_g, w_in, w_out, lam_q1, lam_k1, lam_q2, lam_k2,
           diff_subln_g, conv_w, conv_b, conv_ln_g, conv_ln_b, q_norm_g, k_norm_g, ffn_gate, ffn_up, ffn_down,
           router_w, moe_gate, moe_up, moe_down, final_g):
    b, s, d = x.shape
    sc = ctx.shape[1]
    depth = ada_w.shape[0]
    n = b * s
    assert depth % 2 == 0, "the final RMSNorm is fused into the MoE combine of the last (odd) layer"

    tm = min(512, s)
    tmc = min(512, sc)
    tq = min(512, s)
    tq_diff = min(1024, s)
    tqc = min(256, sc)

    tabs_x = _rope_tables(s, DIFF_QK) + _rope_tables(s, HEAD_V)
    ones_c, zeros_c = jnp.ones((sc, LANES), F32), jnp.zeros((sc, LANES), F32)
    tabs_c = (ones_c, zeros_c, ones_c, zeros_c)
    mats = (_rotate_matrix(DIFF_QK), _rotate_matrix(HEAD_V), _head_mean_matrix())

    cc = jnp.zeros((16, d), F32).at[:b].set(c).at[b].set(c_ctx)

    for i in range(depth):
        last = i == depth - 1
        lam_init = 0.8 - 0.6 * math.exp(-0.3 * i)
        mod_all = _ada_mod(cc, ada_w[i], ada_b[i])
        mod = mod_all[:b].reshape(b, 6, 1, d)
        modc = jnp.broadcast_to(mod_all[b].reshape(1, 6, 1, d), (b, 6, 1, d))

        w_aug = _widen_in_proj(w_in[i])
        g1 = norm1_g[i].reshape(1, d)
        qg = jnp.tile(q_norm_g[i], GQA_HEADS).reshape(1, -1)
        kg = jnp.tile(k_norm_g[i], GQA_KV).reshape(1, -1)
        lam_vecs = jnp.stack([lam_q1[i], lam_k1[i], lam_q2[i], lam_k2[i]]).astype(F32)
        subln = diff_subln_g[i].reshape(1, HEAD_V)
        conv_args = (conv_w[i], conv_b[i].reshape(1, -1), conv_ln_g[i].reshape(1, -1), conv_ln_b[i].reshape(1, -1))
        w_o = w_out[i].astype(BF16)
        g2 = norm2_g[i].reshape(1, d)

        qa, kat, va, gb, qc, kct, vc, amax = _in_projection(
            x, mod[:, 0], mod[:, 1], g1, w_aug, tabs_x, mats, qg, kg, tm)
        qa_x, kat_x, va_x, gb_x, qc_x, kct_x, vc_x, amax_x = _in_projection(
            ctx, modc[:, 0], modc[:, 1], g1, w_aug, tabs_c, mats, qg, kg, tmc)
        amax = jnp.max(amax, axis=(1, 3))
        amax_x = jnp.max(amax_x, axis=(1, 3))
        amax_kv = jnp.maximum(amax, amax_x)
        scl_a = _fp8_scales(amax[:, AMAX_QA], amax_kv[:, AMAX_KA], amax_kv[:, AMAX_VA], DIFF_QK)
        scl_c = _fp8_scales(amax[:, AMAX_QC], amax_kv[:, AMAX_KC], amax_kv[:, AMAX_VC], HEAD_V)

        oa = _diff_attention(qa, [kat, kat_x], [va, va_x], scl_a, lam_vecs, subln, lam_init, tq_diff)
        ob = _conformer_conv(gb, *conv_args, tm)
        oc = _gqa_attention(qc, [kct, kct_x], [vc, vc_x], scl_c, tq)

        j = i // 2
        if i % 2 == 0:
            x, h2 = _merge(oa, ob, oc, x, mod[:, 2], mod[:, 3], mod[:, 4], g2, w_o, None, tm)
            wg, wu, wd = ffn_gate[j].astype(BF16), ffn_up[j].astype(BF16), ffn_down[j].astype(BF16)
            x = _dense_ffn(h2.reshape(n, d), x.reshape(n, d), mod[:, 5], wg, wu, wd, tm).reshape(b, s, d)
        else:
            rwt = router_w[j].T.astype(F32)
            x, h2, logits_t = _merge(oa, ob, oc, x, mod[:, 2], mod[:, 3], mod[:, 4], g2, w_o, rwt, tm)
            block_rows = 512 if n >= 8192 else 256
            n_blocks = (2 * n) // block_rows + N_EXPERTS
            n_blocks_pad = -(-n_blocks // LANES) * LANES
            dest, gates, blk_e = _route(logits_t, block_rows, n_blocks_pad, min(512, n))
            xb = _scatter_rows(h2, dest[0], dest[1], n_blocks * block_rows, min(256, n))
            yb = _expert_ffn(xb, blk_e[0, :n_blocks], moe_gate[j].astype(BF16), moe_up[j].astype(BF16),
                             moe_down[j].astype(BF16), block_rows)
            assert last
            x = _combine_final(x.reshape(n, d), yb, dest[0], dest[1], gates, mod[:, 5], final_g.reshape(1, d),
                               tm).reshape(b, s, d)

        if not last:
            scl_ax = _fp8_scales(amax_x[:, AMAX_QA], amax_x[:, AMAX_KA], amax_x[:, AMAX_VA], DIFF_QK)
            scl_cx = _fp8_scales(amax_x[:, AMAX_QC], amax_x[:, AMAX_KC], amax_x[:, AMAX_VC], HEAD_V)
            oa_x = _diff_attention(qa_x, [kat_x], [va_x], scl_ax, lam_vecs, subln, lam_init, tqc)
            ob_x = _conformer_conv(gb_x, *conv_args, tmc)
            oc_x = _gqa_attention(qc_x, [kct_x], [vc_x], scl_cx, tqc)
            assert i % 2 == 0, "context tokens only ever pass through dense channel mixers"
            ctx, hc2 = _merge(oa_x, ob_x, oc_x, ctx, modc[:, 2], modc[:, 3], modc[:, 4], g2, w_o, None, tmc)
            ctx = _dense_ffn(hc2.reshape(b * sc, d), ctx.reshape(b * sc, d), modc[:, 5], wg, wu, wd,
                             tmc).reshape(b, sc, d)

    return x
```

```python
import functools
import math

import numpy as np
import jax
import jax.numpy as jnp
from jax import lax
from jax.experimental import pallas as pl
from jax.experimental.pallas import tpu as pltpu

F32 = jnp.float32
BF16 = jnp.bfloat16
I32 = jnp.int32

EPS = 1e-6
ROPE_THETA = 10000.0
GRID_W = 64

DIFF_HEADS = 4
DIFF_QK = 32
HEAD_V = 64
GQA_HEADS = 8
GQA_KV = 2
GQA_GROUP = GQA_HEADS // GQA_KV
CONV_K = 31
N_EXPERTS = 8
LOG2E = math.log2(math.e)

LANES = 128
SUBLANES = 8
MXU_DIM = 256
VMEM_LIMIT = 52 * 1024 * 1024
NEG_BIG = -1e30
V_DTYPE = jnp.float8_e4m3fn
QK_DTYPE = jnp.float8_e4m3fn
P_E4M3 = (jnp.float8_e4m3fn, 8.5)
P_E5M2 = (jnp.float8_e5m2, 15.5)
P_HEADROOM_RUNNING = 8.0
ATTN_UNROLL = 8
DMA_UNROLL = 8

C_QA, C_KA, C_VA, C_GB, C_QC, C_KC, C_VC, C_END = 0, 256, 512, 1024, 1536, 2048, 2176, 2432


def _cparams(semantics):
    return pltpu.CompilerParams(dimension_semantics=semantics, vmem_limit_bytes=VMEM_LIMIT)


def _dot(a, b):
    return jnp.dot(a, b, preferred_element_type=F32)


def _sigmoid(z):
    return 1.0 / (1.0 + jnp.exp(-z))


ROW_TILE = 8


def _store_row_tiles(ref, val, rows):
    for a in range(ROW_TILE):
        ref[pl.ds(a, rows, stride=ROW_TILE), :] = val[:, a * LANES:(a + 1) * LANES]


def _load_row_tiles(ref, rows):
    return jnp.concatenate([ref[pl.ds(a, rows, stride=ROW_TILE), :] for a in range(ROW_TILE)], axis=1)


def _mod_kernel(c_ref, w_ref, b_ref, o_ref):
    c = c_ref[...]
    s = c * _sigmoid(c)
    o_ref[...] = jnp.dot(s, w_ref[...], preferred_element_type=F32, precision=lax.Precision.HIGHEST) + b_ref[...]


def _ada_mod(cc, w, b):
    rows, d = cc.shape
    n = w.shape[1]
    tn = d
    return pl.pallas_call(
        _mod_kernel,
        out_shape=jax.ShapeDtypeStruct((rows, n), F32),
        grid=(n // tn,),
        in_specs=[pl.BlockSpec((rows, d), lambda j: (0, 0)),
                  pl.BlockSpec((d, tn), lambda j: (0, j)),
                  pl.BlockSpec((1, tn), lambda j: (0, j))],
        out_specs=pl.BlockSpec((rows, tn), lambda j: (0, j)),
        compiler_params=_cparams(("arbitrary",)),
        name="ada_mod",
    )(cc, w, b.reshape(1, n))


def _inproj_kernel(x_ref, shift_ref, scale_ref, g_ref, w_ref, cosa_ref, sina_ref, cosc_ref, sinc_ref,
                   pa_ref, pc_ref, hm_ref, qg_ref, kg_ref,
                   qa_o, kat_o, va_o, gb_o, qc_o, kct_o, vc_o, amax_o):
    n_sub, sub = kat_o.shape[1], kat_o.shape[3]
    stats = None
    for t in range(n_sub):
        rows = slice(t * sub, (t + 1) * sub)
        tabs = [ref[rows, :] for ref in (cosa_ref, sina_ref, cosc_ref, sinc_ref)]
        outs = [o.at[0, rows, :] for o in (qa_o, va_o, gb_o, qc_o, vc_o)] + [kat_o.at[0, t], kct_o.at[0, t]]
        st = _inproj_subtile(x_ref[0, rows, :], shift_ref, scale_ref, g_ref, w_ref, tabs, pa_ref, pc_ref, hm_ref,
                             qg_ref, kg_ref, outs, amax_o.shape[3])
        stats = st if stats is None else jnp.maximum(stats, st)
    amax_o[0, 0] = stats


def _inproj_subtile(x, shift_ref, scale_ref, g_ref, w_ref, tabs, pa_ref, pc_ref, hm_ref, qg_ref, kg_ref, outs, stat_w):
    qa_o, va_o, gb_o, qc_o, vc_o, kat_o, kct_o = outs
    cosa, sina, cosc, sinc = tabs
    ms = jnp.mean(x * x, axis=-1, keepdims=True)
    h = x * lax.rsqrt(ms + EPS) * g_ref[...]
    h = h * (1.0 + scale_ref[0]) + shift_ref[0]
    hb = h.astype(BF16)

    def proj(lo, hi):
        return _dot(hb, w_ref[:, lo:hi])

    def blockmat(y, m_ref):
        yb = y.astype(BF16)
        w = y.shape[1]
        if w == LANES:
            return _dot(yb, m_ref[:LANES, :LANES])
        return jnp.concatenate([_dot(yb[:, c:c + MXU_DIM], m_ref[...]) for c in range(0, w, MXU_DIM)], axis=1)

    def rope(y, cos, sin, p_ref):
        reps = y.shape[1] // LANES
        cos = jnp.tile(cos, (1, reps))
        sin = jnp.tile(sin, (1, reps))
        return y * cos + blockmat(y, p_ref) * sin

    def ones_col(width):
        lane = lax.broadcasted_iota(I32, (1, width), 1)
        return jnp.where(lane % LANES == HEAD_V, 1.0, 0.0).astype(F32)

    def col_amax(y):
        cm = jnp.max(jnp.abs(y), axis=0, keepdims=True)
        pad = stat_w - y.shape[1]
        return cm if pad == 0 else jnp.concatenate([cm, jnp.zeros((1, pad), F32)], axis=1)

    qa = rope(proj(C_QA, C_KA), cosa, sina, pa_ref)
    qa_o[...] = qa.astype(BF16)
    ka = rope(proj(C_KA, C_VA), cosa, sina, pa_ref)
    kat_o[...] = ka.T.astype(BF16)
    va = proj(C_VA, C_GB)
    va_o[...] = (va + ones_col(C_GB - C_VA)).astype(BF16)
    gb_o[...] = proj(C_GB, C_QC).astype(BF16)

    y = proj(C_QC, C_KC)
    yn = y * lax.rsqrt(blockmat(y * y, hm_ref) + EPS) * qg_ref[...]
    qc = rope(yn, cosc, sinc, pc_ref)
    qc_o[...] = qc.astype(BF16)

    y = proj(C_KC, C_VC)
    yn = y * lax.rsqrt(blockmat(y * y, hm_ref) + EPS) * kg_ref[...]
    kc = rope(yn, cosc, sinc, pc_ref)
    kct_o[...] = kc.T.astype(BF16)

    vc = proj(C_VC, C_END)
    vc_o[...] = (vc + ones_col(C_END - C_VC)).astype(BF16)

    stats = [col_amax(t) for t in (qa, ka, va, qc, kc, vc)]
    stats += [jnp.zeros((1, stat_w), F32)] * (SUBLANES - len(stats))
    return jnp.concatenate(stats, axis=0)


def _in_projection(x, shift, scale, g1, w_aug, tabs, mats, qg, kg, tk, n_sub):
    b, s, d = x.shape
    tm = tk * n_sub
    nt = s // tm
    cosa, sina, cosc, sinc = tabs
    pa, pc, hm = mats
    row = lambda bb, i: (bb, i, 0)
    const2 = lambda bb, i: (0, 0)
    per_b = lambda bb, i: (bb, 0, 0)
    tab = lambda bb, i: (i, 0)
    out_shape = (
        jax.ShapeDtypeStruct((b, s, 256), BF16),
        jax.ShapeDtypeStruct((b, s // tk, 256, tk), BF16),
        jax.ShapeDtypeStruct((b, s, 512), BF16),
        jax.ShapeDtypeStruct((b, s, 512), BF16),
        jax.ShapeDtypeStruct((b, s, 512), BF16),
        jax.ShapeDtypeStruct((b, s // tk, 128, tk), BF16),
        jax.ShapeDtypeStruct((b, s, 256), BF16),
        jax.ShapeDtypeStruct((b, nt, SUBLANES, 512), F32),
    )
    out_specs = (
        pl.BlockSpec((1, tm, 256), row),
        pl.BlockSpec((1, n_sub, 256, tk), lambda bb, i: (bb, i, 0, 0)),
        pl.BlockSpec((1, tm, 512), row),
        pl.BlockSpec((1, tm, 512), row),
        pl.BlockSpec((1, tm, 512), row),
        pl.BlockSpec((1, n_sub, 128, tk), lambda bb, i: (bb, i, 0, 0)),
        pl.BlockSpec((1, tm, 256), row),
        pl.BlockSpec((1, 1, SUBLANES, 512), lambda bb, i: (bb, i, 0, 0)),
    )
    in_specs = [
        pl.BlockSpec((1, tm, d), row),
        pl.BlockSpec((1, 1, d), per_b),
        pl.BlockSpec((1, 1, d), per_b),
        pl.BlockSpec((1, d), const2),
        pl.BlockSpec((d, C_END), const2),
        pl.BlockSpec((tm, LANES), tab), pl.BlockSpec((tm, LANES), tab),
        pl.BlockSpec((tm, LANES), tab), pl.BlockSpec((tm, LANES), tab),
        pl.BlockSpec((MXU_DIM, MXU_DIM), const2), pl.BlockSpec((MXU_DIM, MXU_DIM), const2),
        pl.BlockSpec((MXU_DIM, MXU_DIM), const2),
        pl.BlockSpec((1, 512), const2), pl.BlockSpec((1, 128), const2),
    ]
    return pl.pallas_call(
        _inproj_kernel, out_shape=out_shape, grid=(b, nt), in_specs=in_specs, out_specs=out_specs,
        compiler_params=_cparams(("arbitrary", "arbitrary")), name="in_projection",
    )(x, shift, scale, g1, w_aug, cosa, sina, cosc, sinc, pa, pc, hm, qg, kg)


SCL_Q, SCL_K, SCL_V, SCL_V_INV = 0, 1, 2, 3


LO_GAIN = 16.0
V_ROW_CHUNK = 512


def _split_fp8(x, dtype):
    hi = x.astype(dtype)
    return hi, ((x - hi.astype(F32)) * LO_GAIN).astype(dtype)


def _stack_qk(x, other_side, dtype, axis, width):
    hi, lo = _split_fp8(x, dtype)
    hi_small = (hi.astype(F32) * (1.0 / LO_GAIN)).astype(dtype)
    parts = [hi, lo, hi_small] if other_side else [hi, hi_small, lo]
    pad = width - 3 * x.shape[axis]
    if pad:
        pad_shape = tuple(pad if a == axis else n for a, n in enumerate(x.shape))
        parts.append(jnp.zeros(pad_shape, dtype))
    return jnp.concatenate(parts, axis=axis)


def _quantize_kv(k_refs, v_refs, k8_refs, v8_refs, scl_ref, dk):
    sk = scl_ref[0, SCL_K:SCL_K + 1, 0:1]
    sv = scl_ref[0, SCL_V:SCL_V + 1, 0:1]
    for k_ref, k8_ref in zip(k_refs, k8_refs):
        n_maps = k_ref.shape[2] // dk
        stack = k8_ref.shape[1] // n_maps

        def k_body(c, carry, k_ref=k_ref, k8_ref=k8_ref, n_maps=n_maps, stack=stack):
            kf = k_ref[0, c].astype(F32) * sk
            k8_ref[c] = jnp.concatenate(
                [_stack_qk(kf[dk * j:dk * (j + 1), :], True, k8_ref.dtype, 0, stack) for j in range(n_maps)], axis=0)
            return carry

        lax.fori_loop(0, k_ref.shape[1], k_body, 0)

    for v_ref, v8_ref in zip(v_refs, v8_refs):
        rows = min(V_ROW_CHUNK, v_ref.shape[1])

        def v_body(c, carry, v_ref=v_ref, v8_ref=v8_ref, rows=rows):
            sl = pl.ds(pl.multiple_of(c * rows, rows), rows)
            vf = v_ref[0, sl, :].astype(F32)
            groups = []
            for g in range(v_ref.shape[2] // LANES):
                vs = vf[:, LANES * g:LANES * g + HEAD_V] * sv
                hi = vs.astype(v8_ref.dtype)
                lo = (vs - hi.astype(F32)).astype(v8_ref.dtype)
                ones = vf[:, LANES * g + HEAD_V:LANES * (g + 1)].astype(v8_ref.dtype)
                groups += [hi, ones, lo, jnp.zeros((rows, HEAD_V), v8_ref.dtype)]
            v8_ref[sl, :] = jnp.concatenate(groups, axis=1)
            return carry

        lax.fori_loop(0, v_ref.shape[1] // rows, v_body, 0)


def _attention_sweeps(qms, k_slices, pv_groups, k_refs, v_refs, m_ref, smax_ref, acc_ref, p_format):
    r = qms[0].shape[0]
    p_dtype, p_max_exp = p_format

    def scores(j, kc):
        return _dot(qms[j], kc[k_slices[j], :])

    def sweep(running_max):
        acc_ref[...] = jnp.zeros(acc_ref.shape, F32)
        if running_max:
            m_ref[...] = jnp.full(m_ref.shape, NEG_BIG, F32)
        else:
            smax_ref[...] = jnp.full(smax_ref.shape, NEG_BIG, smax_ref.dtype)
            kc0 = k_refs[0][0][:, :MXU_DIM]
            for j in range(len(qms)):
                m0 = jnp.max(scores(j, kc0), axis=-1, keepdims=True)
                m_ref[j * r:(j + 1) * r, :] = jnp.broadcast_to(m0, (r, LANES))

        for k_ref, v_ref in zip(k_refs, v_refs):
            n_chunks, tk = k_ref.shape[0], k_ref.shape[2]

            def body(c, carry, k_ref=k_ref, v_ref=v_ref, tk=tk):
                kc = k_ref[c]
                vc = v_ref[pl.ds(pl.multiple_of(c * tk, tk), tk), :]
                for ids, v_lanes in pv_groups:
                    ps, alphas = [], []
                    for j in ids:
                        rows = slice(j * r, (j + 1) * r)
                        s = scores(j, kc)
                        m = m_ref[rows, :]
                        if running_max:
                            m_new = jnp.maximum(m, jnp.max(s, axis=-1, keepdims=True) - P_HEADROOM_RUNNING)
                            m_ref[rows, :] = m_new
                            alphas.append(jnp.exp2(m - m_new))
                            m = m_new
                        d = (s - jnp.tile(m, (1, tk // LANES))).astype(BF16)
                        if not running_max:
                            cm = functools.reduce(jnp.maximum, [d[:, l:l + LANES] for l in range(0, tk, LANES)])
                            smax_ref[rows, :] = jnp.maximum(smax_ref[rows, :], cm)
                        ps.append(jnp.exp2(d).astype(p_dtype))
                    rows = slice(ids[0] * r, (ids[-1] + 1) * r)
                    pv = _dot(jnp.concatenate(ps, axis=0), vc[:, v_lanes])
                    pv = pv[:, :LANES] + pv[:, LANES:]
                    if running_max:
                        acc_ref[rows, :] = acc_ref[rows, :] * jnp.concatenate(alphas, axis=0) + pv
                    else:
                        acc_ref[rows, :] += pv
                return carry

            unroll = 1 if running_max else math.gcd(n_chunks, ATTN_UNROLL)
            lax.fori_loop(0, n_chunks, body, 0, unroll=unroll)

    sweep(False)
    top_exp = jnp.max(smax_ref[...].astype(F32))

    @pl.when(jnp.logical_not(top_exp <= p_max_exp))
    def _():
        sweep(True)


def _gqa_kernel(*refs, n_parts, tq):
    q_ref = refs[0]
    k_refs = refs[1:1 + n_parts]
    v_refs = refs[1 + n_parts:1 + 2 * n_parts]
    scl_ref, o_ref, m_ref, smax_ref, acc_ref = refs[1 + 2 * n_parts:6 + 2 * n_parts]
    k8_refs = refs[6 + 2 * n_parts:6 + 3 * n_parts]
    v8_refs = refs[6 + 3 * n_parts:]

    @pl.when(pl.program_id(2) == 0)
    def _():
        _quantize_kv(k_refs, v_refs, k8_refs, v8_refs, scl_ref, HEAD_V)

    stack = k8_refs[0].shape[1]
    q = q_ref[0].astype(F32) * scl_ref[0, SCL_Q:SCL_Q + 1, 0:1]
    qs = jnp.concatenate([_stack_qk(q[:, HEAD_V * j:HEAD_V * (j + 1)], False, QK_DTYPE, 1, stack)
                          for j in range(GQA_GROUP)], axis=0)
    _attention_sweeps([qs], [slice(0, stack)], [((0,), slice(0, 2 * LANES))], k8_refs, v8_refs,
                      m_ref, smax_ref, acc_ref, P_E4M3)
    o = _attention_output(acc_ref[...], scl_ref)
    for j in range(GQA_GROUP):
        o_ref[0, :, HEAD_V * j:HEAD_V * (j + 1)] = o[j * tq:(j + 1) * tq].astype(BF16)


def _attention_output(acc, scl_ref):
    return acc[:, :HEAD_V] * scl_ref[0, SCL_V_INV:SCL_V_INV + 1, 0:1] / acc[:, HEAD_V:HEAD_V + 1]


def _softmax_scratch(rows):
    return [pltpu.VMEM((rows, LANES), F32), pltpu.VMEM((rows, LANES), BF16), pltpu.VMEM((rows, LANES), F32)]


def _stack_height(dk):
    return max(4 * dk, LANES)


def _fp8_kv_scratch(k_blocks, v_blocks, dk):
    ks = [pltpu.VMEM((kb[1], kb[2] // dk * _stack_height(dk), kb[3]), QK_DTYPE) for kb in k_blocks]
    vs = [pltpu.VMEM((vb[1], 2 * vb[2]), V_DTYPE) for vb in v_blocks]
    return ks + vs


def _gqa_attention(q, k_parts, v_parts, scales, tq):
    b, sq, _ = q.shape
    n_parts = len(k_parts)
    k_blocks = [(1, kp.shape[1], HEAD_V, kp.shape[3]) for kp in k_parts]
    v_blocks = [(1, vp.shape[1], LANES) for vp in v_parts]
    in_specs = [pl.BlockSpec((1, tq, 256), lambda bb, g, i: (bb, i, g))]
    in_specs += [pl.BlockSpec(kb, lambda bb, g, i: (bb, 0, g, 0)) for kb in k_blocks]
    in_specs += [pl.BlockSpec(vb, lambda bb, g, i: (bb, 0, g)) for vb in v_blocks]
    in_specs.append(pl.BlockSpec((1, SUBLANES, LANES), lambda bb, g, i: (bb, 0, 0)))
    return pl.pallas_call(
        functools.partial(_gqa_kernel, n_parts=n_parts, tq=tq),
        out_shape=jax.ShapeDtypeStruct((b, sq, 512), BF16),
        grid=(b, GQA_KV, sq // tq),
        in_specs=in_specs,
        out_specs=pl.BlockSpec((1, tq, 256), lambda bb, g, i: (bb, i, g)),
        scratch_shapes=_softmax_scratch(GQA_GROUP * tq) + _fp8_kv_scratch(k_blocks, v_blocks, HEAD_V),
        compiler_params=_cparams(("arbitrary", "arbitrary", "arbitrary")),
        name="gqa_attention",
    )(q, *k_parts, *v_parts, scales)


def _diff_kernel(*refs, n_parts, tq, lam_init):
    q_ref = refs[0]
    k_refs = refs[1:1 + n_parts]
    v_refs = refs[1 + n_parts:1 + 2 * n_parts]
    scl_ref, lam_ref, sg_ref, o_ref, m_ref, smax_ref, acc_ref = refs[1 + 2 * n_parts:8 + 2 * n_parts]
    k8_refs = refs[8 + 2 * n_parts:8 + 3 * n_parts]
    v8_refs = refs[8 + 3 * n_parts:]

    @pl.when(pl.program_id(2) == 0)
    def _():
        _quantize_kv(k_refs, v_refs, k8_refs, v8_refs, scl_ref, DIFF_QK)

    stack = k8_refs[0].shape[1] // 4
    q = q_ref[0].astype(F32) * scl_ref[0, SCL_Q:SCL_Q + 1, 0:1]
    qmaps = [_stack_qk(q[:, DIFF_QK * j:DIFF_QK * (j + 1)], False, QK_DTYPE, 1, stack) for j in range(4)]
    k_slices = [slice(stack * j, stack * (j + 1)) for j in range(4)]
    pv_groups = [((0, 1), slice(0, 2 * LANES)), ((2, 3), slice(2 * LANES, 4 * LANES))]
    _attention_sweeps(qmaps, k_slices, pv_groups, k8_refs, v8_refs, m_ref, smax_ref, acc_ref, P_E5M2)

    lv = lam_ref[...]
    lam = (jnp.exp(jnp.sum(lv[0:1] * lv[1:2], axis=-1, keepdims=True))
           - jnp.exp(jnp.sum(lv[2:3] * lv[3:4], axis=-1, keepdims=True)) + lam_init)
    outs = _attention_output(acc_ref[...], scl_ref)
    for hh in range(2):
        o = outs[(2 * hh) * tq:(2 * hh + 1) * tq] - lam * outs[(2 * hh + 1) * tq:(2 * hh + 2) * tq]
        ms = jnp.mean(o * o, axis=-1, keepdims=True)
        on = o * lax.rsqrt(ms + EPS) * sg_ref[...] * (1.0 - lam_init)
        o_ref[0, :, HEAD_V * hh:HEAD_V * (hh + 1)] = on.astype(BF16)


def _diff_attention(q, k_parts, v_parts, scales, lam_vecs, subln_g, lam_init, tq):
    b, sq, _ = q.shape
    n_parts = len(k_parts)
    k_blocks = [(1, kp.shape[1], LANES, kp.shape[3]) for kp in k_parts]
    v_blocks = [(1, vp.shape[1], 2 * LANES) for vp in v_parts]
    in_specs = [pl.BlockSpec((1, tq, LANES), lambda bb, p, i: (bb, i, p))]
    in_specs += [pl.BlockSpec(kb, lambda bb, p, i: (bb, 0, p, 0)) for kb in k_blocks]
    in_specs += [pl.BlockSpec(vb, lambda bb, p, i: (bb, 0, p)) for vb in v_blocks]
    in_specs.append(pl.BlockSpec((1, SUBLANES, LANES), lambda bb, p, i: (bb, 0, 0)))
    in_specs.append(pl.BlockSpec((4, DIFF_QK), lambda bb, p, i: (0, 0)))
    in_specs.append(pl.BlockSpec((1, HEAD_V), lambda bb, p, i: (0, 0)))
    return pl.pallas_call(
        functools.partial(_diff_kernel, n_parts=n_parts, tq=tq, lam_init=lam_init),
        out_shape=jax.ShapeDtypeStruct((b, sq, 256), BF16),
        grid=(b, DIFF_HEADS // 2, sq // tq),
        in_specs=in_specs,
        out_specs=pl.BlockSpec((1, tq, LANES), lambda bb, p, i: (bb, i, p)),
        scratch_shapes=_softmax_scratch(4 * tq) + _fp8_kv_scratch(k_blocks, v_blocks, DIFF_QK),
        compiler_params=_cparams(("arbitrary", "arbitrary", "arbitrary")),
        name="diff_attention",
    )(q, *k_parts, *v_parts, scales, lam_vecs, subln_g)


CONV_HALO = 16
CONV_ROWS = 64


def _conv_kernel(gb_ref, prev_ref, next_ref, w_ref, b_ref, lg_ref, lb_ref, o_ref, u_ref, sh_ref, *, tm):
    i = pl.program_id(1)
    last = pl.num_programs(1) - 1
    ch = w_ref.shape[1]

    def glu(z):
        z = z.astype(F32)
        return z[:, :ch] * _sigmoid(z[:, ch:])

    u_ref[CONV_HALO:CONV_HALO + tm, :] = glu(gb_ref[0])
    u_ref[0:CONV_HALO, :] = jnp.where(i > 0, glu(prev_ref[0]), 0.0)
    u_ref[CONV_HALO + tm:2 * CONV_HALO + tm, :] = jnp.where(i < last, glu(next_ref[0]), 0.0)

    span = sh_ref.shape[1]
    for r in range(1, SUBLANES):
        sh_ref[r - 1] = u_ref[r:r + span, :]

    off = CONV_HALO - CONV_K // 2
    for r0 in range(0, tm, CONV_ROWS):
        acc = jnp.zeros((CONV_ROWS, ch), F32)
        for j in range(CONV_K):
            phase, base = (off + j) % SUBLANES, r0 + (off + j) // SUBLANES * SUBLANES
            taps = u_ref[base:base + CONV_ROWS, :] if phase == 0 else sh_ref[phase - 1, base:base + CONV_ROWS, :]
            acc = acc + taps * w_ref[j:j + 1, :]
        y = acc + b_ref[...]
        mu = jnp.mean(y, axis=-1, keepdims=True)
        yc = y - mu
        var = jnp.mean(yc * yc, axis=-1, keepdims=True)
        z = yc * lax.rsqrt(var + EPS) * lg_ref[...] + lb_ref[...]
        o_ref[0, r0:r0 + CONV_ROWS, :] = (z * _sigmoid(z)).astype(BF16)


def _conformer_conv(gb, w, bias, ln_g, ln_b, tm):
    b, s, two_ch = gb.shape
    ch = two_ch // 2
    hb = tm // CONV_HALO
    n_halo = s // CONV_HALO
    const2 = lambda bb, i: (0, 0)
    return pl.pallas_call(
        functools.partial(_conv_kernel, tm=tm),
        out_shape=jax.ShapeDtypeStruct((b, s, ch), BF16),
        grid=(b, s // tm),
        in_specs=[
            pl.BlockSpec((1, tm, two_ch), lambda bb, i: (bb, i, 0)),
            pl.BlockSpec((1, CONV_HALO, two_ch), lambda bb, i: (bb, jnp.maximum(i * hb - 1, 0), 0)),
            pl.BlockSpec((1, CONV_HALO, two_ch), lambda bb, i: (bb, jnp.minimum((i + 1) * hb, n_halo - 1), 0)),
            pl.BlockSpec((CONV_K, ch), const2), pl.BlockSpec((1, ch), const2),
            pl.BlockSpec((1, ch), const2), pl.BlockSpec((1, ch), const2),
        ],
        out_specs=pl.BlockSpec((1, tm, ch), lambda bb, i: (bb, i, 0)),
        scratch_shapes=[pltpu.VMEM((tm + 2 * CONV_HALO, ch), F32),
                        pltpu.VMEM((SUBLANES - 1, tm + 2 * CONV_HALO - SUBLANES, ch), F32)],
        compiler_params=_cparams(("arbitrary", "arbitrary")),
        name="conformer_conv",
    )(gb, gb, gb, w, bias, ln_g, ln_b)


def _merge_kernel(oa_ref, ob_ref, oc_ref, x_ref, gate_ref, shift_ref, scale_ref, g2_ref, w_ref, rw_ref,
                  xo_ref, h2_ref, lg_ref):
    tm = x_ref.shape[1]
    wa = oa_ref.shape[2]
    wb = wa + ob_ref.shape[2]
    y = _dot(oa_ref[0], w_ref[0:wa, :]) + _dot(ob_ref[0], w_ref[wa:wb, :]) + _dot(oc_ref[0], w_ref[wb:, :])
    xn = x_ref[0] + gate_ref[0] * y
    xo_ref[0] = xn
    ms = jnp.mean(xn * xn, axis=-1, keepdims=True)
    h2 = xn * lax.rsqrt(ms + EPS) * g2_ref[...] * (1.0 + scale_ref[0]) + shift_ref[0]
    _store_row_tiles(h2_ref, h2, tm)
    lg_ref[...] = lax.dot_general(rw_ref[...], h2, (((1,), (1,)), ((), ())),
                                  preferred_element_type=F32, precision=lax.Precision.HIGHEST)


def _merge_route(oa, ob, oc, x, gate, shift, scale, g2, w_out, router_wt, tm):
    b, s, d = x.shape
    row = lambda bb, i: (bb, i, 0)
    per_b = lambda bb, i: (bb, 0, 0)
    const2 = lambda bb, i: (0, 0)
    nt = s // tm
    in_specs = [
        pl.BlockSpec((1, tm, oa.shape[2]), row), pl.BlockSpec((1, tm, ob.shape[2]), row),
        pl.BlockSpec((1, tm, oc.shape[2]), row), pl.BlockSpec((1, tm, d), row),
        pl.BlockSpec((1, 1, d), per_b), pl.BlockSpec((1, 1, d), per_b), pl.BlockSpec((1, 1, d), per_b),
        pl.BlockSpec((1, d), const2), pl.BlockSpec((d, d), const2),
        pl.BlockSpec((N_EXPERTS, d), const2),
    ]
    out_shape = (jax.ShapeDtypeStruct((b, s, d), F32),
                 jax.ShapeDtypeStruct((b * s * ROW_TILE, LANES), F32),
                 jax.ShapeDtypeStruct((N_EXPERTS, b * s), F32))
    out_specs = (pl.BlockSpec((1, tm, d), row),
                 pl.BlockSpec((tm * ROW_TILE, LANES), lambda bb, i: (bb * nt + i, 0)),
                 pl.BlockSpec((N_EXPERTS, tm), lambda bb, i: (0, bb * nt + i)))
    return pl.pallas_call(
        _merge_kernel, out_shape=out_shape, grid=(b, nt), in_specs=in_specs, out_specs=out_specs,
        compiler_params=_cparams(("arbitrary", "arbitrary")), name="merge_route",
    )(oa, ob, oc, x, gate, shift, scale, g2, w_out, router_wt)


def _merge_ffn_kernel(oa_ref, ob_ref, oc_ref, x_ref, gate1_ref, shift_ref, scale_ref, g2_ref, w_ref, gate2_ref,
                      wg_ref, wu_ref, wd_ref, o_ref, *, tf):
    wa = oa_ref.shape[2]
    wb = wa + ob_ref.shape[2]
    y = _dot(oa_ref[0], w_ref[0:wa, :]) + _dot(ob_ref[0], w_ref[wa:wb, :]) + _dot(oc_ref[0], w_ref[wb:, :])
    xn = x_ref[0] + gate1_ref[0] * y
    ms = jnp.mean(xn * xn, axis=-1, keepdims=True)
    h = (xn * lax.rsqrt(ms + EPS) * g2_ref[...] * (1.0 + scale_ref[0]) + shift_ref[0]).astype(BF16)
    ff = wg_ref.shape[1]
    acc = jnp.zeros(xn.shape, F32)
    for f in range(0, ff, tf):
        g = _dot(h, wg_ref[:, f:f + tf])
        u = _dot(h, wu_ref[:, f:f + tf])
        a = (g * _sigmoid(g) * u).astype(BF16)
        acc = acc + _dot(a, wd_ref[f:f + tf, :])
    o_ref[0] = xn + gate2_ref[0] * acc


def _merge_dense_ffn(oa, ob, oc, x, gate1, shift, scale, g2, w_out, gate2, wg, wu, wd, tm):
    b, s, d = x.shape
    ff = wg.shape[1]
    row = lambda bb, i: (bb, i, 0)
    per_b = lambda bb, i: (bb, 0, 0)
    const2 = lambda bb, i: (0, 0)
    resident = pl.Buffered(1)
    return pl.pallas_call(
        functools.partial(_merge_ffn_kernel, tf=MXU_DIM),
        out_shape=jax.ShapeDtypeStruct((b, s, d), F32),
        grid=(b, s // tm),
        in_specs=[
            pl.BlockSpec((1, tm, oa.shape[2]), row), pl.BlockSpec((1, tm, ob.shape[2]), row),
            pl.BlockSpec((1, tm, oc.shape[2]), row), pl.BlockSpec((1, tm, d), row),
            pl.BlockSpec((1, 1, d), per_b), pl.BlockSpec((1, 1, d), per_b), pl.BlockSpec((1, 1, d), per_b),
            pl.BlockSpec((1, d), const2), pl.BlockSpec((d, d), const2, pipeline_mode=resident),
            pl.BlockSpec((1, 1, d), per_b),
            pl.BlockSpec((d, ff), const2, pipeline_mode=resident),
            pl.BlockSpec((d, ff), const2, pipeline_mode=resident),
            pl.BlockSpec((ff, d), const2, pipeline_mode=resident),
        ],
        out_specs=pl.BlockSpec((1, tm, d), row),
        compiler_params=_cparams(("arbitrary", "arbitrary")), name="merge_dense_ffn",
    )(oa, ob, oc, x, gate1, shift, scale, g2, w_out, gate2, wg, wu, wd)


def _top2(lg):
    sub = lax.broadcasted_iota(I32, lg.shape, 0)
    l1 = jnp.max(lg, axis=0, keepdims=True)
    i1 = jnp.min(jnp.where(lg == l1, sub, N_EXPERTS), axis=0, keepdims=True)
    m1 = sub == i1
    lg2 = jnp.where(m1, -jnp.inf, lg)
    l2 = jnp.max(lg2, axis=0, keepdims=True)
    i2 = jnp.min(jnp.where(lg2 == l2, sub, N_EXPERTS), axis=0, keepdims=True)
    m2 = sub == i2
    return l1, l2, m1, m2


def _sublane_cumsum(x):
    sub = lax.broadcasted_iota(I32, x.shape, 0)
    for sh in (1, 2, 4):
        x = x + jnp.where(sub >= sh, pltpu.roll(x, sh, 0), 0.0)
    return x


def _route_kernel(lg_ref, tri_ref, dest_ref, gates_ref, be_ref, pad_ref, base_ref, start_ref, *, block_rows, total_rows):
    phase = pl.program_id(0)
    j = pl.program_id(1)
    l1, l2, m1, m2 = _top2(lg_ref[...])
    e = jnp.where(m1 | m2, 1.0, 0.0).astype(F32)
    cnt = jnp.sum(e, axis=1, keepdims=True)

    @pl.when((phase == 0) & (j == 0))
    def _():
        base_ref[...] = jnp.zeros(base_ref.shape, F32)

    @pl.when((phase == 1) & (j == 0))
    def _():
        counts = base_ref[...]
        nblk = jnp.floor((counts + (block_rows - 1)) * (1.0 / block_rows))
        end_blk = _sublane_cumsum(nblk)
        start_ref[...] = (end_blk - nblk) * block_rows
        blk = lax.broadcasted_iota(I32, be_ref.shape, 1).astype(F32)
        owner = jnp.sum(jnp.where(end_blk[:, :1] <= blk, 1.0, 0.0), axis=0, keepdims=True)
        be_ref[...] = jnp.broadcast_to(jnp.minimum(owner, N_EXPERTS - 1.0), be_ref.shape).astype(I32)
        sub = lax.broadcasted_iota(I32, pad_ref.shape, 0)
        lane = lax.broadcasted_iota(I32, pad_ref.shape, 1)
        pad_end = jnp.where(sub == N_EXPERTS - 1, float(total_rows), end_blk * block_rows)
        pad_ref[...] = jnp.where(lane < LANES // 2, start_ref[...] + counts, pad_end).astype(I32)
        base_ref[...] = jnp.zeros(base_ref.shape, F32)

    @pl.when(phase == 1)
    def _():
        prefix = _dot(e.astype(BF16), tri_ref[...]) + base_ref[:, :1] + start_ref[:, :1]
        d1 = jnp.sum(jnp.where(m1, prefix, 0.0), axis=0, keepdims=True)
        d2 = jnp.sum(jnp.where(m2, prefix, 0.0), axis=0, keepdims=True)
        sub = lax.broadcasted_iota(I32, dest_ref.shape, 0)
        dest_ref[...] = jnp.where(sub == 0, d1, jnp.where(sub == 1, d2, 0.0)).astype(I32)
        ex = jnp.exp(l2 - l1)
        g1 = 1.0 / (1.0 + ex)
        g2 = ex / (1.0 + ex)
        half = lax.broadcasted_iota(I32, (LANES, lg_ref.shape[1]), 0) < LANES // 2
        gates_ref[...] = jnp.where(half, g1, g2).T

    base_ref[...] = base_ref[...] + cnt


def _route(logits_t, block_rows, n_blocks, tr):
    n = logits_t.shape[1]
    n_blocks_pad = -(-n_blocks // LANES) * LANES
    tri = jnp.asarray(np.triu(np.ones((tr, tr), np.float32), k=1), BF16)
    return pl.pallas_call(
        functools.partial(_route_kernel, block_rows=block_rows, total_rows=n_blocks * block_rows),
        out_shape=(jax.ShapeDtypeStruct((N_EXPERTS, n), I32),
                   jax.ShapeDtypeStruct((n, LANES), F32),
                   jax.ShapeDtypeStruct((N_EXPERTS, n_blocks_pad), I32),
                   jax.ShapeDtypeStruct((N_EXPERTS, LANES), I32)),
        grid=(2, n // tr),
        in_specs=[pl.BlockSpec((N_EXPERTS, tr), lambda p, j: (0, j)),
                  pl.BlockSpec((tr, tr), lambda p, j: (0, 0))],
        out_specs=(pl.BlockSpec((N_EXPERTS, tr), lambda p, j: (0, j * p)),
                   pl.BlockSpec((tr, LANES), lambda p, j: (j * p, 0)),
                   pl.BlockSpec((N_EXPERTS, n_blocks_pad), lambda p, j: (0, 0)),
                   pl.BlockSpec((N_EXPERTS, LANES), lambda p, j: (0, 0))),
        scratch_shapes=[pltpu.VMEM((N_EXPERTS, LANES), F32), pltpu.VMEM((N_EXPERTS, LANES), F32)],
        compiler_params=_cparams(("arbitrary", "arbitrary")), name="moe_route",
    )(logits_t, tri)


def _row_copy(src_hbm, src_row, dst_hbm, dst_row, sem):
    src = pl.ds(pl.multiple_of(src_row * ROW_TILE, ROW_TILE), ROW_TILE)
    dst = pl.ds(pl.multiple_of(dst_row * ROW_TILE, ROW_TILE), ROW_TILE)
    return pltpu.make_async_copy(src_hbm.at[src], dst_hbm.at[dst], sem)


def _scatter_kernel(pad_lo_ref, pad_hi_ref, d1_ref, d2_ref, src_ref, out_hbm, zero_ref, sem, zero_sem, *, rows):
    @pl.when(pl.program_id(0) == 0)
    def _():
        zero_ref[...] = jnp.zeros(zero_ref.shape, zero_ref.dtype)
        for e in range(N_EXPERTS):
            def fill(row, c):
                _row_copy(zero_ref, 0, out_hbm, row, zero_sem).start()
                return c

            def drain(row, c):
                _row_copy(zero_ref, 0, out_hbm, 0, zero_sem).wait()
                return c

            lax.fori_loop(pad_lo_ref[e], pad_hi_ref[e], fill, 0)
            lax.fori_loop(pad_lo_ref[e], pad_hi_ref[e], drain, 0)

    def start(r, c):
        _row_copy(src_ref, r, out_hbm, d1_ref[0, 0, r], sem).start(priority=0)
        _row_copy(src_ref, r, out_hbm, d2_ref[0, 0, r], sem).start(priority=1)
        return c

    lax.fori_loop(0, rows, start, 0, unroll=DMA_UNROLL)
    for _ in range(2):
        pltpu.make_async_copy(src_ref, out_hbm.at[pl.ds(0, rows * ROW_TILE)], sem).wait()


def _scatter_rows(src, d1, d2, pad_lo, pad_hi, total_rows, rows):
    n = d1.shape[0]
    idx_spec = pl.BlockSpec((1, 1, rows), lambda i, lo, hi: (i, 0, 0), memory_space=pltpu.SMEM)
    any_spec = pl.BlockSpec(memory_space=pl.ANY)
    grid_spec = pltpu.PrefetchScalarGridSpec(
        num_scalar_prefetch=2,
        grid=(n // rows,),
        in_specs=[idx_spec, idx_spec, pl.BlockSpec((rows * ROW_TILE, LANES), lambda i, lo, hi: (i, 0))],
        out_specs=any_spec,
        scratch_shapes=[pltpu.VMEM((ROW_TILE, LANES), src.dtype), pltpu.SemaphoreType.DMA(()),
                        pltpu.SemaphoreType.DMA(())],
    )
    return pl.pallas_call(
        functools.partial(_scatter_kernel, rows=rows),
        out_shape=jax.ShapeDtypeStruct((total_rows * ROW_TILE, LANES), src.dtype),
        grid_spec=grid_spec,
        compiler_params=pltpu.CompilerParams(dimension_semantics=("arbitrary",), has_side_effects=True),
        name="moe_scatter_rows",
    )(pad_lo, pad_hi, d1.reshape(n // rows, 1, rows), d2.reshape(n // rows, 1, rows), src)


def _expert_kernel(be_ref, x_ref, wg_ref, wu_ref, wd_ref, o_ref, *, block_rows, tf):
    del be_ref
    x = _load_row_tiles(x_ref, block_rows).astype(BF16)
    ff = wg_ref.shape[2]
    acc = jnp.zeros((block_rows, wd_ref.shape[2]), F32)
    for f in range(0, ff, tf):
        g = _dot(x, wg_ref[0, :, f:f + tf])
        u = _dot(x, wu_ref[0, :, f:f + tf])
        a = (g * _sigmoid(g) * u).astype(BF16)
        acc = acc + _dot(a, wd_ref[0, f:f + tf, :])
    _store_row_tiles(o_ref, acc, block_rows)


def _expert_ffn(xb, blk_expert, wg, wu, wd, block_rows):
    d, ff = wg.shape[1], wg.shape[2]
    rows = xb.shape[0] // ROW_TILE
    resident = pl.Buffered(1)
    grid_spec = pltpu.PrefetchScalarGridSpec(
        num_scalar_prefetch=1,
        grid=(rows // block_rows,),
        in_specs=[
            pl.BlockSpec((block_rows * ROW_TILE, LANES), lambda i, be: (i, 0)),
            pl.BlockSpec((1, d, ff), lambda i, be: (be[i], 0, 0), pipeline_mode=resident),
            pl.BlockSpec((1, d, ff), lambda i, be: (be[i], 0, 0), pipeline_mode=resident),
            pl.BlockSpec((1, ff, d), lambda i, be: (be[i], 0, 0), pipeline_mode=resident),
        ],
        out_specs=pl.BlockSpec((block_rows * ROW_TILE, LANES), lambda i, be: (i, 0)),
    )
    return pl.pallas_call(
        functools.partial(_expert_kernel, block_rows=block_rows, tf=MXU_DIM),
        out_shape=jax.ShapeDtypeStruct(xb.shape, F32), grid_spec=grid_spec,
        compiler_params=_cparams(("arbitrary",)), name="moe_expert_ffn",
    )(blk_expert, xb, wg, wu, wd)


def _combine_kernel(d1_ref, d2_ref, d1n_ref, d2n_ref, x_ref, yb_hbm, gates_ref, gate_ref, fg_ref, o_ref,
                    y1_ref, y2_ref, sems):
    tm = x_ref.shape[0]
    i = pl.program_id(0)
    slot = i % 2

    def gather(i1_ref, i2_ref, to_slot):
        def start(r, c):
            _row_copy(yb_hbm, i1_ref[0, 0, r], y1_ref.at[to_slot], r, sems.at[to_slot]).start(priority=0)
            _row_copy(yb_hbm, i2_ref[0, 0, r], y2_ref.at[to_slot], r, sems.at[to_slot]).start(priority=1)
            return c

        lax.fori_loop(0, tm, start, 0, unroll=DMA_UNROLL)

    @pl.when(i == 0)
    def _():
        gather(d1_ref, d2_ref, slot)

    @pl.when(i + 1 < pl.num_programs(0))
    def _():
        gather(d1n_ref, d2n_ref, 1 - slot)

    for y_ref in (y1_ref, y2_ref):
        pltpu.make_async_copy(yb_hbm.at[pl.ds(0, tm * ROW_TILE)], y_ref.at[slot], sems.at[slot]).wait()

    gts = gates_ref[...]
    y = (gts[:, 0:1] * _load_row_tiles(y1_ref.at[slot], tm)
         + gts[:, LANES // 2:LANES // 2 + 1] * _load_row_tiles(y2_ref.at[slot], tm))
    xn = x_ref[...] + gate_ref[0] * y
    ms = jnp.mean(xn * xn, axis=-1, keepdims=True)
    o_ref[...] = xn * lax.rsqrt(ms + EPS) * fg_ref[...]


def _combine_final(x, yb, d1, d2, gates, gate, final_g, tm):
    n, d = x.shape
    s = n // gate.shape[0]
    row = lambda i: (i, 0)
    steps = n // tm
    idx_spec = pl.BlockSpec((1, 1, tm), lambda i: (i, 0, 0), memory_space=pltpu.SMEM)
    next_spec = pl.BlockSpec((1, 1, tm), lambda i: (jnp.minimum(i + 1, steps - 1), 0, 0), memory_space=pltpu.SMEM)
    d1 = d1.reshape(steps, 1, tm)
    d2 = d2.reshape(steps, 1, tm)
    return pl.pallas_call(
        _combine_kernel,
        out_shape=jax.ShapeDtypeStruct((n, d), F32),
        grid=(steps,),
        in_specs=[idx_spec, idx_spec, next_spec, next_spec,
                  pl.BlockSpec((tm, d), row),
                  pl.BlockSpec(memory_space=pl.ANY),
                  pl.BlockSpec((tm, LANES), row),
                  pl.BlockSpec((1, 1, d), lambda i: ((i * tm) // s, 0, 0)),
                  pl.BlockSpec((1, d), lambda i: (0, 0))],
        out_specs=pl.BlockSpec((tm, d), row),
        scratch_shapes=[pltpu.VMEM((2, tm * ROW_TILE, LANES), F32), pltpu.VMEM((2, tm * ROW_TILE, LANES), F32),
                        pltpu.SemaphoreType.DMA((2,))],
        compiler_params=_cparams(("arbitrary",)), name="moe_combine_final",
    )(d1, d2, d1, d2, x, yb, gates, gate, final_g)


AMAX_QA, AMAX_KA, AMAX_VA, AMAX_QC, AMAX_KC, AMAX_VC = range(6)
FP8_TARGET_MAX = 256.0


def _fp8_scales(q_max, k_max, v_max, head_dim):
    tiny = jnp.finfo(F32).tiny
    c = head_dim ** -0.5 * LOG2E
    ratio = jnp.where((q_max > 0) & (k_max > 0), c * k_max / jnp.maximum(q_max, tiny), 1.0)
    sq = jnp.exp2(jnp.round(0.5 * jnp.log2(ratio)))
    sk = c / sq
    sv = jnp.where(v_max > 0, jnp.exp2(jnp.floor(jnp.log2(FP8_TARGET_MAX / jnp.maximum(v_max, tiny)))), 1.0)
    rows = jnp.stack([sq, sk, sv, 1.0 / sv] + [jnp.zeros_like(sq)] * (SUBLANES - 4), axis=1)
    return jnp.broadcast_to(rows[:, :, None], rows.shape + (LANES,)).astype(F32)


def _rope_tables(s, dim):
    half = dim // 2
    t = jnp.arange(s)
    inv = 1.0 / (ROPE_THETA ** (jnp.arange(0, half, 2, dtype=F32) / half))
    ang_r = (t // GRID_W).astype(F32)[:, None] * inv
    ang_c = (t % GRID_W).astype(F32)[:, None] * inv
    ang = jnp.concatenate([ang_r, ang_r, ang_c, ang_c], axis=-1)
    reps = LANES // dim
    return jnp.tile(jnp.cos(ang), (1, reps)), jnp.tile(jnp.sin(ang), (1, reps))


def _rotate_matrix(dim):
    q = dim // 4
    p = np.zeros((MXU_DIM, MXU_DIM), np.float32)
    for j in range(MXU_DIM):
        if (j % (2 * q)) < q:
            p[j + q, j] = -1.0
        else:
            p[j - q, j] = 1.0
    return jnp.asarray(p, BF16)


def _head_mean_matrix():
    m = np.kron(np.eye(MXU_DIM // HEAD_V, dtype=np.float32), np.full((HEAD_V, HEAD_V), 1.0 / HEAD_V, np.float32))
    return jnp.asarray(m, BF16)


def _widen_values(w, heads):
    d = w.shape[0]
    w = w.reshape(d, heads, HEAD_V)
    return jnp.concatenate([w, jnp.zeros_like(w)], axis=-1).reshape(d, heads * LANES)


def _widen_in_proj(w):
    qa, ka, va, gb, qc, kc, vc = jnp.split(w, [256, 512, 768, 1280, 1792, 1920], axis=1)
    return jnp.concatenate([qa, ka, _widen_values(va, DIFF_HEADS), gb, qc, kc, _widen_values(vc, GQA_KV)],
                           axis=1).astype(BF16)


def kernel(x, c, ctx, c_ctx, ada_w, ada_b, norm1_g, norm2_g, w_in, w_out, lam_q1, lam_k1, lam_q2, lam_k2,
           diff_subln_g, conv_w, conv_b, conv_ln_g, conv_ln_b, q_norm_g, k_norm_g, ffn_gate, ffn_up, ffn_down,
           router_w, moe_gate, moe_up, moe_down, final_g):
    b, s, d = x.shape
    sc = ctx.shape[1]
    depth = ada_w.shape[0]
    n = b * s
    assert depth % 2 == 0, "the final RMSNorm is fused into the MoE combine of the last (odd) layer"

    tm = min(512, s)
    tmc = min(512, sc)
    tq = min(512, s)
    tq_diff = min(1024, s)
    tqc = min(256, sc)

    tabs_x = _rope_tables(s, DIFF_QK) + _rope_tables(s, HEAD_V)
    ones_c, zeros_c = jnp.ones((sc, LANES), F32), jnp.zeros((sc, LANES), F32)
    tabs_c = (ones_c, zeros_c, ones_c, zeros_c)
    mats = (_rotate_matrix(DIFF_QK), _rotate_matrix(HEAD_V), _head_mean_matrix())

    cc = jnp.zeros((16, d), F32).at[:b].set(c).at[b].set(c_ctx)

    for i in range(depth):
        last = i == depth - 1
        lam_init = 0.8 - 0.6 * math.exp(-0.3 * i)
        mod_all = _ada_mod(cc, ada_w[i], ada_b[i])
        mod = mod_all[:b].reshape(b, 6, 1, d)
        modc = jnp.broadcast_to(mod_all[b].reshape(1, 6, 1, d), (b, 6, 1, d))

        w_aug = _widen_in_proj(w_in[i])
        g1 = norm1_g[i].reshape(1, d)
        qg = jnp.tile(q_norm_g[i], GQA_HEADS).reshape(1, -1)
        kg = jnp.tile(k_norm_g[i], GQA_KV).reshape(1, -1)
        lam_vecs = jnp.stack([lam_q1[i], lam_k1[i], lam_q2[i], lam_k2[i]]).astype(F32)
        subln = diff_subln_g[i].reshape(1, HEAD_V)
        conv_args = (conv_w[i], conv_b[i].reshape(1, -1), conv_ln_g[i].reshape(1, -1), conv_ln_b[i].reshape(1, -1))
        w_o = w_out[i].astype(BF16)
        g2 = norm2_g[i].reshape(1, d)

        qa, kat, va, gb, qc, kct, vc, amax = _in_projection(
            x, mod[:, 0], mod[:, 1], g1, w_aug, tabs_x, mats, qg, kg, tm, 2 if s % (2 * tm) == 0 else 1)
        qa_x, kat_x, va_x, gb_x, qc_x, kct_x, vc_x, amax_x = _in_projection(
            ctx, modc[:, 0], modc[:, 1], g1, w_aug, tabs_c, mats, qg, kg, tmc, 1)
        amax = jnp.max(amax, axis=(1, 3))
        amax_x = jnp.max(amax_x, axis=(1, 3))
        amax_kv = jnp.maximum(amax, amax_x)
        scl_a = _fp8_scales(amax[:, AMAX_QA], amax_kv[:, AMAX_KA], amax_kv[:, AMAX_VA], DIFF_QK)
        scl_c = _fp8_scales(amax[:, AMAX_QC], amax_kv[:, AMAX_KC], amax_kv[:, AMAX_VC], HEAD_V)

        oa = _diff_attention(qa, [kat, kat_x], [va, va_x], scl_a, lam_vecs, subln, lam_init, tq_diff)
        ob = _conformer_conv(gb, *conv_args, tm)
        oc = _gqa_attention(qc, [kct, kct_x], [vc, vc_x], scl_c, tq)

        j = i // 2
        if i % 2 == 0:
            wg, wu, wd = ffn_gate[j].astype(BF16), ffn_up[j].astype(BF16), ffn_down[j].astype(BF16)
            x = _merge_dense_ffn(oa, ob, oc, x, mod[:, 2], mod[:, 3], mod[:, 4], g2, w_o, mod[:, 5], wg, wu, wd, tm)
        else:
            rwt = router_w[j].T.astype(F32)
            x, h2, logits_t = _merge_route(oa, ob, oc, x, mod[:, 2], mod[:, 3], mod[:, 4], g2, w_o, rwt, tm)
            block_rows = 512 if n >= 8192 else 256
            n_blocks = (2 * n) // block_rows + N_EXPERTS
            dest, gates, blk_e, pad = _route(logits_t, block_rows, n_blocks, min(512, n))
            xb = _scatter_rows(h2, dest[0], dest[1], pad[:, 0], pad[:, LANES // 2], n_blocks * block_rows,
                               min(1024, n))
            yb = _expert_ffn(xb, blk_e[0, :n_blocks], moe_gate[j].astype(BF16), moe_up[j].astype(BF16),
                             moe_down[j].astype(BF16), block_rows)
            assert last
            x = _combine_final(x.reshape(n, d), yb, dest[0], dest[1], gates, mod[:, 5], final_g.reshape(1, d),
                               tm).reshape(b, s, d)

        if not last:
            scl_ax = _fp8_scales(amax_x[:, AMAX_QA], amax_x[:, AMAX_KA], amax_x[:, AMAX_VA], DIFF_QK)
            scl_cx = _fp8_scales(amax_x[:, AMAX_QC], amax_x[:, AMAX_KC], amax_x[:, AMAX_VC], HEAD_V)
            oa_x = _diff_attention(qa_x, [kat_x], [va_x], scl_ax, lam_vecs, subln, lam_init, tqc)
            ob_x = _conformer_conv(gb_x, *conv_args, tmc)
            oc_x = _gqa_attention(qc_x, [kct_x], [vc_x], scl_cx, tqc)
            assert i % 2 == 0, "context tokens only ever pass through dense channel mixers"
            ctx = _merge_dense_ffn(oa_x, ob_x, oc_x, ctx, modc[:, 2], modc[:, 3], modc[:, 4], g2, w_o, modc[:, 5],
                                   wg, wu, wd, tmc)

    return x
```

```python
import functools
import math

import numpy as np
import jax
import jax.numpy as jnp
from jax import lax
from jax.experimental import pallas as pl
from jax.experimental.pallas import tpu as pltpu

F32 = jnp.float32
BF16 = jnp.bfloat16
I32 = jnp.int32

EPS = 1e-6
ROPE_THETA = 10000.0
GRID_W = 64

DIFF_HEADS = 4
DIFF_QK = 32
HEAD_V = 64
GQA_HEADS = 8
GQA_KV = 2
GQA_GROUP = GQA_HEADS // GQA_KV
CONV_K = 31
N_EXPERTS = 8
LOG2E = math.log2(math.e)

LANES = 128
SUBLANES = 8
MXU_DIM = 256
VMEM_LIMIT = 52 * 1024 * 1024
NEG_BIG = -1e30
V_DTYPE = jnp.float8_e4m3fn
QK_DTYPE = jnp.float8_e4m3fn
P_E4M3 = (jnp.float8_e4m3fn, 8.5)
P_E5M2 = (jnp.float8_e5m2, 15.5)
P_HEADROOM_RUNNING = 8.0
ATTN_UNROLL = 8
DMA_UNROLL = 8

C_QA, C_KA, C_VA, C_GB, C_QC, C_KC, C_VC, C_END = 0, 256, 512, 1024, 1536, 2048, 2176, 2432


def _cparams(semantics):
    return pltpu.CompilerParams(dimension_semantics=semantics, vmem_limit_bytes=VMEM_LIMIT)


def _dot(a, b):
    return jnp.dot(a, b, preferred_element_type=F32)


def _sigmoid(z):
    return 1.0 / (1.0 + jnp.exp(-z))


ROW_TILE = 8


def _store_row_tiles(ref, val, rows):
    for a in range(ROW_TILE):
        ref[pl.ds(a, rows, stride=ROW_TILE), :] = val[:, a * LANES:(a + 1) * LANES]


def _load_row_tiles(ref, rows):
    return jnp.concatenate([ref[pl.ds(a, rows, stride=ROW_TILE), :] for a in range(ROW_TILE)], axis=1)


def _mod_kernel(c_ref, w_ref, b_ref, o_ref):
    c = c_ref[...]
    s = c * _sigmoid(c)
    o_ref[...] = jnp.dot(s, w_ref[...], preferred_element_type=F32, precision=lax.Precision.HIGHEST) + b_ref[...]


def _ada_mod(cc, w, b):
    rows, d = cc.shape
    n = w.shape[1]
    tn = d
    return pl.pallas_call(
        _mod_kernel,
        out_shape=jax.ShapeDtypeStruct((rows, n), F32),
        grid=(n // tn,),
        in_specs=[pl.BlockSpec((rows, d), lambda j: (0, 0)),
                  pl.BlockSpec((d, tn), lambda j: (0, j)),
                  pl.BlockSpec((1, tn), lambda j: (0, j))],
        out_specs=pl.BlockSpec((rows, tn), lambda j: (0, j)),
        compiler_params=_cparams(("arbitrary",)),
        name="ada_mod",
    )(cc, w, b.reshape(1, n))


def _inproj_kernel(x_ref, shift_ref, scale_ref, g_ref, w_ref, cosa_ref, sina_ref, cosc_ref, sinc_ref,
                   pa_ref, pc_ref, hm_ref, qg_ref, kg_ref,
                   qa_o, kat_o, va_o, gb_o, qc_o, kct_o, vc_o, amax_o):
    n_sub, sub = kat_o.shape[1], kat_o.shape[3]
    stats = None
    for t in range(n_sub):
        rows = slice(t * sub, (t + 1) * sub)
        tabs = [ref[rows, :] for ref in (cosa_ref, sina_ref, cosc_ref, sinc_ref)]
        outs = [o.at[0, rows, :] for o in (qa_o, va_o, gb_o, qc_o, vc_o)] + [kat_o.at[0, t], kct_o.at[0, t]]
        st = _inproj_subtile(x_ref[0, rows, :], shift_ref, scale_ref, g_ref, w_ref, tabs, pa_ref, pc_ref, hm_ref,
                             qg_ref, kg_ref, outs, amax_o.shape[3])
        stats = st if stats is None else jnp.maximum(stats, st)
    amax_o[0, 0] = stats


def _inproj_subtile(x, shift_ref, scale_ref, g_ref, w_ref, tabs, pa_ref, pc_ref, hm_ref, qg_ref, kg_ref, outs, stat_w):
    qa_o, va_o, gb_o, qc_o, vc_o, kat_o, kct_o = outs
    cosa, sina, cosc, sinc = tabs
    ms = jnp.mean(x * x, axis=-1, keepdims=True)
    h = x * lax.rsqrt(ms + EPS) * g_ref[...]
    h = h * (1.0 + scale_ref[0]) + shift_ref[0]
    hb = h.astype(BF16)

    def proj(lo, hi):
        return _dot(hb, w_ref[:, lo:hi])

    def blockmat(y, m_ref):
        yb = y.astype(BF16)
        w = y.shape[1]
        if w == LANES:
            return _dot(yb, m_ref[:LANES, :LANES])
        return jnp.concatenate([_dot(yb[:, c:c + MXU_DIM], m_ref[...]) for c in range(0, w, MXU_DIM)], axis=1)

    def rope(y, cos, sin, p_ref):
        reps = y.shape[1] // LANES
        cos = jnp.tile(cos, (1, reps))
        sin = jnp.tile(sin, (1, reps))
        return y * cos + blockmat(y, p_ref) * sin

    def ones_col(width):
        lane = lax.broadcasted_iota(I32, (1, width), 1)
        return jnp.where(lane % LANES >= HEAD_V, 1.0, 0.0).astype(F32)

    def col_amax(y):
        cm = jnp.max(jnp.abs(y), axis=0, keepdims=True)
        pad = stat_w - y.shape[1]
        return cm if pad == 0 else jnp.concatenate([cm, jnp.zeros((1, pad), F32)], axis=1)

    qa = rope(proj(C_QA, C_KA), cosa, sina, pa_ref)
    qa_o[...] = qa.astype(BF16)
    ka = rope(proj(C_KA, C_VA), cosa, sina, pa_ref)
    kat_o[...] = ka.T.astype(BF16)
    va = proj(C_VA, C_GB)
    va_o[...] = (va + ones_col(C_GB - C_VA)).astype(BF16)
    gb_o[...] = proj(C_GB, C_QC).astype(BF16)

    y = proj(C_QC, C_KC)
    yn = y * lax.rsqrt(blockmat(y * y, hm_ref) + EPS) * qg_ref[...]
    qc = rope(yn, cosc, sinc, pc_ref)
    qc_o[...] = qc.astype(BF16)

    y = proj(C_KC, C_VC)
    yn = y * lax.rsqrt(blockmat(y * y, hm_ref) + EPS) * kg_ref[...]
    kc = rope(yn, cosc, sinc, pc_ref)
    kct_o[...] = kc.T.astype(BF16)

    vc = proj(C_VC, C_END)
    vc_o[...] = (vc + ones_col(C_END - C_VC)).astype(BF16)

    stats = [col_amax(t) for t in (qa, ka, va, qc, kc, vc)]
    stats += [jnp.zeros((1, stat_w), F32)] * (SUBLANES - len(stats))
    return jnp.concatenate(stats, axis=0)


def _in_projection(x, shift, scale, g1, w_aug, tabs, mats, qg, kg, tk, n_sub):
    b, s, d = x.shape
    tm = tk * n_sub
    nt = s // tm
    cosa, sina, cosc, sinc = tabs
    pa, pc, hm = mats
    row = lambda bb, i: (bb, i, 0)
    const2 = lambda bb, i: (0, 0)
    per_b = lambda bb, i: (bb, 0, 0)
    tab = lambda bb, i: (i, 0)
    out_shape = (
        jax.ShapeDtypeStruct((b, s, 256), BF16),
        jax.ShapeDtypeStruct((b, s // tk, 256, tk), BF16),
        jax.ShapeDtypeStruct((b, s, 512), BF16),
        jax.ShapeDtypeStruct((b, s, 512), BF16),
        jax.ShapeDtypeStruct((b, s, 512), BF16),
        jax.ShapeDtypeStruct((b, s // tk, 128, tk), BF16),
        jax.ShapeDtypeStruct((b, s, 256), BF16),
        jax.ShapeDtypeStruct((b, nt, SUBLANES, 512), F32),
    )
    out_specs = (
        pl.BlockSpec((1, tm, 256), row),
        pl.BlockSpec((1, n_sub, 256, tk), lambda bb, i: (bb, i, 0, 0)),
        pl.BlockSpec((1, tm, 512), row),
        pl.BlockSpec((1, tm, 512), row),
        pl.BlockSpec((1, tm, 512), row),
        pl.BlockSpec((1, n_sub, 128, tk), lambda bb, i: (bb, i, 0, 0)),
        pl.BlockSpec((1, tm, 256), row),
        pl.BlockSpec((1, 1, SUBLANES, 512), lambda bb, i: (bb, i, 0, 0)),
    )
    in_specs = [
        pl.BlockSpec((1, tm, d), row),
        pl.BlockSpec((1, 1, d), per_b),
        pl.BlockSpec((1, 1, d), per_b),
        pl.BlockSpec((1, d), const2),
        pl.BlockSpec((d, C_END), const2),
        pl.BlockSpec((tm, LANES), tab), pl.BlockSpec((tm, LANES), tab),
        pl.BlockSpec((tm, LANES), tab), pl.BlockSpec((tm, LANES), tab),
        pl.BlockSpec((MXU_DIM, MXU_DIM), const2), pl.BlockSpec((MXU_DIM, MXU_DIM), const2),
        pl.BlockSpec((MXU_DIM, MXU_DIM), const2),
        pl.BlockSpec((1, 512), const2), pl.BlockSpec((1, 128), const2),
    ]
    return pl.pallas_call(
        _inproj_kernel, out_shape=out_shape, grid=(b, nt), in_specs=in_specs, out_specs=out_specs,
        compiler_params=_cparams(("arbitrary", "arbitrary")), name="in_projection",
    )(x, shift, scale, g1, w_aug, cosa, sina, cosc, sinc, pa, pc, hm, qg, kg)


SCL_Q, SCL_K, SCL_V, SCL_V_INV = 0, 1, 2, 3


LO_GAIN = 16.0
V_ROW_CHUNK = 512


def _split_fp8(x, dtype):
    hi = x.astype(dtype)
    return hi, ((x - hi.astype(F32)) * LO_GAIN).astype(dtype)


def _stack_qk(x, other_side, dtype, axis, width):
    hi, lo = _split_fp8(x, dtype)
    hi_small = (hi.astype(F32) * (1.0 / LO_GAIN)).astype(dtype)
    parts = [hi, lo, hi_small] if other_side else [hi, hi_small, lo]
    pad = width - 3 * x.shape[axis]
    if pad:
        pad_shape = tuple(pad if a == axis else n for a, n in enumerate(x.shape))
        parts.append(jnp.zeros(pad_shape, dtype))
    return jnp.concatenate(parts, axis=axis)


def _quantize_kv(k_refs, v_refs, k8_refs, v8_refs, scl_ref, dk):
    sk = scl_ref[0, SCL_K:SCL_K + 1, 0:1]
    sv = scl_ref[0, SCL_V:SCL_V + 1, 0:1]
    for k_ref, k8_ref in zip(k_refs, k8_refs):
        n_maps = k_ref.shape[2] // dk
        stack = k8_ref.shape[1] // n_maps

        def k_body(c, carry, k_ref=k_ref, k8_ref=k8_ref, n_maps=n_maps, stack=stack):
            kf = k_ref[0, c].astype(F32) * sk
            k8_ref[c] = jnp.concatenate(
                [_stack_qk(kf[dk * j:dk * (j + 1), :], True, k8_ref.dtype, 0, stack) for j in range(n_maps)], axis=0)
            return carry

        lax.fori_loop(0, k_ref.shape[1], k_body, 0)

    for v_ref, v8_ref in zip(v_refs, v8_refs):
        rows = min(V_ROW_CHUNK, v_ref.shape[1])

        def v_body(c, carry, v_ref=v_ref, v8_ref=v8_ref, rows=rows):
            sl = pl.ds(pl.multiple_of(c * rows, rows), rows)
            vf = v_ref[0, sl, :].astype(F32)
            groups = []
            for g in range(v_ref.shape[2] // LANES):
                vs = vf[:, LANES * g:LANES * g + HEAD_V] * sv
                hi = vs.astype(v8_ref.dtype)
                lo = (vs - hi.astype(F32)).astype(v8_ref.dtype)
                ones = vf[:, LANES * g + HEAD_V:LANES * (g + 1)].astype(v8_ref.dtype)
                groups += [hi, ones, lo, jnp.zeros((rows, HEAD_V), v8_ref.dtype)]
            v8_ref[sl, :] = jnp.concatenate(groups, axis=1)
            return carry

        lax.fori_loop(0, v_ref.shape[1] // rows, v_body, 0)


def _attention_sweeps(qms, k_slices, pv_groups, k_refs, v_refs, m_ref, smax_ref, acc_ref, p_format):
    r = qms[0].shape[0]
    p_dtype, p_max_exp = p_format

    def scores(j, kc):
        return _dot(qms[j], kc[k_slices[j], :])

    def sweep(running_max):
        acc_ref[...] = jnp.zeros(acc_ref.shape, F32)
        if running_max:
            m_ref[...] = jnp.full(m_ref.shape, NEG_BIG, F32)
        else:
            smax_ref[...] = jnp.full(smax_ref.shape, NEG_BIG, smax_ref.dtype)
            kc0 = k_refs[0][0][:, :MXU_DIM]
            for j in range(len(qms)):
                m0 = jnp.max(scores(j, kc0), axis=-1, keepdims=True)
                m_ref[j * r:(j + 1) * r, :] = jnp.broadcast_to(m0, (r, LANES))

        for k_ref, v_ref in zip(k_refs, v_refs):
            n_chunks, tk = k_ref.shape[0], k_ref.shape[2]

            def body(c, carry, k_ref=k_ref, v_ref=v_ref, tk=tk):
                kc = k_ref[c]
                vc = v_ref[pl.ds(pl.multiple_of(c * tk, tk), tk), :]
                for ids, v_lanes in pv_groups:
                    ps, alphas = [], []
                    for j in ids:
                        rows = slice(j * r, (j + 1) * r)
                        s = scores(j, kc)
                        m = m_ref[rows, :]
                        if running_max:
                            m_new = jnp.maximum(m, jnp.max(s, axis=-1, keepdims=True) - P_HEADROOM_RUNNING)
                            m_ref[rows, :] = m_new
                            alphas.append(jnp.exp2(m - m_new))
                            m = m_new
                        d = (s - jnp.tile(m, (1, tk // LANES))).astype(BF16)
                        if not running_max:
                            cm = functools.reduce(jnp.maximum, [d[:, l:l + LANES] for l in range(0, tk, LANES)])
                            smax_ref[rows, :] = jnp.maximum(smax_ref[rows, :], cm)
                        ps.append(jnp.exp2(d).astype(p_dtype))
                    rows = slice(ids[0] * r, (ids[-1] + 1) * r)
                    pv = _dot(jnp.concatenate(ps, axis=0), vc[:, v_lanes])
                    pv = pv[:, :LANES] + pv[:, LANES:]
                    if running_max:
                        acc_ref[rows, :] = acc_ref[rows, :] * jnp.concatenate(alphas, axis=0) + pv
                    else:
                        acc_ref[rows, :] += pv
                return carry

            unroll = 1 if running_max else math.gcd(n_chunks, ATTN_UNROLL)
            lax.fori_loop(0, n_chunks, body, 0, unroll=unroll)

    sweep(False)
    top_exp = jnp.max(smax_ref[...].astype(F32))

    @pl.when(jnp.logical_not(top_exp <= p_max_exp))
    def _():
        sweep(True)


def _gqa_kernel(*refs, n_parts, tq):
    q_ref = refs[0]
    k_refs = refs[1:1 + n_parts]
    v_refs = refs[1 + n_parts:1 + 2 * n_parts]
    scl_ref, o_ref, m_ref, smax_ref, acc_ref = refs[1 + 2 * n_parts:6 + 2 * n_parts]
    k8_refs = refs[6 + 2 * n_parts:6 + 3 * n_parts]
    v8_refs = refs[6 + 3 * n_parts:]

    @pl.when(pl.program_id(2) == 0)
    def _():
        _quantize_kv(k_refs, v_refs, k8_refs, v8_refs, scl_ref, HEAD_V)

    stack = k8_refs[0].shape[1]
    q = q_ref[0].astype(F32) * scl_ref[0, SCL_Q:SCL_Q + 1, 0:1]
    qs = jnp.concatenate([_stack_qk(q[:, HEAD_V * j:HEAD_V * (j + 1)], False, QK_DTYPE, 1, stack)
                          for j in range(GQA_GROUP)], axis=0)
    _attention_sweeps([qs], [slice(0, stack)], [((0,), slice(0, 2 * LANES))], k8_refs, v8_refs,
                      m_ref, smax_ref, acc_ref, P_E4M3)
    o = _attention_output(acc_ref[...], scl_ref)
    o_ref[0] = jnp.concatenate([o[j * tq:(j + 1) * tq] for j in range(GQA_GROUP)], axis=1).astype(BF16)


def _attention_output(acc, scl_ref):
    return acc[:, :HEAD_V] * scl_ref[0, SCL_V_INV:SCL_V_INV + 1, 0:1] / acc[:, HEAD_V:]


def _softmax_scratch(rows):
    return [pltpu.VMEM((rows, LANES), F32), pltpu.VMEM((rows, LANES), BF16), pltpu.VMEM((rows, LANES), F32)]


def _stack_height(dk):
    return max(4 * dk, LANES)


def _fp8_kv_scratch(k_blocks, v_blocks, dk):
    ks = [pltpu.VMEM((kb[1], kb[2] // dk * _stack_height(dk), kb[3]), QK_DTYPE) for kb in k_blocks]
    vs = [pltpu.VMEM((vb[1], 2 * vb[2]), V_DTYPE) for vb in v_blocks]
    return ks + vs


def _gqa_attention(q, k_parts, v_parts, scales, tq):
    b, sq, _ = q.shape
    n_parts = len(k_parts)
    k_blocks = [(1, kp.shape[1], HEAD_V, kp.shape[3]) for kp in k_parts]
    v_blocks = [(1, vp.shape[1], LANES) for vp in v_parts]
    in_specs = [pl.BlockSpec((1, tq, 256), lambda bb, g, i: (bb, i, g))]
    in_specs += [pl.BlockSpec(kb, lambda bb, g, i: (bb, 0, g, 0)) for kb in k_blocks]
    in_specs += [pl.BlockSpec(vb, lambda bb, g, i: (bb, 0, g)) for vb in v_blocks]
    in_specs.append(pl.BlockSpec((1, SUBLANES, LANES), lambda bb, g, i: (bb, 0, 0)))
    return pl.pallas_call(
        functools.partial(_gqa_kernel, n_parts=n_parts, tq=tq),
        out_shape=jax.ShapeDtypeStruct((b, sq, 512), BF16),
        grid=(b, GQA_KV, sq // tq),
        in_specs=in_specs,
        out_specs=pl.BlockSpec((1, tq, 256), lambda bb, g, i: (bb, i, g)),
        scratch_shapes=_softmax_scratch(GQA_GROUP * tq) + _fp8_kv_scratch(k_blocks, v_blocks, HEAD_V),
        compiler_params=_cparams(("arbitrary", "arbitrary", "arbitrary")),
        name="gqa_attention",
    )(q, *k_parts, *v_parts, scales)


def _diff_kernel(*refs, n_parts, tq, lam_init):
    q_ref = refs[0]
    k_refs = refs[1:1 + n_parts]
    v_refs = refs[1 + n_parts:1 + 2 * n_parts]
    scl_ref, lam_ref, sg_ref, hm_ref, o_ref, m_ref, smax_ref, acc_ref = refs[1 + 2 * n_parts:9 + 2 * n_parts]
    k8_refs = refs[9 + 2 * n_parts:9 + 3 * n_parts]
    v8_refs = refs[9 + 3 * n_parts:]

    @pl.when(pl.program_id(2) == 0)
    def _():
        _quantize_kv(k_refs, v_refs, k8_refs, v8_refs, scl_ref, DIFF_QK)

    stack = k8_refs[0].shape[1] // 4
    q = q_ref[0].astype(F32) * scl_ref[0, SCL_Q:SCL_Q + 1, 0:1]
    qmaps = [_stack_qk(q[:, DIFF_QK * j:DIFF_QK * (j + 1)], False, QK_DTYPE, 1, stack) for j in range(4)]
    k_slices = [slice(stack * j, stack * (j + 1)) for j in range(4)]
    pv_groups = [((0, 1), slice(0, 2 * LANES)), ((2, 3), slice(2 * LANES, 4 * LANES))]
    _attention_sweeps(qmaps, k_slices, pv_groups, k8_refs, v8_refs, m_ref, smax_ref, acc_ref, P_E5M2)

    lv = lam_ref[...]
    lam = (jnp.exp(jnp.sum(lv[0:1] * lv[1:2], axis=-1, keepdims=True))
           - jnp.exp(jnp.sum(lv[2:3] * lv[3:4], axis=-1, keepdims=True)) + lam_init)
    outs = _attention_output(acc_ref[...], scl_ref)
    o = jnp.concatenate([outs[(2 * hh) * tq:(2 * hh + 1) * tq] - lam * outs[(2 * hh + 1) * tq:(2 * hh + 2) * tq]
                         for hh in range(2)], axis=1)
    ms = _dot((o * o).astype(BF16), hm_ref[...])
    o_ref[0] = (o * lax.rsqrt(ms + EPS) * sg_ref[...] * (1.0 - lam_init)).astype(BF16)


def _diff_attention(q, k_parts, v_parts, scales, lam_vecs, subln_g, lam_init, tq):
    b, sq, _ = q.shape
    n_parts = len(k_parts)
    k_blocks = [(1, kp.shape[1], LANES, kp.shape[3]) for kp in k_parts]
    v_blocks = [(1, vp.shape[1], 2 * LANES) for vp in v_parts]
    in_specs = [pl.BlockSpec((1, tq, LANES), lambda bb, p, i: (bb, i, p))]
    in_specs += [pl.BlockSpec(kb, lambda bb, p, i: (bb, 0, p, 0)) for kb in k_blocks]
    in_specs += [pl.BlockSpec(vb, lambda bb, p, i: (bb, 0, p)) for vb in v_blocks]
    in_specs.append(pl.BlockSpec((1, SUBLANES, LANES), lambda bb, p, i: (bb, 0, 0)))
    in_specs.append(pl.BlockSpec((4, DIFF_QK), lambda bb, p, i: (0, 0)))
    in_specs.append(pl.BlockSpec((1, LANES), lambda bb, p, i: (0, 0)))
    in_specs.append(pl.BlockSpec((LANES, LANES), lambda bb, p, i: (0, 0)))
    head_mean = _head_mean_matrix()[:LANES, :LANES]
    subln_pair = jnp.tile(subln_g, (1, LANES // HEAD_V))
    return pl.pallas_call(
        functools.partial(_diff_kernel, n_parts=n_parts, tq=tq, lam_init=lam_init),
        out_shape=jax.ShapeDtypeStruct((b, sq, 256), BF16),
        grid=(b, DIFF_HEADS // 2, sq // tq),
        in_specs=in_specs,
        out_specs=pl.BlockSpec((1, tq, LANES), lambda bb, p, i: (bb, i, p)),
        scratch_shapes=_softmax_scratch(4 * tq) + _fp8_kv_scratch(k_blocks, v_blocks, DIFF_QK),
        compiler_params=_cparams(("arbitrary", "arbitrary", "arbitrary")),
        name="diff_attention",
    )(q, *k_parts, *v_parts, scales, lam_vecs, subln_pair, head_mean)


CONV_HALO = 16
CONV_ROWS = 64


def _conv_kernel(gb_ref, prev_ref, next_ref, w_ref, b_ref, lg_ref, lb_ref, o_ref, u_ref, sh_ref, *, tm):
    i = pl.program_id(1)
    last = pl.num_programs(1) - 1
    ch = w_ref.shape[1]

    def glu(z):
        z = z.astype(F32)
        return z[:, :ch] * _sigmoid(z[:, ch:])

    u_ref[CONV_HALO:CONV_HALO + tm, :] = glu(gb_ref[0])
    u_ref[0:CONV_HALO, :] = jnp.where(i > 0, glu(prev_ref[0]), 0.0)
    u_ref[CONV_HALO + tm:2 * CONV_HALO + tm, :] = jnp.where(i < last, glu(next_ref[0]), 0.0)

    span = sh_ref.shape[1]
    for r in range(1, SUBLANES):
        sh_ref[r - 1] = u_ref[r:r + span, :]

    off = CONV_HALO - CONV_K // 2
    for r0 in range(0, tm, CONV_ROWS):
        acc = jnp.zeros((CONV_ROWS, ch), F32)
        for j in range(CONV_K):
            phase, base = (off + j) % SUBLANES, r0 + (off + j) // SUBLANES * SUBLANES
            taps = u_ref[base:base + CONV_ROWS, :] if phase == 0 else sh_ref[phase - 1, base:base + CONV_ROWS, :]
            acc = acc + taps * w_ref[j:j + 1, :]
        y = acc + b_ref[...]
        mu = jnp.mean(y, axis=-1, keepdims=True)
        yc = y - mu
        var = jnp.mean(yc * yc, axis=-1, keepdims=True)
        z = yc * lax.rsqrt(var + EPS) * lg_ref[...] + lb_ref[...]
        o_ref[0, r0:r0 + CONV_ROWS, :] = (z * _sigmoid(z)).astype(BF16)


def _conformer_conv(gb, w, bias, ln_g, ln_b, tm):
    b, s, two_ch = gb.shape
    ch = two_ch // 2
    hb = tm // CONV_HALO
    n_halo = s // CONV_HALO
    const2 = lambda bb, i: (0, 0)
    return pl.pallas_call(
        functools.partial(_conv_kernel, tm=tm),
        out_shape=jax.ShapeDtypeStruct((b, s, ch), BF16),
        grid=(b, s // tm),
        in_specs=[
            pl.BlockSpec((1, tm, two_ch), lambda bb, i: (bb, i, 0)),
            pl.BlockSpec((1, CONV_HALO, two_ch), lambda bb, i: (bb, jnp.maximum(i * hb - 1, 0), 0)),
            pl.BlockSpec((1, CONV_HALO, two_ch), lambda bb, i: (bb, jnp.minimum((i + 1) * hb, n_halo - 1), 0)),
            pl.BlockSpec((CONV_K, ch), const2), pl.BlockSpec((1, ch), const2),
            pl.BlockSpec((1, ch), const2), pl.BlockSpec((1, ch), const2),
        ],
        out_specs=pl.BlockSpec((1, tm, ch), lambda bb, i: (bb, i, 0)),
        scratch_shapes=[pltpu.VMEM((tm + 2 * CONV_HALO, ch), F32),
                        pltpu.VMEM((SUBLANES - 1, tm + 2 * CONV_HALO - SUBLANES, ch), F32)],
        compiler_params=_cparams(("arbitrary", "arbitrary")),
        name="conformer_conv",
    )(gb, gb, gb, w, bias, ln_g, ln_b)


def _merge_kernel(oa_ref, ob_ref, oc_ref, x_ref, gate_ref, shift_ref, scale_ref, g2_ref, w_ref, rw_ref,
                  xo_ref, h2_ref, lg_ref):
    tm = x_ref.shape[1]
    wa = oa_ref.shape[2]
    wb = wa + ob_ref.shape[2]
    y = _dot(oa_ref[0], w_ref[0:wa, :]) + _dot(ob_ref[0], w_ref[wa:wb, :]) + _dot(oc_ref[0], w_ref[wb:, :])
    xn = x_ref[0] + gate_ref[0] * y
    xo_ref[0] = xn
    ms = jnp.mean(xn * xn, axis=-1, keepdims=True)
    h2 = xn * lax.rsqrt(ms + EPS) * g2_ref[...] * (1.0 + scale_ref[0]) + shift_ref[0]
    _store_row_tiles(h2_ref, h2, tm)
    lg_ref[...] = lax.dot_general(rw_ref[...], h2, (((1,), (1,)), ((), ())),
                                  preferred_element_type=F32, precision=lax.Precision.HIGHEST)


def _merge_route(oa, ob, oc, x, gate, shift, scale, g2, w_out, router_wt, tm):
    b, s, d = x.shape
    row = lambda bb, i: (bb, i, 0)
    per_b = lambda bb, i: (bb, 0, 0)
    const2 = lambda bb, i: (0, 0)
    nt = s // tm
    in_specs = [
        pl.BlockSpec((1, tm, oa.shape[2]), row), pl.BlockSpec((1, tm, ob.shape[2]), row),
        pl.BlockSpec((1, tm, oc.shape[2]), row), pl.BlockSpec((1, tm, d), row),
        pl.BlockSpec((1, 1, d), per_b), pl.BlockSpec((1, 1, d), per_b), pl.BlockSpec((1, 1, d), per_b),
        pl.BlockSpec((1, d), const2), pl.BlockSpec((d, d), const2),
        pl.BlockSpec((N_EXPERTS, d), const2),
    ]
    out_shape = (jax.ShapeDtypeStruct((b, s, d), F32),
                 jax.ShapeDtypeStruct((b * s * ROW_TILE, LANES), F32),
                 jax.ShapeDtypeStruct((N_EXPERTS, b * s), F32))
    out_specs = (pl.BlockSpec((1, tm, d), row),
                 pl.BlockSpec((tm * ROW_TILE, LANES), lambda bb, i: (bb * nt + i, 0)),
                 pl.BlockSpec((N_EXPERTS, tm), lambda bb, i: (0, bb * nt + i)))
    return pl.pallas_call(
        _merge_kernel, out_shape=out_shape, grid=(b, nt), in_specs=in_specs, out_specs=out_specs,
        compiler_params=_cparams(("arbitrary", "arbitrary")), name="merge_route",
    )(oa, ob, oc, x, gate, shift, scale, g2, w_out, router_wt)


def _merge_ffn_kernel(oa_ref, ob_ref, oc_ref, x_ref, gate1_ref, shift_ref, scale_ref, g2_ref, w_ref, gate2_ref,
                      wg_ref, wu_ref, wd_ref, o_ref, *, tf):
    wa = oa_ref.shape[2]
    wb = wa + ob_ref.shape[2]
    y = _dot(oa_ref[0], w_ref[0:wa, :]) + _dot(ob_ref[0], w_ref[wa:wb, :]) + _dot(oc_ref[0], w_ref[wb:, :])
    xn = x_ref[0] + gate1_ref[0] * y
    ms = jnp.mean(xn * xn, axis=-1, keepdims=True)
    h = (xn * lax.rsqrt(ms + EPS) * g2_ref[...] * (1.0 + scale_ref[0]) + shift_ref[0]).astype(BF16)
    ff = wg_ref.shape[1]
    acc = jnp.zeros(xn.shape, F32)
    for f in range(0, ff, tf):
        g = _dot(h, wg_ref[:, f:f + tf])
        u = _dot(h, wu_ref[:, f:f + tf])
        a = (g * _sigmoid(g) * u).astype(BF16)
        acc = acc + _dot(a, wd_ref[f:f + tf, :])
    o_ref[0] = xn + gate2_ref[0] * acc


def _merge_dense_ffn(oa, ob, oc, x, gate1, shift, scale, g2, w_out, gate2, wg, wu, wd, tm):
    b, s, d = x.shape
    ff = wg.shape[1]
    row = lambda bb, i: (bb, i, 0)
    per_b = lambda bb, i: (bb, 0, 0)
    const2 = lambda bb, i: (0, 0)
    resident = pl.Buffered(1)
    return pl.pallas_call(
        functools.partial(_merge_ffn_kernel, tf=MXU_DIM),
        out_shape=jax.ShapeDtypeStruct((b, s, d), F32),
        grid=(b, s // tm),
        in_specs=[
            pl.BlockSpec((1, tm, oa.shape[2]), row), pl.BlockSpec((1, tm, ob.shape[2]), row),
            pl.BlockSpec((1, tm, oc.shape[2]), row), pl.BlockSpec((1, tm, d), row),
            pl.BlockSpec((1, 1, d), per_b), pl.BlockSpec((1, 1, d), per_b), pl.BlockSpec((1, 1, d), per_b),
            pl.BlockSpec((1, d), const2), pl.BlockSpec((d, d), const2, pipeline_mode=resident),
            pl.BlockSpec((1, 1, d), per_b),
            pl.BlockSpec((d, ff), const2, pipeline_mode=resident),
            pl.BlockSpec((d, ff), const2, pipeline_mode=resident),
            pl.BlockSpec((ff, d), const2, pipeline_mode=resident),
        ],
        out_specs=pl.BlockSpec((1, tm, d), row),
        compiler_params=_cparams(("arbitrary", "arbitrary")), name="merge_dense_ffn",
    )(oa, ob, oc, x, gate1, shift, scale, g2, w_out, gate2, wg, wu, wd)


def _top2(lg):
    sub = lax.broadcasted_iota(I32, lg.shape, 0)
    l1 = jnp.max(lg, axis=0, keepdims=True)
    i1 = jnp.min(jnp.where(lg == l1, sub, N_EXPERTS), axis=0, keepdims=True)
    m1 = sub == i1
    lg2 = jnp.where(m1, -jnp.inf, lg)
    l2 = jnp.max(lg2, axis=0, keepdims=True)
    i2 = jnp.min(jnp.where(lg2 == l2, sub, N_EXPERTS), axis=0, keepdims=True)
    m2 = sub == i2
    return l1, l2, m1, m2


def _sublane_cumsum(x):
    sub = lax.broadcasted_iota(I32, x.shape, 0)
    for sh in (1, 2, 4):
        x = x + jnp.where(sub >= sh, pltpu.roll(x, sh, 0), 0.0)
    return x


def _route_kernel(lg_ref, tri_ref, dest_ref, gates_ref, be_ref, pad_ref, base_ref, start_ref, *, block_rows, total_rows):
    phase = pl.program_id(0)
    j = pl.program_id(1)
    l1, l2, m1, m2 = _top2(lg_ref[...])
    e = jnp.where(m1 | m2, 1.0, 0.0).astype(F32)
    cnt = jnp.sum(e, axis=1, keepdims=True)

    @pl.when((phase == 0) & (j == 0))
    def _():
        base_ref[...] = jnp.zeros(base_ref.shape, F32)

    @pl.when((phase == 1) & (j == 0))
    def _():
        counts = base_ref[...]
        nblk = jnp.floor((counts + (block_rows - 1)) * (1.0 / block_rows))
        end_blk = _sublane_cumsum(nblk)
        start_ref[...] = (end_blk - nblk) * block_rows
        blk = lax.broadcasted_iota(I32, be_ref.shape, 1).astype(F32)
        owner = jnp.sum(jnp.where(end_blk[:, :1] <= blk, 1.0, 0.0), axis=0, keepdims=True)
        be_ref[...] = jnp.broadcast_to(jnp.minimum(owner, N_EXPERTS - 1.0), be_ref.shape).astype(I32)
        sub = lax.broadcasted_iota(I32, pad_ref.shape, 0)
        lane = lax.broadcasted_iota(I32, pad_ref.shape, 1)
        pad_end = jnp.where(sub == N_EXPERTS - 1, float(total_rows), end_blk * block_rows)
        pad_ref[...] = jnp.where(lane < LANES // 2, start_ref[...] + counts, pad_end).astype(I32)
        base_ref[...] = jnp.zeros(base_ref.shape, F32)

    @pl.when(phase == 1)
    def _():
        prefix = _dot(e.astype(BF16), tri_ref[...]) + base_ref[:, :1] + start_ref[:, :1]
        d1 = jnp.sum(jnp.where(m1, prefix, 0.0), axis=0, keepdims=True)
        d2 = jnp.sum(jnp.where(m2, prefix, 0.0), axis=0, keepdims=True)
        sub = lax.broadcasted_iota(I32, dest_ref.shape, 0)
        dest_ref[...] = jnp.where(sub == 0, d1, jnp.where(sub == 1, d2, 0.0)).astype(I32)
        ex = jnp.exp(l2 - l1)
        g1 = 1.0 / (1.0 + ex)
        g2 = ex / (1.0 + ex)
        half = lax.broadcasted_iota(I32, (LANES, lg_ref.shape[1]), 0) < LANES // 2
        gates_ref[...] = jnp.where(half, g1, g2).T

    base_ref[...] = base_ref[...] + cnt


def _route(logits_t, block_rows, n_blocks, tr):
    n = logits_t.shape[1]
    n_blocks_pad = -(-n_blocks // LANES) * LANES
    tri = jnp.asarray(np.triu(np.ones((tr, tr), np.float32), k=1), BF16)
    return pl.pallas_call(
        functools.partial(_route_kernel, block_rows=block_rows, total_rows=n_blocks * block_rows),
        out_shape=(jax.ShapeDtypeStruct((N_EXPERTS, n), I32),
                   jax.ShapeDtypeStruct((n, LANES), F32),
                   jax.ShapeDtypeStruct((N_EXPERTS, n_blocks_pad), I32),
                   jax.ShapeDtypeStruct((N_EXPERTS, LANES), I32)),
        grid=(2, n // tr),
        in_specs=[pl.BlockSpec((N_EXPERTS, tr), lambda p, j: (0, j)),
                  pl.BlockSpec((tr, tr), lambda p, j: (0, 0))],
        out_specs=(pl.BlockSpec((N_EXPERTS, tr), lambda p, j: (0, j * p)),
                   pl.BlockSpec((tr, LANES), lambda p, j: (j * p, 0)),
                   pl.BlockSpec((N_EXPERTS, n_blocks_pad), lambda p, j: (0, 0)),
                   pl.BlockSpec((N_EXPERTS, LANES), lambda p, j: (0, 0))),
        scratch_shapes=[pltpu.VMEM((N_EXPERTS, LANES), F32), pltpu.VMEM((N_EXPERTS, LANES), F32)],
        compiler_params=_cparams(("arbitrary", "arbitrary")), name="moe_route",
    )(logits_t, tri)


def _row_copy(src_hbm, src_row, dst_hbm, dst_row, sem):
    src = pl.ds(pl.multiple_of(src_row * ROW_TILE, ROW_TILE), ROW_TILE)
    dst = pl.ds(pl.multiple_of(dst_row * ROW_TILE, ROW_TILE), ROW_TILE)
    return pltpu.make_async_copy(src_hbm.at[src], dst_hbm.at[dst], sem)


def _scatter_kernel(pad_lo_ref, pad_hi_ref, d1_ref, d2_ref, src_ref, out_hbm, zero_ref, sem, zero_sem, *, rows):
    @pl.when(pl.program_id(0) == 0)
    def _():
        zero_ref[...] = jnp.zeros(zero_ref.shape, zero_ref.dtype)
        for e in range(N_EXPERTS):
            def fill(row, c):
                _row_copy(zero_ref, 0, out_hbm, row, zero_sem).start()
                return c

            def drain(row, c):
                _row_copy(zero_ref, 0, out_hbm, 0, zero_sem).wait()
                return c

            lax.fori_loop(pad_lo_ref[e], pad_hi_ref[e], fill, 0)
            lax.fori_loop(pad_lo_ref[e], pad_hi_ref[e], drain, 0)

    def start(r, c):
        _row_copy(src_ref, r, out_hbm, d1_ref[0, 0, r], sem).start(priority=0)
        _row_copy(src_ref, r, out_hbm, d2_ref[0, 0, r], sem).start(priority=1)
        return c

    lax.fori_loop(0, rows, start, 0, unroll=DMA_UNROLL)
    for _ in range(2):
        pltpu.make_async_copy(src_ref, out_hbm.at[pl.ds(0, rows * ROW_TILE)], sem).wait()


def _scatter_rows(src, d1, d2, pad_lo, pad_hi, total_rows, rows):
    n = d1.shape[0]
    idx_spec = pl.BlockSpec((1, 1, rows), lambda i, lo, hi: (i, 0, 0), memory_space=pltpu.SMEM)
    any_spec = pl.BlockSpec(memory_space=pl.ANY)
    grid_spec = pltpu.PrefetchScalarGridSpec(
        num_scalar_prefetch=2,
        grid=(n // rows,),
        in_specs=[idx_spec, idx_spec, pl.BlockSpec((rows * ROW_TILE, LANES), lambda i, lo, hi: (i, 0))],
        out_specs=any_spec,
        scratch_shapes=[pltpu.VMEM((ROW_TILE, LANES), src.dtype), pltpu.SemaphoreType.DMA(()),
                        pltpu.SemaphoreType.DMA(())],
    )
    return pl.pallas_call(
        functools.partial(_scatter_kernel, rows=rows),
        out_shape=jax.ShapeDtypeStruct((total_rows * ROW_TILE, LANES), src.dtype),
        grid_spec=grid_spec,
        compiler_params=pltpu.CompilerParams(dimension_semantics=("arbitrary",), has_side_effects=True),
        name="moe_scatter_rows",
    )(pad_lo, pad_hi, d1.reshape(n // rows, 1, rows), d2.reshape(n // rows, 1, rows), src)


def _expert_kernel(be_ref, x_ref, wg_ref, wu_ref, wd_ref, o_ref, *, block_rows, tf):
    del be_ref
    x = _load_row_tiles(x_ref, block_rows).astype(BF16)
    ff = wg_ref.shape[2]
    acc = jnp.zeros((block_rows, wd_ref.shape[2]), F32)
    for f in range(0, ff, tf):
        g = _dot(x, wg_ref[0, :, f:f + tf])
        u = _dot(x, wu_ref[0, :, f:f + tf])
        a = (g * _sigmoid(g) * u).astype(BF16)
        acc = acc + _dot(a, wd_ref[0, f:f + tf, :])
    _store_row_tiles(o_ref, acc, block_rows)


def _expert_ffn(xb, blk_expert, wg, wu, wd, block_rows):
    d, ff = wg.shape[1], wg.shape[2]
    rows = xb.shape[0] // ROW_TILE
    resident = pl.Buffered(1)
    grid_spec = pltpu.PrefetchScalarGridSpec(
        num_scalar_prefetch=1,
        grid=(rows // block_rows,),
        in_specs=[
            pl.BlockSpec((block_rows * ROW_TILE, LANES), lambda i, be: (i, 0)),
            pl.BlockSpec((1, d, ff), lambda i, be: (be[i], 0, 0), pipeline_mode=resident),
            pl.BlockSpec((1, d, ff), lambda i, be: (be[i], 0, 0), pipeline_mode=resident),
            pl.BlockSpec((1, ff, d), lambda i, be: (be[i], 0, 0), pipeline_mode=resident),
        ],
        out_specs=pl.BlockSpec((block_rows * ROW_TILE, LANES), lambda i, be: (i, 0)),
    )
    return pl.pallas_call(
        functools.partial(_expert_kernel, block_rows=block_rows, tf=MXU_DIM),
        out_shape=jax.ShapeDtypeStruct(xb.shape, F32), grid_spec=grid_spec,
        compiler_params=_cparams(("arbitrary",)), name="moe_expert_ffn",
    )(blk_expert, xb, wg, wu, wd)


def _combine_kernel(d1_ref, d2_ref, d1n_ref, d2n_ref, x_ref, yb_hbm, gates_ref, gate_ref, fg_ref, o_ref,
                    y1_ref, y2_ref, sems):
    tm = x_ref.shape[0]
    i = pl.program_id(0)
    slot = i % 2

    def gather(i1_ref, i2_ref, to_slot):
        def start(r, c):
            _row_copy(yb_hbm, i1_ref[0, 0, r], y1_ref.at[to_slot], r, sems.at[to_slot]).start(priority=0)
            _row_copy(yb_hbm, i2_ref[0, 0, r], y2_ref.at[to_slot], r, sems.at[to_slot]).start(priority=1)
            return c

        lax.fori_loop(0, tm, start, 0, unroll=DMA_UNROLL)

    @pl.when(i == 0)
    def _():
        gather(d1_ref, d2_ref, slot)

    @pl.when(i + 1 < pl.num_programs(0))
    def _():
        gather(d1n_ref, d2n_ref, 1 - slot)

    for y_ref in (y1_ref, y2_ref):
        pltpu.make_async_copy(yb_hbm.at[pl.ds(0, tm * ROW_TILE)], y_ref.at[slot], sems.at[slot]).wait()

    gts = gates_ref[...]
    y = (gts[:, 0:1] * _load_row_tiles(y1_ref.at[slot], tm)
         + gts[:, LANES // 2:LANES // 2 + 1] * _load_row_tiles(y2_ref.at[slot], tm))
    xn = x_ref[...] + gate_ref[0] * y
    ms = jnp.mean(xn * xn, axis=-1, keepdims=True)
    o_ref[...] = xn * lax.rsqrt(ms + EPS) * fg_ref[...]


def _combine_final(x, yb, d1, d2, gates, gate, final_g, tm):
    n, d = x.shape
    s = n // gate.shape[0]
    row = lambda i: (i, 0)
    steps = n // tm
    idx_spec = pl.BlockSpec((1, 1, tm), lambda i: (i, 0, 0), memory_space=pltpu.SMEM)
    next_spec = pl.BlockSpec((1, 1, tm), lambda i: (jnp.minimum(i + 1, steps - 1), 0, 0), memory_space=pltpu.SMEM)
    d1 = d1.reshape(steps, 1, tm)
    d2 = d2.reshape(steps, 1, tm)
    return pl.pallas_call(
        _combine_kernel,
        out_shape=jax.ShapeDtypeStruct((n, d), F32),
        grid=(steps,),
        in_specs=[idx_spec, idx_spec, next_spec, next_spec,
                  pl.BlockSpec((tm, d), row),
                  pl.BlockSpec(memory_space=pl.ANY),
                  pl.BlockSpec((tm, LANES), row),
                  pl.BlockSpec((1, 1, d), lambda i: ((i * tm) // s, 0, 0)),
                  pl.BlockSpec((1, d), lambda i: (0, 0))],
        out_specs=pl.BlockSpec((tm, d), row),
        scratch_shapes=[pltpu.VMEM((2, tm * ROW_TILE, LANES), F32), pltpu.VMEM((2, tm * ROW_TILE, LANES), F32),
                        pltpu.SemaphoreType.DMA((2,))],
        compiler_params=_cparams(("arbitrary",)), name="moe_combine_final",
    )(d1, d2, d1, d2, x, yb, gates, gate, final_g)


AMAX_QA, AMAX_KA, AMAX_VA, AMAX_QC, AMAX_KC, AMAX_VC = range(6)
FP8_TARGET_MAX = 256.0


def _fp8_scales(q_max, k_max, v_max, head_dim):
    tiny = jnp.finfo(F32).tiny
    c = head_dim ** -0.5 * LOG2E
    ratio = jnp.where((q_max > 0) & (k_max > 0), c * k_max / jnp.maximum(q_max, tiny), 1.0)
    sq = jnp.exp2(jnp.round(0.5 * jnp.log2(ratio)))
    sk = c / sq
    sv = jnp.where(v_max > 0, jnp.exp2(jnp.floor(jnp.log2(FP8_TARGET_MAX / jnp.maximum(v_max, tiny)))), 1.0)
    rows = jnp.stack([sq, sk, sv, 1.0 / sv] + [jnp.zeros_like(sq)] * (SUBLANES - 4), axis=1)
    return jnp.broadcast_to(rows[:, :, None], rows.shape + (LANES,)).astype(F32)


def _rope_tables(s, dim):
    half = dim // 2
    t = jnp.arange(s)
    inv = 1.0 / (ROPE_THETA ** (jnp.arange(0, half, 2, dtype=F32) / half))
    ang_r = (t // GRID_W).astype(F32)[:, None] * inv
    ang_c = (t % GRID_W).astype(F32)[:, None] * inv
    ang = jnp.concatenate([ang_r, ang_r, ang_c, ang_c], axis=-1)
    reps = LANES // dim
    return jnp.tile(jnp.cos(ang), (1, reps)), jnp.tile(jnp.sin(ang), (1, reps))


def _rotate_matrix(dim):
    q = dim // 4
    p = np.zeros((MXU_DIM, MXU_DIM), np.float32)
    for j in range(MXU_DIM):
        if (j % (2 * q)) < q:
            p[j + q, j] = -1.0
        else:
            p[j - q, j] = 1.0
    return jnp.asarray(p, BF16)


def _head_mean_matrix():
    m = np.kron(np.eye(MXU_DIM // HEAD_V, dtype=np.float32), np.full((HEAD_V, HEAD_V), 1.0 / HEAD_V, np.float32))
    return jnp.asarray(m, BF16)


def _widen_values(w, heads):
    d = w.shape[0]
    w = w.reshape(d, heads, HEAD_V)
    return jnp.concatenate([w, jnp.zeros_like(w)], axis=-1).reshape(d, heads * LANES)


def _widen_in_proj(w):
    qa, ka, va, gb, qc, kc, vc = jnp.split(w, [256, 512, 768, 1280, 1792, 1920], axis=1)
    return jnp.concatenate([qa, ka, _widen_values(va, DIFF_HEADS), gb, qc, kc, _widen_values(vc, GQA_KV)],
                           axis=1).astype(BF16)


def kernel(x, c, ctx, c_ctx, ada_w, ada_b, norm1_g, norm2_g, w_in, w_out, lam_q1, lam_k1, lam_q2, lam_k2,
           diff_subln_g, conv_w, conv_b, conv_ln_g, conv_ln_b, q_norm_g, k_norm_g, ffn_gate, ffn_up, ffn_down,
           router_w, moe_gate, moe_up, moe_down, final_g):
    b, s, d = x.shape
    sc = ctx.shape[1]
    depth = ada_w.shape[0]
    n = b * s
    assert depth % 2 == 0, "the final RMSNorm is fused into the MoE combine of the last (odd) layer"

    tm = min(512, s)
    tmc = min(512, sc)
    tq = min(512, s)
    tq_diff = min(1024, s)
    tqc = min(256, sc)

    tabs_x = _rope_tables(s, DIFF_QK) + _rope_tables(s, HEAD_V)
    ones_c, zeros_c = jnp.ones((sc, LANES), F32), jnp.zeros((sc, LANES), F32)
    tabs_c = (ones_c, zeros_c, ones_c, zeros_c)
    mats = (_rotate_matrix(DIFF_QK), _rotate_matrix(HEAD_V), _head_mean_matrix())

    cc = jnp.zeros((16, d), F32).at[:b].set(c).at[b].set(c_ctx)

    for i in range(depth):
        last = i == depth - 1
        lam_init = 0.8 - 0.6 * math.exp(-0.3 * i)
        mod_all = _ada_mod(cc, ada_w[i], ada_b[i])
        mod = mod_all[:b].reshape(b, 6, 1, d)
        modc = jnp.broadcast_to(mod_all[b].reshape(1, 6, 1, d), (b, 6, 1, d))

        w_aug = _widen_in_proj(w_in[i])
        g1 = norm1_g[i].reshape(1, d)
        qg = jnp.tile(q_norm_g[i], GQA_HEADS).reshape(1, -1)
        kg = jnp.tile(k_norm_g[i], GQA_KV).reshape(1, -1)
        lam_vecs = jnp.stack([lam_q1[i], lam_k1[i], lam_q2[i], lam_k2[i]]).astype(F32)
        subln = diff_subln_g[i].reshape(1, HEAD_V)
        conv_args = (conv_w[i], conv_b[i].reshape(1, -1), conv_ln_g[i].reshape(1, -1), conv_ln_b[i].reshape(1, -1))
        w_o = w_out[i].astype(BF16)
        g2 = norm2_g[i].reshape(1, d)

        qa, kat, va, gb, qc, kct, vc, amax = _in_projection(
            x, mod[:, 0], mod[:, 1], g1, w_aug, tabs_x, mats, qg, kg, tm, 2 if s % (2 * tm) == 0 else 1)
        qa_x, kat_x, va_x, gb_x, qc_x, kct_x, vc_x, amax_x = _in_projection(
            ctx, modc[:, 0], modc[:, 1], g1, w_aug, tabs_c, mats, qg, kg, tmc, 1)
        amax = jnp.max(amax, axis=(1, 3))
        amax_x = jnp.max(amax_x, axis=(1, 3))
        amax_kv = jnp.maximum(amax, amax_x)
        scl_a = _fp8_scales(amax[:, AMAX_QA], amax_kv[:, AMAX_KA], amax_kv[:, AMAX_VA], DIFF_QK)
        scl_c = _fp8_scales(amax[:, AMAX_QC], amax_kv[:, AMAX_KC], amax_kv[:, AMAX_VC], HEAD_V)

        oa = _diff_attention(qa, [kat, kat_x], [va, va_x], scl_a, lam_vecs, subln, lam_init, tq_diff)
        ob = _conformer_conv(gb, *conv_args, tm)
        oc = _gqa_attention(qc, [kct, kct_x], [vc, vc_x], scl_c, tq)

        j = i // 2
        if i % 2 == 0:
            wg, wu, wd = ffn_gate[j].astype(BF16), ffn_up[j].astype(BF16), ffn_down[j].astype(BF16)
            x = _merge_dense_ffn(oa, ob, oc, x, mod[:, 2], mod[:, 3], mod[:, 4], g2, w_o, mod[:, 5], wg, wu, wd, tm)
        else:
            rwt = router_w[j].T.astype(F32)
            x, h2, logits_t = _merge_route(oa, ob, oc, x, mod[:, 2], mod[:, 3], mod[:, 4], g2, w_o, rwt, tm)
            block_rows = 512 if n >= 8192 else 256
            n_blocks = (2 * n) // block_rows + N_EXPERTS
            dest, gates, blk_e, pad = _route(logits_t, block_rows, n_blocks, min(512, n))
            xb = _scatter_rows(h2, dest[0], dest[1], pad[:, 0], pad[:, LANES // 2], n_blocks * block_rows,
                               min(1024, n))
            yb = _expert_ffn(xb, blk_e[0, :n_blocks], moe_gate[j].astype(BF16), moe_up[j].astype(BF16),
                             moe_down[j].astype(BF16), block_rows)
            assert last
            x = _combine_final(x.reshape(n, d), yb, dest[0], dest[1], gates, mod[:, 5], final_g.reshape(1, d),
                               tm).reshape(b, s, d)

        if not last:
            scl_ax = _fp8_scales(amax_x[:, AMAX_QA], amax_x[:, AMAX_KA], amax_x[:, AMAX_VA], DIFF_QK)
            scl_cx = _fp8_scales(amax_x[:, AMAX_QC], amax_x[:, AMAX_KC], amax_x[:, AMAX_VC], HEAD_V)
            oa_x = _diff_attention(qa_x, [kat_x], [va_x], scl_ax, lam_vecs, subln, lam_init, tqc)
            ob_x = _conformer_conv(gb_x, *conv_args, tmc)
            oc_x = _gqa_attention(qc_x, [kct_x], [vc_x], scl_cx, tqc)
            assert i % 2 == 0, "context tokens only ever pass through dense channel mixers"
            ctx = _merge_dense_ffn(oa_x, ob_x, oc_x, ctx, modc[:, 2], modc[:, 3], modc[:, 4], g2, w_o, modc[:, 5],
                                   wg, wu, wd, tmc)

    return x
```

```python
import functools
import math

import numpy as np
import jax
import jax.numpy as jnp
from jax import lax
from jax.experimental import pallas as pl
from jax.experimental.pallas import tpu as pltpu

F32 = jnp.float32
BF16 = jnp.bfloat16
I32 = jnp.int32

EPS = 1e-6
ROPE_THETA = 10000.0
GRID_W = 64

DIFF_HEADS = 4
DIFF_QK = 32
HEAD_V = 64
GQA_HEADS = 8
GQA_KV = 2
GQA_GROUP = GQA_HEADS // GQA_KV
CONV_K = 31
N_EXPERTS = 8
LOG2E = math.log2(math.e)

LANES = 128
SUBLANES = 8
MXU_DIM = 256
VMEM_LIMIT = 52 * 1024 * 1024
NEG_BIG = -1e30
V_DTYPE = jnp.float8_e4m3fn
QK_DTYPE = jnp.float8_e4m3fn
P_E4M3 = (jnp.float8_e4m3fn, 8.5)
P_E5M2 = (jnp.float8_e5m2, 15.5)
P_HEADROOM_RUNNING = 8.0
ATTN_UNROLL = 8
DMA_UNROLL = 8

C_QA, C_KA, C_VA, C_GB, C_QC, C_KC, C_VC, C_END = 0, 256, 512, 1024, 1536, 2048, 2176, 2432


def _cparams(semantics):
    return pltpu.CompilerParams(dimension_semantics=semantics, vmem_limit_bytes=VMEM_LIMIT)


def _dot(a, b):
    return jnp.dot(a, b, preferred_element_type=F32)


def _sigmoid(z):
    return 1.0 / (1.0 + jnp.exp(-z))


ROW_TILE = 8


def _store_row_tiles(ref, val, rows):
    for a in range(ROW_TILE):
        ref[pl.ds(a, rows, stride=ROW_TILE), :] = val[:, a * LANES:(a + 1) * LANES]


def _load_row_tiles(ref, rows):
    return jnp.concatenate([ref[pl.ds(a, rows, stride=ROW_TILE), :] for a in range(ROW_TILE)], axis=1)


def _mod_kernel(c_ref, w_ref, b_ref, o_ref):
    c = c_ref[...]
    s = c * _sigmoid(c)
    o_ref[...] = jnp.dot(s, w_ref[...], preferred_element_type=F32, precision=lax.Precision.HIGHEST) + b_ref[...]


def _ada_mod(cc, w, b):
    rows, d = cc.shape
    n = w.shape[1]
    tn = d
    return pl.pallas_call(
        _mod_kernel,
        out_shape=jax.ShapeDtypeStruct((rows, n), F32),
        grid=(n // tn,),
        in_specs=[pl.BlockSpec((rows, d), lambda j: (0, 0)),
                  pl.BlockSpec((d, tn), lambda j: (0, j)),
                  pl.BlockSpec((1, tn), lambda j: (0, j))],
        out_specs=pl.BlockSpec((rows, tn), lambda j: (0, j)),
        compiler_params=_cparams(("arbitrary",)),
        name="ada_mod",
    )(cc, w, b.reshape(1, n))


def _inproj_kernel(x_ref, shift_ref, scale_ref, g_ref, w_ref, cosa_ref, sina_ref, cosc_ref, sinc_ref,
                   pa_ref, pc_ref, hm_ref, qg_ref, kg_ref,
                   qa_o, kat_o, va_o, gb_o, qc_o, kct_o, vc_o, amax_o):
    n_sub, sub = kat_o.shape[1], kat_o.shape[3]
    stats = None
    for t in range(n_sub):
        rows = slice(t * sub, (t + 1) * sub)
        tabs = [ref[rows, :] for ref in (cosa_ref, sina_ref, cosc_ref, sinc_ref)]
        outs = [o.at[0, rows, :] for o in (qa_o, va_o, gb_o, qc_o, vc_o)] + [kat_o.at[0, t], kct_o.at[0, t]]
        st = _inproj_subtile(x_ref[0, rows, :], shift_ref, scale_ref, g_ref, w_ref, tabs, pa_ref, pc_ref, hm_ref,
                             qg_ref, kg_ref, outs, amax_o.shape[3])
        stats = st if stats is None else jnp.maximum(stats, st)
    amax_o[0, 0] = stats


def _inproj_subtile(x, shift_ref, scale_ref, g_ref, w_ref, tabs, pa_ref, pc_ref, hm_ref, qg_ref, kg_ref, outs, stat_w):
    qa_o, va_o, gb_o, qc_o, vc_o, kat_o, kct_o = outs
    cosa, sina, cosc, sinc = tabs
    ms = jnp.mean(x * x, axis=-1, keepdims=True)
    h = x * lax.rsqrt(ms + EPS) * g_ref[...]
    h = h * (1.0 + scale_ref[0]) + shift_ref[0]
    hb = h.astype(BF16)

    def proj(lo, hi):
        return _dot(hb, w_ref[:, lo:hi])

    def blockmat(y, m_ref):
        yb = y.astype(BF16)
        w = y.shape[1]
        if w == LANES:
            return _dot(yb, m_ref[:LANES, :LANES])
        return jnp.concatenate([_dot(yb[:, c:c + MXU_DIM], m_ref[...]) for c in range(0, w, MXU_DIM)], axis=1)

    def rope(y, cos, sin, p_ref):
        reps = y.shape[1] // LANES
        cos = jnp.tile(cos, (1, reps))
        sin = jnp.tile(sin, (1, reps))
        return y * cos + blockmat(y, p_ref) * sin

    def ones_col(width):
        lane = lax.broadcasted_iota(I32, (1, width), 1)
        return jnp.where(lane % LANES >= HEAD_V, 1.0, 0.0).astype(F32)

    def col_amax(y):
        cm = jnp.max(jnp.abs(y), axis=0, keepdims=True)
        pad = stat_w - y.shape[1]
        return cm if pad == 0 else jnp.concatenate([cm, jnp.zeros((1, pad), F32)], axis=1)

    qa = rope(proj(C_QA, C_KA), cosa, sina, pa_ref)
    qa_o[...] = qa.astype(BF16)
    ka = rope(proj(C_KA, C_VA), cosa, sina, pa_ref)
    kat_o[...] = ka.T.astype(BF16)
    va = proj(C_VA, C_GB)
    va_o[...] = (va + ones_col(C_GB - C_VA)).astype(BF16)
    gb_o[...] = proj(C_GB, C_QC).astype(BF16)

    y = proj(C_QC, C_KC)
    yn = y * lax.rsqrt(blockmat(y * y, hm_ref) + EPS) * qg_ref[...]
    qc = rope(yn, cosc, sinc, pc_ref)
    qc_o[...] = qc.astype(BF16)

    y = proj(C_KC, C_VC)
    yn = y * lax.rsqrt(blockmat(y * y, hm_ref) + EPS) * kg_ref[...]
    kc = rope(yn, cosc, sinc, pc_ref)
    kct_o[...] = kc.T.astype(BF16)

    vc = proj(C_VC, C_END)
    vc_o[...] = (vc + ones_col(C_END - C_VC)).astype(BF16)

    stats = [col_amax(t) for t in (qa, ka, va, qc, kc, vc)]
    stats += [jnp.zeros((1, stat_w), F32)] * (SUBLANES - len(stats))
    return jnp.concatenate(stats, axis=0)


def _in_projection(x, shift, scale, g1, w_aug, tabs, mats, qg, kg, tk, n_sub):
    b, s, d = x.shape
    tm = tk * n_sub
    nt = s // tm
    cosa, sina, cosc, sinc = tabs
    pa, pc, hm = mats
    row = lambda bb, i: (bb, i, 0)
    const2 = lambda bb, i: (0, 0)
    per_b = lambda bb, i: (bb, 0, 0)
    tab = lambda bb, i: (i, 0)
    out_shape = (
        jax.ShapeDtypeStruct((b, s, 256), BF16),
        jax.ShapeDtypeStruct((b, s // tk, 256, tk), BF16),
        jax.ShapeDtypeStruct((b, s, 512), BF16),
        jax.ShapeDtypeStruct((b, s, 512), BF16),
        jax.ShapeDtypeStruct((b, s, 512), BF16),
        jax.ShapeDtypeStruct((b, s // tk, 128, tk), BF16),
        jax.ShapeDtypeStruct((b, s, 256), BF16),
        jax.ShapeDtypeStruct((b, nt, SUBLANES, 512), F32),
    )
    out_specs = (
        pl.BlockSpec((1, tm, 256), row),
        pl.BlockSpec((1, n_sub, 256, tk), lambda bb, i: (bb, i, 0, 0)),
        pl.BlockSpec((1, tm, 512), row),
        pl.BlockSpec((1, tm, 512), row),
        pl.BlockSpec((1, tm, 512), row),
        pl.BlockSpec((1, n_sub, 128, tk), lambda bb, i: (bb, i, 0, 0)),
        pl.BlockSpec((1, tm, 256), row),
        pl.BlockSpec((1, 1, SUBLANES, 512), lambda bb, i: (bb, i, 0, 0)),
    )
    in_specs = [
        pl.BlockSpec((1, tm, d), row),
        pl.BlockSpec((1, 1, d), per_b),
        pl.BlockSpec((1, 1, d), per_b),
        pl.BlockSpec((1, d), const2),
        pl.BlockSpec((d, C_END), const2),
        pl.BlockSpec((tm, LANES), tab), pl.BlockSpec((tm, LANES), tab),
        pl.BlockSpec((tm, LANES), tab), pl.BlockSpec((tm, LANES), tab),
        pl.BlockSpec((MXU_DIM, MXU_DIM), const2), pl.BlockSpec((MXU_DIM, MXU_DIM), const2),
        pl.BlockSpec((MXU_DIM, MXU_DIM), const2),
        pl.BlockSpec((1, 512), const2), pl.BlockSpec((1, 128), const2),
    ]
    return pl.pallas_call(
        _inproj_kernel, out_shape=out_shape, grid=(b, nt), in_specs=in_specs, out_specs=out_specs,
        compiler_params=_cparams(("arbitrary", "arbitrary")), name="in_projection",
    )(x, shift, scale, g1, w_aug, cosa, sina, cosc, sinc, pa, pc, hm, qg, kg)


SCL_Q, SCL_K, SCL_V, SCL_V_INV = 0, 1, 2, 3


LO_GAIN = 16.0
V_ROW_CHUNK = 512


def _split_fp8(x, dtype):
    hi = x.astype(dtype)
    return hi, ((x - hi.astype(F32)) * LO_GAIN).astype(dtype)


def _stack_qk(x, other_side, dtype, axis, width):
    hi, lo = _split_fp8(x, dtype)
    hi_small = (hi.astype(F32) * (1.0 / LO_GAIN)).astype(dtype)
    parts = [hi, lo, hi_small] if other_side else [hi, hi_small, lo]
    pad = width - 3 * x.shape[axis]
    if pad:
        pad_shape = tuple(pad if a == axis else n for a, n in enumerate(x.shape))
        parts.append(jnp.zeros(pad_shape, dtype))
    return jnp.concatenate(parts, axis=axis)


def _quantize_kv(k_refs, v_refs, k8_refs, v8_refs, scl_ref, dk):
    sk = scl_ref[0, SCL_K:SCL_K + 1, 0:1]
    sv = scl_ref[0, SCL_V:SCL_V + 1, 0:1]
    for k_ref, k8_ref in zip(k_refs, k8_refs):
        n_maps = k_ref.shape[2] // dk
        stack = k8_ref.shape[1] // n_maps

        def k_body(c, carry, k_ref=k_ref, k8_ref=k8_ref, n_maps=n_maps, stack=stack):
            kf = k_ref[0, c].astype(F32) * sk
            k8_ref[c] = jnp.concatenate(
                [_stack_qk(kf[dk * j:dk * (j + 1), :], True, k8_ref.dtype, 0, stack) for j in range(n_maps)], axis=0)
            return carry

        lax.fori_loop(0, k_ref.shape[1], k_body, 0)

    for v_ref, v8_ref in zip(v_refs, v8_refs):
        rows = min(V_ROW_CHUNK, v_ref.shape[1])

        def v_body(c, carry, v_ref=v_ref, v8_ref=v8_ref, rows=rows):
            sl = pl.ds(pl.multiple_of(c * rows, rows), rows)
            vf = v_ref[0, sl, :].astype(F32)
            groups = []
            for g in range(v_ref.shape[2] // LANES):
                vs = vf[:, LANES * g:LANES * g + HEAD_V] * sv
                hi = vs.astype(v8_ref.dtype)
                lo = (vs - hi.astype(F32)).astype(v8_ref.dtype)
                ones = vf[:, LANES * g + HEAD_V:LANES * (g + 1)].astype(v8_ref.dtype)
                groups += [hi, ones, lo, jnp.zeros((rows, HEAD_V), v8_ref.dtype)]
            v8_ref[sl, :] = jnp.concatenate(groups, axis=1)
            return carry

        lax.fori_loop(0, v_ref.shape[1] // rows, v_body, 0)


def _attention_sweeps(qms, k_slices, pv_groups, k_refs, v_refs, m_ref, smax_ref, acc_ref, p_format):
    r = qms[0].shape[0]
    p_dtype, p_max_exp = p_format

    def scores(j, kc):
        return _dot(qms[j], kc[k_slices[j], :])

    def sweep(running_max):
        acc_ref[...] = jnp.zeros(acc_ref.shape, F32)
        if running_max:
            m_ref[...] = jnp.full(m_ref.shape, NEG_BIG, F32)
        else:
            smax_ref[...] = jnp.full(smax_ref.shape, NEG_BIG, smax_ref.dtype)
            kc0 = k_refs[0][0][:, :MXU_DIM]
            for j in range(len(qms)):
                m0 = jnp.max(scores(j, kc0), axis=-1, keepdims=True)
                m_ref[j * r:(j + 1) * r, :] = jnp.broadcast_to(m0, (r, LANES))

        for k_ref, v_ref in zip(k_refs, v_refs):
            n_chunks, tk = k_ref.shape[0], k_ref.shape[2]

            def body(c, carry, k_ref=k_ref, v_ref=v_ref, tk=tk):
                kc = k_ref[c]
                vc = v_ref[pl.ds(pl.multiple_of(c * tk, tk), tk), :]
                for ids, v_lanes in pv_groups:
                    ps, alphas = [], []
                    for j in ids:
                        rows = slice(j * r, (j + 1) * r)
                        s = scores(j, kc)
                        m = m_ref[rows, :]
                        if running_max:
                            m_new = jnp.maximum(m, jnp.max(s, axis=-1, keepdims=True) - P_HEADROOM_RUNNING)
                            m_ref[rows, :] = m_new
                            alphas.append(jnp.exp2(m - m_new))
                            m = m_new
                        d = (s - jnp.tile(m, (1, tk // LANES))).astype(BF16)
                        if not running_max:
                            cm = functools.reduce(jnp.maximum, [d[:, l:l + LANES] for l in range(0, tk, LANES)])
                            smax_ref[rows, :] = jnp.maximum(smax_ref[rows, :], cm)
                        ps.append(jnp.exp2(d).astype(p_dtype))
                    rows = slice(ids[0] * r, (ids[-1] + 1) * r)
                    pv = _dot(jnp.concatenate(ps, axis=0), vc[:, v_lanes])
                    pv = pv[:, :LANES] + pv[:, LANES:]
                    if running_max:
                        acc_ref[rows, :] = acc_ref[rows, :] * jnp.concatenate(alphas, axis=0) + pv
                    else:
                        acc_ref[rows, :] += pv
                return carry

            unroll = 1 if running_max else math.gcd(n_chunks, ATTN_UNROLL)
            lax.fori_loop(0, n_chunks, body, 0, unroll=unroll)

    sweep(False)
    top_exp = jnp.max(smax_ref[...].astype(F32))

    @pl.when(jnp.logical_not(top_exp <= p_max_exp))
    def _():
        sweep(True)


def _gqa_kernel(*refs, n_parts, tq):
    q_ref = refs[0]
    k_refs = refs[1:1 + n_parts]
    v_refs = refs[1 + n_parts:1 + 2 * n_parts]
    scl_ref, o_ref, m_ref, smax_ref, acc_ref = refs[1 + 2 * n_parts:6 + 2 * n_parts]
    k8_refs = refs[6 + 2 * n_parts:6 + 3 * n_parts]
    v8_refs = refs[6 + 3 * n_parts:]

    @pl.when(pl.program_id(2) == 0)
    def _():
        _quantize_kv(k_refs, v_refs, k8_refs, v8_refs, scl_ref, HEAD_V)

    stack = k8_refs[0].shape[1]
    q = q_ref[0].astype(F32) * scl_ref[0, SCL_Q:SCL_Q + 1, 0:1]
    qs = jnp.concatenate([_stack_qk(q[:, HEAD_V * j:HEAD_V * (j + 1)], False, QK_DTYPE, 1, stack)
                          for j in range(GQA_GROUP)], axis=0)
    _attention_sweeps([qs], [slice(0, stack)], [((0,), slice(0, 2 * LANES))], k8_refs, v8_refs,
                      m_ref, smax_ref, acc_ref, P_E4M3)
    o = _attention_output(acc_ref[...], scl_ref)
    o_ref[0] = jnp.concatenate([o[j * tq:(j + 1) * tq] for j in range(GQA_GROUP)], axis=1).astype(BF16)


def _attention_output(acc, scl_ref):
    return acc[:, :HEAD_V] * scl_ref[0, SCL_V_INV:SCL_V_INV + 1, 0:1] / acc[:, HEAD_V:]


def _softmax_scratch(rows):
    return [pltpu.VMEM((rows, LANES), F32), pltpu.VMEM((rows, LANES), BF16), pltpu.VMEM((rows, LANES), F32)]


def _stack_height(dk):
    return max(4 * dk, LANES)


def _fp8_kv_scratch(k_blocks, v_blocks, dk):
    ks = [pltpu.VMEM((kb[1], kb[2] // dk * _stack_height(dk), kb[3]), QK_DTYPE) for kb in k_blocks]
    vs = [pltpu.VMEM((vb[1], 2 * vb[2]), V_DTYPE) for vb in v_blocks]
    return ks + vs


def _gqa_attention(q, k_parts, v_parts, scales, tq):
    b, sq, _ = q.shape
    n_parts = len(k_parts)
    k_blocks = [(1, kp.shape[1], HEAD_V, kp.shape[3]) for kp in k_parts]
    v_blocks = [(1, vp.shape[1], LANES) for vp in v_parts]
    in_specs = [pl.BlockSpec((1, tq, 256), lambda bb, g, i: (bb, i, g))]
    in_specs += [pl.BlockSpec(kb, lambda bb, g, i: (bb, 0, g, 0)) for kb in k_blocks]
    in_specs += [pl.BlockSpec(vb, lambda bb, g, i: (bb, 0, g)) for vb in v_blocks]
    in_specs.append(pl.BlockSpec((1, SUBLANES, LANES), lambda bb, g, i: (bb, 0, 0)))
    return pl.pallas_call(
        functools.partial(_gqa_kernel, n_parts=n_parts, tq=tq),
        out_shape=jax.ShapeDtypeStruct((b, sq, 512), BF16),
        grid=(b, GQA_KV, sq // tq),
        in_specs=in_specs,
        out_specs=pl.BlockSpec((1, tq, 256), lambda bb, g, i: (bb, i, g)),
        scratch_shapes=_softmax_scratch(GQA_GROUP * tq) + _fp8_kv_scratch(k_blocks, v_blocks, HEAD_V),
        compiler_params=_cparams(("arbitrary", "arbitrary", "arbitrary")),
        name="gqa_attention",
    )(q, *k_parts, *v_parts, scales)


def _diff_kernel(*refs, n_parts, tq, lam_init):
    q_ref = refs[0]
    k_refs = refs[1:1 + n_parts]
    v_refs = refs[1 + n_parts:1 + 2 * n_parts]
    scl_ref, lam_ref, sg_ref, hm_ref, o_ref, m_ref, smax_ref, acc_ref = refs[1 + 2 * n_parts:9 + 2 * n_parts]
    k8_refs = refs[9 + 2 * n_parts:9 + 3 * n_parts]
    v8_refs = refs[9 + 3 * n_parts:]

    @pl.when(pl.program_id(2) == 0)
    def _():
        _quantize_kv(k_refs, v_refs, k8_refs, v8_refs, scl_ref, DIFF_QK)

    stack = k8_refs[0].shape[1] // 4
    q = q_ref[0].astype(F32) * scl_ref[0, SCL_Q:SCL_Q + 1, 0:1]
    qmaps = [_stack_qk(q[:, DIFF_QK * j:DIFF_QK * (j + 1)], False, QK_DTYPE, 1, stack) for j in range(4)]
    k_slices = [slice(stack * j, stack * (j + 1)) for j in range(4)]
    pv_groups = [((0, 1), slice(0, 2 * LANES)), ((2, 3), slice(2 * LANES, 4 * LANES))]
    _attention_sweeps(qmaps, k_slices, pv_groups, k8_refs, v8_refs, m_ref, smax_ref, acc_ref, P_E5M2)

    lv = lam_ref[...]
    lam = (jnp.exp(jnp.sum(lv[0:1] * lv[1:2], axis=-1, keepdims=True))
           - jnp.exp(jnp.sum(lv[2:3] * lv[3:4], axis=-1, keepdims=True)) + lam_init)
    outs = _attention_output(acc_ref[...], scl_ref)
    o = jnp.concatenate([outs[(2 * hh) * tq:(2 * hh + 1) * tq] - lam * outs[(2 * hh + 1) * tq:(2 * hh + 2) * tq]
                         for hh in range(2)], axis=1)
    ms = _dot((o * o).astype(BF16), hm_ref[...])
    o_ref[0] = (o * lax.rsqrt(ms + EPS) * sg_ref[...] * (1.0 - lam_init)).astype(BF16)


def _diff_attention(q, k_parts, v_parts, scales, lam_vecs, subln_g, lam_init, tq):
    b, sq, _ = q.shape
    n_parts = len(k_parts)
    k_blocks = [(1, kp.shape[1], LANES, kp.shape[3]) for kp in k_parts]
    v_blocks = [(1, vp.shape[1], 2 * LANES) for vp in v_parts]
    in_specs = [pl.BlockSpec((1, tq, LANES), lambda bb, p, i: (bb, i, p))]
    in_specs += [pl.BlockSpec(kb, lambda bb, p, i: (bb, 0, p, 0)) for kb in k_blocks]
    in_specs += [pl.BlockSpec(vb, lambda bb, p, i: (bb, 0, p)) for vb in v_blocks]
    in_specs.append(pl.BlockSpec((1, SUBLANES, LANES), lambda bb, p, i: (bb, 0, 0)))
    in_specs.append(pl.BlockSpec((4, DIFF_QK), lambda bb, p, i: (0, 0)))
    in_specs.append(pl.BlockSpec((1, LANES), lambda bb, p, i: (0, 0)))
    in_specs.append(pl.BlockSpec((LANES, LANES), lambda bb, p, i: (0, 0)))
    head_mean = _head_mean_matrix()[:LANES, :LANES]
    subln_pair = jnp.tile(subln_g, (1, LANES // HEAD_V))
    return pl.pallas_call(
        functools.partial(_diff_kernel, n_parts=n_parts, tq=tq, lam_init=lam_init),
        out_shape=jax.ShapeDtypeStruct((b, sq, 256), BF16),
        grid=(b, DIFF_HEADS // 2, sq // tq),
        in_specs=in_specs,
        out_specs=pl.BlockSpec((1, tq, LANES), lambda bb, p, i: (bb, i, p)),
        scratch_shapes=_softmax_scratch(4 * tq) + _fp8_kv_scratch(k_blocks, v_blocks, DIFF_QK),
        compiler_params=_cparams(("arbitrary", "arbitrary", "arbitrary")),
        name="diff_attention",
    )(q, *k_parts, *v_parts, scales, lam_vecs, subln_pair, head_mean)


CONV_HALO = 16
CONV_ROWS = 64


def _conv_kernel(gb_ref, prev_ref, next_ref, w_ref, b_ref, lg_ref, lb_ref, o_ref, u_ref, sh_ref, *, tm):
    i = pl.program_id(1)
    last = pl.num_programs(1) - 1
    ch = w_ref.shape[1]

    def glu(z):
        z = z.astype(F32)
        return z[:, :ch] * _sigmoid(z[:, ch:])

    u_ref[CONV_HALO:CONV_HALO + tm, :] = glu(gb_ref[0])
    u_ref[0:CONV_HALO, :] = jnp.where(i > 0, glu(prev_ref[0]), 0.0)
    u_ref[CONV_HALO + tm:2 * CONV_HALO + tm, :] = jnp.where(i < last, glu(next_ref[0]), 0.0)

    span = sh_ref.shape[1]
    for r in range(1, SUBLANES):
        sh_ref[r - 1] = u_ref[r:r + span, :]

    off = CONV_HALO - CONV_K // 2
    for r0 in range(0, tm, CONV_ROWS):
        acc = jnp.zeros((CONV_ROWS, ch), F32)
        for j in range(CONV_K):
            phase, base = (off + j) % SUBLANES, r0 + (off + j) // SUBLANES * SUBLANES
            taps = u_ref[base:base + CONV_ROWS, :] if phase == 0 else sh_ref[phase - 1, base:base + CONV_ROWS, :]
            acc = acc + taps * w_ref[j:j + 1, :]
        y = acc + b_ref[...]
        mu = jnp.mean(y, axis=-1, keepdims=True)
        yc = y - mu
        var = jnp.mean(yc * yc, axis=-1, keepdims=True)
        z = yc * lax.rsqrt(var + EPS) * lg_ref[...] + lb_ref[...]
        o_ref[0, r0:r0 + CONV_ROWS, :] = (z * _sigmoid(z)).astype(BF16)


def _conformer_conv(gb, w, bias, ln_g, ln_b, tm):
    b, s, two_ch = gb.shape
    ch = two_ch // 2
    hb = tm // CONV_HALO
    n_halo = s // CONV_HALO
    const2 = lambda bb, i: (0, 0)
    return pl.pallas_call(
        functools.partial(_conv_kernel, tm=tm),
        out_shape=jax.ShapeDtypeStruct((b, s, ch), BF16),
        grid=(b, s // tm),
        in_specs=[
            pl.BlockSpec((1, tm, two_ch), lambda bb, i: (bb, i, 0)),
            pl.BlockSpec((1, CONV_HALO, two_ch), lambda bb, i: (bb, jnp.maximum(i * hb - 1, 0), 0)),
            pl.BlockSpec((1, CONV_HALO, two_ch), lambda bb, i: (bb, jnp.minimum((i + 1) * hb, n_halo - 1), 0)),
            pl.BlockSpec((CONV_K, ch), const2), pl.BlockSpec((1, ch), const2),
            pl.BlockSpec((1, ch), const2), pl.BlockSpec((1, ch), const2),
        ],
        out_specs=pl.BlockSpec((1, tm, ch), lambda bb, i: (bb, i, 0)),
        scratch_shapes=[pltpu.VMEM((tm + 2 * CONV_HALO, ch), F32),
                        pltpu.VMEM((SUBLANES - 1, tm + 2 * CONV_HALO - SUBLANES, ch), F32)],
        compiler_params=_cparams(("arbitrary", "arbitrary")),
        name="conformer_conv",
    )(gb, gb, gb, w, bias, ln_g, ln_b)


def _merge_kernel(oa_ref, ob_ref, oc_ref, x_ref, gate_ref, shift_ref, scale_ref, g2_ref, w_ref, rw_ref,
                  xo_ref, h2_ref, lg_ref):
    tm = x_ref.shape[1]
    wa = oa_ref.shape[2]
    wb = wa + ob_ref.shape[2]
    y = _dot(oa_ref[0], w_ref[0:wa, :]) + _dot(ob_ref[0], w_ref[wa:wb, :]) + _dot(oc_ref[0], w_ref[wb:, :])
    xn = x_ref[0] + gate_ref[0] * y
    xo_ref[0] = xn
    ms = jnp.mean(xn * xn, axis=-1, keepdims=True)
    h2 = xn * lax.rsqrt(ms + EPS) * g2_ref[...] * (1.0 + scale_ref[0]) + shift_ref[0]
    _store_row_tiles(h2_ref, h2, tm)
    lg_ref[...] = lax.dot_general(rw_ref[...], h2, (((1,), (1,)), ((), ())),
                                  preferred_element_type=F32, precision=lax.Precision.HIGHEST)


def _merge_route(oa, ob, oc, x, gate, shift, scale, g2, w_out, router_wt, tm):
    b, s, d = x.shape
    row = lambda bb, i: (bb, i, 0)
    per_b = lambda bb, i: (bb, 0, 0)
    const2 = lambda bb, i: (0, 0)
    nt = s // tm
    in_specs = [
        pl.BlockSpec((1, tm, oa.shape[2]), row), pl.BlockSpec((1, tm, ob.shape[2]), row),
        pl.BlockSpec((1, tm, oc.shape[2]), row), pl.BlockSpec((1, tm, d), row),
        pl.BlockSpec((1, 1, d), per_b), pl.BlockSpec((1, 1, d), per_b), pl.BlockSpec((1, 1, d), per_b),
        pl.BlockSpec((1, d), const2), pl.BlockSpec((d, d), const2),
        pl.BlockSpec((N_EXPERTS, d), const2),
    ]
    out_shape = (jax.ShapeDtypeStruct((b, s, d), F32),
                 jax.ShapeDtypeStruct((b * s * ROW_TILE, LANES), F32),
                 jax.ShapeDtypeStruct((N_EXPERTS, b * s), F32))
    out_specs = (pl.BlockSpec((1, tm, d), row),
                 pl.BlockSpec((tm * ROW_TILE, LANES), lambda bb, i: (bb * nt + i, 0)),
                 pl.BlockSpec((N_EXPERTS, tm), lambda bb, i: (0, bb * nt + i)))
    return pl.pallas_call(
        _merge_kernel, out_shape=out_shape, grid=(b, nt), in_specs=in_specs, out_specs=out_specs,
        compiler_params=_cparams(("arbitrary", "arbitrary")), name="merge_route",
    )(oa, ob, oc, x, gate, shift, scale, g2, w_out, router_wt)


def _merge_ffn_kernel(oa_ref, ob_ref, oc_ref, x_ref, gate1_ref, shift_ref, scale_ref, g2_ref, w_ref, gate2_ref,
                      wg_ref, wu_ref, wd_ref, o_ref, *, tf):
    wa = oa_ref.shape[2]
    wb = wa + ob_ref.shape[2]
    y = _dot(oa_ref[0], w_ref[0:wa, :]) + _dot(ob_ref[0], w_ref[wa:wb, :]) + _dot(oc_ref[0], w_ref[wb:, :])
    xn = x_ref[0] + gate1_ref[0] * y
    ms = jnp.mean(xn * xn, axis=-1, keepdims=True)
    h = (xn * lax.rsqrt(ms + EPS) * g2_ref[...] * (1.0 + scale_ref[0]) + shift_ref[0]).astype(BF16)
    ff = wg_ref.shape[1]
    acc = jnp.zeros(xn.shape, F32)
    for f in range(0, ff, tf):
        g = _dot(h, wg_ref[:, f:f + tf])
        u = _dot(h, wu_ref[:, f:f + tf])
        a = (g * _sigmoid(g) * u).astype(BF16)
        acc = acc + _dot(a, wd_ref[f:f + tf, :])
    o_ref[0] = xn + gate2_ref[0] * acc


def _merge_dense_ffn(oa, ob, oc, x, gate1, shift, scale, g2, w_out, gate2, wg, wu, wd, tm):
    b, s, d = x.shape
    ff = wg.shape[1]
    row = lambda bb, i: (bb, i, 0)
    per_b = lambda bb, i: (bb, 0, 0)
    const2 = lambda bb, i: (0, 0)
    resident = pl.Buffered(1)
    return pl.pallas_call(
        functools.partial(_merge_ffn_kernel, tf=MXU_DIM),
        out_shape=jax.ShapeDtypeStruct((b, s, d), F32),
        grid=(b, s // tm),
        in_specs=[
            pl.BlockSpec((1, tm, oa.shape[2]), row), pl.BlockSpec((1, tm, ob.shape[2]), row),
            pl.BlockSpec((1, tm, oc.shape[2]), row), pl.BlockSpec((1, tm, d), row),
            pl.BlockSpec((1, 1, d), per_b), pl.BlockSpec((1, 1, d), per_b), pl.BlockSpec((1, 1, d), per_b),
            pl.BlockSpec((1, d), const2), pl.BlockSpec((d, d), const2, pipeline_mode=resident),
            pl.BlockSpec((1, 1, d), per_b),
            pl.BlockSpec((d, ff), const2, pipeline_mode=resident),
            pl.BlockSpec((d, ff), const2, pipeline_mode=resident),
            pl.BlockSpec((ff, d), const2, pipeline_mode=resident),
        ],
        out_specs=pl.BlockSpec((1, tm, d), row),
        compiler_params=_cparams(("arbitrary", "arbitrary")), name="merge_dense_ffn",
    )(oa, ob, oc, x, gate1, shift, scale, g2, w_out, gate2, wg, wu, wd)


def _top2(lg):
    sub = lax.broadcasted_iota(I32, lg.shape, 0)
    l1 = jnp.max(lg, axis=0, keepdims=True)
    i1 = jnp.min(jnp.where(lg == l1, sub, N_EXPERTS), axis=0, keepdims=True)
    m1 = sub == i1
    lg2 = jnp.where(m1, -jnp.inf, lg)
    l2 = jnp.max(lg2, axis=0, keepdims=True)
    i2 = jnp.min(jnp.where(lg2 == l2, sub, N_EXPERTS), axis=0, keepdims=True)
    m2 = sub == i2
    return l1, l2, m1, m2


def _sublane_cumsum(x):
    sub = lax.broadcasted_iota(I32, x.shape, 0)
    for sh in (1, 2, 4):
        x = x + jnp.where(sub >= sh, pltpu.roll(x, sh, 0), 0.0)
    return x


def _route_kernel(lg_ref, tri_ref, dest_ref, gates_ref, be_ref, pad_ref, base_ref, start_ref, *, block_rows, total_rows):
    phase = pl.program_id(0)
    j = pl.program_id(1)
    l1, l2, m1, m2 = _top2(lg_ref[...])
    e = jnp.where(m1 | m2, 1.0, 0.0).astype(F32)
    cnt = jnp.sum(e, axis=1, keepdims=True)

    @pl.when((phase == 0) & (j == 0))
    def _():
        base_ref[...] = jnp.zeros(base_ref.shape, F32)

    @pl.when((phase == 1) & (j == 0))
    def _():
        counts = base_ref[...]
        nblk = jnp.floor((counts + (block_rows - 1)) * (1.0 / block_rows))
        end_blk = _sublane_cumsum(nblk)
        start_ref[...] = (end_blk - nblk) * block_rows
        blk = lax.broadcasted_iota(I32, be_ref.shape, 1).astype(F32)
        owner = jnp.sum(jnp.where(end_blk[:, :1] <= blk, 1.0, 0.0), axis=0, keepdims=True)
        be_ref[...] = jnp.broadcast_to(jnp.minimum(owner, N_EXPERTS - 1.0), be_ref.shape).astype(I32)
        sub = lax.broadcasted_iota(I32, pad_ref.shape, 0)
        lane = lax.broadcasted_iota(I32, pad_ref.shape, 1)
        pad_end = jnp.where(sub == N_EXPERTS - 1, float(total_rows), end_blk * block_rows)
        pad_ref[...] = jnp.where(lane < LANES // 2, start_ref[...] + counts, pad_end).astype(I32)
        base_ref[...] = jnp.zeros(base_ref.shape, F32)

    @pl.when(phase == 1)
    def _():
        prefix = _dot(e.astype(BF16), tri_ref[...]) + base_ref[:, :1] + start_ref[:, :1]
        d1 = jnp.sum(jnp.where(m1, prefix, 0.0), axis=0, keepdims=True)
        d2 = jnp.sum(jnp.where(m2, prefix, 0.0), axis=0, keepdims=True)
        sub = lax.broadcasted_iota(I32, dest_ref.shape, 0)
        dest_ref[...] = jnp.where(sub == 0, d1, jnp.where(sub == 1, d2, 0.0)).astype(I32)
        ex = jnp.exp(l2 - l1)
        g1 = 1.0 / (1.0 + ex)
        g2 = ex / (1.0 + ex)
        half = lax.broadcasted_iota(I32, (LANES, lg_ref.shape[1]), 0) < LANES // 2
        gates_ref[...] = jnp.where(half, g1, g2).T

    base_ref[...] = base_ref[...] + cnt


def _route(logits_t, block_rows, n_blocks, tr):
    n = logits_t.shape[1]
    n_blocks_pad = -(-n_blocks // LANES) * LANES
    tri = jnp.asarray(np.triu(np.ones((tr, tr), np.float32), k=1), BF16)
    return pl.pallas_call(
        functools.partial(_route_kernel, block_rows=block_rows, total_rows=n_blocks * block_rows),
        out_shape=(jax.ShapeDtypeStruct((N_EXPERTS, n), I32),
                   jax.ShapeDtypeStruct((n, LANES), F32),
                   jax.ShapeDtypeStruct((N_EXPERTS, n_blocks_pad), I32),
                   jax.ShapeDtypeStruct((N_EXPERTS, LANES), I32)),
        grid=(2, n // tr),
        in_specs=[pl.BlockSpec((N_EXPERTS, tr), lambda p, j: (0, j)),
                  pl.BlockSpec((tr, tr), lambda p, j: (0, 0))],
        out_specs=(pl.BlockSpec((N_EXPERTS, tr), lambda p, j: (0, j * p)),
                   pl.BlockSpec((tr, LANES), lambda p, j: (j * p, 0)),
                   pl.BlockSpec((N_EXPERTS, n_blocks_pad), lambda p, j: (0, 0)),
                   pl.BlockSpec((N_EXPERTS, LANES), lambda p, j: (0, 0))),
        scratch_shapes=[pltpu.VMEM((N_EXPERTS, LANES), F32), pltpu.VMEM((N_EXPERTS, LANES), F32)],
        compiler_params=_cparams(("arbitrary", "arbitrary")), name="moe_route",
    )(logits_t, tri)


def _row_copy(src_hbm, src_row, dst_hbm, dst_row, sem):
    src = pl.ds(pl.multiple_of(src_row * ROW_TILE, ROW_TILE), ROW_TILE)
    dst = pl.ds(pl.multiple_of(dst_row * ROW_TILE, ROW_TILE), ROW_TILE)
    return pltpu.make_async_copy(src_hbm.at[src], dst_hbm.at[dst], sem)


def _scatter_kernel(pad_lo_ref, pad_hi_ref, d1_ref, d2_ref, src_ref, out_hbm, zero_ref, sem, zero_sem, *, rows):
    @pl.when(pl.program_id(0) == 0)
    def _():
        zero_ref[...] = jnp.zeros(zero_ref.shape, zero_ref.dtype)
        for e in range(N_EXPERTS):
            def fill(row, c):
                _row_copy(zero_ref, 0, out_hbm, row, zero_sem).start()
                return c

            def drain(row, c):
                _row_copy(zero_ref, 0, out_hbm, 0, zero_sem).wait()
                return c

            lax.fori_loop(pad_lo_ref[e], pad_hi_ref[e], fill, 0)
            lax.fori_loop(pad_lo_ref[e], pad_hi_ref[e], drain, 0)

    def start(r, c):
        _row_copy(src_ref, r, out_hbm, d1_ref[0, 0, r], sem).start(priority=0)
        _row_copy(src_ref, r, out_hbm, d2_ref[0, 0, r], sem).start(priority=1)
        return c

    lax.fori_loop(0, rows, start, 0, unroll=DMA_UNROLL)
    for _ in range(2):
        pltpu.make_async_copy(src_ref, out_hbm.at[pl.ds(0, rows * ROW_TILE)], sem).wait()


def _scatter_rows(src, d1, d2, pad_lo, pad_hi, total_rows, rows):
    n = d1.shape[0]
    idx_spec = pl.BlockSpec((1, 1, rows), lambda i, lo, hi: (i, 0, 0), memory_space=pltpu.SMEM)
    any_spec = pl.BlockSpec(memory_space=pl.ANY)
    grid_spec = pltpu.PrefetchScalarGridSpec(
        num_scalar_prefetch=2,
        grid=(n // rows,),
        in_specs=[idx_spec, idx_spec, pl.BlockSpec((rows * ROW_TILE, LANES), lambda i, lo, hi: (i, 0))],
        out_specs=any_spec,
        scratch_shapes=[pltpu.VMEM((ROW_TILE, LANES), src.dtype), pltpu.SemaphoreType.DMA(()),
                        pltpu.SemaphoreType.DMA(())],
    )
    return pl.pallas_call(
        functools.partial(_scatter_kernel, rows=rows),
        out_shape=jax.ShapeDtypeStruct((total_rows * ROW_TILE, LANES), src.dtype),
        grid_spec=grid_spec,
        compiler_params=pltpu.CompilerParams(dimension_semantics=("arbitrary",), has_side_effects=True),
        name="moe_scatter_rows",
    )(pad_lo, pad_hi, d1.reshape(n // rows, 1, rows), d2.reshape(n // rows, 1, rows), src)


def _expert_kernel(be_ref, x_ref, wg_ref, wu_ref, wd_ref, o_ref, *, block_rows, tf):
    del be_ref
    x = _load_row_tiles(x_ref, block_rows).astype(BF16)
    ff = wg_ref.shape[2]
    acc = jnp.zeros((block_rows, wd_ref.shape[2]), F32)
    for f in range(0, ff, tf):
        g = _dot(x, wg_ref[0, :, f:f + tf])
        u = _dot(x, wu_ref[0, :, f:f + tf])
        a = (g * _sigmoid(g) * u).astype(BF16)
        acc = acc + _dot(a, wd_ref[0, f:f + tf, :])
    _store_row_tiles(o_ref, acc, block_rows)


def _expert_ffn(xb, blk_expert, wg, wu, wd, block_rows):
    d, ff = wg.shape[1], wg.shape[2]
    rows = xb.shape[0] // ROW_TILE
    resident = pl.Buffered(1)
    grid_spec = pltpu.PrefetchScalarGridSpec(
        num_scalar_prefetch=1,
        grid=(rows // block_rows,),
        in_specs=[
            pl.BlockSpec((block_rows * ROW_TILE, LANES), lambda i, be: (i, 0)),
            pl.BlockSpec((1, d, ff), lambda i, be: (be[i], 0, 0), pipeline_mode=resident),
            pl.BlockSpec((1, d, ff), lambda i, be: (be[i], 0, 0), pipeline_mode=resident),
            pl.BlockSpec((1, ff, d), lambda i, be: (be[i], 0, 0), pipeline_mode=resident),
        ],
        out_specs=pl.BlockSpec((block_rows * ROW_TILE, LANES), lambda i, be: (i, 0)),
    )
    return pl.pallas_call(
        functools.partial(_expert_kernel, block_rows=block_rows, tf=MXU_DIM),
        out_shape=jax.ShapeDtypeStruct(xb.shape, F32), grid_spec=grid_spec,
        compiler_params=_cparams(("arbitrary",)), name="moe_expert_ffn",
    )(blk_expert, xb, wg, wu, wd)


def _combine_kernel(d1_ref, d2_ref, d1n_ref, d2n_ref, x_ref, yb_hbm, gates_ref, gate_ref, fg_ref, o_ref,
                    y1_ref, y2_ref, sems):
    tm = x_ref.shape[0]
    i = pl.program_id(0)
    slot = i % 2

    def gather(i1_ref, i2_ref, to_slot):
        def start(r, c):
            _row_copy(yb_hbm, i1_ref[0, 0, r], y1_ref.at[to_slot], r, sems.at[to_slot]).start(priority=0)
            _row_copy(yb_hbm, i2_ref[0, 0, r], y2_ref.at[to_slot], r, sems.at[to_slot]).start(priority=1)
            return c

        lax.fori_loop(0, tm, start, 0, unroll=DMA_UNROLL)

    @pl.when(i == 0)
    def _():
        gather(d1_ref, d2_ref, slot)

    @pl.when(i + 1 < pl.num_programs(0))
    def _():
        gather(d1n_ref, d2n_ref, 1 - slot)

    for y_ref in (y1_ref, y2_ref):
        pltpu.make_async_copy(yb_hbm.at[pl.ds(0, tm * ROW_TILE)], y_ref.at[slot], sems.at[slot]).wait()

    gts = gates_ref[...]
    y = (gts[:, 0:1] * _load_row_tiles(y1_ref.at[slot], tm)
         + gts[:, LANES // 2:LANES // 2 + 1] * _load_row_tiles(y2_ref.at[slot], tm))
    xn = x_ref[...] + gate_ref[0] * y
    ms = jnp.mean(xn * xn, axis=-1, keepdims=True)
    o_ref[...] = xn * lax.rsqrt(ms + EPS) * fg_ref[...]


def _combine_final(x, yb, d1, d2, gates, gate, final_g, tm):
    n, d = x.shape
    s = n // gate.shape[0]
    row = lambda i: (i, 0)
    steps = n // tm
    idx_spec = pl.BlockSpec((1, 1, tm), lambda i: (i, 0, 0), memory_space=pltpu.SMEM)
    next_spec = pl.BlockSpec((1, 1, tm), lambda i: (jnp.minimum(i + 1, steps - 1), 0, 0), memory_space=pltpu.SMEM)
    d1 = d1.reshape(steps, 1, tm)
    d2 = d2.reshape(steps, 1, tm)
    return pl.pallas_call(
        _combine_kernel,
        out_shape=jax.ShapeDtypeStruct((n, d), F32),
        grid=(steps,),
        in_specs=[idx_spec, idx_spec, next_spec, next_spec,
                  pl.BlockSpec((tm, d), row),
                  pl.BlockSpec(memory_space=pl.ANY),
                  pl.BlockSpec((tm, LANES), row),
                  pl.BlockSpec((1, 1, d), lambda i: ((i * tm) // s, 0, 0)),
                  pl.BlockSpec((1, d), lambda i: (0, 0))],
        out_specs=pl.BlockSpec((tm, d), row),
        scratch_shapes=[pltpu.VMEM((2, tm * ROW_TILE, LANES), F32), pltpu.VMEM((2, tm * ROW_TILE, LANES), F32),
                        pltpu.SemaphoreType.DMA((2,))],
        compiler_params=_cparams(("arbitrary",)), name="moe_combine_final",
    )(d1, d2, d1, d2, x, yb, gates, gate, final_g)


AMAX_QA, AMAX_KA, AMAX_VA, AMAX_QC, AMAX_KC, AMAX_VC = range(6)
FP8_TARGET_MAX = 256.0


def _fp8_scales(q_max, k_max, v_max, head_dim):
    tiny = jnp.finfo(F32).tiny
    c = head_dim ** -0.5 * LOG2E
    ratio = jnp.where((q_max > 0) & (k_max > 0), c * k_max / jnp.maximum(q_max, tiny), 1.0)
    sq = jnp.exp2(jnp.round(0.5 * jnp.log2(ratio)))
    sk = c / sq
    sv = jnp.where(v_max > 0, jnp.exp2(jnp.floor(jnp.log2(FP8_TARGET_MAX / jnp.maximum(v_max, tiny)))), 1.0)
    rows = jnp.stack([sq, sk, sv, 1.0 / sv] + [jnp.zeros_like(sq)] * (SUBLANES - 4), axis=1)
    return jnp.broadcast_to(rows[:, :, None], rows.shape + (LANES,)).astype(F32)


def _rope_tables(s, dim):
    half = dim // 2
    t = jnp.arange(s)
    inv = 1.0 / (ROPE_THETA ** (jnp.arange(0, half, 2, dtype=F32) / half))
    ang_r = (t // GRID_W).astype(F32)[:, None] * inv
    ang_c = (t % GRID_W).astype(F32)[:, None] * inv
    ang = jnp.concatenate([ang_r, ang_r, ang_c, ang_c], axis=-1)
    reps = LANES // dim
    return jnp.tile(jnp.cos(ang), (1, reps)), jnp.tile(jnp.sin(ang), (1, reps))


def _rotate_matrix(dim):
    q = dim // 4
    p = np.zeros((MXU_DIM, MXU_DIM), np.float32)
    for j in range(MXU_DIM):
        if (j % (2 * q)) < q:
            p[j + q, j] = -1.0
        else:
            p[j - q, j] = 1.0
    return jnp.asarray(p, BF16)


def _head_mean_matrix():
    m = np.kron(np.eye(MXU_DIM // HEAD_V, dtype=np.float32), np.full((HEAD_V, HEAD_V), 1.0 / HEAD_V, np.float32))
    return jnp.asarray(m, BF16)


def _widen_values(w, heads):
    d = w.shape[0]
    w = w.reshape(d, heads, HEAD_V)
    return jnp.concatenate([w, jnp.zeros_like(w)], axis=-1).reshape(d, heads * LANES)


def _widen_in_proj(w):
    qa, ka, va, gb, qc, kc, vc = jnp.split(w, [256, 512, 768, 1280, 1792, 1920], axis=1)
    return jnp.concatenate([qa, ka, _widen_values(va, DIFF_HEADS), gb, qc, kc, _widen_values(vc, GQA_KV)],
                           axis=1).astype(BF16)


def kernel(x, c, ctx, c_ctx, ada_w, ada_b, norm1_g, norm2_g, w_in, w_out, lam_q1, lam_k1, lam_q2, lam_k2,
           diff_subln_g, conv_w, conv_b, conv_ln_g, conv_ln_b, q_norm_g, k_norm_g, ffn_gate, ffn_up, ffn_down,
           router_w, moe_gate, moe_up, moe_down, final_g):
    b, s, d = x.shape
    sc = ctx.shape[1]
    depth = ada_w.shape[0]
    n = b * s
    assert depth % 2 == 0, "the final RMSNorm is fused into the MoE combine of the last (odd) layer"

    tm = min(512, s)
    tmc = min(512, sc)
    tq = min(256, s)
    tq_diff = min(512, s)
    tqc = min(256, sc)

    tabs_x = _rope_tables(s, DIFF_QK) + _rope_tables(s, HEAD_V)
    ones_c, zeros_c = jnp.ones((sc, LANES), F32), jnp.zeros((sc, LANES), F32)
    tabs_c = (ones_c, zeros_c, ones_c, zeros_c)
    mats = (_rotate_matrix(DIFF_QK), _rotate_matrix(HEAD_V), _head_mean_matrix())

    cc = jnp.zeros((16, d), F32).at[:b].set(c).at[b].set(c_ctx)

    for i in range(depth):
        last = i == depth - 1
        lam_init = 0.8 - 0.6 * math.exp(-0.3 * i)
        mod_all = _ada_mod(cc, ada_w[i], ada_b[i])
        mod = mod_all[:b].reshape(b, 6, 1, d)
        modc = jnp.broadcast_to(mod_all[b].reshape(1, 6, 1, d), (b, 6, 1, d))

        w_aug = _widen_in_proj(w_in[i])
        g1 = norm1_g[i].reshape(1, d)
        qg = jnp.tile(q_norm_g[i], GQA_HEADS).reshape(1, -1)
        kg = jnp.tile(k_norm_g[i], GQA_KV).reshape(1, -1)
        lam_vecs = jnp.stack([lam_q1[i], lam_k1[i], lam_q2[i], lam_k2[i]]).astype(F32)
        subln = diff_subln_g[i].reshape(1, HEAD_V)
        conv_args = (conv_w[i], conv_b[i].reshape(1, -1), conv_ln_g[i].reshape(1, -1), conv_ln_b[i].reshape(1, -1))
        w_o = w_out[i].astype(BF16)
        g2 = norm2_g[i].reshape(1, d)

        qa, kat, va, gb, qc, kct, vc, amax = _in_projection(
            x, mod[:, 0], mod[:, 1], g1, w_aug, tabs_x, mats, qg, kg, tm, 2 if s % (2 * tm) == 0 else 1)
        qa_x, kat_x, va_x, gb_x, qc_x, kct_x, vc_x, amax_x = _in_projection(
            ctx, modc[:, 0], modc[:, 1], g1, w_aug, tabs_c, mats, qg, kg, tmc, 1)
        amax = jnp.max(amax, axis=(1, 3))
        amax_x = jnp.max(amax_x, axis=(1, 3))
        amax_kv = jnp.maximum(amax, amax_x)
        scl_a = _fp8_scales(amax[:, AMAX_QA], amax_kv[:, AMAX_KA], amax_kv[:, AMAX_VA], DIFF_QK)
        scl_c = _fp8_scales(amax[:, AMAX_QC], amax_kv[:, AMAX_KC], amax_kv[:, AMAX_VC], HEAD_V)

        oa = _diff_attention(qa, [kat, kat_x], [va, va_x], scl_a, lam_vecs, subln, lam_init, tq_diff)
        ob = _conformer_conv(gb, *conv_args, tm)
        oc = _gqa_attention(qc, [kct, kct_x], [vc, vc_x], scl_c, tq)

        j = i // 2
        if i % 2 == 0:
            wg, wu, wd = ffn_gate[j].astype(BF16), ffn_up[j].astype(BF16), ffn_down[j].astype(BF16)
            x = _merge_dense_ffn(oa, ob, oc, x, mod[:, 2], mod[:, 3], mod[:, 4], g2, w_o, mod[:, 5], wg, wu, wd, tm)
        else:
            rwt = router_w[j].T.astype(F32)
            x, h2, logits_t = _merge_route(oa, ob, oc, x, mod[:, 2], mod[:, 3], mod[:, 4], g2, w_o, rwt, tm)
            block_rows = 512 if n >= 8192 else 256
            n_blocks = (2 * n) // block_rows + N_EXPERTS
            dest, gates, blk_e, pad = _route(logits_t, block_rows, n_blocks, min(512, n))
            xb = _scatter_rows(h2, dest[0], dest[1], pad[:, 0], pad[:, LANES // 2], n_blocks * block_rows,
                               min(1024, n))
            yb = _expert_ffn(xb, blk_e[0, :n_blocks], moe_gate[j].astype(BF16), moe_up[j].astype(BF16),
                             moe_down[j].astype(BF16), block_rows)
            assert last
            x = _combine_final(x.reshape(n, d), yb, dest[0], dest[1], gates, mod[:, 5], final_g.reshape(1, d),
                               tm).reshape(b, s, d)

        if not last:
            scl_ax = _fp8_scales(amax_x[:, AMAX_QA], amax_x[:, AMAX_KA], amax_x[:, AMAX_VA], DIFF_QK)
            scl_cx = _fp8_scales(amax_x[:, AMAX_QC], amax_x[:, AMAX_KC], amax_x[:, AMAX_VC], HEAD_V)
            oa_x = _diff_attention(qa_x, [kat_x], [va_x], scl_ax, lam_vecs, subln, lam_init, tqc)
            ob_x = _conformer_conv(gb_x, *conv_args, tmc)
            oc_x = _gqa_attention(qc_x, [kct_x], [vc_x], scl_cx, tqc)
            assert i % 2 == 0, "context tokens only ever pass through dense channel mixers"
            ctx = _merge_dense_ffn(oa_x, ob_x, oc_x, ctx, modc[:, 2], modc[:, 3], modc[:, 4], g2, w_o, modc[:, 5],
                                   wg, wu, wd, tmc)

    return x
```

```python
import functools
import math

import numpy as np
import jax
import jax.numpy as jnp
from jax import lax
from jax.experimental import pallas as pl
from jax.experimental.pallas import tpu as pltpu

F32 = jnp.float32
BF16 = jnp.bfloat16
I32 = jnp.int32

EPS = 1e-6
ROPE_THETA = 10000.0
GRID_W = 64

DIFF_HEADS = 4
DIFF_QK = 32
HEAD_V = 64
GQA_HEADS = 8
GQA_KV = 2
GQA_GROUP = GQA_HEADS // GQA_KV
CONV_K = 31
N_EXPERTS = 8
LOG2E = math.log2(math.e)

LANES = 128
SUBLANES = 8
MXU_DIM = 256
VMEM_LIMIT = 52 * 1024 * 1024
NEG_BIG = -1e30
V_DTYPE = jnp.float8_e4m3fn
QK_DTYPE = jnp.float8_e4m3fn
P_E4M3 = (jnp.float8_e4m3fn, 8.5)
P_E5M2 = (jnp.float8_e5m2, 15.5)
P_HEADROOM_RUNNING = 8.0
ATTN_UNROLL = 16
DMA_UNROLL = 8
INPROJ_SUBTILES = 2
ROW_BLOCK = 512
GQA_QUERY_BLOCK = 256
DIFF_QUERY_BLOCK = 512
MOE_BLOCK_ROWS = 512
MOE_DMA_ROWS = 1024

C_QA, C_KA, C_VA, C_GB, C_QC, C_KC, C_VC, C_END = 0, 256, 512, 1024, 1536, 2048, 2176, 2432


def _cparams(semantics):
    return pltpu.CompilerParams(dimension_semantics=semantics, vmem_limit_bytes=VMEM_LIMIT)


def _dot(a, b):
    return jnp.dot(a, b, preferred_element_type=F32)


def _sigmoid(z):
    return 1.0 / (1.0 + jnp.exp(-z))


ROW_TILE = 8


def _store_row_tiles(ref, val, rows):
    for a in range(ROW_TILE):
        ref[pl.ds(a, rows, stride=ROW_TILE), :] = val[:, a * LANES:(a + 1) * LANES]


def _load_row_tiles(ref, rows):
    return jnp.concatenate([ref[pl.ds(a, rows, stride=ROW_TILE), :] for a in range(ROW_TILE)], axis=1)


def _mod_kernel(c_ref, w_ref, b_ref, o_ref):
    c = c_ref[...]
    s = c * _sigmoid(c)
    o_ref[...] = jnp.dot(s, w_ref[...], preferred_element_type=F32, precision=lax.Precision.HIGHEST) + b_ref[...]


def _ada_mod(cc, w, b):
    rows, d = cc.shape
    n = w.shape[1]
    tn = d
    return pl.pallas_call(
        _mod_kernel,
        out_shape=jax.ShapeDtypeStruct((rows, n), F32),
        grid=(n // tn,),
        in_specs=[pl.BlockSpec((rows, d), lambda j: (0, 0)),
                  pl.BlockSpec((d, tn), lambda j: (0, j)),
                  pl.BlockSpec((1, tn), lambda j: (0, j))],
        out_specs=pl.BlockSpec((rows, tn), lambda j: (0, j)),
        compiler_params=_cparams(("arbitrary",)),
        name="ada_mod",
    )(cc, w, b.reshape(1, n))


def _inproj_kernel(x_ref, shift_ref, scale_ref, g_ref, w_ref, cosa_ref, sina_ref, cosc_ref, sinc_ref,
                   pa_ref, pc_ref, hm_ref, qg_ref, kg_ref,
                   qa_o, kat_o, va_o, gb_o, qc_o, kct_o, vc_o, amax_o):
    n_sub, sub = kat_o.shape[1], kat_o.shape[3]
    stats = None
    for t in range(n_sub):
        rows = slice(t * sub, (t + 1) * sub)
        tabs = [ref[rows, :] for ref in (cosa_ref, sina_ref, cosc_ref, sinc_ref)]
        outs = [o.at[0, rows, :] for o in (qa_o, va_o, gb_o, qc_o, vc_o)] + [kat_o.at[0, t], kct_o.at[0, t]]
        st = _inproj_subtile(x_ref[0, rows, :], shift_ref, scale_ref, g_ref, w_ref, tabs, pa_ref, pc_ref, hm_ref,
                             qg_ref, kg_ref, outs, amax_o.shape[3])
        stats = st if stats is None else jnp.maximum(stats, st)
    amax_o[0, 0] = stats


def _inproj_subtile(x, shift_ref, scale_ref, g_ref, w_ref, tabs, pa_ref, pc_ref, hm_ref, qg_ref, kg_ref, outs, stat_w):
    qa_o, va_o, gb_o, qc_o, vc_o, kat_o, kct_o = outs
    cosa, sina, cosc, sinc = tabs
    ms = jnp.mean(x * x, axis=-1, keepdims=True)
    h = x * lax.rsqrt(ms + EPS) * g_ref[...]
    h = h * (1.0 + scale_ref[0]) + shift_ref[0]
    hb = h.astype(BF16)

    def proj(lo, hi):
        return _dot(hb, w_ref[:, lo:hi])

    def blockmat(y, m_ref):
        yb = y.astype(BF16)
        w = y.shape[1]
        if w == LANES:
            return _dot(yb, m_ref[:LANES, :LANES])
        return jnp.concatenate([_dot(yb[:, c:c + MXU_DIM], m_ref[...]) for c in range(0, w, MXU_DIM)], axis=1)

    def rope(y, cos, sin, p_ref):
        reps = y.shape[1] // LANES
        cos = jnp.tile(cos, (1, reps))
        sin = jnp.tile(sin, (1, reps))
        return y * cos + blockmat(y, p_ref) * sin

    def ones_col(width):
        lane = lax.broadcasted_iota(I32, (1, width), 1)
        return jnp.where(lane % LANES >= HEAD_V, 1.0, 0.0).astype(F32)

    def col_amax(y):
        cm = jnp.max(jnp.abs(y), axis=0, keepdims=True)
        pad = stat_w - y.shape[1]
        return cm if pad == 0 else jnp.concatenate([cm, jnp.zeros((1, pad), F32)], axis=1)

    qa = rope(proj(C_QA, C_KA), cosa, sina, pa_ref)
    qa_o[...] = qa.astype(BF16)
    ka = rope(proj(C_KA, C_VA), cosa, sina, pa_ref)
    kat_o[...] = ka.T.astype(BF16)
    va = proj(C_VA, C_GB)
    va_o[...] = (va + ones_col(C_GB - C_VA)).astype(BF16)
    gb_o[...] = proj(C_GB, C_QC).astype(BF16)

    y = proj(C_QC, C_KC)
    yn = y * lax.rsqrt(blockmat(y * y, hm_ref) + EPS) * qg_ref[...]
    qc = rope(yn, cosc, sinc, pc_ref)
    qc_o[...] = qc.astype(BF16)

    y = proj(C_KC, C_VC)
    yn = y * lax.rsqrt(blockmat(y * y, hm_ref) + EPS) * kg_ref[...]
    kc = rope(yn, cosc, sinc, pc_ref)
    kct_o[...] = kc.T.astype(BF16)

    vc = proj(C_VC, C_END)
    vc_o[...] = (vc + ones_col(C_END - C_VC)).astype(BF16)

    stats = [col_amax(t) for t in (qa, ka, va, qc, kc, vc)]
    stats += [jnp.zeros((1, stat_w), F32)] * (SUBLANES - len(stats))
    return jnp.concatenate(stats, axis=0)


def _in_projection(x, shift, scale, g1, w_aug, tabs, mats, qg, kg, tk, n_sub):
    b, s, d = x.shape
    tm = tk * n_sub
    nt = s // tm
    cosa, sina, cosc, sinc = tabs
    pa, pc, hm = mats
    row = lambda bb, i: (bb, i, 0)
    const2 = lambda bb, i: (0, 0)
    per_b = lambda bb, i: (bb, 0, 0)
    tab = lambda bb, i: (i, 0)
    out_shape = (
        jax.ShapeDtypeStruct((b, s, 256), BF16),
        jax.ShapeDtypeStruct((b, s // tk, 256, tk), BF16),
        jax.ShapeDtypeStruct((b, s, 512), BF16),
        jax.ShapeDtypeStruct((b, s, 512), BF16),
        jax.ShapeDtypeStruct((b, s, 512), BF16),
        jax.ShapeDtypeStruct((b, s // tk, 128, tk), BF16),
        jax.ShapeDtypeStruct((b, s, 256), BF16),
        jax.ShapeDtypeStruct((b, nt, SUBLANES, 512), F32),
    )
    out_specs = (
        pl.BlockSpec((1, tm, 256), row),
        pl.BlockSpec((1, n_sub, 256, tk), lambda bb, i: (bb, i, 0, 0)),
        pl.BlockSpec((1, tm, 512), row),
        pl.BlockSpec((1, tm, 512), row),
        pl.BlockSpec((1, tm, 512), row),
        pl.BlockSpec((1, n_sub, 128, tk), lambda bb, i: (bb, i, 0, 0)),
        pl.BlockSpec((1, tm, 256), row),
        pl.BlockSpec((1, 1, SUBLANES, 512), lambda bb, i: (bb, i, 0, 0)),
    )
    in_specs = [
        pl.BlockSpec((1, tm, d), row),
        pl.BlockSpec((1, 1, d), per_b),
        pl.BlockSpec((1, 1, d), per_b),
        pl.BlockSpec((1, d), const2),
        pl.BlockSpec((d, C_END), const2),
        pl.BlockSpec((tm, LANES), tab), pl.BlockSpec((tm, LANES), tab),
        pl.BlockSpec((tm, LANES), tab), pl.BlockSpec((tm, LANES), tab),
        pl.BlockSpec((MXU_DIM, MXU_DIM), const2), pl.BlockSpec((MXU_DIM, MXU_DIM), const2),
        pl.BlockSpec((MXU_DIM, MXU_DIM), const2),
        pl.BlockSpec((1, 512), const2), pl.BlockSpec((1, 128), const2),
    ]
    return pl.pallas_call(
        _inproj_kernel, out_shape=out_shape, grid=(b, nt), in_specs=in_specs, out_specs=out_specs,
        compiler_params=_cparams(("arbitrary", "arbitrary")), name="in_projection",
    )(x, shift, scale, g1, w_aug, cosa, sina, cosc, sinc, pa, pc, hm, qg, kg)


SCL_Q, SCL_K, SCL_V, SCL_V_INV = 0, 1, 2, 3


LO_GAIN = 16.0
V_ROW_CHUNK = 512


def _split_fp8(x, dtype):
    hi = x.astype(dtype)
    return hi, ((x - hi.astype(F32)) * LO_GAIN).astype(dtype)


def _stack_qk(x, other_side, dtype, axis, width):
    hi, lo = _split_fp8(x, dtype)
    hi_small = (hi.astype(F32) * (1.0 / LO_GAIN)).astype(dtype)
    parts = [hi, lo, hi_small] if other_side else [hi, hi_small, lo]
    pad = width - 3 * x.shape[axis]
    if pad:
        pad_shape = tuple(pad if a == axis else n for a, n in enumerate(x.shape))
        parts.append(jnp.zeros(pad_shape, dtype))
    return jnp.concatenate(parts, axis=axis)


def _quantize_kv(k_refs, v_refs, k8_refs, v8_refs, scl_ref, dk):
    sk = scl_ref[0, SCL_K:SCL_K + 1, 0:1]
    sv = scl_ref[0, SCL_V:SCL_V + 1, 0:1]
    for k_ref, k8_ref in zip(k_refs, k8_refs):
        n_maps = k_ref.shape[2] // dk
        stack = k8_ref.shape[1] // n_maps

        def k_body(c, carry, k_ref=k_ref, k8_ref=k8_ref, n_maps=n_maps, stack=stack):
            kf = k_ref[0, c].astype(F32) * sk
            k8_ref[c] = jnp.concatenate(
                [_stack_qk(kf[dk * j:dk * (j + 1), :], True, k8_ref.dtype, 0, stack) for j in range(n_maps)], axis=0)
            return carry

        lax.fori_loop(0, k_ref.shape[1], k_body, 0)

    for v_ref, v8_ref in zip(v_refs, v8_refs):
        rows = min(V_ROW_CHUNK, v_ref.shape[1])

        def v_body(c, carry, v_ref=v_ref, v8_ref=v8_ref, rows=rows):
            sl = pl.ds(pl.multiple_of(c * rows, rows), rows)
            vf = v_ref[0, sl, :].astype(F32)
            groups = []
            for g in range(v_ref.shape[2] // LANES):
                vs = vf[:, LANES * g:LANES * g + HEAD_V] * sv
                hi = vs.astype(v8_ref.dtype)
                lo = (vs - hi.astype(F32)).astype(v8_ref.dtype)
                ones = vf[:, LANES * g + HEAD_V:LANES * (g + 1)].astype(v8_ref.dtype)
                groups += [hi, ones, lo, jnp.zeros((rows, HEAD_V), v8_ref.dtype)]
            v8_ref[sl, :] = jnp.concatenate(groups, axis=1)
            return carry

        lax.fori_loop(0, v_ref.shape[1] // rows, v_body, 0)


def _attention_sweeps(qms, k_slices, pv_groups, k_refs, v_refs, m_ref, smax_ref, acc_ref, p_format, finalize):
    r = qms[0].shape[0]
    p_dtype, p_max_exp = p_format

    def scores(j, kc):
        return _dot(qms[j], kc[k_slices[j], :])

    def sweep(running_max):
        acc_ref[...] = jnp.zeros(acc_ref.shape, F32)
        if running_max:
            m_ref[...] = jnp.full(m_ref.shape, NEG_BIG, F32)
        else:
            smax_ref[...] = jnp.full(smax_ref.shape, NEG_BIG, smax_ref.dtype)
            kc0 = k_refs[0][0][:, :MXU_DIM]
            for j in range(len(qms)):
                m0 = jnp.max(scores(j, kc0), axis=-1, keepdims=True)
                m_ref[j * r:(j + 1) * r, :] = jnp.broadcast_to(m0, (r, LANES))

        for k_ref, v_ref in zip(k_refs, v_refs):
            n_chunks, tk = k_ref.shape[0], k_ref.shape[2]

            def body(c, carry, k_ref=k_ref, v_ref=v_ref, tk=tk):
                kc = k_ref[c]
                vc = v_ref[pl.ds(pl.multiple_of(c * tk, tk), tk), :]
                for ids, v_lanes in pv_groups:
                    ps, alphas = [], []
                    for j in ids:
                        rows = slice(j * r, (j + 1) * r)
                        s = scores(j, kc)
                        m = m_ref[rows, :]
                        if running_max:
                            m_new = jnp.maximum(m, jnp.max(s, axis=-1, keepdims=True) - P_HEADROOM_RUNNING)
                            m_ref[rows, :] = m_new
                            alphas.append(jnp.exp2(m - m_new))
                            m = m_new
                        d = (s - jnp.tile(m, (1, tk // LANES))).astype(BF16)
                        if not running_max:
                            cm = functools.reduce(jnp.maximum, [d[:, l:l + LANES] for l in range(0, tk, LANES)])
                            smax_ref[rows, :] = jnp.maximum(smax_ref[rows, :], cm)
                        ps.append(jnp.exp2(d).astype(p_dtype))
                    rows = slice(ids[0] * r, (ids[-1] + 1) * r)
                    pv = _dot(jnp.concatenate(ps, axis=0), vc[:, v_lanes])
                    pv = pv[:, :LANES] + pv[:, LANES:]
                    if running_max:
                        acc_ref[rows, :] = acc_ref[rows, :] * jnp.concatenate(alphas, axis=0) + pv
                    else:
                        acc_ref[rows, :] += pv
                return carry

            unroll = 1 if running_max else math.gcd(n_chunks, ATTN_UNROLL)
            lax.fori_loop(0, n_chunks, body, 0, unroll=unroll)

    sweep(False)
    finalize()
    top_exp = jnp.max(smax_ref[...].astype(F32))

    @pl.when(jnp.logical_not(top_exp <= p_max_exp))
    def _():
        sweep(True)
        finalize()


def _gqa_kernel(*refs, n_parts, tq):
    q_ref = refs[0]
    k_refs = refs[1:1 + n_parts]
    v_refs = refs[1 + n_parts:1 + 2 * n_parts]
    scl_ref, o_ref, m_ref, smax_ref, acc_ref = refs[1 + 2 * n_parts:6 + 2 * n_parts]
    k8_refs = refs[6 + 2 * n_parts:6 + 3 * n_parts]
    v8_refs = refs[6 + 3 * n_parts:]

    @pl.when(pl.program_id(2) == 0)
    def _():
        _quantize_kv(k_refs, v_refs, k8_refs, v8_refs, scl_ref, HEAD_V)

    stack = k8_refs[0].shape[1]
    q = q_ref[0].astype(F32) * scl_ref[0, SCL_Q:SCL_Q + 1, 0:1]
    qs = jnp.concatenate([_stack_qk(q[:, HEAD_V * j:HEAD_V * (j + 1)], False, QK_DTYPE, 1, stack)
                          for j in range(GQA_GROUP)], axis=0)
    def finalize():
        o = _attention_output(acc_ref[...], scl_ref)
        o_ref[0] = jnp.concatenate([o[j * tq:(j + 1) * tq] for j in range(GQA_GROUP)], axis=1).astype(BF16)

    _attention_sweeps([qs], [slice(0, stack)], [((0,), slice(0, 2 * LANES))], k8_refs, v8_refs,
                      m_ref, smax_ref, acc_ref, P_E4M3, finalize)


def _attention_output(acc, scl_ref):
    return acc[:, :HEAD_V] * scl_ref[0, SCL_V_INV:SCL_V_INV + 1, 0:1] / acc[:, HEAD_V:]


def _softmax_scratch(rows):
    return [pltpu.VMEM((rows, LANES), F32), pltpu.VMEM((rows, LANES), BF16), pltpu.VMEM((rows, LANES), F32)]


def _stack_height(dk):
    return max(4 * dk, LANES)


def _fp8_kv_scratch(k_blocks, v_blocks, dk):
    ks = [pltpu.VMEM((kb[1], kb[2] // dk * _stack_height(dk), kb[3]), QK_DTYPE) for kb in k_blocks]
    vs = [pltpu.VMEM((vb[1], 2 * vb[2]), V_DTYPE) for vb in v_blocks]
    return ks + vs


def _gqa_attention(q, k_parts, v_parts, scales, tq):
    b, sq, _ = q.shape
    n_parts = len(k_parts)
    k_blocks = [(1, kp.shape[1], HEAD_V, kp.shape[3]) for kp in k_parts]
    v_blocks = [(1, vp.shape[1], LANES) for vp in v_parts]
    in_specs = [pl.BlockSpec((1, tq, 256), lambda bb, g, i: (bb, i, g))]
    in_specs += [pl.BlockSpec(kb, lambda bb, g, i: (bb, 0, g, 0)) for kb in k_blocks]
    in_specs += [pl.BlockSpec(vb, lambda bb, g, i: (bb, 0, g)) for vb in v_blocks]
    in_specs.append(pl.BlockSpec((1, SUBLANES, LANES), lambda bb, g, i: (bb, 0, 0)))
    return pl.pallas_call(
        functools.partial(_gqa_kernel, n_parts=n_parts, tq=tq),
        out_shape=jax.ShapeDtypeStruct((b, sq, 512), BF16),
        grid=(b, GQA_KV, sq // tq),
        in_specs=in_specs,
        out_specs=pl.BlockSpec((1, tq, 256), lambda bb, g, i: (bb, i, g)),
        scratch_shapes=_softmax_scratch(GQA_GROUP * tq) + _fp8_kv_scratch(k_blocks, v_blocks, HEAD_V),
        compiler_params=_cparams(("arbitrary", "arbitrary", "arbitrary")),
        name="gqa_attention",
    )(q, *k_parts, *v_parts, scales)


def _diff_kernel(*refs, n_parts, tq, lam_init):
    q_ref = refs[0]
    k_refs = refs[1:1 + n_parts]
    v_refs = refs[1 + n_parts:1 + 2 * n_parts]
    scl_ref, lam_ref, sg_ref, hm_ref, o_ref, m_ref, smax_ref, acc_ref = refs[1 + 2 * n_parts:9 + 2 * n_parts]
    k8_refs = refs[9 + 2 * n_parts:9 + 3 * n_parts]
    v8_refs = refs[9 + 3 * n_parts:]

    @pl.when(pl.program_id(2) == 0)
    def _():
        _quantize_kv(k_refs, v_refs, k8_refs, v8_refs, scl_ref, DIFF_QK)

    stack = k8_refs[0].shape[1] // 4
    q = q_ref[0].astype(F32) * scl_ref[0, SCL_Q:SCL_Q + 1, 0:1]
    qmaps = [_stack_qk(q[:, DIFF_QK * j:DIFF_QK * (j + 1)], False, QK_DTYPE, 1, stack) for j in range(4)]
    k_slices = [slice(stack * j, stack * (j + 1)) for j in range(4)]
    pv_groups = [((0, 1), slice(0, 2 * LANES)), ((2, 3), slice(2 * LANES, 4 * LANES))]
    lv = lam_ref[...]
    lam = (jnp.exp(jnp.sum(lv[0:1] * lv[1:2], axis=-1, keepdims=True))
           - jnp.exp(jnp.sum(lv[2:3] * lv[3:4], axis=-1, keepdims=True)) + lam_init)

    def finalize():
        outs = _attention_output(acc_ref[...], scl_ref)
        o = jnp.concatenate([outs[(2 * hh) * tq:(2 * hh + 1) * tq] - lam * outs[(2 * hh + 1) * tq:(2 * hh + 2) * tq]
                             for hh in range(2)], axis=1)
        ms = _dot((o * o).astype(BF16), hm_ref[...])
        o_ref[0] = (o * lax.rsqrt(ms + EPS) * sg_ref[...] * (1.0 - lam_init)).astype(BF16)

    _attention_sweeps(qmaps, k_slices, pv_groups, k8_refs, v8_refs, m_ref, smax_ref, acc_ref, P_E5M2, finalize)


def _diff_attention(q, k_parts, v_parts, scales, lam_vecs, subln_g, lam_init, tq):
    b, sq, _ = q.shape
    n_parts = len(k_parts)
    k_blocks = [(1, kp.shape[1], LANES, kp.shape[3]) for kp in k_parts]
    v_blocks = [(1, vp.shape[1], 2 * LANES) for vp in v_parts]
    in_specs = [pl.BlockSpec((1, tq, LANES), lambda bb, p, i: (bb, i, p))]
    in_specs += [pl.BlockSpec(kb, lambda bb, p, i: (bb, 0, p, 0)) for kb in k_blocks]
    in_specs += [pl.BlockSpec(vb, lambda bb, p, i: (bb, 0, p)) for vb in v_blocks]
    in_specs.append(pl.BlockSpec((1, SUBLANES, LANES), lambda bb, p, i: (bb, 0, 0)))
    in_specs.append(pl.BlockSpec((4, DIFF_QK), lambda bb, p, i: (0, 0)))
    in_specs.append(pl.BlockSpec((1, LANES), lambda bb, p, i: (0, 0)))
    in_specs.append(pl.BlockSpec((LANES, LANES), lambda bb, p, i: (0, 0)))
    head_mean = _head_mean_matrix()[:LANES, :LANES]
    subln_pair = jnp.tile(subln_g, (1, LANES // HEAD_V))
    return pl.pallas_call(
        functools.partial(_diff_kernel, n_parts=n_parts, tq=tq, lam_init=lam_init),
        out_shape=jax.ShapeDtypeStruct((b, sq, 256), BF16),
        grid=(b, DIFF_HEADS // 2, sq // tq),
        in_specs=in_specs,
        out_specs=pl.BlockSpec((1, tq, LANES), lambda bb, p, i: (bb, i, p)),
        scratch_shapes=_softmax_scratch(4 * tq) + _fp8_kv_scratch(k_blocks, v_blocks, DIFF_QK),
        compiler_params=_cparams(("arbitrary", "arbitrary", "arbitrary")),
        name="diff_attention",
    )(q, *k_parts, *v_parts, scales, lam_vecs, subln_pair, head_mean)


CONV_HALO = 16
CONV_ROWS = 64


def _conv_kernel(gb_ref, prev_ref, next_ref, w_ref, b_ref, lg_ref, lb_ref, o_ref, u_ref, sh_ref, *, tm):
    i = pl.program_id(1)
    last = pl.num_programs(1) - 1
    ch = w_ref.shape[1]

    def glu(z):
        z = z.astype(F32)
        return z[:, :ch] * _sigmoid(z[:, ch:])

    u_ref[CONV_HALO:CONV_HALO + tm, :] = glu(gb_ref[0])
    u_ref[0:CONV_HALO, :] = jnp.where(i > 0, glu(prev_ref[0]), 0.0)
    u_ref[CONV_HALO + tm:2 * CONV_HALO + tm, :] = jnp.where(i < last, glu(next_ref[0]), 0.0)

    span = sh_ref.shape[1]
    for r in range(1, SUBLANES):
        sh_ref[r - 1] = u_ref[r:r + span, :]

    off = CONV_HALO - CONV_K // 2
    for r0 in range(0, tm, CONV_ROWS):
        acc = jnp.zeros((CONV_ROWS, ch), F32)
        for j in range(CONV_K):
            phase, base = (off + j) % SUBLANES, r0 + (off + j) // SUBLANES * SUBLANES
            taps = u_ref[base:base + CONV_ROWS, :] if phase == 0 else sh_ref[phase - 1, base:base + CONV_ROWS, :]
            acc = acc + taps * w_ref[j:j + 1, :]
        y = acc + b_ref[...]
        mu = jnp.mean(y, axis=-1, keepdims=True)
        yc = y - mu
        var = jnp.mean(yc * yc, axis=-1, keepdims=True)
        z = yc * lax.rsqrt(var + EPS) * lg_ref[...] + lb_ref[...]
        o_ref[0, r0:r0 + CONV_ROWS, :] = (z * _sigmoid(z)).astype(BF16)


def _conformer_conv(gb, w, bias, ln_g, ln_b, tm):
    b, s, two_ch = gb.shape
    ch = two_ch // 2
    hb = tm // CONV_HALO
    n_halo = s // CONV_HALO
    const2 = lambda bb, i: (0, 0)
    return pl.pallas_call(
        functools.partial(_conv_kernel, tm=tm),
        out_shape=jax.ShapeDtypeStruct((b, s, ch), BF16),
        grid=(b, s // tm),
        in_specs=[
            pl.BlockSpec((1, tm, two_ch), lambda bb, i: (bb, i, 0)),
            pl.BlockSpec((1, CONV_HALO, two_ch), lambda bb, i: (bb, jnp.maximum(i * hb - 1, 0), 0)),
            pl.BlockSpec((1, CONV_HALO, two_ch), lambda bb, i: (bb, jnp.minimum((i + 1) * hb, n_halo - 1), 0)),
            pl.BlockSpec((CONV_K, ch), const2), pl.BlockSpec((1, ch), const2),
            pl.BlockSpec((1, ch), const2), pl.BlockSpec((1, ch), const2),
        ],
        out_specs=pl.BlockSpec((1, tm, ch), lambda bb, i: (bb, i, 0)),
        scratch_shapes=[pltpu.VMEM((tm + 2 * CONV_HALO, ch), F32),
                        pltpu.VMEM((SUBLANES - 1, tm + 2 * CONV_HALO - SUBLANES, ch), F32)],
        compiler_params=_cparams(("arbitrary", "arbitrary")),
        name="conformer_conv",
    )(gb, gb, gb, w, bias, ln_g, ln_b)


def _merge_kernel(oa_ref, ob_ref, oc_ref, x_ref, gate_ref, shift_ref, scale_ref, g2_ref, w_ref, rw_ref,
                  xo_ref, h2_ref, lg_ref):
    tm = x_ref.shape[1]
    wa = oa_ref.shape[2]
    wb = wa + ob_ref.shape[2]
    y = _dot(oa_ref[0], w_ref[0:wa, :]) + _dot(ob_ref[0], w_ref[wa:wb, :]) + _dot(oc_ref[0], w_ref[wb:, :])
    xn = x_ref[0] + gate_ref[0] * y
    xo_ref[0] = xn
    ms = jnp.mean(xn * xn, axis=-1, keepdims=True)
    h2 = xn * lax.rsqrt(ms + EPS) * g2_ref[...] * (1.0 + scale_ref[0]) + shift_ref[0]
    _store_row_tiles(h2_ref, h2, tm)
    lg_ref[...] = lax.dot_general(rw_ref[...], h2, (((1,), (1,)), ((), ())),
                                  preferred_element_type=F32, precision=lax.Precision.HIGHEST)


def _merge_route(oa, ob, oc, x, gate, shift, scale, g2, w_out, router_wt, tm):
    b, s, d = x.shape
    row = lambda bb, i: (bb, i, 0)
    per_b = lambda bb, i: (bb, 0, 0)
    const2 = lambda bb, i: (0, 0)
    nt = s // tm
    in_specs = [
        pl.BlockSpec((1, tm, oa.shape[2]), row), pl.BlockSpec((1, tm, ob.shape[2]), row),
        pl.BlockSpec((1, tm, oc.shape[2]), row), pl.BlockSpec((1, tm, d), row),
        pl.BlockSpec((1, 1, d), per_b), pl.BlockSpec((1, 1, d), per_b), pl.BlockSpec((1, 1, d), per_b),
        pl.BlockSpec((1, d), const2), pl.BlockSpec((d, d), const2),
        pl.BlockSpec((N_EXPERTS, d), const2),
    ]
    out_shape = (jax.ShapeDtypeStruct((b, s, d), F32),
                 jax.ShapeDtypeStruct((b * s * ROW_TILE, LANES), F32),
                 jax.ShapeDtypeStruct((N_EXPERTS, b * s), F32))
    out_specs = (pl.BlockSpec((1, tm, d), row),
                 pl.BlockSpec((tm * ROW_TILE, LANES), lambda bb, i: (bb * nt + i, 0)),
                 pl.BlockSpec((N_EXPERTS, tm), lambda bb, i: (0, bb * nt + i)))
    return pl.pallas_call(
        _merge_kernel, out_shape=out_shape, grid=(b, nt), in_specs=in_specs, out_specs=out_specs,
        compiler_params=_cparams(("arbitrary", "arbitrary")), name="merge_route",
    )(oa, ob, oc, x, gate, shift, scale, g2, w_out, router_wt)


def _merge_ffn_kernel(oa_ref, ob_ref, oc_ref, x_ref, gate1_ref, shift_ref, scale_ref, g2_ref, w_ref, gate2_ref,
                      wg_ref, wu_ref, wd_ref, o_ref, *, tf):
    wa = oa_ref.shape[2]
    wb = wa + ob_ref.shape[2]
    y = _dot(oa_ref[0], w_ref[0:wa, :]) + _dot(ob_ref[0], w_ref[wa:wb, :]) + _dot(oc_ref[0], w_ref[wb:, :])
    xn = x_ref[0] + gate1_ref[0] * y
    ms = jnp.mean(xn * xn, axis=-1, keepdims=True)
    h = (xn * lax.rsqrt(ms + EPS) * g2_ref[...] * (1.0 + scale_ref[0]) + shift_ref[0]).astype(BF16)
    ff = wg_ref.shape[1]
    acc = jnp.zeros(xn.shape, F32)
    for f in range(0, ff, tf):
        g = _dot(h, wg_ref[:, f:f + tf])
        u = _dot(h, wu_ref[:, f:f + tf])
        a = (g * _sigmoid(g) * u).astype(BF16)
        acc = acc + _dot(a, wd_ref[f:f + tf, :])
    o_ref[0] = xn + gate2_ref[0] * acc


def _merge_dense_ffn(oa, ob, oc, x, gate1, shift, scale, g2, w_out, gate2, wg, wu, wd, tm):
    b, s, d = x.shape
    ff = wg.shape[1]
    row = lambda bb, i: (bb, i, 0)
    per_b = lambda bb, i: (bb, 0, 0)
    const2 = lambda bb, i: (0, 0)
    resident = pl.Buffered(1)
    return pl.pallas_call(
        functools.partial(_merge_ffn_kernel, tf=MXU_DIM),
        out_shape=jax.ShapeDtypeStruct((b, s, d), F32),
        grid=(b, s // tm),
        in_specs=[
            pl.BlockSpec((1, tm, oa.shape[2]), row), pl.BlockSpec((1, tm, ob.shape[2]), row),
            pl.BlockSpec((1, tm, oc.shape[2]), row), pl.BlockSpec((1, tm, d), row),
            pl.BlockSpec((1, 1, d), per_b), pl.BlockSpec((1, 1, d), per_b), pl.BlockSpec((1, 1, d), per_b),
            pl.BlockSpec((1, d), const2), pl.BlockSpec((d, d), const2, pipeline_mode=resident),
            pl.BlockSpec((1, 1, d), per_b),
            pl.BlockSpec((d, ff), const2, pipeline_mode=resident),
            pl.BlockSpec((d, ff), const2, pipeline_mode=resident),
            pl.BlockSpec((ff, d), const2, pipeline_mode=resident),
        ],
        out_specs=pl.BlockSpec((1, tm, d), row),
        compiler_params=_cparams(("arbitrary", "arbitrary")), name="merge_dense_ffn",
    )(oa, ob, oc, x, gate1, shift, scale, g2, w_out, gate2, wg, wu, wd)


def _top2(lg):
    sub = lax.broadcasted_iota(I32, lg.shape, 0)
    l1 = jnp.max(lg, axis=0, keepdims=True)
    i1 = jnp.min(jnp.where(lg == l1, sub, N_EXPERTS), axis=0, keepdims=True)
    m1 = sub == i1
    lg2 = jnp.where(m1, -jnp.inf, lg)
    l2 = jnp.max(lg2, axis=0, keepdims=True)
    i2 = jnp.min(jnp.where(lg2 == l2, sub, N_EXPERTS), axis=0, keepdims=True)
    m2 = sub == i2
    return l1, l2, m1, m2


def _sublane_cumsum(x):
    sub = lax.broadcasted_iota(I32, x.shape, 0)
    for sh in (1, 2, 4):
        x = x + jnp.where(sub >= sh, pltpu.roll(x, sh, 0), 0.0)
    return x


def _route_kernel(lg_ref, tri_ref, dest_ref, gates_ref, be_ref, pad_ref, base_ref, start_ref, *, block_rows, total_rows):
    phase = pl.program_id(0)
    j = pl.program_id(1)
    l1, l2, m1, m2 = _top2(lg_ref[...])
    e = jnp.where(m1 | m2, 1.0, 0.0).astype(F32)
    cnt = jnp.sum(e, axis=1, keepdims=True)

    @pl.when((phase == 0) & (j == 0))
    def _():
        base_ref[...] = jnp.zeros(base_ref.shape, F32)

    @pl.when((phase == 1) & (j == 0))
    def _():
        counts = base_ref[...]
        nblk = jnp.floor((counts + (block_rows - 1)) * (1.0 / block_rows))
        end_blk = _sublane_cumsum(nblk)
        start_ref[...] = (end_blk - nblk) * block_rows
        blk = lax.broadcasted_iota(I32, be_ref.shape, 1).astype(F32)
        owner = jnp.sum(jnp.where(end_blk[:, :1] <= blk, 1.0, 0.0), axis=0, keepdims=True)
        be_ref[...] = jnp.broadcast_to(jnp.minimum(owner, N_EXPERTS - 1.0), be_ref.shape).astype(I32)
        sub = lax.broadcasted_iota(I32, pad_ref.shape, 0)
        lane = lax.broadcasted_iota(I32, pad_ref.shape, 1)
        pad_end = jnp.where(sub == N_EXPERTS - 1, float(total_rows), end_blk * block_rows)
        pad_ref[...] = jnp.where(lane < LANES // 2, start_ref[...] + counts, pad_end).astype(I32)
        base_ref[...] = jnp.zeros(base_ref.shape, F32)

    @pl.when(phase == 1)
    def _():
        prefix = _dot(e.astype(BF16), tri_ref[...]) + base_ref[:, :1] + start_ref[:, :1]
        d1 = jnp.sum(jnp.where(m1, prefix, 0.0), axis=0, keepdims=True)
        d2 = jnp.sum(jnp.where(m2, prefix, 0.0), axis=0, keepdims=True)
        sub = lax.broadcasted_iota(I32, dest_ref.shape, 0)
        dest_ref[...] = jnp.where(sub == 0, d1, jnp.where(sub == 1, d2, 0.0)).astype(I32)
        ex = jnp.exp(l2 - l1)
        g1 = 1.0 / (1.0 + ex)
        g2 = ex / (1.0 + ex)
        half = lax.broadcasted_iota(I32, (LANES, lg_ref.shape[1]), 0) < LANES // 2
        gates_ref[...] = jnp.where(half, g1, g2).T

    base_ref[...] = base_ref[...] + cnt


def _route(logits_t, block_rows, n_blocks, tr):
    n = logits_t.shape[1]
    n_blocks_pad = -(-n_blocks // LANES) * LANES
    tri = jnp.asarray(np.triu(np.ones((tr, tr), np.float32), k=1), BF16)
    return pl.pallas_call(
        functools.partial(_route_kernel, block_rows=block_rows, total_rows=n_blocks * block_rows),
        out_shape=(jax.ShapeDtypeStruct((N_EXPERTS, n), I32),
                   jax.ShapeDtypeStruct((n, LANES), F32),
                   jax.ShapeDtypeStruct((N_EXPERTS, n_blocks_pad), I32),
                   jax.ShapeDtypeStruct((N_EXPERTS, LANES), I32)),
        grid=(2, n // tr),
        in_specs=[pl.BlockSpec((N_EXPERTS, tr), lambda p, j: (0, j)),
                  pl.BlockSpec((tr, tr), lambda p, j: (0, 0))],
        out_specs=(pl.BlockSpec((N_EXPERTS, tr), lambda p, j: (0, j * p)),
                   pl.BlockSpec((tr, LANES), lambda p, j: (j * p, 0)),
                   pl.BlockSpec((N_EXPERTS, n_blocks_pad), lambda p, j: (0, 0)),
                   pl.BlockSpec((N_EXPERTS, LANES), lambda p, j: (0, 0))),
        scratch_shapes=[pltpu.VMEM((N_EXPERTS, LANES), F32), pltpu.VMEM((N_EXPERTS, LANES), F32)],
        compiler_params=_cparams(("arbitrary", "arbitrary")), name="moe_route",
    )(logits_t, tri)


def _row_copy(src_hbm, src_row, dst_hbm, dst_row, sem):
    src = pl.ds(pl.multiple_of(src_row * ROW_TILE, ROW_TILE), ROW_TILE)
    dst = pl.ds(pl.multiple_of(dst_row * ROW_TILE, ROW_TILE), ROW_TILE)
    return pltpu.make_async_copy(src_hbm.at[src], dst_hbm.at[dst], sem)


def _scatter_kernel(pad_lo_ref, pad_hi_ref, d1_ref, d2_ref, src_ref, out_hbm, zero_ref, sem, zero_sem, *, rows):
    @pl.when(pl.program_id(0) == 0)
    def _():
        zero_ref[...] = jnp.zeros(zero_ref.shape, zero_ref.dtype)
        for e in range(N_EXPERTS):
            def fill(row, c):
                _row_copy(zero_ref, 0, out_hbm, row, zero_sem).start()
                return c

            def drain(row, c):
                _row_copy(zero_ref, 0, out_hbm, 0, zero_sem).wait()
                return c

            lax.fori_loop(pad_lo_ref[e], pad_hi_ref[e], fill, 0)
            lax.fori_loop(pad_lo_ref[e], pad_hi_ref[e], drain, 0)

    def start(r, c):
        _row_copy(src_ref, r, out_hbm, d1_ref[0, 0, r], sem).start(priority=0)
        _row_copy(src_ref, r, out_hbm, d2_ref[0, 0, r], sem).start(priority=1)
        return c

    lax.fori_loop(0, rows, start, 0, unroll=DMA_UNROLL)
    for _ in range(2):
        pltpu.make_async_copy(src_ref, out_hbm.at[pl.ds(0, rows * ROW_TILE)], sem).wait()


def _scatter_rows(src, d1, d2, pad_lo, pad_hi, total_rows, rows):
    n = d1.shape[0]
    idx_spec = pl.BlockSpec((1, 1, rows), lambda i, lo, hi: (i, 0, 0), memory_space=pltpu.SMEM)
    any_spec = pl.BlockSpec(memory_space=pl.ANY)
    grid_spec = pltpu.PrefetchScalarGridSpec(
        num_scalar_prefetch=2,
        grid=(n // rows,),
        in_specs=[idx_spec, idx_spec, pl.BlockSpec((rows * ROW_TILE, LANES), lambda i, lo, hi: (i, 0))],
        out_specs=any_spec,
        scratch_shapes=[pltpu.VMEM((ROW_TILE, LANES), src.dtype), pltpu.SemaphoreType.DMA(()),
                        pltpu.SemaphoreType.DMA(())],
    )
    return pl.pallas_call(
        functools.partial(_scatter_kernel, rows=rows),
        out_shape=jax.ShapeDtypeStruct((total_rows * ROW_TILE, LANES), src.dtype),
        grid_spec=grid_spec,
        compiler_params=pltpu.CompilerParams(dimension_semantics=("arbitrary",), has_side_effects=True),
        name="moe_scatter_rows",
    )(pad_lo, pad_hi, d1.reshape(n // rows, 1, rows), d2.reshape(n // rows, 1, rows), src)


def _expert_kernel(be_ref, x_ref, wg_ref, wu_ref, wd_ref, o_ref, *, block_rows, tf):
    del be_ref
    x = _load_row_tiles(x_ref, block_rows).astype(BF16)
    ff = wg_ref.shape[2]
    acc = jnp.zeros((block_rows, wd_ref.shape[2]), F32)
    for f in range(0, ff, tf):
        g = _dot(x, wg_ref[0, :, f:f + tf])
        u = _dot(x, wu_ref[0, :, f:f + tf])
        a = (g * _sigmoid(g) * u).astype(BF16)
        acc = acc + _dot(a, wd_ref[0, f:f + tf, :])
    _store_row_tiles(o_ref, acc, block_rows)


def _expert_ffn(xb, blk_expert, wg, wu, wd, block_rows):
    d, ff = wg.shape[1], wg.shape[2]
    rows = xb.shape[0] // ROW_TILE
    resident = pl.Buffered(1)
    grid_spec = pltpu.PrefetchScalarGridSpec(
        num_scalar_prefetch=1,
        grid=(rows // block_rows,),
        in_specs=[
            pl.BlockSpec((block_rows * ROW_TILE, LANES), lambda i, be: (i, 0)),
            pl.BlockSpec((1, d, ff), lambda i, be: (be[i], 0, 0), pipeline_mode=resident),
            pl.BlockSpec((1, d, ff), lambda i, be: (be[i], 0, 0), pipeline_mode=resident),
            pl.BlockSpec((1, ff, d), lambda i, be: (be[i], 0, 0), pipeline_mode=resident),
        ],
        out_specs=pl.BlockSpec((block_rows * ROW_TILE, LANES), lambda i, be: (i, 0)),
    )
    return pl.pallas_call(
        functools.partial(_expert_kernel, block_rows=block_rows, tf=MXU_DIM),
        out_shape=jax.ShapeDtypeStruct(xb.shape, F32), grid_spec=grid_spec,
        compiler_params=_cparams(("arbitrary",)), name="moe_expert_ffn",
    )(blk_expert, xb, wg, wu, wd)


def _combine_kernel(d1_ref, d2_ref, d1n_ref, d2n_ref, x_ref, yb_hbm, gates_ref, gate_ref, fg_ref, o_ref,
                    y1_ref, y2_ref, sems):
    tm = x_ref.shape[0]
    i = pl.program_id(0)
    slot = i % 2

    def gather(i1_ref, i2_ref, to_slot):
        def start(r, c):
            _row_copy(yb_hbm, i1_ref[0, 0, r], y1_ref.at[to_slot], r, sems.at[to_slot]).start(priority=0)
            _row_copy(yb_hbm, i2_ref[0, 0, r], y2_ref.at[to_slot], r, sems.at[to_slot]).start(priority=1)
            return c

        lax.fori_loop(0, tm, start, 0, unroll=DMA_UNROLL)

    @pl.when(i == 0)
    def _():
        gather(d1_ref, d2_ref, slot)

    @pl.when(i + 1 < pl.num_programs(0))
    def _():
        gather(d1n_ref, d2n_ref, 1 - slot)

    for y_ref in (y1_ref, y2_ref):
        pltpu.make_async_copy(yb_hbm.at[pl.ds(0, tm * ROW_TILE)], y_ref.at[slot], sems.at[slot]).wait()

    gts = gates_ref[...]
    y = (gts[:, 0:1] * _load_row_tiles(y1_ref.at[slot], tm)
         + gts[:, LANES // 2:LANES // 2 + 1] * _load_row_tiles(y2_ref.at[slot], tm))
    xn = x_ref[...] + gate_ref[0] * y
    ms = jnp.mean(xn * xn, axis=-1, keepdims=True)
    o_ref[...] = xn * lax.rsqrt(ms + EPS) * fg_ref[...]


def _combine_final(x, yb, d1, d2, gates, gate, final_g, tm):
    n, d = x.shape
    s = n // gate.shape[0]
    row = lambda i: (i, 0)
    steps = n // tm
    idx_spec = pl.BlockSpec((1, 1, tm), lambda i: (i, 0, 0), memory_space=pltpu.SMEM)
    next_spec = pl.BlockSpec((1, 1, tm), lambda i: (jnp.minimum(i + 1, steps - 1), 0, 0), memory_space=pltpu.SMEM)
    d1 = d1.reshape(steps, 1, tm)
    d2 = d2.reshape(steps, 1, tm)
    return pl.pallas_call(
        _combine_kernel,
        out_shape=jax.ShapeDtypeStruct((n, d), F32),
        grid=(steps,),
        in_specs=[idx_spec, idx_spec, next_spec, next_spec,
                  pl.BlockSpec((tm, d), row),
                  pl.BlockSpec(memory_space=pl.ANY),
                  pl.BlockSpec((tm, LANES), row),
                  pl.BlockSpec((1, 1, d), lambda i: ((i * tm) // s, 0, 0)),
                  pl.BlockSpec((1, d), lambda i: (0, 0))],
        out_specs=pl.BlockSpec((tm, d), row),
        scratch_shapes=[pltpu.VMEM((2, tm * ROW_TILE, LANES), F32), pltpu.VMEM((2, tm * ROW_TILE, LANES), F32),
                        pltpu.SemaphoreType.DMA((2,))],
        compiler_params=_cparams(("arbitrary",)), name="moe_combine_final",
    )(d1, d2, d1, d2, x, yb, gates, gate, final_g)


AMAX_QA, AMAX_KA, AMAX_VA, AMAX_QC, AMAX_KC, AMAX_VC = range(6)
FP8_TARGET_MAX = 256.0


def _fp8_scales(q_max, k_max, v_max, head_dim):
    tiny = jnp.finfo(F32).tiny
    c = head_dim ** -0.5 * LOG2E
    ratio = jnp.where((q_max > 0) & (k_max > 0), c * k_max / jnp.maximum(q_max, tiny), 1.0)
    sq = jnp.exp2(jnp.round(0.5 * jnp.log2(ratio)))
    sk = c / sq
    sv = jnp.where(v_max > 0, jnp.exp2(jnp.floor(jnp.log2(FP8_TARGET_MAX / jnp.maximum(v_max, tiny)))), 1.0)
    rows = jnp.stack([sq, sk, sv, 1.0 / sv] + [jnp.zeros_like(sq)] * (SUBLANES - 4), axis=1)
    return jnp.broadcast_to(rows[:, :, None], rows.shape + (LANES,)).astype(F32)


def _rope_tables(s, dim):
    half = dim // 2
    t = jnp.arange(s)
    inv = 1.0 / (ROPE_THETA ** (jnp.arange(0, half, 2, dtype=F32) / half))
    ang_r = (t // GRID_W).astype(F32)[:, None] * inv
    ang_c = (t % GRID_W).astype(F32)[:, None] * inv
    ang = jnp.concatenate([ang_r, ang_r, ang_c, ang_c], axis=-1)
    reps = LANES // dim
    return jnp.tile(jnp.cos(ang), (1, reps)), jnp.tile(jnp.sin(ang), (1, reps))


def _rotate_matrix(dim):
    q = dim // 4
    p = np.zeros((MXU_DIM, MXU_DIM), np.float32)
    for j in range(MXU_DIM):
        if (j % (2 * q)) < q:
            p[j + q, j] = -1.0
        else:
            p[j - q, j] = 1.0
    return jnp.asarray(p, BF16)


def _head_mean_matrix():
    m = np.kron(np.eye(MXU_DIM // HEAD_V, dtype=np.float32), np.full((HEAD_V, HEAD_V), 1.0 / HEAD_V, np.float32))
    return jnp.asarray(m, BF16)


def _widen_values(w, heads):
    d = w.shape[0]
    w = w.reshape(d, heads, HEAD_V)
    return jnp.concatenate([w, jnp.zeros_like(w)], axis=-1).reshape(d, heads * LANES)


def _widen_in_proj(w):
    qa, ka, va, gb, qc, kc, vc = jnp.split(w, [256, 512, 768, 1280, 1792, 1920], axis=1)
    return jnp.concatenate([qa, ka, _widen_values(va, DIFF_HEADS), gb, qc, kc, _widen_values(vc, GQA_KV)],
                           axis=1).astype(BF16)


def kernel(x, c, ctx, c_ctx, ada_w, ada_b, norm1_g, norm2_g, w_in, w_out, lam_q1, lam_k1, lam_q2, lam_k2,
           diff_subln_g, conv_w, conv_b, conv_ln_g, conv_ln_b, q_norm_g, k_norm_g, ffn_gate, ffn_up, ffn_down,
           router_w, moe_gate, moe_up, moe_down, final_g):
    b, s, d = x.shape
    sc = ctx.shape[1]
    depth = ada_w.shape[0]
    n = b * s
    assert depth % 2 == 0, "the final RMSNorm is fused into the MoE combine of the last (odd) layer"

    tm = min(ROW_BLOCK, s)
    tmc = min(ROW_BLOCK, sc)
    tq = min(GQA_QUERY_BLOCK, s)
    tq_diff = min(DIFF_QUERY_BLOCK, s)
    tqc = min(GQA_QUERY_BLOCK, sc)

    tabs_x = _rope_tables(s, DIFF_QK) + _rope_tables(s, HEAD_V)
    ones_c, zeros_c = jnp.ones((sc, LANES), F32), jnp.zeros((sc, LANES), F32)
    tabs_c = (ones_c, zeros_c, ones_c, zeros_c)
    mats = (_rotate_matrix(DIFF_QK), _rotate_matrix(HEAD_V), _head_mean_matrix())

    cc = jnp.zeros((16, d), F32).at[:b].set(c).at[b].set(c_ctx)

    for i in range(depth):
        last = i == depth - 1
        lam_init = 0.8 - 0.6 * math.exp(-0.3 * i)
        mod_all = _ada_mod(cc, ada_w[i], ada_b[i])
        mod = mod_all[:b].reshape(b, 6, 1, d)
        modc = jnp.broadcast_to(mod_all[b].reshape(1, 6, 1, d), (b, 6, 1, d))

        w_aug = _widen_in_proj(w_in[i])
        g1 = norm1_g[i].reshape(1, d)
        qg = jnp.tile(q_norm_g[i], GQA_HEADS).reshape(1, -1)
        kg = jnp.tile(k_norm_g[i], GQA_KV).reshape(1, -1)
        lam_vecs = jnp.stack([lam_q1[i], lam_k1[i], lam_q2[i], lam_k2[i]]).astype(F32)
        subln = diff_subln_g[i].reshape(1, HEAD_V)
        conv_args = (conv_w[i], conv_b[i].reshape(1, -1), conv_ln_g[i].reshape(1, -1), conv_ln_b[i].reshape(1, -1))
        w_o = w_out[i].astype(BF16)
        g2 = norm2_g[i].reshape(1, d)

        qa, kat, va, gb, qc, kct, vc, amax = _in_projection(
            x, mod[:, 0], mod[:, 1], g1, w_aug, tabs_x, mats, qg, kg, tm, math.gcd(s // tm, INPROJ_SUBTILES))
        qa_x, kat_x, va_x, gb_x, qc_x, kct_x, vc_x, amax_x = _in_projection(
            ctx, modc[:, 0], modc[:, 1], g1, w_aug, tabs_c, mats, qg, kg, tmc, 1)
        amax = jnp.max(amax, axis=(1, 3))
        amax_x = jnp.max(amax_x, axis=(1, 3))
        amax_kv = jnp.maximum(amax, amax_x)
        scl_a = _fp8_scales(amax[:, AMAX_QA], amax_kv[:, AMAX_KA], amax_kv[:, AMAX_VA], DIFF_QK)
        scl_c = _fp8_scales(amax[:, AMAX_QC], amax_kv[:, AMAX_KC], amax_kv[:, AMAX_VC], HEAD_V)

        oa = _diff_attention(qa, [kat, kat_x], [va, va_x], scl_a, lam_vecs, subln, lam_init, tq_diff)
        ob = _conformer_conv(gb, *conv_args, tm)
        oc = _gqa_attention(qc, [kct, kct_x], [vc, vc_x], scl_c, tq)

        j = i // 2
        if i % 2 == 0:
            wg, wu, wd = ffn_gate[j].astype(BF16), ffn_up[j].astype(BF16), ffn_down[j].astype(BF16)
            x = _merge_dense_ffn(oa, ob, oc, x, mod[:, 2], mod[:, 3], mod[:, 4], g2, w_o, mod[:, 5], wg, wu, wd, tm)
        else:
            rwt = router_w[j].T.astype(F32)
            x, h2, logits_t = _merge_route(oa, ob, oc, x, mod[:, 2], mod[:, 3], mod[:, 4], g2, w_o, rwt, tm)
            block_rows = min(MOE_BLOCK_ROWS, n // 4)
            n_blocks = (2 * n) // block_rows + N_EXPERTS
            dest, gates, blk_e, pad = _route(logits_t, block_rows, n_blocks, min(ROW_BLOCK, n))
            xb = _scatter_rows(h2, dest[0], dest[1], pad[:, 0], pad[:, LANES // 2], n_blocks * block_rows,
                               min(MOE_DMA_ROWS, n))
            yb = _expert_ffn(xb, blk_e[0, :n_blocks], moe_gate[j].astype(BF16), moe_up[j].astype(BF16),
                             moe_down[j].astype(BF16), block_rows)
            assert last
            x = _combine_final(x.reshape(n, d), yb, dest[0], dest[1], gates, mod[:, 5], final_g.reshape(1, d),
                               tm).reshape(b, s, d)

        if not last:
            scl_ax = _fp8_scales(amax_x[:, AMAX_QA], amax_x[:, AMAX_KA], amax_x[:, AMAX_VA], DIFF_QK)
            scl_cx = _fp8_scales(amax_x[:, AMAX_QC], amax_x[:, AMAX_KC], amax_x[:, AMAX_VC], HEAD_V)
            oa_x = _diff_attention(qa_x, [kat_x], [va_x], scl_ax, lam_vecs, subln, lam_init, tqc)
            ob_x = _conformer_conv(gb_x, *conv_args, tmc)
            oc_x = _gqa_attention(qc_x, [kct_x], [vc_x], scl_cx, tqc)
            assert i % 2 == 0, "context tokens only ever pass through dense channel mixers"
            ctx = _merge_dense_ffn(oa_x, ob_x, oc_x, ctx, modc[:, 2], modc[:, 3], modc[:, 4], g2, w_o, modc[:, 5],
                                   wg, wu, wd, tmc)

    return x
```

```python
import functools
import math

import numpy as np
import jax
import jax.numpy as jnp
from jax import lax
from jax.experimental import pallas as pl
from jax.experimental.pallas import tpu as pltpu

F32 = jnp.float32
BF16 = jnp.bfloat16
I32 = jnp.int32

EPS = 1e-6
ROPE_THETA = 10000.0
GRID_W = 64

DIFF_HEADS = 4
DIFF_QK = 32
HEAD_V = 64
GQA_HEADS = 8
GQA_KV = 2
GQA_GROUP = GQA_HEADS // GQA_KV
CONV_K = 31
N_EXPERTS = 8
LOG2E = math.log2(math.e)

LANES = 128
SUBLANES = 8
MXU_DIM = 256
VMEM_LIMIT = 52 * 1024 * 1024
NEG_BIG = -1e30
V_DTYPE = jnp.float8_e4m3fn
QK_DTYPE = jnp.float8_e4m3fn
P_E4M3 = (jnp.float8_e4m3fn, 8.5)
P_E5M2 = (jnp.float8_e5m2, 15.5)
P_HEADROOM_RUNNING = 8.0
ATTN_UNROLL = 16
DMA_UNROLL = 8
INPROJ_SUBTILES = 2
ROW_BLOCK = 512
GQA_QUERY_BLOCK = 256
DIFF_QUERY_BLOCK = 512
MOE_BLOCK_ROWS = 512
MOE_DMA_ROWS = 1024

C_QA, C_KA, C_VA, C_GB, C_QC, C_KC, C_VC, C_END = 0, 256, 512, 1024, 1536, 2048, 2176, 2432


def _cparams(semantics):
    return pltpu.CompilerParams(dimension_semantics=semantics, vmem_limit_bytes=VMEM_LIMIT)


def _dot(a, b):
    return jnp.dot(a, b, preferred_element_type=F32)


def _sigmoid(z):
    return 1.0 / (1.0 + jnp.exp(-z))


ROW_TILE = 8


def _store_row_tiles(ref, val, rows):
    for a in range(ROW_TILE):
        ref[pl.ds(a, rows, stride=ROW_TILE), :] = val[:, a * LANES:(a + 1) * LANES]


def _load_row_tiles(ref, rows):
    return jnp.concatenate([ref[pl.ds(a, rows, stride=ROW_TILE), :] for a in range(ROW_TILE)], axis=1)


def _mod_kernel(c_ref, w_ref, b_ref, o_ref):
    c = c_ref[...]
    s = c * _sigmoid(c)
    o_ref[...] = jnp.dot(s, w_ref[...], preferred_element_type=F32, precision=lax.Precision.HIGHEST) + b_ref[...]


def _ada_mod(cc, w, b):
    rows, d = cc.shape
    n = w.shape[1]
    tn = d
    return pl.pallas_call(
        _mod_kernel,
        out_shape=jax.ShapeDtypeStruct((rows, n), F32),
        grid=(n // tn,),
        in_specs=[pl.BlockSpec((rows, d), lambda j: (0, 0)),
                  pl.BlockSpec((d, tn), lambda j: (0, j)),
                  pl.BlockSpec((1, tn), lambda j: (0, j))],
        out_specs=pl.BlockSpec((rows, tn), lambda j: (0, j)),
        compiler_params=_cparams(("arbitrary",)),
        name="ada_mod",
    )(cc, w, b.reshape(1, n))


def _inproj_kernel(x_ref, shift_ref, scale_ref, g_ref, w_ref, cosa_ref, sina_ref, cosc_ref, sinc_ref,
                   pa_ref, pc_ref, hm_ref, qg_ref, kg_ref,
                   qa_o, kat_o, va_o, gb_o, qc_o, kct_o, vc_o, amax_o):
    n_sub, sub = kat_o.shape[1], kat_o.shape[3]
    stats = None
    for t in range(n_sub):
        rows = slice(t * sub, (t + 1) * sub)
        tabs = [ref[rows, :] for ref in (cosa_ref, sina_ref, cosc_ref, sinc_ref)]
        outs = [o.at[0, rows, :] for o in (qa_o, va_o, gb_o, qc_o, vc_o)] + [kat_o.at[0, t], kct_o.at[0, t]]
        st = _inproj_subtile(x_ref[0, rows, :], shift_ref, scale_ref, g_ref, w_ref, tabs, pa_ref, pc_ref, hm_ref,
                             qg_ref, kg_ref, outs, amax_o.shape[3])
        stats = st if stats is None else jnp.maximum(stats, st)
    amax_o[0, 0] = stats


def _inproj_subtile(x, shift_ref, scale_ref, g_ref, w_ref, tabs, pa_ref, pc_ref, hm_ref, qg_ref, kg_ref, outs, stat_w):
    qa_o, va_o, gb_o, qc_o, vc_o, kat_o, kct_o = outs
    cosa, sina, cosc, sinc = tabs
    ms = jnp.mean(x * x, axis=-1, keepdims=True)
    h = x * lax.rsqrt(ms + EPS) * g_ref[...]
    h = h * (1.0 + scale_ref[0]) + shift_ref[0]
    hb = h.astype(BF16)

    def proj(lo, hi):
        return _dot(hb, w_ref[:, lo:hi])

    def blockmat(y, m_ref):
        yb = y.astype(BF16)
        w = y.shape[1]
        if w == LANES:
            return _dot(yb, m_ref[:LANES, :LANES])
        return jnp.concatenate([_dot(yb[:, c:c + MXU_DIM], m_ref[...]) for c in range(0, w, MXU_DIM)], axis=1)

    def rope(y, cos, sin, p_ref):
        reps = y.shape[1] // LANES
        cos = jnp.tile(cos, (1, reps))
        sin = jnp.tile(sin, (1, reps))
        return y * cos + blockmat(y, p_ref) * sin

    def ones_col(width):
        lane = lax.broadcasted_iota(I32, (1, width), 1)
        return jnp.where(lane % LANES >= HEAD_V, 1.0, 0.0).astype(F32)

    def col_amax(y):
        cm = jnp.max(jnp.abs(y), axis=0, keepdims=True)
        pad = stat_w - y.shape[1]
        return cm if pad == 0 else jnp.concatenate([cm, jnp.zeros((1, pad), F32)], axis=1)

    qa = rope(proj(C_QA, C_KA), cosa, sina, pa_ref)
    qa_o[...] = qa.astype(BF16)
    ka = rope(proj(C_KA, C_VA), cosa, sina, pa_ref)
    kat_o[...] = ka.T.astype(BF16)
    va = proj(C_VA, C_GB)
    va_o[...] = (va + ones_col(C_GB - C_VA)).astype(BF16)
    gb_o[...] = proj(C_GB, C_QC).astype(BF16)

    y = proj(C_QC, C_KC)
    yn = y * lax.rsqrt(blockmat(y * y, hm_ref) + EPS) * qg_ref[...]
    qc = rope(yn, cosc, sinc, pc_ref)
    qc_o[...] = qc.astype(BF16)

    y = proj(C_KC, C_VC)
    yn = y * lax.rsqrt(blockmat(y * y, hm_ref) + EPS) * kg_ref[...]
    kc = rope(yn, cosc, sinc, pc_ref)
    kct_o[...] = kc.T.astype(BF16)

    vc = proj(C_VC, C_END)
    vc_o[...] = (vc + ones_col(C_END - C_VC)).astype(BF16)

    stats = [col_amax(t) for t in (qa, ka, va, qc, kc, vc)]
    stats += [jnp.zeros((1, stat_w), F32)] * (SUBLANES - len(stats))
    return jnp.concatenate(stats, axis=0)


def _in_projection(x, shift, scale, g1, w_aug, tabs, mats, qg, kg, tk, n_sub):
    b, s, d = x.shape
    tm = tk * n_sub
    nt = s // tm
    cosa, sina, cosc, sinc = tabs
    pa, pc, hm = mats
    row = lambda bb, i: (bb, i, 0)
    const2 = lambda bb, i: (0, 0)
    per_b = lambda bb, i: (bb, 0, 0)
    tab = lambda bb, i: (i, 0)
    out_shape = (
        jax.ShapeDtypeStruct((b, s, 256), BF16),
        jax.ShapeDtypeStruct((b, s // tk, 256, tk), BF16),
        jax.ShapeDtypeStruct((b, s, 512), BF16),
        jax.ShapeDtypeStruct((b, s, 512), BF16),
        jax.ShapeDtypeStruct((b, s, 512), BF16),
        jax.ShapeDtypeStruct((b, s // tk, 128, tk), BF16),
        jax.ShapeDtypeStruct((b, s, 256), BF16),
        jax.ShapeDtypeStruct((b, nt, SUBLANES, 512), F32),
    )
    out_specs = (
        pl.BlockSpec((1, tm, 256), row),
        pl.BlockSpec((1, n_sub, 256, tk), lambda bb, i: (bb, i, 0, 0)),
        pl.BlockSpec((1, tm, 512), row),
        pl.BlockSpec((1, tm, 512), row),
        pl.BlockSpec((1, tm, 512), row),
        pl.BlockSpec((1, n_sub, 128, tk), lambda bb, i: (bb, i, 0, 0)),
        pl.BlockSpec((1, tm, 256), row),
        pl.BlockSpec((1, 1, SUBLANES, 512), lambda bb, i: (bb, i, 0, 0)),
    )
    in_specs = [
        pl.BlockSpec((1, tm, d), row),
        pl.BlockSpec((1, 1, d), per_b),
        pl.BlockSpec((1, 1, d), per_b),
        pl.BlockSpec((1, d), const2),
        pl.BlockSpec((d, C_END), const2),
        pl.BlockSpec((tm, LANES), tab), pl.BlockSpec((tm, LANES), tab),
        pl.BlockSpec((tm, LANES), tab), pl.BlockSpec((tm, LANES), tab),
        pl.BlockSpec((MXU_DIM, MXU_DIM), const2), pl.BlockSpec((MXU_DIM, MXU_DIM), const2),
        pl.BlockSpec((MXU_DIM, MXU_DIM), const2),
        pl.BlockSpec((1, 512), const2), pl.BlockSpec((1, 128), const2),
    ]
    return pl.pallas_call(
        _inproj_kernel, out_shape=out_shape, grid=(b, nt), in_specs=in_specs, out_specs=out_specs,
        compiler_params=_cparams(("arbitrary", "arbitrary")), name="in_projection",
    )(x, shift, scale, g1, w_aug, cosa, sina, cosc, sinc, pa, pc, hm, qg, kg)


SCL_Q, SCL_K, SCL_V, SCL_V_INV = 0, 1, 2, 3


LO_GAIN = 16.0
V_ROW_CHUNK = 512


def _split_fp8(x, dtype):
    hi = x.astype(dtype)
    return hi, ((x - hi.astype(F32)) * LO_GAIN).astype(dtype)


def _stack_qk(x, other_side, dtype, axis, width):
    hi, lo = _split_fp8(x, dtype)
    hi_small = (hi.astype(F32) * (1.0 / LO_GAIN)).astype(dtype)
    parts = [hi, lo, hi_small] if other_side else [hi, hi_small, lo]
    pad = width - 3 * x.shape[axis]
    if pad:
        pad_shape = tuple(pad if a == axis else n for a, n in enumerate(x.shape))
        parts.append(jnp.zeros(pad_shape, dtype))
    return jnp.concatenate(parts, axis=axis)


def _quantize_kv(k_refs, v_refs, k8_refs, v8_refs, scl_ref, dk):
    sk = scl_ref[0, SCL_K:SCL_K + 1, 0:1]
    sv = scl_ref[0, SCL_V:SCL_V + 1, 0:1]
    for k_ref, k8_ref in zip(k_refs, k8_refs):
        n_maps = k_ref.shape[2] // dk
        stack = k8_ref.shape[1] // n_maps

        def k_body(c, carry, k_ref=k_ref, k8_ref=k8_ref, n_maps=n_maps, stack=stack):
            kf = k_ref[0, c].astype(F32) * sk
            k8_ref[c] = jnp.concatenate(
                [_stack_qk(kf[dk * j:dk * (j + 1), :], True, k8_ref.dtype, 0, stack) for j in range(n_maps)], axis=0)
            return carry

        lax.fori_loop(0, k_ref.shape[1], k_body, 0)

    for v_ref, v8_ref in zip(v_refs, v8_refs):
        rows = min(V_ROW_CHUNK, v_ref.shape[1])

        def v_body(c, carry, v_ref=v_ref, v8_ref=v8_ref, rows=rows):
            sl = pl.ds(pl.multiple_of(c * rows, rows), rows)
            vf = v_ref[0, sl, :].astype(F32)
            groups = []
            for g in range(v_ref.shape[2] // LANES):
                vs = vf[:, LANES * g:LANES * g + HEAD_V] * sv
                hi = vs.astype(v8_ref.dtype)
                lo = (vs - hi.astype(F32)).astype(v8_ref.dtype)
                ones = vf[:, LANES * g + HEAD_V:LANES * (g + 1)].astype(v8_ref.dtype)
                groups += [hi, ones, lo, jnp.zeros((rows, HEAD_V), v8_ref.dtype)]
            v8_ref[sl, :] = jnp.concatenate(groups, axis=1)
            return carry

        lax.fori_loop(0, v_ref.shape[1] // rows, v_body, 0)


def _attention_sweeps(qms, k_slices, pv_groups, k_refs, v_refs, m_ref, smax_ref, acc_ref, p_format, finalize):
    r = qms[0].shape[0]
    p_dtype, p_max_exp = p_format

    def scores(j, kc):
        return _dot(qms[j], kc[k_slices[j], :])

    def sweep(running_max):
        acc_ref[...] = jnp.zeros(acc_ref.shape, F32)
        if running_max:
            m_ref[...] = jnp.full(m_ref.shape, NEG_BIG, F32)
        else:
            smax_ref[...] = jnp.full(smax_ref.shape, NEG_BIG, smax_ref.dtype)
            kc0 = k_refs[0][0][:, :MXU_DIM]
            for j in range(len(qms)):
                m0 = jnp.max(scores(j, kc0), axis=-1, keepdims=True)
                m_ref[j * r:(j + 1) * r, :] = jnp.broadcast_to(m0, (r, LANES))

        for k_ref, v_ref in zip(k_refs, v_refs):
            n_chunks, tk = k_ref.shape[0], k_ref.shape[2]

            def body(c, carry, k_ref=k_ref, v_ref=v_ref, tk=tk):
                kc = k_ref[c]
                vc = v_ref[pl.ds(pl.multiple_of(c * tk, tk), tk), :]
                for ids, v_lanes in pv_groups:
                    ps, alphas = [], []
                    for j in ids:
                        rows = slice(j * r, (j + 1) * r)
                        s = scores(j, kc)
                        m = m_ref[rows, :]
                        if running_max:
                            m_new = jnp.maximum(m, jnp.max(s, axis=-1, keepdims=True) - P_HEADROOM_RUNNING)
                            m_ref[rows, :] = m_new
                            alphas.append(jnp.exp2(m - m_new))
                            m = m_new
                        d = (s - jnp.tile(m, (1, tk // LANES))).astype(BF16)
                        if not running_max:
                            cm = functools.reduce(jnp.maximum, [d[:, l:l + LANES] for l in range(0, tk, LANES)])
                            smax_ref[rows, :] = jnp.maximum(smax_ref[rows, :], cm)
                        ps.append(jnp.exp2(d).astype(p_dtype))
                    rows = slice(ids[0] * r, (ids[-1] + 1) * r)
                    pv = _dot(jnp.concatenate(ps, axis=0), vc[:, v_lanes])
                    pv = pv[:, :LANES] + pv[:, LANES:]
                    if running_max:
                        acc_ref[rows, :] = acc_ref[rows, :] * jnp.concatenate(alphas, axis=0) + pv
                    else:
                        acc_ref[rows, :] += pv
                return carry

            unroll = 1 if running_max else math.gcd(n_chunks, ATTN_UNROLL)
            lax.fori_loop(0, n_chunks, body, 0, unroll=unroll)

    sweep(False)
    finalize()
    top_exp = jnp.max(smax_ref[...].astype(F32))

    @pl.when(jnp.logical_not(top_exp <= p_max_exp))
    def _():
        sweep(True)
        finalize()


def _gqa_kernel(*refs, n_parts, tq):
    q_ref = refs[0]
    k_refs = refs[1:1 + n_parts]
    v_refs = refs[1 + n_parts:1 + 2 * n_parts]
    scl_ref, o_ref, m_ref, smax_ref, acc_ref = refs[1 + 2 * n_parts:6 + 2 * n_parts]
    k8_refs = refs[6 + 2 * n_parts:6 + 3 * n_parts]
    v8_refs = refs[6 + 3 * n_parts:]

    @pl.when(pl.program_id(1) == 0)
    def _():
        _quantize_kv(k_refs, v_refs, k8_refs, v8_refs, scl_ref, HEAD_V)

    stack = k8_refs[0].shape[1] // GQA_KV
    q = q_ref[0].astype(F32) * scl_ref[0, SCL_Q:SCL_Q + 1, 0:1]
    qms = [jnp.concatenate([_stack_qk(q[:, HEAD_V * h:HEAD_V * (h + 1)], False, QK_DTYPE, 1, stack)
                            for h in range(GQA_GROUP * g, GQA_GROUP * (g + 1))], axis=0) for g in range(GQA_KV)]
    k_slices = [slice(stack * g, stack * (g + 1)) for g in range(GQA_KV)]
    pv_groups = [((g,), slice(2 * LANES * g, 2 * LANES * (g + 1))) for g in range(GQA_KV)]

    def finalize():
        o = _attention_output(acc_ref[...], scl_ref)
        o_ref[0] = jnp.concatenate([o[h * tq:(h + 1) * tq] for h in range(GQA_HEADS)], axis=1).astype(BF16)

    _attention_sweeps(qms, k_slices, pv_groups, k8_refs, v8_refs, m_ref, smax_ref, acc_ref, P_E4M3, finalize)


def _attention_output(acc, scl_ref):
    return acc[:, :HEAD_V] * scl_ref[0, SCL_V_INV:SCL_V_INV + 1, 0:1] / acc[:, HEAD_V:]


def _softmax_scratch(rows):
    return [pltpu.VMEM((rows, LANES), F32), pltpu.VMEM((rows, LANES), BF16), pltpu.VMEM((rows, LANES), F32)]


def _stack_height(dk):
    return max(4 * dk, LANES)


def _fp8_kv_scratch(k_blocks, v_blocks, dk):
    ks = [pltpu.VMEM((kb[1], kb[2] // dk * _stack_height(dk), kb[3]), QK_DTYPE) for kb in k_blocks]
    vs = [pltpu.VMEM((vb[1], 2 * vb[2]), V_DTYPE) for vb in v_blocks]
    return ks + vs


def _gqa_attention(q, k_parts, v_parts, scales, tq):
    b, sq, _ = q.shape
    n_parts = len(k_parts)
    k_blocks = [(1, kp.shape[1], GQA_KV * HEAD_V, kp.shape[3]) for kp in k_parts]
    v_blocks = [(1, vp.shape[1], GQA_KV * LANES) for vp in v_parts]
    width = GQA_HEADS * HEAD_V
    in_specs = [pl.BlockSpec((1, tq, width), lambda bb, i: (bb, i, 0))]
    in_specs += [pl.BlockSpec(kb, lambda bb, i: (bb, 0, 0, 0)) for kb in k_blocks]
    in_specs += [pl.BlockSpec(vb, lambda bb, i: (bb, 0, 0)) for vb in v_blocks]
    in_specs.append(pl.BlockSpec((1, SUBLANES, LANES), lambda bb, i: (bb, 0, 0)))
    return pl.pallas_call(
        functools.partial(_gqa_kernel, n_parts=n_parts, tq=tq),
        out_shape=jax.ShapeDtypeStruct((b, sq, width), BF16),
        grid=(b, sq // tq),
        in_specs=in_specs,
        out_specs=pl.BlockSpec((1, tq, width), lambda bb, i: (bb, i, 0)),
        scratch_shapes=_softmax_scratch(GQA_HEADS * tq) + _fp8_kv_scratch(k_blocks, v_blocks, HEAD_V),
        compiler_params=_cparams(("arbitrary", "arbitrary")),
        name="gqa_attention",
    )(q, *k_parts, *v_parts, scales)


def _diff_kernel(*refs, n_parts, tq, lam_init):
    q_ref = refs[0]
    k_refs = refs[1:1 + n_parts]
    v_refs = refs[1 + n_parts:1 + 2 * n_parts]
    scl_ref, lam_ref, sg_ref, hm_ref, o_ref, m_ref, smax_ref, acc_ref = refs[1 + 2 * n_parts:9 + 2 * n_parts]
    k8_refs = refs[9 + 2 * n_parts:9 + 3 * n_parts]
    v8_refs = refs[9 + 3 * n_parts:]

    @pl.when(pl.program_id(2) == 0)
    def _():
        _quantize_kv(k_refs, v_refs, k8_refs, v8_refs, scl_ref, DIFF_QK)

    stack = k8_refs[0].shape[1] // 4
    q = q_ref[0].astype(F32) * scl_ref[0, SCL_Q:SCL_Q + 1, 0:1]
    qmaps = [_stack_qk(q[:, DIFF_QK * j:DIFF_QK * (j + 1)], False, QK_DTYPE, 1, stack) for j in range(4)]
    k_slices = [slice(stack * j, stack * (j + 1)) for j in range(4)]
    pv_groups = [((0, 1), slice(0, 2 * LANES)), ((2, 3), slice(2 * LANES, 4 * LANES))]
    lv = lam_ref[...]
    lam = (jnp.exp(jnp.sum(lv[0:1] * lv[1:2], axis=-1, keepdims=True))
           - jnp.exp(jnp.sum(lv[2:3] * lv[3:4], axis=-1, keepdims=True)) + lam_init)

    def finalize():
        outs = _attention_output(acc_ref[...], scl_ref)
        o = jnp.concatenate([outs[(2 * hh) * tq:(2 * hh + 1) * tq] - lam * outs[(2 * hh + 1) * tq:(2 * hh + 2) * tq]
                             for hh in range(2)], axis=1)
        ms = _dot((o * o).astype(BF16), hm_ref[...])
        o_ref[0] = (o * lax.rsqrt(ms + EPS) * sg_ref[...] * (1.0 - lam_init)).astype(BF16)

    _attention_sweeps(qmaps, k_slices, pv_groups, k8_refs, v8_refs, m_ref, smax_ref, acc_ref, P_E5M2, finalize)


def _diff_attention(q, k_parts, v_parts, scales, lam_vecs, subln_g, lam_init, tq):
    b, sq, _ = q.shape
    n_parts = len(k_parts)
    k_blocks = [(1, kp.shape[1], LANES, kp.shape[3]) for kp in k_parts]
    v_blocks = [(1, vp.shape[1], 2 * LANES) for vp in v_parts]
    in_specs = [pl.BlockSpec((1, tq, LANES), lambda bb, p, i: (bb, i, p))]
    in_specs += [pl.BlockSpec(kb, lambda bb, p, i: (bb, 0, p, 0)) for kb in k_blocks]
    in_specs += [pl.BlockSpec(vb, lambda bb, p, i: (bb, 0, p)) for vb in v_blocks]
    in_specs.append(pl.BlockSpec((1, SUBLANES, LANES), lambda bb, p, i: (bb, 0, 0)))
    in_specs.append(pl.BlockSpec((4, DIFF_QK), lambda bb, p, i: (0, 0)))
    in_specs.append(pl.BlockSpec((1, LANES), lambda bb, p, i: (0, 0)))
    in_specs.append(pl.BlockSpec((LANES, LANES), lambda bb, p, i: (0, 0)))
    head_mean = _head_mean_matrix()[:LANES, :LANES]
    subln_pair = jnp.tile(subln_g, (1, LANES // HEAD_V))
    return pl.pallas_call(
        functools.partial(_diff_kernel, n_parts=n_parts, tq=tq, lam_init=lam_init),
        out_shape=jax.ShapeDtypeStruct((b, sq, 256), BF16),
        grid=(b, DIFF_HEADS // 2, sq // tq),
        in_specs=in_specs,
        out_specs=pl.BlockSpec((1, tq, LANES), lambda bb, p, i: (bb, i, p)),
        scratch_shapes=_softmax_scratch(4 * tq) + _fp8_kv_scratch(k_blocks, v_blocks, DIFF_QK),
        compiler_params=_cparams(("arbitrary", "arbitrary", "arbitrary")),
        name="diff_attention",
    )(q, *k_parts, *v_parts, scales, lam_vecs, subln_pair, head_mean)


CONV_HALO = 16
CONV_ROWS = 64


def _conv_kernel(gb_ref, prev_ref, next_ref, w_ref, b_ref, lg_ref, lb_ref, o_ref, u_ref, sh_ref, *, tm):
    i = pl.program_id(1)
    last = pl.num_programs(1) - 1
    ch = w_ref.shape[1]

    def glu(z):
        z = z.astype(F32)
        return z[:, :ch] * _sigmoid(z[:, ch:])

    u_ref[CONV_HALO:CONV_HALO + tm, :] = glu(gb_ref[0])
    u_ref[0:CONV_HALO, :] = jnp.where(i > 0, glu(prev_ref[0]), 0.0)
    u_ref[CONV_HALO + tm:2 * CONV_HALO + tm, :] = jnp.where(i < last, glu(next_ref[0]), 0.0)

    span = sh_ref.shape[1]
    for r in range(1, SUBLANES):
        sh_ref[r - 1] = u_ref[r:r + span, :]

    off = CONV_HALO - CONV_K // 2
    for r0 in range(0, tm, CONV_ROWS):
        acc = jnp.zeros((CONV_ROWS, ch), F32)
        for j in range(CONV_K):
            phase, base = (off + j) % SUBLANES, r0 + (off + j) // SUBLANES * SUBLANES
            taps = u_ref[base:base + CONV_ROWS, :] if phase == 0 else sh_ref[phase - 1, base:base + CONV_ROWS, :]
            acc = acc + taps * w_ref[j:j + 1, :]
        y = acc + b_ref[...]
        mu = jnp.mean(y, axis=-1, keepdims=True)
        yc = y - mu
        var = jnp.mean(yc * yc, axis=-1, keepdims=True)
        z = yc * lax.rsqrt(var + EPS) * lg_ref[...] + lb_ref[...]
        o_ref[0, r0:r0 + CONV_ROWS, :] = (z * _sigmoid(z)).astype(BF16)


def _conformer_conv(gb, w, bias, ln_g, ln_b, tm):
    b, s, two_ch = gb.shape
    ch = two_ch // 2
    hb = tm // CONV_HALO
    n_halo = s // CONV_HALO
    const2 = lambda bb, i: (0, 0)
    return pl.pallas_call(
        functools.partial(_conv_kernel, tm=tm),
        out_shape=jax.ShapeDtypeStruct((b, s, ch), BF16),
        grid=(b, s // tm),
        in_specs=[
            pl.BlockSpec((1, tm, two_ch), lambda bb, i: (bb, i, 0)),
            pl.BlockSpec((1, CONV_HALO, two_ch), lambda bb, i: (bb, jnp.maximum(i * hb - 1, 0), 0)),
            pl.BlockSpec((1, CONV_HALO, two_ch), lambda bb, i: (bb, jnp.minimum((i + 1) * hb, n_halo - 1), 0)),
            pl.BlockSpec((CONV_K, ch), const2), pl.BlockSpec((1, ch), const2),
            pl.BlockSpec((1, ch), const2), pl.BlockSpec((1, ch), const2),
        ],
        out_specs=pl.BlockSpec((1, tm, ch), lambda bb, i: (bb, i, 0)),
        scratch_shapes=[pltpu.VMEM((tm + 2 * CONV_HALO, ch), F32),
                        pltpu.VMEM((SUBLANES - 1, tm + 2 * CONV_HALO - SUBLANES, ch), F32)],
        compiler_params=_cparams(("arbitrary", "arbitrary")),
        name="conformer_conv",
    )(gb, gb, gb, w, bias, ln_g, ln_b)


def _merge_kernel(oa_ref, ob_ref, oc_ref, x_ref, gate_ref, shift_ref, scale_ref, g2_ref, w_ref, rw_ref,
                  xo_ref, h2_ref, lg_ref):
    tm = x_ref.shape[1]
    wa = oa_ref.shape[2]
    wb = wa + ob_ref.shape[2]
    y = _dot(oa_ref[0], w_ref[0:wa, :]) + _dot(ob_ref[0], w_ref[wa:wb, :]) + _dot(oc_ref[0], w_ref[wb:, :])
    xn = x_ref[0] + gate_ref[0] * y
    xo_ref[0] = xn
    ms = jnp.mean(xn * xn, axis=-1, keepdims=True)
    h2 = xn * lax.rsqrt(ms + EPS) * g2_ref[...] * (1.0 + scale_ref[0]) + shift_ref[0]
    _store_row_tiles(h2_ref, h2, tm)
    lg_ref[...] = lax.dot_general(rw_ref[...], h2, (((1,), (1,)), ((), ())),
                                  preferred_element_type=F32, precision=lax.Precision.HIGHEST)


def _merge_route(oa, ob, oc, x, gate, shift, scale, g2, w_out, router_wt, tm):
    b, s, d = x.shape
    row = lambda bb, i: (bb, i, 0)
    per_b = lambda bb, i: (bb, 0, 0)
    const2 = lambda bb, i: (0, 0)
    nt = s // tm
    in_specs = [
        pl.BlockSpec((1, tm, oa.shape[2]), row), pl.BlockSpec((1, tm, ob.shape[2]), row),
        pl.BlockSpec((1, tm, oc.shape[2]), row), pl.BlockSpec((1, tm, d), row),
        pl.BlockSpec((1, 1, d), per_b), pl.BlockSpec((1, 1, d), per_b), pl.BlockSpec((1, 1, d), per_b),
        pl.BlockSpec((1, d), const2), pl.BlockSpec((d, d), const2),
        pl.BlockSpec((N_EXPERTS, d), const2),
    ]
    out_shape = (jax.ShapeDtypeStruct((b, s, d), F32),
                 jax.ShapeDtypeStruct((b * s * ROW_TILE, LANES), F32),
                 jax.ShapeDtypeStruct((N_EXPERTS, b * s), F32))
    out_specs = (pl.BlockSpec((1, tm, d), row),
                 pl.BlockSpec((tm * ROW_TILE, LANES), lambda bb, i: (bb * nt + i, 0)),
                 pl.BlockSpec((N_EXPERTS, tm), lambda bb, i: (0, bb * nt + i)))
    return pl.pallas_call(
        _merge_kernel, out_shape=out_shape, grid=(b, nt), in_specs=in_specs, out_specs=out_specs,
        compiler_params=_cparams(("arbitrary", "arbitrary")), name="merge_route",
    )(oa, ob, oc, x, gate, shift, scale, g2, w_out, router_wt)


def _merge_ffn_kernel(oa_ref, ob_ref, oc_ref, x_ref, gate1_ref, shift_ref, scale_ref, g2_ref, w_ref, gate2_ref,
                      wg_ref, wu_ref, wd_ref, o_ref, *, tf):
    wa = oa_ref.shape[2]
    wb = wa + ob_ref.shape[2]
    y = _dot(oa_ref[0], w_ref[0:wa, :]) + _dot(ob_ref[0], w_ref[wa:wb, :]) + _dot(oc_ref[0], w_ref[wb:, :])
    xn = x_ref[0] + gate1_ref[0] * y
    ms = jnp.mean(xn * xn, axis=-1, keepdims=True)
    h = (xn * lax.rsqrt(ms + EPS) * g2_ref[...] * (1.0 + scale_ref[0]) + shift_ref[0]).astype(BF16)
    ff = wg_ref.shape[1]
    acc = jnp.zeros(xn.shape, F32)
    for f in range(0, ff, tf):
        g = _dot(h, wg_ref[:, f:f + tf])
        u = _dot(h, wu_ref[:, f:f + tf])
        a = (g * _sigmoid(g) * u).astype(BF16)
        acc = acc + _dot(a, wd_ref[f:f + tf, :])
    o_ref[0] = xn + gate2_ref[0] * acc


def _merge_dense_ffn(oa, ob, oc, x, gate1, shift, scale, g2, w_out, gate2, wg, wu, wd, tm):
    b, s, d = x.shape
    ff = wg.shape[1]
    row = lambda bb, i: (bb, i, 0)
    per_b = lambda bb, i: (bb, 0, 0)
    const2 = lambda bb, i: (0, 0)
    resident = pl.Buffered(1)
    return pl.pallas_call(
        functools.partial(_merge_ffn_kernel, tf=MXU_DIM),
        out_shape=jax.ShapeDtypeStruct((b, s, d), F32),
        grid=(b, s // tm),
        in_specs=[
            pl.BlockSpec((1, tm, oa.shape[2]), row), pl.BlockSpec((1, tm, ob.shape[2]), row),
            pl.BlockSpec((1, tm, oc.shape[2]), row), pl.BlockSpec((1, tm, d), row),
            pl.BlockSpec((1, 1, d), per_b), pl.BlockSpec((1, 1, d), per_b), pl.BlockSpec((1, 1, d), per_b),
            pl.BlockSpec((1, d), const2), pl.BlockSpec((d, d), const2, pipeline_mode=resident),
            pl.BlockSpec((1, 1, d), per_b),
            pl.BlockSpec((d, ff), const2, pipeline_mode=resident),
            pl.BlockSpec((d, ff), const2, pipeline_mode=resident),
            pl.BlockSpec((ff, d), const2, pipeline_mode=resident),
        ],
        out_specs=pl.BlockSpec((1, tm, d), row),
        compiler_params=_cparams(("arbitrary", "arbitrary")), name="merge_dense_ffn",
    )(oa, ob, oc, x, gate1, shift, scale, g2, w_out, gate2, wg, wu, wd)


def _top2(lg):
    sub = lax.broadcasted_iota(I32, lg.shape, 0)
    l1 = jnp.max(lg, axis=0, keepdims=True)
    i1 = jnp.min(jnp.where(lg == l1, sub, N_EXPERTS), axis=0, keepdims=True)
    m1 = sub == i1
    lg2 = jnp.where(m1, -jnp.inf, lg)
    l2 = jnp.max(lg2, axis=0, keepdims=True)
    i2 = jnp.min(jnp.where(lg2 == l2, sub, N_EXPERTS), axis=0, keepdims=True)
    m2 = sub == i2
    return l1, l2, m1, m2


def _sublane_cumsum(x):
    sub = lax.broadcasted_iota(I32, x.shape, 0)
    for sh in (1, 2, 4):
        x = x + jnp.where(sub >= sh, pltpu.roll(x, sh, 0), 0.0)
    return x


def _route_kernel(lg_ref, tri_ref, dest_ref, gates_ref, be_ref, pad_ref, base_ref, start_ref, *, block_rows, total_rows):
    phase = pl.program_id(0)
    j = pl.program_id(1)
    l1, l2, m1, m2 = _top2(lg_ref[...])
    e = jnp.where(m1 | m2, 1.0, 0.0).astype(F32)
    cnt = jnp.sum(e, axis=1, keepdims=True)

    @pl.when((phase == 0) & (j == 0))
    def _():
        base_ref[...] = jnp.zeros(base_ref.shape, F32)

    @pl.when((phase == 1) & (j == 0))
    def _():
        counts = base_ref[...]
        nblk = jnp.floor((counts + (block_rows - 1)) * (1.0 / block_rows))
        end_blk = _sublane_cumsum(nblk)
        start_ref[...] = (end_blk - nblk) * block_rows
        blk = lax.broadcasted_iota(I32, be_ref.shape, 1).astype(F32)
        owner = jnp.sum(jnp.where(end_blk[:, :1] <= blk, 1.0, 0.0), axis=0, keepdims=True)
        be_ref[...] = jnp.broadcast_to(jnp.minimum(owner, N_EXPERTS - 1.0), be_ref.shape).astype(I32)
        sub = lax.broadcasted_iota(I32, pad_ref.shape, 0)
        lane = lax.broadcasted_iota(I32, pad_ref.shape, 1)
        pad_end = jnp.where(sub == N_EXPERTS - 1, float(total_rows), end_blk * block_rows)
        pad_ref[...] = jnp.where(lane < LANES // 2, start_ref[...] + counts, pad_end).astype(I32)
        base_ref[...] = jnp.zeros(base_ref.shape, F32)

    @pl.when(phase == 1)
    def _():
        prefix = _dot(e.astype(BF16), tri_ref[...]) + base_ref[:, :1] + start_ref[:, :1]
        d1 = jnp.sum(jnp.where(m1, prefix, 0.0), axis=0, keepdims=True)
        d2 = jnp.sum(jnp.where(m2, prefix, 0.0), axis=0, keepdims=True)
        sub = lax.broadcasted_iota(I32, dest_ref.shape, 0)
        dest_ref[...] = jnp.where(sub == 0, d1, jnp.where(sub == 1, d2, 0.0)).astype(I32)
        ex = jnp.exp(l2 - l1)
        g1 = 1.0 / (1.0 + ex)
        g2 = ex / (1.0 + ex)
        half = lax.broadcasted_iota(I32, (LANES, lg_ref.shape[1]), 0) < LANES // 2
        gates_ref[...] = jnp.where(half, g1, g2).T

    base_ref[...] = base_ref[...] + cnt


def _route(logits_t, block_rows, n_blocks, tr):
    n = logits_t.shape[1]
    n_blocks_pad = -(-n_blocks // LANES) * LANES
    tri = jnp.asarray(np.triu(np.ones((tr, tr), np.float32), k=1), BF16)
    return pl.pallas_call(
        functools.partial(_route_kernel, block_rows=block_rows, total_rows=n_blocks * block_rows),
        out_shape=(jax.ShapeDtypeStruct((N_EXPERTS, n), I32),
                   jax.ShapeDtypeStruct((n, LANES), F32),
                   jax.ShapeDtypeStruct((N_EXPERTS, n_blocks_pad), I32),
                   jax.ShapeDtypeStruct((N_EXPERTS, LANES), I32)),
        grid=(2, n // tr),
        in_specs=[pl.BlockSpec((N_EXPERTS, tr), lambda p, j: (0, j)),
                  pl.BlockSpec((tr, tr), lambda p, j: (0, 0))],
        out_specs=(pl.BlockSpec((N_EXPERTS, tr), lambda p, j: (0, j * p)),
                   pl.BlockSpec((tr, LANES), lambda p, j: (j * p, 0)),
                   pl.BlockSpec((N_EXPERTS, n_blocks_pad), lambda p, j: (0, 0)),
                   pl.BlockSpec((N_EXPERTS, LANES), lambda p, j: (0, 0))),
        scratch_shapes=[pltpu.VMEM((N_EXPERTS, LANES), F32), pltpu.VMEM((N_EXPERTS, LANES), F32)],
        compiler_params=_cparams(("arbitrary", "arbitrary")), name="moe_route",
    )(logits_t, tri)


def _row_copy(src_hbm, src_row, dst_hbm, dst_row, sem):
    src = pl.ds(pl.multiple_of(src_row * ROW_TILE, ROW_TILE), ROW_TILE)
    dst = pl.ds(pl.multiple_of(dst_row * ROW_TILE, ROW_TILE), ROW_TILE)
    return pltpu.make_async_copy(src_hbm.at[src], dst_hbm.at[dst], sem)


def _scatter_kernel(pad_lo_ref, pad_hi_ref, d1_ref, d2_ref, src_ref, out_hbm, zero_ref, sem, zero_sem, *, rows):
    @pl.when(pl.program_id(0) == 0)
    def _():
        zero_ref[...] = jnp.zeros(zero_ref.shape, zero_ref.dtype)
        for e in range(N_EXPERTS):
            def fill(row, c):
                _row_copy(zero_ref, 0, out_hbm, row, zero_sem).start()
                return c

            def drain(row, c):
                _row_copy(zero_ref, 0, out_hbm, 0, zero_sem).wait()
                return c

            lax.fori_loop(pad_lo_ref[e], pad_hi_ref[e], fill, 0)
            lax.fori_loop(pad_lo_ref[e], pad_hi_ref[e], drain, 0)

    def start(r, c):
        _row_copy(src_ref, r, out_hbm, d1_ref[0, 0, r], sem).start(priority=0)
        _row_copy(src_ref, r, out_hbm, d2_ref[0, 0, r], sem).start(priority=1)
        return c

    lax.fori_loop(0, rows, start, 0, unroll=DMA_UNROLL)
    for _ in range(2):
        pltpu.make_async_copy(src_ref, out_hbm.at[pl.ds(0, rows * ROW_TILE)], sem).wait()


def _scatter_rows(src, d1, d2, pad_lo, pad_hi, total_rows, rows):
    n = d1.shape[0]
    idx_spec = pl.BlockSpec((1, 1, rows), lambda i, lo, hi: (i, 0, 0), memory_space=pltpu.SMEM)
    any_spec = pl.BlockSpec(memory_space=pl.ANY)
    grid_spec = pltpu.PrefetchScalarGridSpec(
        num_scalar_prefetch=2,
        grid=(n // rows,),
        in_specs=[idx_spec, idx_spec, pl.BlockSpec((rows * ROW_TILE, LANES), lambda i, lo, hi: (i, 0))],
        out_specs=any_spec,
        scratch_shapes=[pltpu.VMEM((ROW_TILE, LANES), src.dtype), pltpu.SemaphoreType.DMA(()),
                        pltpu.SemaphoreType.DMA(())],
    )
    return pl.pallas_call(
        functools.partial(_scatter_kernel, rows=rows),
        out_shape=jax.ShapeDtypeStruct((total_rows * ROW_TILE, LANES), src.dtype),
        grid_spec=grid_spec,
        compiler_params=pltpu.CompilerParams(dimension_semantics=("arbitrary",), has_side_effects=True),
        name="moe_scatter_rows",
    )(pad_lo, pad_hi, d1.reshape(n // rows, 1, rows), d2.reshape(n // rows, 1, rows), src)


def _expert_kernel(be_ref, x_ref, wg_ref, wu_ref, wd_ref, o_ref, *, block_rows, tf):
    del be_ref
    x = _load_row_tiles(x_ref, block_rows).astype(BF16)
    ff = wg_ref.shape[2]
    acc = jnp.zeros((block_rows, wd_ref.shape[2]), F32)
    for f in range(0, ff, tf):
        g = _dot(x, wg_ref[0, :, f:f + tf])
        u = _dot(x, wu_ref[0, :, f:f + tf])
        a = (g * _sigmoid(g) * u).astype(BF16)
        acc = acc + _dot(a, wd_ref[0, f:f + tf, :])
    _store_row_tiles(o_ref, acc, block_rows)


def _expert_ffn(xb, blk_expert, wg, wu, wd, block_rows):
    d, ff = wg.shape[1], wg.shape[2]
    rows = xb.shape[0] // ROW_TILE
    resident = pl.Buffered(1)
    grid_spec = pltpu.PrefetchScalarGridSpec(
        num_scalar_prefetch=1,
        grid=(rows // block_rows,),
        in_specs=[
            pl.BlockSpec((block_rows * ROW_TILE, LANES), lambda i, be: (i, 0)),
            pl.BlockSpec((1, d, ff), lambda i, be: (be[i], 0, 0), pipeline_mode=resident),
            pl.BlockSpec((1, d, ff), lambda i, be: (be[i], 0, 0), pipeline_mode=resident),
            pl.BlockSpec((1, ff, d), lambda i, be: (be[i], 0, 0), pipeline_mode=resident),
        ],
        out_specs=pl.BlockSpec((block_rows * ROW_TILE, LANES), lambda i, be: (i, 0)),
    )
    return pl.pallas_call(
        functools.partial(_expert_kernel, block_rows=block_rows, tf=MXU_DIM),
        out_shape=jax.ShapeDtypeStruct(xb.shape, F32), grid_spec=grid_spec,
        compiler_params=_cparams(("arbitrary",)), name="moe_expert_ffn",
    )(blk_expert, xb, wg, wu, wd)


def _combine_kernel(d1_ref, d2_ref, d1n_ref, d2n_ref, x_ref, yb_hbm, gates_ref, gate_ref, fg_ref, o_ref,
                    y1_ref, y2_ref, sems):
    tm = x_ref.shape[0]
    i = pl.program_id(0)
    slot = i % 2

    def gather(i1_ref, i2_ref, to_slot):
        def start(r, c):
            _row_copy(yb_hbm, i1_ref[0, 0, r], y1_ref.at[to_slot], r, sems.at[to_slot]).start(priority=0)
            _row_copy(yb_hbm, i2_ref[0, 0, r], y2_ref.at[to_slot], r, sems.at[to_slot]).start(priority=1)
            return c

        lax.fori_loop(0, tm, start, 0, unroll=DMA_UNROLL)

    @pl.when(i == 0)
    def _():
        gather(d1_ref, d2_ref, slot)

    @pl.when(i + 1 < pl.num_programs(0))
    def _():
        gather(d1n_ref, d2n_ref, 1 - slot)

    for y_ref in (y1_ref, y2_ref):
        pltpu.make_async_copy(yb_hbm.at[pl.ds(0, tm * ROW_TILE)], y_ref.at[slot], sems.at[slot]).wait()

    gts = gates_ref[...]
    y = (gts[:, 0:1] * _load_row_tiles(y1_ref.at[slot], tm)
         + gts[:, LANES // 2:LANES // 2 + 1] * _load_row_tiles(y2_ref.at[slot], tm))
    xn = x_ref[...] + gate_ref[0] * y
    ms = jnp.mean(xn * xn, axis=-1, keepdims=True)
    o_ref[...] = xn * lax.rsqrt(ms + EPS) * fg_ref[...]


def _combine_final(x, yb, d1, d2, gates, gate, final_g, tm):
    n, d = x.shape
    s = n // gate.shape[0]
    row = lambda i: (i, 0)
    steps = n // tm
    idx_spec = pl.BlockSpec((1, 1, tm), lambda i: (i, 0, 0), memory_space=pltpu.SMEM)
    next_spec = pl.BlockSpec((1, 1, tm), lambda i: (jnp.minimum(i + 1, steps - 1), 0, 0), memory_space=pltpu.SMEM)
    d1 = d1.reshape(steps, 1, tm)
    d2 = d2.reshape(steps, 1, tm)
    return pl.pallas_call(
        _combine_kernel,
        out_shape=jax.ShapeDtypeStruct((n, d), F32),
        grid=(steps,),
        in_specs=[idx_spec, idx_spec, next_spec, next_spec,
                  pl.BlockSpec((tm, d), row),
                  pl.BlockSpec(memory_space=pl.ANY),
                  pl.BlockSpec((tm, LANES), row),
                  pl.BlockSpec((1, 1, d), lambda i: ((i * tm) // s, 0, 0)),
                  pl.BlockSpec((1, d), lambda i: (0, 0))],
        out_specs=pl.BlockSpec((tm, d), row),
        scratch_shapes=[pltpu.VMEM((2, tm * ROW_TILE, LANES), F32), pltpu.VMEM((2, tm * ROW_TILE, LANES), F32),
                        pltpu.SemaphoreType.DMA((2,))],
        compiler_params=_cparams(("arbitrary",)), name="moe_combine_final",
    )(d1, d2, d1, d2, x, yb, gates, gate, final_g)


AMAX_QA, AMAX_KA, AMAX_VA, AMAX_QC, AMAX_KC, AMAX_VC = range(6)
FP8_TARGET_MAX = 256.0


def _fp8_scales(q_max, k_max, v_max, head_dim):
    tiny = jnp.finfo(F32).tiny
    c = head_dim ** -0.5 * LOG2E
    ratio = jnp.where((q_max > 0) & (k_max > 0), c * k_max / jnp.maximum(q_max, tiny), 1.0)
    sq = jnp.exp2(jnp.round(0.5 * jnp.log2(ratio)))
    sk = c / sq
    sv = jnp.where(v_max > 0, jnp.exp2(jnp.floor(jnp.log2(FP8_TARGET_MAX / jnp.maximum(v_max, tiny)))), 1.0)
    rows = jnp.stack([sq, sk, sv, 1.0 / sv] + [jnp.zeros_like(sq)] * (SUBLANES - 4), axis=1)
    return jnp.broadcast_to(rows[:, :, None], rows.shape + (LANES,)).astype(F32)


def _rope_tables(s, dim):
    half = dim // 2
    t = jnp.arange(s)
    inv = 1.0 / (ROPE_THETA ** (jnp.arange(0, half, 2, dtype=F32) / half))
    ang_r = (t // GRID_W).astype(F32)[:, None] * inv
    ang_c = (t % GRID_W).astype(F32)[:, None] * inv
    ang = jnp.concatenate([ang_r, ang_r, ang_c, ang_c], axis=-1)
    reps = LANES // dim
    return jnp.tile(jnp.cos(ang), (1, reps)), jnp.tile(jnp.sin(ang), (1, reps))


def _rotate_matrix(dim):
    q = dim // 4
    p = np.zeros((MXU_DIM, MXU_DIM), np.float32)
    for j in range(MXU_DIM):
        if (j % (2 * q)) < q:
            p[j + q, j] = -1.0
        else:
            p[j - q, j] = 1.0
    return jnp.asarray(p, BF16)


def _head_mean_matrix():
    m = np.kron(np.eye(MXU_DIM // HEAD_V, dtype=np.float32), np.full((HEAD_V, HEAD_V), 1.0 / HEAD_V, np.float32))
    return jnp.asarray(m, BF16)


def _widen_values(w, heads):
    d = w.shape[0]
    w = w.reshape(d, heads, HEAD_V)
    return jnp.concatenate([w, jnp.zeros_like(w)], axis=-1).reshape(d, heads * LANES)


def _widen_in_proj(w):
    qa, ka, va, gb, qc, kc, vc = jnp.split(w, [256, 512, 768, 1280, 1792, 1920], axis=1)
    return jnp.concatenate([qa, ka, _widen_values(va, DIFF_HEADS), gb, qc, kc, _widen_values(vc, GQA_KV)],
                           axis=1).astype(BF16)


def kernel(x, c, ctx, c_ctx, ada_w, ada_b, norm1_g, norm2_g, w_in, w_out, lam_q1, lam_k1, lam_q2, lam_k2,
           diff_subln_g, conv_w, conv_b, conv_ln_g, conv_ln_b, q_norm_g, k_norm_g, ffn_gate, ffn_up, ffn_down,
           router_w, moe_gate, moe_up, moe_down, final_g):
    b, s, d = x.shape
    sc = ctx.shape[1]
    depth = ada_w.shape[0]
    n = b * s
    assert depth % 2 == 0, "the final RMSNorm is fused into the MoE combine of the last (odd) layer"

    tm = min(ROW_BLOCK, s)
    tmc = min(ROW_BLOCK, sc)
    tq = min(GQA_QUERY_BLOCK, s)
    tq_diff = min(DIFF_QUERY_BLOCK, s)
    tqc = min(GQA_QUERY_BLOCK, sc)

    tabs_x = _rope_tables(s, DIFF_QK) + _rope_tables(s, HEAD_V)
    ones_c, zeros_c = jnp.ones((sc, LANES), F32), jnp.zeros((sc, LANES), F32)
    tabs_c = (ones_c, zeros_c, ones_c, zeros_c)
    mats = (_rotate_matrix(DIFF_QK), _rotate_matrix(HEAD_V), _head_mean_matrix())

    cc = jnp.zeros((16, d), F32).at[:b].set(c).at[b].set(c_ctx)

    for i in range(depth):
        last = i == depth - 1
        lam_init = 0.8 - 0.6 * math.exp(-0.3 * i)
        mod_all = _ada_mod(cc, ada_w[i], ada_b[i])
        mod = mod_all[:b].reshape(b, 6, 1, d)
        modc = jnp.broadcast_to(mod_all[b].reshape(1, 6, 1, d), (b, 6, 1, d))

        w_aug = _widen_in_proj(w_in[i])
        g1 = norm1_g[i].reshape(1, d)
        qg = jnp.tile(q_norm_g[i], GQA_HEADS).reshape(1, -1)
        kg = jnp.tile(k_norm_g[i], GQA_KV).reshape(1, -1)
        lam_vecs = jnp.stack([lam_q1[i], lam_k1[i], lam_q2[i], lam_k2[i]]).astype(F32)
        subln = diff_subln_g[i].reshape(1, HEAD_V)
        conv_args = (conv_w[i], conv_b[i].reshape(1, -1), conv_ln_g[i].reshape(1, -1), conv_ln_b[i].reshape(1, -1))
        w_o = w_out[i].astype(BF16)
        g2 = norm2_g[i].reshape(1, d)

        qa, kat, va, gb, qc, kct, vc, amax = _in_projection(
            x, mod[:, 0], mod[:, 1], g1, w_aug, tabs_x, mats, qg, kg, tm, math.gcd(s // tm, INPROJ_SUBTILES))
        qa_x, kat_x, va_x, gb_x, qc_x, kct_x, vc_x, amax_x = _in_projection(
            ctx, modc[:, 0], modc[:, 1], g1, w_aug, tabs_c, mats, qg, kg, tmc, 1)
        amax = jnp.max(amax, axis=(1, 3))
        amax_x = jnp.max(amax_x, axis=(1, 3))
        amax_kv = jnp.maximum(amax, amax_x)
        scl_a = _fp8_scales(amax[:, AMAX_QA], amax_kv[:, AMAX_KA], amax_kv[:, AMAX_VA], DIFF_QK)
        scl_c = _fp8_scales(amax[:, AMAX_QC], amax_kv[:, AMAX_KC], amax_kv[:, AMAX_VC], HEAD_V)

        oa = _diff_attention(qa, [kat, kat_x], [va, va_x], scl_a, lam_vecs, subln, lam_init, tq_diff)
        ob = _conformer_conv(gb, *conv_args, tm)
        oc = _gqa_attention(qc, [kct, kct_x], [vc, vc_x], scl_c, tq)

        j = i // 2
        if i % 2 == 0:
            wg, wu, wd = ffn_gate[j].astype(BF16), ffn_up[j].astype(BF16), ffn_down[j].astype(BF16)
            x = _merge_dense_ffn(oa, ob, oc, x, mod[:, 2], mod[:, 3], mod[:, 4], g2, w_o, mod[:, 5], wg, wu, wd, tm)
        else:
            rwt = router_w[j].T.astype(F32)
            x, h2, logits_t = _merge_route(oa, ob, oc, x, mod[:, 2], mod[:, 3], mod[:, 4], g2, w_o, rwt, tm)
            block_rows = min(MOE_BLOCK_ROWS, n // 4)
            n_blocks = (2 * n) // block_rows + N_EXPERTS
            dest, gates, blk_e, pad = _route(logits_t, block_rows, n_blocks, min(ROW_BLOCK, n))
            xb = _scatter_rows(h2, dest[0], dest[1], pad[:, 0], pad[:, LANES // 2], n_blocks * block_rows,
                               min(MOE_DMA_ROWS, n))
            yb = _expert_ffn(xb, blk_e[0, :n_blocks], moe_gate[j].astype(BF16), moe_up[j].astype(BF16),
                             moe_down[j].astype(BF16), block_rows)
            assert last
            x = _combine_final(x.reshape(n, d), yb, dest[0], dest[1], gates, mod[:, 5], final_g.reshape(1, d),
                               tm).reshape(b, s, d)

        if not last:
            scl_ax = _fp8_scales(amax_x[:, AMAX_QA], amax_x[:, AMAX_KA], amax_x[:, AMAX_VA], DIFF_QK)
            scl_cx = _fp8_scales(amax_x[:, AMAX_QC], amax_x[:, AMAX_KC], amax_x[:, AMAX_VC], HEAD_V)
            oa_x = _diff_attention(qa_x, [kat_x], [va_x], scl_ax, lam_vecs, subln, lam_init, tqc)
            ob_x = _conformer_conv(gb_x, *conv_args, tmc)
            oc_x = _gqa_attention(qc_x, [kct_x], [vc_x], scl_cx, tqc)
            assert i % 2 == 0, "context tokens only ever pass through dense channel mixers"
            ctx = _merge_dense_ffn(oa_x, ob_x, oc_x, ctx, modc[:, 2], modc[:, 3], modc[:, 4], g2, w_o, modc[:, 5],
                                   wg, wu, wd, tmc)

    return x
```
